```python
import jax
import jax.numpy as jnp
from jax import lax
import numpy as np

D_MODEL = 1024
BATCH = 8
SEQ = 2048
DEPTH = 2
DEC_BATCH = 128
DEC_SEQ = 4
PAST_LEN = 16384
PAGE_SIZE = 128

MIX_WIDTH = D_MODEL
CONV_WIDTH = MIX_WIDTH // 2
CONV_K = 3
DN_HEADS = 4
DN_HEAD_DIM = (MIX_WIDTH - CONV_WIDTH) // DN_HEADS
DN_WIDTH = DN_HEADS * DN_HEAD_DIM
DN_CONV_K = 4
DN_CHUNK = 64
D_FF = 2816
N_EXPERTS = 8
TOP_K = 2
D_EXPERT = D_FF // 2
N_DENSE = (DEPTH + 1) // 2
N_MOE = DEPTH // 2
DEEPNORM_ALPHA = (2.0 * DEPTH) ** 0.25
DEEPNORM_BETA = (8.0 * DEPTH) ** -0.25
LN_EPS = 1e-5
RMS_EPS = 1e-6

OFF_CB = 0
OFF_CC = OFF_CB + CONV_WIDTH
OFF_CH = OFF_CC + CONV_WIDTH
OFF_QKV = OFF_CH + CONV_WIDTH
OFF_Z = OFF_QKV + 3 * DN_WIDTH
OFF_BETA = OFF_Z + DN_WIDTH
OFF_A = OFF_BETA + DN_HEADS
P_TOTAL = OFF_A + DN_HEADS

kernel_name = 'hymba_conv_gdn_deepnorm_adaln_step'

F32 = jnp.float32


def layer_norm(x, g, b):
    xf = x.astype(F32)
    mu = jnp.mean(xf, -1, keepdims=True)
    xc = xf - mu
    var = jnp.mean(xc * xc, -1, keepdims=True)
    return (xc * lax.rsqrt(var + LN_EPS) * g.astype(F32) + b.astype(F32)).astype(x.dtype)


def l2norm(x):
    return x * lax.rsqrt(jnp.sum(x * x, -1, keepdims=True) + RMS_EPS)


def causal_dwconv(u, buf, w):
    k = w.shape[0]
    t = u.shape[1]
    ext = jnp.concatenate([buf.astype(u.dtype), u], axis=1)
    y = ext[:, 0:t] * w[0]
    for i in range(1, k):
        y = y + ext[:, i:i + t] * w[i]
    return y, ext[:, ext.shape[1] - (k - 1):]


def gated_delta_rule(q, k, v, g, beta, s0):
    bsz, t, h, dk = q.shape
    dv = v.shape[-1]
    c = min(DN_CHUNK, t)
    n = -(-t // c)
    pad = n * c - t
    if pad:
        pw = ((0, 0), (0, pad), (0, 0), (0, 0))
        q = jnp.pad(q, pw)
        k = jnp.pad(k, pw)
        v = jnp.pad(v, pw)
        g = jnp.pad(g, pw[:3])
        beta = jnp.pad(beta, pw[:3])

    def blk(a):
        a = a.reshape((bsz, n, c, h) + a.shape[3:])
        return jnp.moveaxis(a, (1, 3), (0, 2))

    qb, kb, vb, gb, bb = blk(q), blk(k), blk(v), blk(g), blk(beta)
    gc = jnp.cumsum(gb, axis=-1)
    causal = jnp.tril(jnp.ones((c, c), dtype=bool))
    strict = jnp.tril(jnp.ones((c, c), dtype=bool), -1)
    gamma = jnp.exp(jnp.where(causal, gc[..., :, None] - gc[..., None, :], -jnp.inf))
    kbeta = kb * bb[..., None]
    m = jnp.where(strict, jnp.einsum('nbhcd,nbhsd->nbhcs', kbeta, kb) * gamma, 0.0)
    a_mat = jnp.eye(c, dtype=F32) + m
    rhs = jnp.concatenate([kbeta * jnp.exp(gc)[..., None], vb * bb[..., None]], axis=-1)
    sol = lax.linalg.triangular_solve(a_mat, rhs, left_side=True, lower=True, unit_diagonal=True)
    w_blk, u_blk = sol[..., :dk], sol[..., dk:]
    attn = jnp.einsum('nbhcd,nbhsd->nbhcs', qb, kb) * gamma
    q_dec = qb * jnp.exp(gc)[..., None]
    k_dec = kb * jnp.exp(gc[..., -1:] - gc)[..., None]
    last_decay = jnp.exp(gc[..., -1])

    def step(s, xs):
        w_c, u_c, q_c, a_c, k_c, d_c = xs
        v_new = u_c - jnp.einsum('bhcd,bhde->bhce', w_c, s)
        o_c = jnp.einsum('bhcd,bhde->bhce', q_c, s) + jnp.einsum('bhcs,bhse->bhce', a_c, v_new)
        s = s * d_c[..., None, None] + jnp.einsum('bhcd,bhce->bhde', k_c, v_new)
        return s, o_c

    s_fin, o = lax.scan(step, s0, (w_blk, u_blk, q_dec, attn, k_dec, last_decay))
    o = jnp.moveaxis(o, (0, 2), (1, 3)).reshape(bsz, n * c, h, dv)[:, :t]
    return o, s_fin


def mixer(u, conv_buf, dn_buf, s0, w_in, conv_w, dn_conv_w, a_log, dt_bias, dn_norm_w, w_out):
    bsz, t, _ = u.shape
    p = u @ w_in
    gate_b = p[..., OFF_CB:OFF_CC]
    gate_c = p[..., OFF_CC:OFF_CH]
    h_in = p[..., OFF_CH:OFF_QKV]
    yc, new_conv_buf = causal_dwconv(gate_c * h_in, conv_buf, conv_w)
    y_conv = gate_b * yc
    qkv, new_dn_buf = causal_dwconv(p[..., OFF_QKV:OFF_Z], dn_buf, dn_conv_w)
    qkv = jax.nn.silu(qkv.astype(F32))
    q = qkv[..., 0:DN_WIDTH].reshape(bsz, t, DN_HEADS, DN_HEAD_DIM)
    k = qkv[..., DN_WIDTH:2 * DN_WIDTH].reshape(bsz, t, DN_HEADS, DN_HEAD_DIM)
    v = qkv[..., 2 * DN_WIDTH:].reshape(bsz, t, DN_HEADS, DN_HEAD_DIM)
    q = l2norm(q) * (DN_HEAD_DIM ** -0.5)
    k = l2norm(k)
    z = p[..., OFF_Z:OFF_BETA].astype(F32)
    beta = jax.nn.sigmoid(p[..., OFF_BETA:OFF_A].astype(F32))
    g = -jnp.exp(a_log.astype(F32)) * jax.nn.softplus(p[..., OFF_A:P_TOTAL].astype(F32) + dt_bias.astype(F32))
    o, s_new = gated_delta_rule(q, k, v, g, beta, s0.astype(F32))
    o = o * lax.rsqrt(jnp.mean(o * o, -1, keepdims=True) + RMS_EPS) * dn_norm_w.astype(F32)
    o = o.reshape(bsz, t, DN_WIDTH) * jax.nn.silu(z)
    mixed = jnp.concatenate([y_conv, o.astype(u.dtype)], axis=-1)
    return mixed @ w_out, new_conv_buf, new_dn_buf, s_new


def dense_ffn(u, w_gate, w_up, w_down):
    return (jax.nn.silu(u @ w_gate) * (u @ w_up)) @ w_down


def moe_ffn(u, w_router, b_router, w_gate, w_up, w_down):
    logits = (u @ w_router).astype(F32) + b_router.astype(F32)
    probs = jax.nn.softmax(logits, axis=-1)
    top_p, top_i = lax.top_k(probs, TOP_K)
    top_p = top_p / jnp.sum(top_p, -1, keepdims=True)
    gates = jnp.sum(jax.nn.one_hot(top_i, N_EXPERTS, dtype=F32) * top_p[..., None], axis=-2)
    y = jnp.zeros(u.shape, F32)
    for e in range(N_EXPERTS):
        h = jax.nn.silu(u @ w_gate[e]) * (u @ w_up[e])
        y = y + gates[..., e:e + 1] * (h @ w_down[e]).astype(F32)
    return y.astype(u.dtype)


def trunk(x, c, conv_bufs, dn_bufs, s0s, prm):
    new_conv, new_dnc, new_s = [], [], []
    for l in range(DEPTH):
        mod = jax.nn.silu(c) @ prm['w_ada'][l] + prm['b_ada'][l]
        sh1, sc1, g1, sh2, sc2, g2 = jnp.split(mod[:, None, :], 6, axis=-1)
        u = x * (1 + sc1) + sh1
        m, cb, db, s = mixer(u, conv_bufs[l], dn_bufs[l], s0s[l], prm['w_in'][l], prm['conv_w'][l],
                             prm['dn_conv_w'][l], prm['a_log'][l], prm['dt_bias'][l],
                             prm['dn_norm_w'][l], prm['w_out'][l])
        new_conv.append(cb)
        new_dnc.append(db)
        new_s.append(s)
        x = layer_norm(DEEPNORM_ALPHA * x + (1 + g1) * m, prm['ln_g'][l, 0], prm['ln_b'][l, 0])
        u = x * (1 + sc2) + sh2
        j = l // 2
        if l % 2 == 0:
            f = dense_ffn(u, prm['w_ff_gate'][j], prm['w_ff_up'][j], prm['w_ff_down'][j])
        else:
            f = moe_ffn(u, prm['w_router'][j], prm['b_router'][j], prm['w_exp_gate'][j],
                        prm['w_exp_up'][j], prm['w_exp_down'][j])
        x = layer_norm(DEEPNORM_ALPHA * x + (1 + g2) * f, prm['ln_g'][l, 1], prm['ln_b'][l, 1])
    return x, jnp.stack(new_conv), jnp.stack(new_dnc), jnp.stack(new_s)


def setup_inputs(seed: int = 0) -> dict:
    key = jax.random.key(seed)
    ks = jax.random.split(key, 32)
    nrm = jax.random.normal
    d = D_MODEL
    dt = jnp.exp(jax.random.uniform(ks[10], (DEPTH, DN_HEADS), F32, np.log(1e-3), np.log(1e-1)))
    return {
        'x_prompt': nrm(ks[0], (BATCH, SEQ, d), F32),
        'x_sample': nrm(ks[1], (DEC_BATCH, DEC_SEQ, d), F32),
        'state_conv': nrm(ks[2], (DEPTH, DEC_BATCH, CONV_K - 1, CONV_WIDTH), F32),
        'state_dn_conv': nrm(ks[3], (DEPTH, DEC_BATCH, DN_CONV_K - 1, 3 * DN_WIDTH), F32),
        'state_dn': 0.1 * nrm(ks[4], (DEPTH, DEC_BATCH, DN_HEADS, DN_HEAD_DIM, DN_HEAD_DIM), F32),
        'c_prompt': nrm(ks[5], (BATCH, d), F32),
        'c_sample': nrm(ks[6], (DEC_BATCH, d), F32),
        'w_in': nrm(ks[7], (DEPTH, d, P_TOTAL), F32) * d ** -0.5,
        'conv_w': nrm(ks[8], (DEPTH, CONV_K, CONV_WIDTH), F32) * CONV_K ** -0.5,
        'dn_conv_w': nrm(ks[9], (DEPTH, DN_CONV_K, 3 * DN_WIDTH), F32) * DN_CONV_K ** -0.5,
        'a_log': jnp.log(jax.random.uniform(ks[11], (DEPTH, DN_HEADS), F32, 1.0, 16.0)),
        'dt_bias': dt + jnp.log(-jnp.expm1(-dt)),
        'dn_norm_w': 1.0 + 0.02 * nrm(ks[12], (DEPTH, DN_HEAD_DIM), F32),
        'w_out': nrm(ks[13], (DEPTH, MIX_WIDTH, d), F32) * MIX_WIDTH ** -0.5 * DEEPNORM_BETA,
        'w_ada': nrm(ks[14], (DEPTH, d, 6 * d), F32) * d ** -0.5 * 0.2,
        'b_ada': 0.02 * nrm(ks[15], (DEPTH, 6 * d), F32),
        'ln_g': 1.0 + 0.02 * nrm(ks[16], (DEPTH, 2, d), F32),
        'ln_b': 0.02 * nrm(ks[17], (DEPTH, 2, d), F32),
        'w_ff_gate': nrm(ks[18], (N_DENSE, d, D_FF), F32) * d ** -0.5,
        'w_ff_up': nrm(ks[19], (N_DENSE, d, D_FF), F32) * d ** -0.5,
        'w_ff_down': nrm(ks[20], (N_DENSE, D_FF, d), F32) * D_FF ** -0.5 * DEEPNORM_BETA,
        'w_router': nrm(ks[21], (N_MOE, d, N_EXPERTS), F32) * d ** -0.5,
        'b_router': 0.01 * nrm(ks[22], (N_MOE, N_EXPERTS), F32),
        'w_exp_gate': nrm(ks[23], (N_MOE, N_EXPERTS, d, D_EXPERT), F32) * d ** -0.5,
        'w_exp_up': nrm(ks[24], (N_MOE, N_EXPERTS, d, D_EXPERT), F32) * d ** -0.5,
        'w_exp_down': nrm(ks[25], (N_MOE, N_EXPERTS, D_EXPERT, d), F32) * D_EXPERT ** -0.5 * DEEPNORM_BETA,
    }


def reference(x_prompt, x_sample, state_conv, state_dn_conv, state_dn, c_prompt, c_sample,
              w_in, conv_w, dn_conv_w, a_log, dt_bias, dn_norm_w, w_out, w_ada, b_ada, ln_g, ln_b,
              w_ff_gate, w_ff_up, w_ff_down, w_router, b_router, w_exp_gate, w_exp_up, w_exp_down):
    prm = {'w_in': w_in, 'conv_w': conv_w, 'dn_conv_w': dn_conv_w, 'a_log': a_log, 'dt_bias': dt_bias,
           'dn_norm_w': dn_norm_w, 'w_out': w_out, 'w_ada': w_ada, 'b_ada': b_ada, 'ln_g': ln_g,
           'ln_b': ln_b, 'w_ff_gate': w_ff_gate, 'w_ff_up': w_ff_up, 'w_ff_down': w_ff_down,
           'w_router': w_router, 'b_router': b_router, 'w_exp_gate': w_exp_gate,
           'w_exp_up': w_exp_up, 'w_exp_down': w_exp_down}
    bp = x_prompt.shape[0]
    zc = [jnp.zeros((bp, CONV_K - 1, CONV_WIDTH), x_prompt.dtype) for _ in range(DEPTH)]
    zd = [jnp.zeros((bp, DN_CONV_K - 1, 3 * DN_WIDTH), x_prompt.dtype) for _ in range(DEPTH)]
    zs = [jnp.zeros((bp, DN_HEADS, DN_HEAD_DIM, DN_HEAD_DIM), F32) for _ in range(DEPTH)]
    y_prompt, conv_p, dnc_p, dn_p = trunk(x_prompt, c_prompt, zc, zd, zs, prm)
    y_sample, conv_s, dnc_s, dn_s = trunk(
        x_sample, c_sample,
        [state_conv[l] for l in range(DEPTH)],
        [state_dn_conv[l] for l in range(DEPTH)],
        [state_dn[l] for l in range(DEPTH)], prm)
    return (y_prompt, y_sample, conv_p, dnc_p, dn_p, conv_s, dnc_s, dn_s)
```

```python
import functools
import math

import jax
import jax.numpy as jnp
from jax import lax
from jax.experimental import pallas as pl
from jax.experimental.pallas import tpu as pltpu

F32 = jnp.float32
BF16 = jnp.bfloat16

D_MODEL = 1024
DEPTH = 2
CONV_WIDTH = 512
CONV_K = 3
DN_HEADS = 4
DN_HEAD_DIM = 128
DN_WIDTH = DN_HEADS * DN_HEAD_DIM
DN_CONV_K = 4
DN_CHUNK = 64
N_EXPERTS = 8
DEEPNORM_ALPHA = (2.0 * DEPTH) ** 0.25
LN_EPS = 1e-5
RMS_EPS = 1e-6

OFF_CC = CONV_WIDTH
OFF_CH = 2 * CONV_WIDTH
OFF_QKV = 3 * CONV_WIDTH
OFF_Z = OFF_QKV + 3 * DN_WIDTH
P_MAIN = OFF_Z + DN_WIDTH
LANES = 128
BETA_LANE = LANES - 2 * DN_HEADS
G_LANE = LANES - DN_HEADS
SAMPLE_CHUNK = 8

VMEM_LIMIT = 56 * 1024 * 1024


def _cparams(*sem):
    return pltpu.CompilerParams(dimension_semantics=sem, vmem_limit_bytes=VMEM_LIMIT)


def _silu(x):
    return x * jax.nn.sigmoid(x)


def _layer_norm(r, g, b):
    mu = jnp.mean(r, -1, keepdims=True)
    xc = r - mu
    var = jnp.mean(xc * xc, -1, keepdims=True)
    return xc * lax.rsqrt(var + LN_EPS) * g + b


def _split_bf16(a):
    hi = a.astype(BF16)
    lo = (a - hi.astype(F32)).astype(BF16)
    return hi, lo


_NN = (((1,), (0,)), ((), ()))
_NT = (((1,), (1,)), ((), ()))
_TN = (((0,), (0,)), ((), ()))


def _mm(a, b, dims=_NN, mode="bf16"):
    if mode == "f32":
        return lax.dot_general(a, b, dims, precision=lax.Precision.HIGHEST, preferred_element_type=F32)
    if mode == "bf16":
        return lax.dot_general(a.astype(BF16), b.astype(BF16), dims, preferred_element_type=F32)
    ah, al = _split_bf16(a)
    bh, bl = _split_bf16(b)
    d = functools.partial(lax.dot_general, dimension_numbers=dims, preferred_element_type=F32)
    return d(ah, bh) + (d(ah, bl) + d(al, bh))


def _ada_kernel(c_ref, w_ref, b_ref, o_ref):
    s = _silu(c_ref[...]).astype(BF16)
    o_ref[0] = jnp.dot(s, w_ref[0].astype(BF16), preferred_element_type=F32) + b_ref[0]


def _ada(c_all, w_ada, b_ada):
    rows = c_all.shape[0]
    tn = 1536
    return pl.pallas_call(
        _ada_kernel,
        grid=(DEPTH, 6 * D_MODEL // tn),
        in_specs=[
            pl.BlockSpec((rows, D_MODEL), lambda l, j: (0, 0)),
            pl.BlockSpec((1, D_MODEL, tn), lambda l, j: (l, 0, j)),
            pl.BlockSpec((1, 1, tn), lambda l, j: (l, 0, j)),
        ],
        out_specs=pl.BlockSpec((1, rows, tn), lambda l, j: (l, 0, j)),
        out_shape=jax.ShapeDtypeStruct((DEPTH, rows, 6 * D_MODEL), F32),
        compiler_params=_cparams("arbitrary", "arbitrary"),
        name="ada",
    )(c_all, w_ada, b_ada.reshape(DEPTH, 1, 6 * D_MODEL))


def _mod_spec(mod3, chunk, tiles_per_group):
    r = mod3.shape[1]
    return pl.BlockSpec((1, r, D_MODEL), lambda i, *_: (i // tiles_per_group, 0, chunk))


def _inproj_kernel(x_ref, sc_ref, sh_ref, wm_ref, ws_ref, pm_ref, ps_ref):
    u = x_ref[...] * (1.0 + sc_ref[0]) + sh_ref[0]
    pm_ref[...] = jnp.dot(u.astype(BF16), wm_ref[...], preferred_element_type=F32)
    ps_ref[...] = _mm(u, ws_ref[...], mode="x3")


def _inproj(x, mod3, tpg, w_main, w_small, tm):
    n = x.shape[0]
    return pl.pallas_call(
        _inproj_kernel,
        grid=(n // tm,),
        in_specs=[
            pl.BlockSpec((tm, D_MODEL), lambda i: (i, 0)),
            _mod_spec(mod3, 1, tpg),
            _mod_spec(mod3, 0, tpg),
            pl.BlockSpec((D_MODEL, P_MAIN), lambda i: (0, 0)),
            pl.BlockSpec((D_MODEL, LANES), lambda i: (0, 0)),
        ],
        out_specs=[
            pl.BlockSpec((tm, P_MAIN), lambda i: (i, 0)),
            pl.BlockSpec((tm, LANES), lambda i: (i, 0)),
        ],
        out_shape=[
            jax.ShapeDtypeStruct((n, P_MAIN), F32),
            jax.ShapeDtypeStruct((n, LANES), F32),
        ],
        compiler_params=_cparams("arbitrary"),
        name="inproj",
    )(x, mod3, mod3, w_main, w_small)


def _gates(ps, hp):
    lane = lax.broadcasted_iota(jnp.int32, ps.shape, 1)
    beta = jax.nn.sigmoid(ps)
    g = -jnp.exp(hp[0:1, :]) * jax.nn.softplus(ps + hp[1:2, :])
    return jnp.where(lane >= G_LANE, g, jnp.where(lane >= BETA_LANE, beta, 0.0))


def _qkv_finish(y, out_ref, idx):
    y = _silu(y)
    for h in range(3 * DN_HEADS):
        seg = y[:, h * DN_HEAD_DIM:(h + 1) * DN_HEAD_DIM]
        if h < 2 * DN_HEADS:
            seg = seg * lax.rsqrt(jnp.sum(seg * seg, -1, keepdims=True) + RMS_EPS)
            if h < DN_HEADS:
                seg = seg * (DN_HEAD_DIM ** -0.5)
        out_ref[idx + (slice(None), slice(h * DN_HEAD_DIM, (h + 1) * DN_HEAD_DIM))] = seg


def _shifted(x, tail_ref, i, k, row):
    s = pltpu.roll(x, i, 0)
    for r in range(i):
        s = jnp.where(row == r, tail_ref[8 - i + r:8 - i + r + 1, :], s)
    return s


def _pre_prompt_kernel(pm_ref, ps_ref, cw_ref, dw_ref, hp_ref, cb_ref, db_ref,
                       yc_ref, qkv_ref, gb_ref, ncb_ref, ndb_ref, tc_scr, td_scr):
    t = pl.program_id(1)
    tc = pm_ref.shape[0]

    @pl.when(t == 0)
    def _():
        tc_scr[8 - (CONV_K - 1):8, :] = cb_ref[0]
        td_scr[8 - (DN_CONV_K - 1):8, :] = db_ref[0]

    row_c = lax.broadcasted_iota(jnp.int32, (tc, CONV_WIDTH), 0)
    gate_b = pm_ref[:, 0:OFF_CC]
    cgh = pm_ref[:, OFF_CC:OFF_CH] * pm_ref[:, OFF_CH:OFF_QKV]
    y = _shifted(cgh, tc_scr, 2, CONV_K, row_c) * cw_ref[0:1, :]
    y = y + _shifted(cgh, tc_scr, 1, CONV_K, row_c) * cw_ref[1:2, :]
    y = y + cgh * cw_ref[2:3, :]
    yc_ref[...] = gate_b * y
    last_c = cgh[tc - (CONV_K - 1):tc, :]

    row_d = lax.broadcasted_iota(jnp.int32, (tc, 3 * DN_WIDTH), 0)
    xq = pm_ref[:, OFF_QKV:OFF_Z]
    yq = _shifted(xq, td_scr, 3, DN_CONV_K, row_d) * dw_ref[0:1, :]
    yq = yq + _shifted(xq, td_scr, 2, DN_CONV_K, row_d) * dw_ref[1:2, :]
    yq = yq + _shifted(xq, td_scr, 1, DN_CONV_K, row_d) * dw_ref[2:3, :]
    yq = yq + xq * dw_ref[3:4, :]
    _qkv_finish(yq, qkv_ref, ())
    last_d = xq[tc - (DN_CONV_K - 1):tc, :]

    gb_ref[...] = _gates(ps_ref[...], hp_ref[...])

    tc_scr[8 - (CONV_K - 1):8, :] = last_c
    td_scr[8 - (DN_CONV_K - 1):8, :] = last_d

    @pl.when(t == pl.num_programs(1) - 1)
    def _():
        ncb_ref[0] = last_c
        ndb_ref[0] = last_d


def _pre_prompt(pm, ps, conv_w, dn_conv_w, hp, conv_buf, dn_buf, bsz, seq, tc):
    n = pm.shape[0]
    nt = seq // tc
    rows = lambda w: pl.BlockSpec((tc, w), lambda b, t: (b * nt + t, 0))
    full = lambda a: pl.BlockSpec(a.shape, lambda b, t: (0,) * a.ndim)
    per_b = lambda k, w: pl.BlockSpec((1, k, w), lambda b, t: (b, 0, 0))
    return pl.pallas_call(
        _pre_prompt_kernel,
        grid=(bsz, nt),
        in_specs=[rows(P_MAIN), rows(LANES), full(conv_w), full(dn_conv_w), full(hp),
                  per_b(CONV_K - 1, CONV_WIDTH), per_b(DN_CONV_K - 1, 3 * DN_WIDTH)],
        out_specs=[rows(CONV_WIDTH), rows(3 * DN_WIDTH), rows(LANES),
                   per_b(CONV_K - 1, CONV_WIDTH), per_b(DN_CONV_K - 1, 3 * DN_WIDTH)],
        out_shape=[
            jax.ShapeDtypeStruct((n, CONV_WIDTH), F32),
            jax.ShapeDtypeStruct((n, 3 * DN_WIDTH), F32),
            jax.ShapeDtypeStruct((n, LANES), F32),
            jax.ShapeDtypeStruct((bsz, CONV_K - 1, CONV_WIDTH), F32),
            jax.ShapeDtypeStruct((bsz, DN_CONV_K - 1, 3 * DN_WIDTH), F32),
        ],
        scratch_shapes=[pltpu.VMEM((8, CONV_WIDTH), F32), pltpu.VMEM((8, 3 * DN_WIDTH), F32)],
        compiler_params=_cparams("arbitrary", "arbitrary"),
        name="pre_prompt",
    )(pm, ps, conv_w, dn_conv_w, hp, conv_buf, dn_buf)


def _pre_sample_kernel(pm_ref, ps_ref, cw_ref, dw_ref, hp_ref, cb_ref, db_ref,
                       yc_ref, qkv_ref, gb_ref, ncb_ref, ndb_ref):
    steps = pm_ref.shape[0]
    cgh = [pm_ref[t, :, OFF_CC:OFF_CH] * pm_ref[t, :, OFF_CH:OFF_QKV] for t in range(steps)]
    ext = [cb_ref[i] for i in range(CONV_K - 1)] + cgh
    for t in range(steps):
        y = ext[t] * cw_ref[0:1, :]
        for i in range(1, CONV_K):
            y = y + ext[t + i] * cw_ref[i:i + 1, :]
        yc_ref[t] = pm_ref[t, :, 0:OFF_CC] * y
    for i in range(CONV_K - 1):
        ncb_ref[i] = ext[len(ext) - (CONV_K - 1) + i]

    xq = [pm_ref[t, :, OFF_QKV:OFF_Z] for t in range(steps)]
    extq = [db_ref[i] for i in range(DN_CONV_K - 1)] + xq
    for t in range(steps):
        y = extq[t] * dw_ref[0:1, :]
        for i in range(1, DN_CONV_K):
            y = y + extq[t + i] * dw_ref[i:i + 1, :]
        _qkv_finish(y, qkv_ref, (t,))
        gb_ref[t] = _gates(ps_ref[t], hp_ref[...])
    for i in range(DN_CONV_K - 1):
        ndb_ref[i] = extq[len(extq) - (DN_CONV_K - 1) + i]


def _pre_sample(pm, ps, conv_w, dn_conv_w, hp, conv_buf_tm, dn_buf_tm, bsz, steps, bt):
    slab = lambda k, w: pl.BlockSpec((k, bt, w), lambda i: (0, i, 0))
    full = lambda a: pl.BlockSpec(a.shape, lambda i: (0,) * a.ndim)
    return pl.pallas_call(
        _pre_sample_kernel,
        grid=(bsz // bt,),
        in_specs=[slab(steps, P_MAIN), slab(steps, LANES), full(conv_w), full(dn_conv_w), full(hp),
                  slab(CONV_K - 1, CONV_WIDTH), slab(DN_CONV_K - 1, 3 * DN_WIDTH)],
        out_specs=[slab(steps, CONV_WIDTH), slab(steps, 3 * DN_WIDTH), slab(steps, LANES),
                   slab(CONV_K - 1, CONV_WIDTH), slab(DN_CONV_K - 1, 3 * DN_WIDTH)],
        out_shape=[
            jax.ShapeDtypeStruct((steps, bsz, CONV_WIDTH), F32),
            jax.ShapeDtypeStruct((steps, bsz, 3 * DN_WIDTH), F32),
            jax.ShapeDtypeStruct((steps, bsz, LANES), F32),
            jax.ShapeDtypeStruct((CONV_K - 1, bsz, CONV_WIDTH), F32),
            jax.ShapeDtypeStruct((DN_CONV_K - 1, bsz, 3 * DN_WIDTH), F32),
        ],
        compiler_params=_cparams("arbitrary"),
        name="pre_sample",
    )(pm.reshape(steps, bsz, P_MAIN), ps.reshape(steps, bsz, LANES), conv_w, dn_conv_w, hp,
      conv_buf_tm, dn_buf_tm)


def _cumsum_rows(x, c):
    row = lax.broadcasted_iota(jnp.int32, x.shape, 0)
    s = 1
    while s < c:
        x = x + jnp.where(row >= s, pltpu.roll(x, s, 0), 0.0)
        s *= 2
    return x


def _unit_lower_solve(m, rhs, c, small, mode):
    if small:
        sol = rhs
        for j in range(c - 1):
            sol = sol - m[:, j:j + 1] * sol[j:j + 1, :]
        return sol
    p = -m
    x = rhs + _mm(p, rhs, mode=mode)
    for _ in range(int(math.log2(c)) - 1):
        p = _mm(p, p, mode=mode)
        x = x + _mm(p, x, mode=mode)
    return x


def _gdn_chunk(qkv, gb, s_heads, *, c, small, mode_solve, mode_gram, mode_state):
    row = lax.broadcasted_iota(jnp.int32, (c, c), 0)
    col = lax.broadcasted_iota(jnp.int32, (c, c), 1)
    gc = _cumsum_rows(gb, c)
    outs, s_new = [], []
    for h in range(DN_HEADS):
        g_b = jnp.broadcast_to(gc[:, G_LANE + h:G_LANE + h + 1], (c, DN_HEAD_DIM))
        beta_b = jnp.broadcast_to(gb[:, BETA_LANE + h:BETA_LANE + h + 1], (c, DN_HEAD_DIM))
        g_cc = g_b[:, :c]
        g_row = jnp.sum(jnp.where(row == col, g_cc, 0.0), axis=0, keepdims=True)
        gamma = jnp.exp(jnp.where(row >= col, g_cc - g_row, -jnp.inf))
        q = qkv[:, h * DN_HEAD_DIM:(h + 1) * DN_HEAD_DIM]
        k = qkv[:, DN_WIDTH + h * DN_HEAD_DIM:DN_WIDTH + (h + 1) * DN_HEAD_DIM]
        v = qkv[:, 2 * DN_WIDTH + h * DN_HEAD_DIM:2 * DN_WIDTH + (h + 1) * DN_HEAD_DIM]
        kbeta = k * beta_b
        exp_g = jnp.exp(g_b)
        m = jnp.where(row > col, _mm(kbeta, k, _NT, mode_gram) * gamma, 0.0)
        rhs = jnp.concatenate([kbeta * exp_g, v * beta_b], axis=-1)
        sol = _unit_lower_solve(m, rhs, c, small, mode_solve)
        w, u = sol[:, :DN_HEAD_DIM], sol[:, DN_HEAD_DIM:]
        attn = _mm(q, k, _NT, mode_gram) * gamma
        g_last = g_b[c - 1:c, :]
        k_dec = k * jnp.exp(g_last - g_b)
        s = s_heads[h]
        v_new = u - _mm(w, s, mode=mode_state)
        outs.append(_mm(q * exp_g, s, mode=mode_state) + _mm(attn, v_new, mode=mode_state))
        s_new.append(s * jnp.exp(g_last) + _mm(k_dec, v_new, _TN, mode_state))
    return outs, s_new


_PROMPT_MODES = dict(mode_solve="x3", mode_gram="x3", mode_state="bf16")
_SAMPLE_MODES = dict(mode_solve="f32", mode_gram="f32", mode_state="f32")


def _delta_prompt_kernel(qkv_ref, gb_ref, s0_ref, o_ref, sn_ref, s_scr):
    t = pl.program_id(1)
    tc = qkv_ref.shape[0]

    @pl.when(t == 0)
    def _():
        s_scr[...] = s0_ref[0]

    s_heads = [s_scr[h] for h in range(DN_HEADS)]
    for ci in range(tc // DN_CHUNK):
        r0 = ci * DN_CHUNK
        outs, s_heads = _gdn_chunk(qkv_ref[r0:r0 + DN_CHUNK, :], gb_ref[r0:r0 + DN_CHUNK, :], s_heads,
                                   c=DN_CHUNK, small=False, **_PROMPT_MODES)
        for h in range(DN_HEADS):
            o_ref[r0:r0 + DN_CHUNK, h * DN_HEAD_DIM:(h + 1) * DN_HEAD_DIM] = outs[h]
    for h in range(DN_HEADS):
        s_scr[h] = s_heads[h]

    @pl.when(t == pl.num_programs(1) - 1)
    def _():
        for h in range(DN_HEADS):
            sn_ref[0, h] = s_heads[h]


def _delta_prompt(qkv, gb, s0, bsz, seq, tc):
    n = qkv.shape[0]
    nt = seq // tc
    rows = lambda w: pl.BlockSpec((tc, w), lambda b, t: (b * nt + t, 0))
    state = pl.BlockSpec((1, DN_HEADS, DN_HEAD_DIM, DN_HEAD_DIM), lambda b, t: (b, 0, 0, 0))
    return pl.pallas_call(
        _delta_prompt_kernel,
        grid=(bsz, nt),
        in_specs=[rows(3 * DN_WIDTH), rows(LANES), state],
        out_specs=[rows(DN_WIDTH), state],
        out_shape=[
            jax.ShapeDtypeStruct((n, DN_WIDTH), F32),
            jax.ShapeDtypeStruct((bsz, DN_HEADS, DN_HEAD_DIM, DN_HEAD_DIM), F32),
        ],
        scratch_shapes=[pltpu.VMEM((DN_HEADS, DN_HEAD_DIM, DN_HEAD_DIM), F32)],
        compiler_params=_cparams("arbitrary", "arbitrary"),
        name="delta_prompt",
    )(qkv, gb, s0)


def _delta_sample_kernel(qkv_ref, gb_ref, s0_ref, o_ref, sn_ref):
    for b in range(qkv_ref.shape[0]):
        s_heads = [s0_ref[b, h] for h in range(DN_HEADS)]
        outs, s_heads = _gdn_chunk(qkv_ref[b], gb_ref[b], s_heads, c=SAMPLE_CHUNK, small=True,
                                   **_SAMPLE_MODES)
        for h in range(DN_HEADS):
            o_ref[b, :, h * DN_HEAD_DIM:(h + 1) * DN_HEAD_DIM] = outs[h]
            sn_ref[b, h] = s_heads[h]


def _delta_sample(qkv_b, gb_b, s0, bt):
    bsz = qkv_b.shape[0]
    blk = lambda w: pl.BlockSpec((bt, SAMPLE_CHUNK, w), lambda i: (i, 0, 0))
    state = pl.BlockSpec((bt, DN_HEADS, DN_HEAD_DIM, DN_HEAD_DIM), lambda i: (i, 0, 0, 0))
    return pl.pallas_call(
        _delta_sample_kernel,
        grid=(bsz // bt,),
        in_specs=[blk(3 * DN_WIDTH), blk(LANES), state],
        out_specs=[blk(DN_WIDTH), state],
        out_shape=[
            jax.ShapeDtypeStruct((bsz, SAMPLE_CHUNK, DN_WIDTH), F32),
            jax.ShapeDtypeStruct((bsz, DN_HEADS, DN_HEAD_DIM, DN_HEAD_DIM), F32),
        ],
        compiler_params=_cparams("arbitrary"),
        name="delta_sample",
    )(qkv_b, gb_b, s0)


def _outproj_kernel(yc_ref, o_ref, z_ref, x_ref, g1_ref, nw_ref, w_ref, lg_ref, lb_ref, out_ref):
    o = o_ref[...]
    parts = []
    for h in range(DN_HEADS):
        oh = o[:, h * DN_HEAD_DIM:(h + 1) * DN_HEAD_DIM]
        parts.append(oh * lax.rsqrt(jnp.mean(oh * oh, -1, keepdims=True) + RMS_EPS) * nw_ref[...])
    og = jnp.concatenate(parts, axis=-1) * _silu(z_ref[...])
    mixed = jnp.concatenate([yc_ref[...], og], axis=-1).astype(BF16)
    m = jnp.dot(mixed, w_ref[...], preferred_element_type=F32)
    r = DEEPNORM_ALPHA * x_ref[...] + (1.0 + g1_ref[0]) * m
    out_ref[...] = _layer_norm(r, lg_ref[...], lb_ref[...])


def _outproj(yc, o, pm, x, mod3, tpg, norm_w, w_out, ln_g, ln_b, tm):
    n = x.shape[0]
    vec = lambda w: pl.BlockSpec((1, w), lambda i: (0, 0))
    return pl.pallas_call(
        _outproj_kernel,
        grid=(n // tm,),
        in_specs=[
            pl.BlockSpec((tm, CONV_WIDTH), lambda i: (i, 0)),
            pl.BlockSpec((tm, DN_WIDTH), lambda i: (i, 0)),
            pl.BlockSpec((tm, DN_WIDTH), lambda i: (i, OFF_Z // DN_WIDTH)),
            pl.BlockSpec((tm, D_MODEL), lambda i: (i, 0)),
            _mod_spec(mod3, 2, tpg),
            vec(DN_HEAD_DIM),
            pl.BlockSpec((D_MODEL, D_MODEL), lambda i: (0, 0)),
            vec(D_MODEL), vec(D_MODEL),
        ],
        out_specs=pl.BlockSpec((tm, D_MODEL), lambda i: (i, 0)),
        out_shape=jax.ShapeDtypeStruct((n, D_MODEL), F32),
        compiler_params=_cparams("arbitrary"),
        name="outproj",
    )(yc, o, pm, x, mod3, norm_w, w_out, ln_g, ln_b)


def _router_kernel(x_ref, sc_ref, sh_ref, wr_ref, br_ref, gates_ref):
    u = x_ref[...] * (1.0 + sc_ref[0]) + sh_ref[0]
    logits = _mm(u, wr_ref[...], mode="x3") + br_ref[...]
    lane = lax.broadcasted_iota(jnp.int32, logits.shape, 1)
    valid = lane < N_EXPERTS
    logits = jnp.where(valid, logits, -jnp.inf)
    ex = jnp.exp(logits - jnp.max(logits, -1, keepdims=True))
    probs = jnp.where(valid, ex / jnp.sum(ex, -1, keepdims=True), -1.0)
    p1 = jnp.max(probs, -1, keepdims=True)
    i1 = jnp.min(jnp.where(probs == p1, lane, LANES), -1, keepdims=True)
    rest = jnp.where(lane == i1, -1.0, probs)
    p2 = jnp.max(rest, -1, keepdims=True)
    i2 = jnp.min(jnp.where(rest == p2, lane, LANES), -1, keepdims=True)
    tot = p1 + p2
    gates_ref[...] = jnp.where(lane == i1, p1 / tot, 0.0) + jnp.where(lane == i2, p2 / tot, 0.0)


def _router(x, mod3, tpg, w_router, b_router, tm):
    n = x.shape[0]
    return pl.pallas_call(
        _router_kernel,
        grid=(n // tm,),
        in_specs=[
            pl.BlockSpec((tm, D_MODEL), lambda i: (i, 0)),
            _mod_spec(mod3, 4, tpg),
            _mod_spec(mod3, 3, tpg),
            pl.BlockSpec((D_MODEL, LANES), lambda i: (0, 0)),
            pl.BlockSpec((1, LANES), lambda i: (0, 0)),
        ],
        out_specs=pl.BlockSpec((tm, LANES), lambda i: (i, 0)),
        out_shape=jax.ShapeDtypeStruct((n, LANES), F32),
        compiler_params=_cparams("arbitrary"),
        name="router",
    )(x, mod3, mod3, w_router, b_router)


def _ffn_kernel(x_ref, sc_ref, sh_ref, g2_ref, gates_ref, wg_ref, wu_ref, wd_ref, lg_ref, lb_ref,
                out_ref, u_scr, acc_scr, *, gated):
    e = pl.program_id(1)
    f = pl.program_id(2)
    first = jnp.logical_and(e == 0, f == 0)
    last = jnp.logical_and(e == pl.num_programs(1) - 1, f == pl.num_programs(2) - 1)

    @pl.when(first)
    def _():
        u_scr[...] = (x_ref[...] * (1.0 + sc_ref[0]) + sh_ref[0]).astype(BF16)
        acc_scr[...] = jnp.zeros_like(acc_scr)

    u = u_scr[...]
    h = _silu(jnp.dot(u, wg_ref[0], preferred_element_type=F32)) * jnp.dot(
        u, wu_ref[0], preferred_element_type=F32)
    y = jnp.dot(h.astype(BF16), wd_ref[0], preferred_element_type=F32)
    if gated:
        lane = lax.broadcasted_iota(jnp.int32, gates_ref.shape, 1)
        y = jnp.sum(jnp.where(lane == e, gates_ref[...], 0.0), -1, keepdims=True) * y
    acc_scr[...] += y

    @pl.when(last)
    def _():
        r = DEEPNORM_ALPHA * x_ref[...] + (1.0 + g2_ref[0]) * acc_scr[...]
        out_ref[...] = _layer_norm(r, lg_ref[...], lb_ref[...])


def _ffn(x, mod3, tpg, gates, w_gate, w_up, w_down, ln_g, ln_b, tm, tf, gated):
    n = x.shape[0]
    n_e, _, d_ff = w_gate.shape
    vec = lambda w: pl.BlockSpec((1, w), lambda i, e, f: (0, 0))
    return pl.pallas_call(
        functools.partial(_ffn_kernel, gated=gated),
        grid=(n // tm, n_e, d_ff // tf),
        in_specs=[
            pl.BlockSpec((tm, D_MODEL), lambda i, e, f: (i, 0)),
            _mod_spec(mod3, 4, tpg),
            _mod_spec(mod3, 3, tpg),
            _mod_spec(mod3, 5, tpg),
            pl.BlockSpec((tm, LANES), lambda i, e, f: (i, 0)),
            pl.BlockSpec((1, D_MODEL, tf), lambda i, e, f: (e, 0, f)),
            pl.BlockSpec((1, D_MODEL, tf), lambda i, e, f: (e, 0, f)),
            pl.BlockSpec((1, tf, D_MODEL), lambda i, e, f: (e, f, 0)),
            vec(D_MODEL), vec(D_MODEL),
        ],
        out_specs=pl.BlockSpec((tm, D_MODEL), lambda i, e, f: (i, 0)),
        out_shape=jax.ShapeDtypeStruct((n, D_MODEL), F32),
        scratch_shapes=[pltpu.VMEM((tm, D_MODEL), BF16), pltpu.VMEM((tm, D_MODEL), F32)],
        compiler_params=_cparams("arbitrary", "arbitrary", "arbitrary"),
        name="ffn_gated" if gated else "ffn_dense",
    )(x, mod3, mod3, mod3, gates, w_gate, w_up, w_down, ln_g, ln_b)


def _trunk(x, mod3s, tpg, prm, tm, mixer_fn):
    n = x.shape[0]
    states = []
    for l in range(DEPTH):
        mod3 = mod3s[l]
        pm, ps = _inproj(x, mod3, tpg, prm["w_main"][l], prm["w_small"][l], tm)
        yc, o, st = mixer_fn(l, pm, ps)
        states.append(st)
        x = _outproj(yc, o, pm, x, mod3, tpg, prm["norm_w"][l], prm["w_out"][l],
                     prm["ln_g"][l, 0:1], prm["ln_b"][l, 0:1], tm)
        j = l // 2
        if l % 2 == 0:
            ones = jnp.ones((n, LANES), F32)
            x = _ffn(x, mod3, tpg, ones, prm["w_ff_gate"][j:j + 1], prm["w_ff_up"][j:j + 1],
                     prm["w_ff_down"][j:j + 1], prm["ln_g"][l, 1:2], prm["ln_b"][l, 1:2],
                     tm, prm["w_ff_gate"].shape[-1] // 2, False)
        else:
            gates = _router(x, mod3, tpg, prm["w_router"][j], prm["b_router"][j], tm)
            x = _ffn(x, mod3, tpg, gates, prm["w_exp_gate"][j], prm["w_exp_up"][j],
                     prm["w_exp_down"][j], prm["ln_g"][l, 1:2], prm["ln_b"][l, 1:2],
                     tm, prm["w_exp_gate"].shape[-1], True)
    return x, states


def kernel(x_prompt, x_sample, state_conv, state_dn_conv, state_dn, c_prompt, c_sample, w_in, conv_w,
           dn_conv_w, a_log, dt_bias, dn_norm_w, w_out, w_ada, b_ada, ln_g, ln_b, w_ff_gate, w_ff_up,
           w_ff_down, w_router, b_router, w_exp_gate, w_exp_up, w_exp_down):
    bp, seq, d = x_prompt.shape
    bs, steps, _ = x_sample.shape
    pad_lanes = LANES - N_EXPERTS
    hp = jnp.zeros((DEPTH, 2, LANES), F32)
    hp = hp.at[:, 0, G_LANE:].set(a_log).at[:, 1, G_LANE:].set(dt_bias)
    prm = {
        "w_main": w_in[:, :, :P_MAIN].astype(BF16),
        "w_small": w_in[:, :, w_in.shape[-1] - LANES:],
        "norm_w": dn_norm_w.reshape(DEPTH, 1, DN_HEAD_DIM),
        "w_out": w_out.astype(BF16),
        "ln_g": ln_g, "ln_b": ln_b,
        "w_ff_gate": w_ff_gate.astype(BF16), "w_ff_up": w_ff_up.astype(BF16),
        "w_ff_down": w_ff_down.astype(BF16),
        "w_router": jnp.pad(w_router, ((0, 0), (0, 0), (0, pad_lanes))),
        "b_router": jnp.pad(b_router, ((0, 0), (0, pad_lanes)))[:, None, :],
        "w_exp_gate": w_exp_gate.astype(BF16), "w_exp_up": w_exp_up.astype(BF16),
        "w_exp_down": w_exp_down.astype(BF16),
    }

    mod = _ada(jnp.concatenate([c_prompt, c_sample], axis=0), w_ada, b_ada)
    mod_p = [mod[l, :bp].reshape(bp, 1, 6 * d) for l in range(DEPTH)]
    mod_s = [mod[l, bp:].reshape(1, bs, 6 * d) for l in range(DEPTH)]

    tm_p, tc_pre, tc_delta = 512, 256, 256
    zc = jnp.zeros((bp, CONV_K - 1, CONV_WIDTH), F32)
    zd = jnp.zeros((bp, DN_CONV_K - 1, 3 * DN_WIDTH), F32)
    zs = jnp.zeros((bp, DN_HEADS, DN_HEAD_DIM, DN_HEAD_DIM), F32)

    def mixer_prompt(l, pm, ps):
        yc, qkv, gb, ncb, ndb = _pre_prompt(pm, ps, conv_w[l], dn_conv_w[l], hp[l], zc, zd, bp, seq, tc_pre)
        o, s_new = _delta_prompt(qkv, gb, zs, bp, seq, tc_delta)
        return yc, o, (ncb, ndb, s_new)

    y_p, st_p = _trunk(x_prompt.reshape(bp * seq, d), mod_p, seq // tm_p, prm, tm_p, mixer_prompt)

    bt_pre, bt_delta = 32, 8

    def mixer_sample(l, pm, ps):
        cb_tm = jnp.transpose(state_conv[l], (1, 0, 2))
        db_tm = jnp.transpose(state_dn_conv[l], (1, 0, 2))
        yc, qkv, gb, ncb, ndb = _pre_sample(pm, ps, conv_w[l], dn_conv_w[l], hp[l], cb_tm, db_tm,
                                            bs, steps, bt_pre)
        to_b = lambda a: jnp.pad(jnp.transpose(a, (1, 0, 2)), ((0, 0), (0, SAMPLE_CHUNK - steps), (0, 0)))
        o_b, s_new = _delta_sample(to_b(qkv), to_b(gb), state_dn[l], bt_delta)
        o = jnp.transpose(o_b[:, :steps], (1, 0, 2)).reshape(steps * bs, DN_WIDTH)
        return (yc.reshape(steps * bs, CONV_WIDTH), o,
                (jnp.transpose(ncb, (1, 0, 2)), jnp.transpose(ndb, (1, 0, 2)), s_new))

    x_s = jnp.transpose(x_sample, (1, 0, 2)).reshape(steps * bs, d)
    y_s, st_s = _trunk(x_s, mod_s, steps, prm, bs, mixer_sample)
    y_s = jnp.transpose(y_s.reshape(steps, bs, d), (1, 0, 2))

    stack = lambda sts, i: jnp.stack([s[i] for s in sts])
    return (y_p.reshape(bp, seq, d), y_s,
            stack(st_p, 0), stack(st_p, 1), stack(st_p, 2),
            stack(st_s, 0), stack(st_s, 1), stack(st_s, 2))
```

```python
import functools
import math

import jax
import jax.numpy as jnp
from jax import lax
from jax.experimental import pallas as pl
from jax.experimental.pallas import tpu as pltpu

F32 = jnp.float32
BF16 = jnp.bfloat16

D_MODEL = 1024
DEPTH = 2
CONV_WIDTH = 512
CONV_K = 3
DN_HEADS = 4
DN_HEAD_DIM = 128
DN_WIDTH = DN_HEADS * DN_HEAD_DIM
DN_CONV_K = 4
DN_CHUNK = 64
N_EXPERTS = 8
DEEPNORM_ALPHA = (2.0 * DEPTH) ** 0.25
LN_EPS = 1e-5
RMS_EPS = 1e-6

OFF_CC = CONV_WIDTH
OFF_CH = 2 * CONV_WIDTH
OFF_QKV = 3 * CONV_WIDTH
OFF_Z = OFF_QKV + 3 * DN_WIDTH
P_MAIN = OFF_Z + DN_WIDTH
LANES = 128
BETA_LANE = LANES - 2 * DN_HEADS
G_LANE = LANES - DN_HEADS
SAMPLE_CHUNK = 8

VMEM_LIMIT = 56 * 1024 * 1024


def _cparams(*sem):
    return pltpu.CompilerParams(dimension_semantics=sem, vmem_limit_bytes=VMEM_LIMIT)


def _silu(x):
    return x * jax.nn.sigmoid(x)


def _layer_norm(r, g, b):
    mu = jnp.mean(r, -1, keepdims=True)
    xc = r - mu
    var = jnp.mean(xc * xc, -1, keepdims=True)
    return xc * lax.rsqrt(var + LN_EPS) * g + b


def _split_bf16(a):
    hi = a.astype(BF16)
    lo = (a - hi.astype(F32)).astype(BF16)
    return hi, lo


_NN = (((1,), (0,)), ((), ()))
_BNN = (((2,), (1,)), ((0,), (0,)))
_BNT = (((2,), (2,)), ((0,), (0,)))
_BTN = (((1,), (1,)), ((0,), (0,)))


def _mm(a, b, dims=_NN, mode="bf16"):
    if mode == "f32":
        return lax.dot_general(a, b, dims, precision=lax.Precision.HIGHEST, preferred_element_type=F32)
    if mode == "bf16":
        return lax.dot_general(a.astype(BF16), b.astype(BF16), dims, preferred_element_type=F32)
    ah, al = _split_bf16(a)
    bh, bl = _split_bf16(b)
    d = functools.partial(lax.dot_general, dimension_numbers=dims, preferred_element_type=F32)
    return d(ah, bh) + (d(ah, bl) + d(al, bh))


def _ada_kernel(c_ref, w_ref, b_ref, o_ref):
    s = _silu(c_ref[...]).astype(BF16)
    o_ref[0] = jnp.dot(s, w_ref[0].astype(BF16), preferred_element_type=F32) + b_ref[0]


def _ada(c_all, w_ada, b_ada):
    rows = c_all.shape[0]
    tn = 1536
    return pl.pallas_call(
        _ada_kernel,
        grid=(DEPTH, 6 * D_MODEL // tn),
        in_specs=[
            pl.BlockSpec((rows, D_MODEL), lambda l, j: (0, 0)),
            pl.BlockSpec((1, D_MODEL, tn), lambda l, j: (l, 0, j)),
            pl.BlockSpec((1, 1, tn), lambda l, j: (l, 0, j)),
        ],
        out_specs=pl.BlockSpec((1, rows, tn), lambda l, j: (l, 0, j)),
        out_shape=jax.ShapeDtypeStruct((DEPTH, rows, 6 * D_MODEL), F32),
        compiler_params=_cparams("arbitrary", "arbitrary"),
        name="ada",
    )(c_all, w_ada, b_ada.reshape(DEPTH, 1, 6 * D_MODEL))


def _mod_spec(mod3, chunk, tiles_per_group):
    r = mod3.shape[1]
    return pl.BlockSpec((1, r, D_MODEL), lambda i, *_: (i // tiles_per_group, 0, chunk))


def _inproj_kernel(x_ref, sc_ref, sh_ref, wm_ref, ws_ref, pm_ref, ps_ref):
    u = x_ref[...] * (1.0 + sc_ref[0]) + sh_ref[0]
    pm_ref[...] = jnp.dot(u.astype(BF16), wm_ref[...], preferred_element_type=F32)
    ps_ref[...] = _mm(u, ws_ref[...], mode="x3")


def _inproj(x, mod3, tpg, w_main, w_small, tm):
    n = x.shape[0]
    return pl.pallas_call(
        _inproj_kernel,
        grid=(n // tm,),
        in_specs=[
            pl.BlockSpec((tm, D_MODEL), lambda i: (i, 0)),
            _mod_spec(mod3, 1, tpg),
            _mod_spec(mod3, 0, tpg),
            pl.BlockSpec((D_MODEL, P_MAIN), lambda i: (0, 0)),
            pl.BlockSpec((D_MODEL, LANES), lambda i: (0, 0)),
        ],
        out_specs=[
            pl.BlockSpec((tm, P_MAIN), lambda i: (i, 0)),
            pl.BlockSpec((tm, LANES), lambda i: (i, 0)),
        ],
        out_shape=[
            jax.ShapeDtypeStruct((n, P_MAIN), F32),
            jax.ShapeDtypeStruct((n, LANES), F32),
        ],
        compiler_params=_cparams("arbitrary"),
        name="inproj",
    )(x, mod3, mod3, w_main, w_small)


def _gates(ps, hp):
    lane = lax.broadcasted_iota(jnp.int32, ps.shape, 1)
    beta = jax.nn.sigmoid(ps)
    g = -jnp.exp(hp[0:1, :]) * jax.nn.softplus(ps + hp[1:2, :])
    return jnp.where(lane >= G_LANE, g, jnp.where(lane >= BETA_LANE, beta, 0.0))


def _qkv_finish(y, out_ref, idx):
    y = _silu(y)
    for h in range(3 * DN_HEADS):
        seg = y[:, h * DN_HEAD_DIM:(h + 1) * DN_HEAD_DIM]
        if h < 2 * DN_HEADS:
            seg = seg * lax.rsqrt(jnp.sum(seg * seg, -1, keepdims=True) + RMS_EPS)
            if h < DN_HEADS:
                seg = seg * (DN_HEAD_DIM ** -0.5)
        out_ref[idx + (slice(None), slice(h * DN_HEAD_DIM, (h + 1) * DN_HEAD_DIM))] = seg


def _shifted(x, tail_ref, i, row):
    s = pltpu.roll(x, i, 0)
    for r in range(i):
        s = jnp.where(row == r, tail_ref[8 - i + r:8 - i + r + 1, :], s)
    return s


def _pre_prompt_kernel(pm_ref, ps_ref, cw_ref, dw_ref, hp_ref, cb_ref, db_ref,
                       yc_ref, qkv_ref, gb_ref, ncb_ref, ndb_ref, tc_scr, td_scr):
    t = pl.program_id(1)
    tc = pm_ref.shape[0]

    @pl.when(t == 0)
    def _():
        tc_scr[8 - (CONV_K - 1):8, :] = cb_ref[0]
        td_scr[8 - (DN_CONV_K - 1):8, :] = db_ref[0]

    row_c = lax.broadcasted_iota(jnp.int32, (tc, CONV_WIDTH), 0)
    gate_b = pm_ref[:, 0:OFF_CC]
    cgh = pm_ref[:, OFF_CC:OFF_CH] * pm_ref[:, OFF_CH:OFF_QKV]
    y = _shifted(cgh, tc_scr, 2, row_c) * cw_ref[0:1, :]
    y = y + _shifted(cgh, tc_scr, 1, row_c) * cw_ref[1:2, :]
    y = y + cgh * cw_ref[2:3, :]
    yc_ref[...] = gate_b * y
    last_c = cgh[tc - (CONV_K - 1):tc, :]

    row_d = lax.broadcasted_iota(jnp.int32, (tc, 3 * DN_WIDTH), 0)
    xq = pm_ref[:, OFF_QKV:OFF_Z]
    yq = _shifted(xq, td_scr, 3, row_d) * dw_ref[0:1, :]
    yq = yq + _shifted(xq, td_scr, 2, row_d) * dw_ref[1:2, :]
    yq = yq + _shifted(xq, td_scr, 1, row_d) * dw_ref[2:3, :]
    yq = yq + xq * dw_ref[3:4, :]
    _qkv_finish(yq, qkv_ref, ())
    last_d = xq[tc - (DN_CONV_K - 1):tc, :]

    gb_ref[...] = _gates(ps_ref[...], hp_ref[...])

    tc_scr[8 - (CONV_K - 1):8, :] = last_c
    td_scr[8 - (DN_CONV_K - 1):8, :] = last_d

    @pl.when(t == pl.num_programs(1) - 1)
    def _():
        ncb_ref[0] = last_c
        ndb_ref[0] = last_d


def _pre_prompt(pm, ps, conv_w, dn_conv_w, hp, conv_buf, dn_buf, bsz, seq, tc):
    n = pm.shape[0]
    nt = seq // tc
    rows = lambda w: pl.BlockSpec((tc, w), lambda b, t: (b * nt + t, 0))
    full = lambda a: pl.BlockSpec(a.shape, lambda b, t: (0,) * a.ndim)
    per_b = lambda k, w: pl.BlockSpec((1, k, w), lambda b, t: (b, 0, 0))
    return pl.pallas_call(
        _pre_prompt_kernel,
        grid=(bsz, nt),
        in_specs=[rows(P_MAIN), rows(LANES), full(conv_w), full(dn_conv_w), full(hp),
                  per_b(CONV_K - 1, CONV_WIDTH), per_b(DN_CONV_K - 1, 3 * DN_WIDTH)],
        out_specs=[rows(CONV_WIDTH), rows(3 * DN_WIDTH), rows(LANES),
                   per_b(CONV_K - 1, CONV_WIDTH), per_b(DN_CONV_K - 1, 3 * DN_WIDTH)],
        out_shape=[
            jax.ShapeDtypeStruct((n, CONV_WIDTH), F32),
            jax.ShapeDtypeStruct((n, 3 * DN_WIDTH), F32),
            jax.ShapeDtypeStruct((n, LANES), F32),
            jax.ShapeDtypeStruct((bsz, CONV_K - 1, CONV_WIDTH), F32),
            jax.ShapeDtypeStruct((bsz, DN_CONV_K - 1, 3 * DN_WIDTH), F32),
        ],
        scratch_shapes=[pltpu.VMEM((8, CONV_WIDTH), F32), pltpu.VMEM((8, 3 * DN_WIDTH), F32)],
        compiler_params=_cparams("arbitrary", "arbitrary"),
        name="pre_prompt",
    )(pm, ps, conv_w, dn_conv_w, hp, conv_buf, dn_buf)


def _pre_sample_kernel(pm_ref, ps_ref, cw_ref, dw_ref, hp_ref, cb_ref, db_ref,
                       yc_ref, qkv_ref, gb_ref, ncb_ref, ndb_ref):
    steps = pm_ref.shape[0]
    cgh = [pm_ref[t, :, OFF_CC:OFF_CH] * pm_ref[t, :, OFF_CH:OFF_QKV] for t in range(steps)]
    ext = [cb_ref[i] for i in range(CONV_K - 1)] + cgh
    for t in range(steps):
        y = ext[t] * cw_ref[0:1, :]
        for i in range(1, CONV_K):
            y = y + ext[t + i] * cw_ref[i:i + 1, :]
        yc_ref[t] = pm_ref[t, :, 0:OFF_CC] * y
    for i in range(CONV_K - 1):
        ncb_ref[i] = ext[len(ext) - (CONV_K - 1) + i]

    xq = [pm_ref[t, :, OFF_QKV:OFF_Z] for t in range(steps)]
    extq = [db_ref[i] for i in range(DN_CONV_K - 1)] + xq
    for t in range(steps):
        y = extq[t] * dw_ref[0:1, :]
        for i in range(1, DN_CONV_K):
            y = y + extq[t + i] * dw_ref[i:i + 1, :]
        _qkv_finish(y, qkv_ref, (t,))
        gb_ref[t] = _gates(ps_ref[t], hp_ref[...])
    for i in range(DN_CONV_K - 1):
        ndb_ref[i] = extq[len(extq) - (DN_CONV_K - 1) + i]


def _pre_sample(pm, ps, conv_w, dn_conv_w, hp, conv_buf_tm, dn_buf_tm, bsz, steps, bt):
    slab = lambda k, w: pl.BlockSpec((k, bt, w), lambda i: (0, i, 0))
    full = lambda a: pl.BlockSpec(a.shape, lambda i: (0,) * a.ndim)
    return pl.pallas_call(
        _pre_sample_kernel,
        grid=(bsz // bt,),
        in_specs=[slab(steps, P_MAIN), slab(steps, LANES), full(conv_w), full(dn_conv_w), full(hp),
                  slab(CONV_K - 1, CONV_WIDTH), slab(DN_CONV_K - 1, 3 * DN_WIDTH)],
        out_specs=[slab(steps, CONV_WIDTH), slab(steps, 3 * DN_WIDTH), slab(steps, LANES),
                   slab(CONV_K - 1, CONV_WIDTH), slab(DN_CONV_K - 1, 3 * DN_WIDTH)],
        out_shape=[
            jax.ShapeDtypeStruct((steps, bsz, CONV_WIDTH), F32),
            jax.ShapeDtypeStruct((steps, bsz, 3 * DN_WIDTH), F32),
            jax.ShapeDtypeStruct((steps, bsz, LANES), F32),
            jax.ShapeDtypeStruct((CONV_K - 1, bsz, CONV_WIDTH), F32),
            jax.ShapeDtypeStruct((DN_CONV_K - 1, bsz, 3 * DN_WIDTH), F32),
        ],
        compiler_params=_cparams("arbitrary"),
        name="pre_sample",
    )(pm.reshape(steps, bsz, P_MAIN), ps.reshape(steps, bsz, LANES), conv_w, dn_conv_w, hp,
      conv_buf_tm, dn_buf_tm)


def _cumsum_chunks(x, c):
    row = lax.broadcasted_iota(jnp.int32, x.shape, 0) & (c - 1)
    s = 1
    while s < c:
        x = x + jnp.where(row >= s, pltpu.roll(x, s, 0), 0.0)
        s *= 2
    return x


def _gdn_problems(qkv, gb, c):
    n = qkv.shape[0] // c
    gc = _cumsum_chunks(gb, c)
    q, k, v, g_b, beta_b = [], [], [], [], []
    for ci in range(n):
        rs = slice(ci * c, (ci + 1) * c)
        for h in range(DN_HEADS):
            ls = lambda base: slice(base + h * DN_HEAD_DIM, base + (h + 1) * DN_HEAD_DIM)
            q.append(qkv[rs, ls(0)])
            k.append(qkv[rs, ls(DN_WIDTH)])
            v.append(qkv[rs, ls(2 * DN_WIDTH)])
            g_b.append(jnp.broadcast_to(gc[rs, G_LANE + h:G_LANE + h + 1], (c, DN_HEAD_DIM)))
            beta_b.append(jnp.broadcast_to(gb[rs, BETA_LANE + h:BETA_LANE + h + 1], (c, DN_HEAD_DIM)))
    return tuple(jnp.stack(a) for a in (q, k, v, g_b, beta_b))


def _unit_lower_solve(m, rhs, c, small, mode):
    if small:
        sol = rhs
        for j in range(c - 1):
            sol = sol - m[:, :, j:j + 1] * sol[:, j:j + 1, :]
        return sol
    w = rhs.shape[-1]
    p, x = -m, rhs
    levels = int(math.log2(c))
    for lvl in range(levels):
        if lvl < levels - 1:
            y = _mm(p, jnp.concatenate([x, p], axis=-1), _BNN, mode)
            x = x + y[..., :w]
            p = y[..., w:]
        else:
            x = x + _mm(p, x, _BNN, mode)
    return x


def _gdn_intra(q, k, v, g_b, beta_b, *, c, small, mode_gram, mode_solve, mode_apply):
    n = q.shape[0]
    row = lax.broadcasted_iota(jnp.int32, (n, c, c), 1)
    col = lax.broadcasted_iota(jnp.int32, (n, c, c), 2)
    g_cc = g_b[:, :, :c]
    g_row = jnp.sum(jnp.where(row == col, g_cc, 0.0), axis=1, keepdims=True)
    gamma = jnp.exp(jnp.where(row >= col, g_cc - g_row, -jnp.inf))
    kbeta = k * beta_b
    exp_g = jnp.exp(g_b)
    gram = _mm(jnp.concatenate([kbeta, q], axis=1), k, _BNT, mode_gram)
    m = jnp.where(row > col, gram[:, :c] * gamma, 0.0)
    attn = gram[:, c:] * gamma
    rhs = jnp.concatenate([kbeta * exp_g, v * beta_b], axis=-1)
    sol = _unit_lower_solve(m, rhs, c, small, mode_solve)
    g_last = g_b[:, c - 1:c, :]
    k_dec = k * jnp.exp(g_last - g_b)
    kd = _mm(k_dec, sol, _BTN, mode_apply)
    at = _mm(attn, sol, _BNN, mode_apply)
    q_t = q * exp_g - at[..., :DN_HEAD_DIM]
    return kd[..., :DN_HEAD_DIM], kd[..., DN_HEAD_DIM:], q_t, at[..., DN_HEAD_DIM:], jnp.exp(g_last)


def _gdn_state_step(a_mat, b_mat, q_t, o_intra, d_last, s, mode):
    r = _mm(jnp.concatenate([a_mat, q_t], axis=1), s, _BNN, mode)
    s_new = s * d_last - r[:, :DN_HEAD_DIM] + b_mat
    return r[:, DN_HEAD_DIM:] + o_intra, s_new


_PROMPT_MODES = dict(mode_gram="x3", mode_solve="x3", mode_apply="bf16")
_PROMPT_STATE_MODE = "bf16"
_SAMPLE_MODES = dict(mode_gram="f32", mode_solve="f32", mode_apply="f32")
_SAMPLE_STATE_MODE = "f32"


def _delta_prompt_kernel(qkv_ref, gb_ref, s0_ref, o_ref, sn_ref, s_scr):
    t = pl.program_id(1)
    tc = qkv_ref.shape[0]

    @pl.when(t == 0)
    def _():
        s_scr[...] = s0_ref[0]

    probs = _gdn_problems(qkv_ref[...], gb_ref[...], DN_CHUNK)
    a_mat, b_mat, q_t, o_intra, d_last = _gdn_intra(*probs, c=DN_CHUNK, small=False, **_PROMPT_MODES)
    s = s_scr[...]
    for ci in range(tc // DN_CHUNK):
        ps = slice(ci * DN_HEADS, (ci + 1) * DN_HEADS)
        o, s = _gdn_state_step(a_mat[ps], b_mat[ps], q_t[ps], o_intra[ps], d_last[ps], s,
                               _PROMPT_STATE_MODE)
        for h in range(DN_HEADS):
            o_ref[ci * DN_CHUNK:(ci + 1) * DN_CHUNK, h * DN_HEAD_DIM:(h + 1) * DN_HEAD_DIM] = o[h]
    s_scr[...] = s

    @pl.when(t == pl.num_programs(1) - 1)
    def _():
        sn_ref[0] = s


def _delta_prompt(qkv, gb, s0, bsz, seq, tc):
    n = qkv.shape[0]
    nt = seq // tc
    rows = lambda w: pl.BlockSpec((tc, w), lambda b, t: (b * nt + t, 0))
    state = pl.BlockSpec((1, DN_HEADS, DN_HEAD_DIM, DN_HEAD_DIM), lambda b, t: (b, 0, 0, 0))
    return pl.pallas_call(
        _delta_prompt_kernel,
        grid=(bsz, nt),
        in_specs=[rows(3 * DN_WIDTH), rows(LANES), state],
        out_specs=[rows(DN_WIDTH), state],
        out_shape=[
            jax.ShapeDtypeStruct((n, DN_WIDTH), F32),
            jax.ShapeDtypeStruct((bsz, DN_HEADS, DN_HEAD_DIM, DN_HEAD_DIM), F32),
        ],
        scratch_shapes=[pltpu.VMEM((DN_HEADS, DN_HEAD_DIM, DN_HEAD_DIM), F32)],
        compiler_params=_cparams("arbitrary", "arbitrary"),
        name="delta_prompt",
    )(qkv, gb, s0)


def _delta_sample_kernel(qkv_ref, gb_ref, s0_ref, o_ref, sn_ref):
    bt = qkv_ref.shape[0]
    c = SAMPLE_CHUNK
    probs = _gdn_problems(qkv_ref[...].reshape(bt * c, 3 * DN_WIDTH), gb_ref[...].reshape(bt * c, LANES), c)
    a_mat, b_mat, q_t, o_intra, d_last = _gdn_intra(*probs, c=c, small=True, **_SAMPLE_MODES)
    s = s0_ref[...].reshape(bt * DN_HEADS, DN_HEAD_DIM, DN_HEAD_DIM)
    o, s = _gdn_state_step(a_mat, b_mat, q_t, o_intra, d_last, s, _SAMPLE_STATE_MODE)
    for b in range(bt):
        for h in range(DN_HEADS):
            o_ref[b, :, h * DN_HEAD_DIM:(h + 1) * DN_HEAD_DIM] = o[b * DN_HEADS + h]
    sn_ref[...] = s.reshape(bt, DN_HEADS, DN_HEAD_DIM, DN_HEAD_DIM)


def _delta_sample(qkv_b, gb_b, s0, bt):
    bsz = qkv_b.shape[0]
    blk = lambda w: pl.BlockSpec((bt, SAMPLE_CHUNK, w), lambda i: (i, 0, 0))
    state = pl.BlockSpec((bt, DN_HEADS, DN_HEAD_DIM, DN_HEAD_DIM), lambda i: (i, 0, 0, 0))
    return pl.pallas_call(
        _delta_sample_kernel,
        grid=(bsz // bt,),
        in_specs=[blk(3 * DN_WIDTH), blk(LANES), state],
        out_specs=[blk(DN_WIDTH), state],
        out_shape=[
            jax.ShapeDtypeStruct((bsz, SAMPLE_CHUNK, DN_WIDTH), F32),
            jax.ShapeDtypeStruct((bsz, DN_HEADS, DN_HEAD_DIM, DN_HEAD_DIM), F32),
        ],
        compiler_params=_cparams("arbitrary"),
        name="delta_sample",
    )(qkv_b, gb_b, s0)


def _outproj_kernel(yc_ref, o_ref, z_ref, x_ref, g1_ref, nw_ref, w_ref, lg_ref, lb_ref, out_ref):
    o = o_ref[...]
    parts = []
    for h in range(DN_HEADS):
        oh = o[:, h * DN_HEAD_DIM:(h + 1) * DN_HEAD_DIM]
        parts.append(oh * lax.rsqrt(jnp.mean(oh * oh, -1, keepdims=True) + RMS_EPS) * nw_ref[...])
    og = jnp.concatenate(parts, axis=-1) * _silu(z_ref[...])
    mixed = jnp.concatenate([yc_ref[...], og], axis=-1).astype(BF16)
    m = jnp.dot(mixed, w_ref[...], preferred_element_type=F32)
    r = DEEPNORM_ALPHA * x_ref[...] + (1.0 + g1_ref[0]) * m
    out_ref[...] = _layer_norm(r, lg_ref[...], lb_ref[...])


def _outproj(yc, o, pm, x, mod3, tpg, norm_w, w_out, ln_g, ln_b, tm):
    n = x.shape[0]
    vec = lambda w: pl.BlockSpec((1, w), lambda i: (0, 0))
    return pl.pallas_call(
        _outproj_kernel,
        grid=(n // tm,),
        in_specs=[
            pl.BlockSpec((tm, CONV_WIDTH), lambda i: (i, 0)),
            pl.BlockSpec((tm, DN_WIDTH), lambda i: (i, 0)),
            pl.BlockSpec((tm, DN_WIDTH), lambda i: (i, OFF_Z // DN_WIDTH)),
            pl.BlockSpec((tm, D_MODEL), lambda i: (i, 0)),
            _mod_spec(mod3, 2, tpg),
            vec(DN_HEAD_DIM),
            pl.BlockSpec((D_MODEL, D_MODEL), lambda i: (0, 0)),
            vec(D_MODEL), vec(D_MODEL),
        ],
        out_specs=pl.BlockSpec((tm, D_MODEL), lambda i: (i, 0)),
        out_shape=jax.ShapeDtypeStruct((n, D_MODEL), F32),
        compiler_params=_cparams("arbitrary"),
        name="outproj",
    )(yc, o, pm, x, mod3, norm_w, w_out, ln_g, ln_b)


def _router_kernel(x_ref, sc_ref, sh_ref, wr_ref, br_ref, gates_ref):
    u = x_ref[...] * (1.0 + sc_ref[0]) + sh_ref[0]
    logits = _mm(u, wr_ref[...], mode="x3") + br_ref[...]
    lane = lax.broadcasted_iota(jnp.int32, logits.shape, 1)
    valid = lane < N_EXPERTS
    logits = jnp.where(valid, logits, -jnp.inf)
    ex = jnp.exp(logits - jnp.max(logits, -1, keepdims=True))
    probs = jnp.where(valid, ex / jnp.sum(ex, -1, keepdims=True), -1.0)
    p1 = jnp.max(probs, -1, keepdims=True)
    i1 = jnp.min(jnp.where(probs == p1, lane, LANES), -1, keepdims=True)
    rest = jnp.where(lane == i1, -1.0, probs)
    p2 = jnp.max(rest, -1, keepdims=True)
    i2 = jnp.min(jnp.where(rest == p2, lane, LANES), -1, keepdims=True)
    tot = p1 + p2
    gates_ref[...] = jnp.where(lane == i1, p1 / tot, 0.0) + jnp.where(lane == i2, p2 / tot, 0.0)


def _router(x, mod3, tpg, w_router, b_router, tm):
    n = x.shape[0]
    return pl.pallas_call(
        _router_kernel,
        grid=(n // tm,),
        in_specs=[
            pl.BlockSpec((tm, D_MODEL), lambda i: (i, 0)),
            _mod_spec(mod3, 4, tpg),
            _mod_spec(mod3, 3, tpg),
            pl.BlockSpec((D_MODEL, LANES), lambda i: (0, 0)),
            pl.BlockSpec((1, LANES), lambda i: (0, 0)),
        ],
        out_specs=pl.BlockSpec((tm, LANES), lambda i: (i, 0)),
        out_shape=jax.ShapeDtypeStruct((n, LANES), F32),
        compiler_params=_cparams("arbitrary"),
        name="router",
    )(x, mod3, mod3, w_router, b_router)


def _ffn_kernel(x_ref, sc_ref, sh_ref, g2_ref, gates_ref, wg_ref, wu_ref, wd_ref, lg_ref, lb_ref,
                out_ref, u_scr, acc_scr, *, gated):
    e = pl.program_id(1)
    f = pl.program_id(2)
    first = jnp.logical_and(e == 0, f == 0)
    last = jnp.logical_and(e == pl.num_programs(1) - 1, f == pl.num_programs(2) - 1)

    @pl.when(first)
    def _():
        u_scr[...] = (x_ref[...] * (1.0 + sc_ref[0]) + sh_ref[0]).astype(BF16)
        acc_scr[...] = jnp.zeros_like(acc_scr)

    u = u_scr[...]
    h = _silu(jnp.dot(u, wg_ref[0], preferred_element_type=F32)) * jnp.dot(
        u, wu_ref[0], preferred_element_type=F32)
    y = jnp.dot(h.astype(BF16), wd_ref[0], preferred_element_type=F32)
    if gated:
        lane = lax.broadcasted_iota(jnp.int32, gates_ref.shape, 1)
        y = jnp.sum(jnp.where(lane == e, gates_ref[...], 0.0), -1, keepdims=True) * y
    acc_scr[...] += y

    @pl.when(last)
    def _():
        r = DEEPNORM_ALPHA * x_ref[...] + (1.0 + g2_ref[0]) * acc_scr[...]
        out_ref[...] = _layer_norm(r, lg_ref[...], lb_ref[...])


def _ffn(x, mod3, tpg, gates, w_gate, w_up, w_down, ln_g, ln_b, tm, tf, gated):
    n = x.shape[0]
    n_e, _, d_ff = w_gate.shape
    vec = lambda w: pl.BlockSpec((1, w), lambda i, e, f: (0, 0))
    return pl.pallas_call(
        functools.partial(_ffn_kernel, gated=gated),
        grid=(n // tm, n_e, d_ff // tf),
        in_specs=[
            pl.BlockSpec((tm, D_MODEL), lambda i, e, f: (i, 0)),
            _mod_spec(mod3, 4, tpg),
            _mod_spec(mod3, 3, tpg),
            _mod_spec(mod3, 5, tpg),
            pl.BlockSpec((tm, LANES), lambda i, e, f: (i, 0)),
            pl.BlockSpec((1, D_MODEL, tf), lambda i, e, f: (e, 0, f)),
            pl.BlockSpec((1, D_MODEL, tf), lambda i, e, f: (e, 0, f)),
            pl.BlockSpec((1, tf, D_MODEL), lambda i, e, f: (e, f, 0)),
            vec(D_MODEL), vec(D_MODEL),
        ],
        out_specs=pl.BlockSpec((tm, D_MODEL), lambda i, e, f: (i, 0)),
        out_shape=jax.ShapeDtypeStruct((n, D_MODEL), F32),
        scratch_shapes=[pltpu.VMEM((tm, D_MODEL), BF16), pltpu.VMEM((tm, D_MODEL), F32)],
        compiler_params=_cparams("arbitrary", "arbitrary", "arbitrary"),
        name="ffn_gated" if gated else "ffn_dense",
    )(x, mod3, mod3, mod3, gates, w_gate, w_up, w_down, ln_g, ln_b)


def _trunk(x, mod3s, tpg, prm, tm, mixer_fn):
    n = x.shape[0]
    states = []
    for l in range(DEPTH):
        mod3 = mod3s[l]
        pm, ps = _inproj(x, mod3, tpg, prm["w_main"][l], prm["w_small"][l], tm)
        yc, o, st = mixer_fn(l, pm, ps)
        states.append(st)
        x = _outproj(yc, o, pm, x, mod3, tpg, prm["norm_w"][l], prm["w_out"][l],
                     prm["ln_g"][l, 0:1], prm["ln_b"][l, 0:1], tm)
        j = l // 2
        if l % 2 == 0:
            ones = jnp.ones((n, LANES), F32)
            x = _ffn(x, mod3, tpg, ones, prm["w_ff_gate"][j:j + 1], prm["w_ff_up"][j:j + 1],
                     prm["w_ff_down"][j:j + 1], prm["ln_g"][l, 1:2], prm["ln_b"][l, 1:2],
                     tm, prm["w_ff_gate"].shape[-1] // 2, False)
        else:
            gates = _router(x, mod3, tpg, prm["w_router"][j], prm["b_router"][j], tm)
            x = _ffn(x, mod3, tpg, gates, prm["w_exp_gate"][j], prm["w_exp_up"][j],
                     prm["w_exp_down"][j], prm["ln_g"][l, 1:2], prm["ln_b"][l, 1:2],
                     tm, prm["w_exp_gate"].shape[-1], True)
    return x, states


def kernel(x_prompt, x_sample, state_conv, state_dn_conv, state_dn, c_prompt, c_sample, w_in, conv_w,
           dn_conv_w, a_log, dt_bias, dn_norm_w, w_out, w_ada, b_ada, ln_g, ln_b, w_ff_gate, w_ff_up,
           w_ff_down, w_router, b_router, w_exp_gate, w_exp_up, w_exp_down):
    bp, seq, d = x_prompt.shape
    bs, steps, _ = x_sample.shape
    pad_lanes = LANES - N_EXPERTS
    hp = jnp.zeros((DEPTH, 2, LANES), F32)
    hp = hp.at[:, 0, G_LANE:].set(a_log).at[:, 1, G_LANE:].set(dt_bias)
    prm = {
        "w_main": w_in[:, :, :P_MAIN].astype(BF16),
        "w_small": w_in[:, :, w_in.shape[-1] - LANES:],
        "norm_w": dn_norm_w.reshape(DEPTH, 1, DN_HEAD_DIM),
        "w_out": w_out.astype(BF16),
        "ln_g": ln_g, "ln_b": ln_b,
        "w_ff_gate": w_ff_gate.astype(BF16), "w_ff_up": w_ff_up.astype(BF16),
        "w_ff_down": w_ff_down.astype(BF16),
        "w_router": jnp.pad(w_router, ((0, 0), (0, 0), (0, pad_lanes))),
        "b_router": jnp.pad(b_router, ((0, 0), (0, pad_lanes)))[:, None, :],
        "w_exp_gate": w_exp_gate.astype(BF16), "w_exp_up": w_exp_up.astype(BF16),
        "w_exp_down": w_exp_down.astype(BF16),
    }

    mod = _ada(jnp.concatenate([c_prompt, c_sample], axis=0), w_ada, b_ada)
    mod_p = [mod[l, :bp].reshape(bp, 1, 6 * d) for l in range(DEPTH)]
    mod_s = [mod[l, bp:].reshape(1, bs, 6 * d) for l in range(DEPTH)]

    tm_p, tc_pre, tc_delta = 512, 256, 256
    zc = jnp.zeros((bp, CONV_K - 1, CONV_WIDTH), F32)
    zd = jnp.zeros((bp, DN_CONV_K - 1, 3 * DN_WIDTH), F32)
    zs = jnp.zeros((bp, DN_HEADS, DN_HEAD_DIM, DN_HEAD_DIM), F32)

    def mixer_prompt(l, pm, ps):
        yc, qkv, gb, ncb, ndb = _pre_prompt(pm, ps, conv_w[l], dn_conv_w[l], hp[l], zc, zd, bp, seq, tc_pre)
        o, s_new = _delta_prompt(qkv, gb, zs, bp, seq, tc_delta)
        return yc, o, (ncb, ndb, s_new)

    y_p, st_p = _trunk(x_prompt.reshape(bp * seq, d), mod_p, seq // tm_p, prm, tm_p, mixer_prompt)

    bt_pre, bt_delta = 32, 8

    def mixer_sample(l, pm, ps):
        cb_tm = jnp.transpose(state_conv[l], (1, 0, 2))
        db_tm = jnp.transpose(state_dn_conv[l], (1, 0, 2))
        yc, qkv, gb, ncb, ndb = _pre_sample(pm, ps, conv_w[l], dn_conv_w[l], hp[l], cb_tm, db_tm,
                                            bs, steps, bt_pre)
        to_b = lambda a: jnp.pad(jnp.transpose(a, (1, 0, 2)), ((0, 0), (0, SAMPLE_CHUNK - steps), (0, 0)))
        o_b, s_new = _delta_sample(to_b(qkv), to_b(gb), state_dn[l], bt_delta)
        o = jnp.transpose(o_b[:, :steps], (1, 0, 2)).reshape(steps * bs, DN_WIDTH)
        return (yc.reshape(steps * bs, CONV_WIDTH), o,
                (jnp.transpose(ncb, (1, 0, 2)), jnp.transpose(ndb, (1, 0, 2)), s_new))

    x_s = jnp.transpose(x_sample, (1, 0, 2)).reshape(steps * bs, d)
    y_s, st_s = _trunk(x_s, mod_s, steps, prm, bs, mixer_sample)
    y_s = jnp.transpose(y_s.reshape(steps, bs, d), (1, 0, 2))

    stack = lambda sts, i: jnp.stack([s[i] for s in sts])
    return (y_p.reshape(bp, seq, d), y_s,
            stack(st_p, 0), stack(st_p, 1), stack(st_p, 2),
            stack(st_s, 0), stack(st_s, 1), stack(st_s, 2))
```

```python
import functools
import math

import jax
import jax.numpy as jnp
from jax import lax
from jax.experimental import pallas as pl
from jax.experimental.pallas import tpu as pltpu

F32 = jnp.float32
BF16 = jnp.bfloat16

D_MODEL = 1024
DEPTH = 2
CONV_WIDTH = 512
CONV_K = 3
DN_HEADS = 4
DN_HEAD_DIM = 128
DN_WIDTH = DN_HEADS * DN_HEAD_DIM
DN_CONV_K = 4
DN_CHUNK = 64
N_EXPERTS = 8
DEEPNORM_ALPHA = (2.0 * DEPTH) ** 0.25
LN_EPS = 1e-5
RMS_EPS = 1e-6

OFF_CC = CONV_WIDTH
OFF_CH = 2 * CONV_WIDTH
OFF_QKV = 3 * CONV_WIDTH
OFF_Z = OFF_QKV + 3 * DN_WIDTH
P_MAIN = OFF_Z + DN_WIDTH
LANES = 128
BETA_LANE = LANES - 2 * DN_HEADS
G_LANE = LANES - DN_HEADS
SAMPLE_CHUNK = 8

VMEM_LIMIT = 56 * 1024 * 1024


def _cparams(*sem):
    return pltpu.CompilerParams(dimension_semantics=sem, vmem_limit_bytes=VMEM_LIMIT)


def _silu(x):
    return x * jax.nn.sigmoid(x)


def _layer_norm(r, g, b):
    mu = jnp.mean(r, -1, keepdims=True)
    xc = r - mu
    var = jnp.mean(xc * xc, -1, keepdims=True)
    return xc * lax.rsqrt(var + LN_EPS) * g + b


def _split_bf16(a):
    hi = a.astype(BF16)
    lo = (a - hi.astype(F32)).astype(BF16)
    return hi, lo


_NN = (((1,), (0,)), ((), ()))
_BNN = (((2,), (1,)), ((0,), (0,)))
_BNT = (((2,), (2,)), ((0,), (0,)))
_BTN = (((1,), (1,)), ((0,), (0,)))


def _mm(a, b, dims=_NN, mode="bf16"):
    if mode == "f32":
        return lax.dot_general(a, b, dims, precision=lax.Precision.HIGHEST, preferred_element_type=F32)
    if mode == "bf16":
        return lax.dot_general(a.astype(BF16), b.astype(BF16), dims, preferred_element_type=F32)
    ah, al = _split_bf16(a)
    bh, bl = _split_bf16(b)
    d = functools.partial(lax.dot_general, dimension_numbers=dims, preferred_element_type=F32)
    return d(ah, bh) + (d(ah, bl) + d(al, bh))


def _ada_kernel(c_ref, w_ref, b_ref, o_ref):
    s = _silu(c_ref[...]).astype(BF16)
    o_ref[0] = jnp.dot(s, w_ref[0].astype(BF16), preferred_element_type=F32) + b_ref[0]


def _ada(c_all, w_ada, b_ada):
    rows = c_all.shape[0]
    tn = 1536
    return pl.pallas_call(
        _ada_kernel,
        grid=(DEPTH, 6 * D_MODEL // tn),
        in_specs=[
            pl.BlockSpec((rows, D_MODEL), lambda l, j: (0, 0)),
            pl.BlockSpec((1, D_MODEL, tn), lambda l, j: (l, 0, j)),
            pl.BlockSpec((1, 1, tn), lambda l, j: (l, 0, j)),
        ],
        out_specs=pl.BlockSpec((1, rows, tn), lambda l, j: (l, 0, j)),
        out_shape=jax.ShapeDtypeStruct((DEPTH, rows, 6 * D_MODEL), F32),
        compiler_params=_cparams("arbitrary", "arbitrary"),
        name="ada",
    )(c_all, w_ada, b_ada.reshape(DEPTH, 1, 6 * D_MODEL))


def _mod_spec(mod3, chunk, tiles_per_group):
    r = mod3.shape[1]
    return pl.BlockSpec((1, r, D_MODEL), lambda i, *_: (i // tiles_per_group, 0, chunk))


def _inproj_kernel(x_ref, sc_ref, sh_ref, wm_ref, ws_ref, pm_ref, ps_ref):
    u = x_ref[...] * (1.0 + sc_ref[0]) + sh_ref[0]
    pm_ref[...] = jnp.dot(u.astype(BF16), wm_ref[...], preferred_element_type=F32)
    ps_ref[...] = _mm(u, ws_ref[...], mode="x3")


def _inproj(x, mod3, tpg, w_main, w_small, tm):
    n = x.shape[0]
    return pl.pallas_call(
        _inproj_kernel,
        grid=(n // tm,),
        in_specs=[
            pl.BlockSpec((tm, D_MODEL), lambda i: (i, 0)),
            _mod_spec(mod3, 1, tpg),
            _mod_spec(mod3, 0, tpg),
            pl.BlockSpec((D_MODEL, P_MAIN), lambda i: (0, 0)),
            pl.BlockSpec((D_MODEL, LANES), lambda i: (0, 0)),
        ],
        out_specs=[
            pl.BlockSpec((tm, P_MAIN), lambda i: (i, 0)),
            pl.BlockSpec((tm, LANES), lambda i: (i, 0)),
        ],
        out_shape=[
            jax.ShapeDtypeStruct((n, P_MAIN), F32),
            jax.ShapeDtypeStruct((n, LANES), F32),
        ],
        compiler_params=_cparams("arbitrary"),
        name="inproj",
    )(x, mod3, mod3, w_main, w_small)


def _gates(ps, hp):
    lane = lax.broadcasted_iota(jnp.int32, ps.shape, 1)
    beta = jax.nn.sigmoid(ps)
    g = -jnp.exp(hp[0:1, :]) * jax.nn.softplus(ps + hp[1:2, :])
    return jnp.where(lane >= G_LANE, g, jnp.where(lane >= BETA_LANE, beta, 0.0))


def _qkv_finish(y, out_ref, idx):
    y = _silu(y)
    for h in range(3 * DN_HEADS):
        seg = y[:, h * DN_HEAD_DIM:(h + 1) * DN_HEAD_DIM]
        if h < 2 * DN_HEADS:
            seg = seg * lax.rsqrt(jnp.sum(seg * seg, -1, keepdims=True) + RMS_EPS)
            if h < DN_HEADS:
                seg = seg * (DN_HEAD_DIM ** -0.5)
        out_ref[idx + (slice(None), slice(h * DN_HEAD_DIM, (h + 1) * DN_HEAD_DIM))] = seg


def _shifted(x, tail_ref, i, row):
    s = pltpu.roll(x, i, 0)
    for r in range(i):
        s = jnp.where(row == r, tail_ref[8 - i + r:8 - i + r + 1, :], s)
    return s


def _pre_prompt_kernel(pm_ref, ps_ref, cw_ref, dw_ref, hp_ref, cb_ref, db_ref,
                       yc_ref, qkv_ref, gb_ref, ncb_ref, ndb_ref, tc_scr, td_scr):
    t = pl.program_id(1)
    tc = pm_ref.shape[0]

    @pl.when(t == 0)
    def _():
        tc_scr[8 - (CONV_K - 1):8, :] = cb_ref[0]
        td_scr[8 - (DN_CONV_K - 1):8, :] = db_ref[0]

    row_c = lax.broadcasted_iota(jnp.int32, (tc, CONV_WIDTH), 0)
    gate_b = pm_ref[:, 0:OFF_CC]
    cgh = pm_ref[:, OFF_CC:OFF_CH] * pm_ref[:, OFF_CH:OFF_QKV]
    y = _shifted(cgh, tc_scr, 2, row_c) * cw_ref[0:1, :]
    y = y + _shifted(cgh, tc_scr, 1, row_c) * cw_ref[1:2, :]
    y = y + cgh * cw_ref[2:3, :]
    yc_ref[...] = gate_b * y
    last_c = cgh[tc - (CONV_K - 1):tc, :]

    row_d = lax.broadcasted_iota(jnp.int32, (tc, 3 * DN_WIDTH), 0)
    xq = pm_ref[:, OFF_QKV:OFF_Z]
    yq = _shifted(xq, td_scr, 3, row_d) * dw_ref[0:1, :]
    yq = yq + _shifted(xq, td_scr, 2, row_d) * dw_ref[1:2, :]
    yq = yq + _shifted(xq, td_scr, 1, row_d) * dw_ref[2:3, :]
    yq = yq + xq * dw_ref[3:4, :]
    _qkv_finish(yq, qkv_ref, ())
    last_d = xq[tc - (DN_CONV_K - 1):tc, :]

    gb_ref[...] = _gates(ps_ref[...], hp_ref[...])

    tc_scr[8 - (CONV_K - 1):8, :] = last_c
    td_scr[8 - (DN_CONV_K - 1):8, :] = last_d

    @pl.when(t == pl.num_programs(1) - 1)
    def _():
        ncb_ref[0] = last_c
        ndb_ref[0] = last_d


def _pre_prompt(pm, ps, conv_w, dn_conv_w, hp, conv_buf, dn_buf, bsz, seq, tc):
    n = pm.shape[0]
    nt = seq // tc
    rows = lambda w: pl.BlockSpec((tc, w), lambda b, t: (b * nt + t, 0))
    full = lambda a: pl.BlockSpec(a.shape, lambda b, t: (0,) * a.ndim)
    per_b = lambda k, w: pl.BlockSpec((1, k, w), lambda b, t: (b, 0, 0))
    return pl.pallas_call(
        _pre_prompt_kernel,
        grid=(bsz, nt),
        in_specs=[rows(P_MAIN), rows(LANES), full(conv_w), full(dn_conv_w), full(hp),
                  per_b(CONV_K - 1, CONV_WIDTH), per_b(DN_CONV_K - 1, 3 * DN_WIDTH)],
        out_specs=[rows(CONV_WIDTH), rows(3 * DN_WIDTH), rows(LANES),
                   per_b(CONV_K - 1, CONV_WIDTH), per_b(DN_CONV_K - 1, 3 * DN_WIDTH)],
        out_shape=[
            jax.ShapeDtypeStruct((n, CONV_WIDTH), F32),
            jax.ShapeDtypeStruct((n, 3 * DN_WIDTH), F32),
            jax.ShapeDtypeStruct((n, LANES), F32),
            jax.ShapeDtypeStruct((bsz, CONV_K - 1, CONV_WIDTH), F32),
            jax.ShapeDtypeStruct((bsz, DN_CONV_K - 1, 3 * DN_WIDTH), F32),
        ],
        scratch_shapes=[pltpu.VMEM((8, CONV_WIDTH), F32), pltpu.VMEM((8, 3 * DN_WIDTH), F32)],
        compiler_params=_cparams("arbitrary", "arbitrary"),
        name="pre_prompt",
    )(pm, ps, conv_w, dn_conv_w, hp, conv_buf, dn_buf)


def _pre_sample_kernel(pm_ref, ps_ref, cw_ref, dw_ref, hp_ref, cb_ref, db_ref,
                       yc_ref, qkv_ref, gb_ref, ncb_ref, ndb_ref):
    steps = pm_ref.shape[0]
    cgh = [pm_ref[t, :, OFF_CC:OFF_CH] * pm_ref[t, :, OFF_CH:OFF_QKV] for t in range(steps)]
    ext = [cb_ref[i] for i in range(CONV_K - 1)] + cgh
    for t in range(steps):
        y = ext[t] * cw_ref[0:1, :]
        for i in range(1, CONV_K):
            y = y + ext[t + i] * cw_ref[i:i + 1, :]
        yc_ref[t] = pm_ref[t, :, 0:OFF_CC] * y
    for i in range(CONV_K - 1):
        ncb_ref[i] = ext[len(ext) - (CONV_K - 1) + i]

    xq = [pm_ref[t, :, OFF_QKV:OFF_Z] for t in range(steps)]
    extq = [db_ref[i] for i in range(DN_CONV_K - 1)] + xq
    for t in range(steps):
        y = extq[t] * dw_ref[0:1, :]
        for i in range(1, DN_CONV_K):
            y = y + extq[t + i] * dw_ref[i:i + 1, :]
        _qkv_finish(y, qkv_ref, (t,))
        gb_ref[t] = _gates(ps_ref[t], hp_ref[...])
    for i in range(DN_CONV_K - 1):
        ndb_ref[i] = extq[len(extq) - (DN_CONV_K - 1) + i]


def _pre_sample(pm, ps, conv_w, dn_conv_w, hp, conv_buf_tm, dn_buf_tm, bsz, steps, bt):
    slab = lambda k, w: pl.BlockSpec((k, bt, w), lambda i: (0, i, 0))
    full = lambda a: pl.BlockSpec(a.shape, lambda i: (0,) * a.ndim)
    return pl.pallas_call(
        _pre_sample_kernel,
        grid=(bsz // bt,),
        in_specs=[slab(steps, P_MAIN), slab(steps, LANES), full(conv_w), full(dn_conv_w), full(hp),
                  slab(CONV_K - 1, CONV_WIDTH), slab(DN_CONV_K - 1, 3 * DN_WIDTH)],
        out_specs=[slab(steps, CONV_WIDTH), slab(steps, 3 * DN_WIDTH), slab(steps, LANES),
                   slab(CONV_K - 1, CONV_WIDTH), slab(DN_CONV_K - 1, 3 * DN_WIDTH)],
        out_shape=[
            jax.ShapeDtypeStruct((steps, bsz, CONV_WIDTH), F32),
            jax.ShapeDtypeStruct((steps, bsz, 3 * DN_WIDTH), F32),
            jax.ShapeDtypeStruct((steps, bsz, LANES), F32),
            jax.ShapeDtypeStruct((CONV_K - 1, bsz, CONV_WIDTH), F32),
            jax.ShapeDtypeStruct((DN_CONV_K - 1, bsz, 3 * DN_WIDTH), F32),
        ],
        compiler_params=_cparams("arbitrary"),
        name="pre_sample",
    )(pm.reshape(steps, bsz, P_MAIN), ps.reshape(steps, bsz, LANES), conv_w, dn_conv_w, hp,
      conv_buf_tm, dn_buf_tm)


def _cumsum_chunks(x, c):
    row = lax.broadcasted_iota(jnp.int32, x.shape, 0) & (c - 1)
    s = 1
    while s < c:
        x = x + jnp.where(row >= s, pltpu.roll(x, s, 0), 0.0)
        s *= 2
    return x


def _gdn_problems(qkv, gb, c):
    n = qkv.shape[0] // c
    gc = _cumsum_chunks(gb, c)
    q, k, v, g_b, beta_b = [], [], [], [], []
    for ci in range(n):
        rs = slice(ci * c, (ci + 1) * c)
        for h in range(DN_HEADS):
            ls = lambda base: slice(base + h * DN_HEAD_DIM, base + (h + 1) * DN_HEAD_DIM)
            q.append(qkv[rs, ls(0)])
            k.append(qkv[rs, ls(DN_WIDTH)])
            v.append(qkv[rs, ls(2 * DN_WIDTH)])
            g_b.append(jnp.broadcast_to(gc[rs, G_LANE + h:G_LANE + h + 1], (c, DN_HEAD_DIM)))
            beta_b.append(jnp.broadcast_to(gb[rs, BETA_LANE + h:BETA_LANE + h + 1], (c, DN_HEAD_DIM)))
    return tuple(jnp.stack(a) for a in (q, k, v, g_b, beta_b))


def _unit_lower_solve(m, rhs, c, small, mode):
    if small:
        sol = rhs
        for j in range(c - 1):
            sol = sol - m[:, :, j:j + 1] * sol[:, j:j + 1, :]
        return sol
    row = lax.broadcasted_iota(jnp.int32, m.shape, 1)
    col = lax.broadcasted_iota(jnp.int32, m.shape, 2)
    p = -m
    t = jnp.where(row == col, 1.0, 0.0) + p
    p = _mm(p, p, _BNN, "bf16")
    levels = int(math.log2(c))
    for lvl in range(1, levels):
        if lvl < levels - 1:
            y = _mm(jnp.concatenate([t, p], axis=1), p, _BNN, "bf16")
            t = t + y[:, :c]
            p = y[:, c:]
        else:
            t = t + _mm(t, p, _BNN, "bf16")
    x = _mm(t, rhs, _BNN, "bf16")
    resid = rhs - x - _mm(m, x, _BNN, mode)
    return x + _mm(t, resid, _BNN, "bf16")


def _gdn_intra(q, k, v, g_b, beta_b, *, c, small, mode_gram, mode_solve, mode_apply):
    n = q.shape[0]
    row = lax.broadcasted_iota(jnp.int32, (n, c, c), 1)
    col = lax.broadcasted_iota(jnp.int32, (n, c, c), 2)
    g_cc = g_b[:, :, :c]
    g_row = jnp.sum(jnp.where(row == col, g_cc, 0.0), axis=1, keepdims=True)
    gamma = jnp.exp(jnp.where(row >= col, g_cc - g_row, -jnp.inf))
    kbeta = k * beta_b
    exp_g = jnp.exp(g_b)
    gram = _mm(jnp.concatenate([kbeta, q], axis=1), k, _BNT, mode_gram)
    m = jnp.where(row > col, gram[:, :c] * gamma, 0.0)
    attn = gram[:, c:] * gamma
    rhs = jnp.concatenate([kbeta * exp_g, v * beta_b], axis=-1)
    sol = _unit_lower_solve(m, rhs, c, small, mode_solve)
    g_last = g_b[:, c - 1:c, :]
    k_dec = k * jnp.exp(g_last - g_b)
    kd = _mm(k_dec, sol, _BTN, mode_apply)
    at = _mm(attn, sol, _BNN, mode_apply)
    q_t = q * exp_g - at[..., :DN_HEAD_DIM]
    return kd[..., :DN_HEAD_DIM], kd[..., DN_HEAD_DIM:], q_t, at[..., DN_HEAD_DIM:], jnp.exp(g_last)


def _gdn_state_step(a_mat, b_mat, q_t, o_intra, d_last, s, mode):
    r = _mm(jnp.concatenate([a_mat, q_t], axis=1), s, _BNN, mode)
    s_new = s * d_last - r[:, :DN_HEAD_DIM] + b_mat
    return r[:, DN_HEAD_DIM:] + o_intra, s_new


_PROMPT_MODES = dict(mode_gram="x3", mode_solve="x3", mode_apply="bf16")
_PROMPT_STATE_MODE = "bf16"
_SAMPLE_MODES = dict(mode_gram="bf16", mode_solve="bf16", mode_apply="bf16")
_SAMPLE_STATE_MODE = "bf16"


def _delta_prompt_kernel(qkv_ref, gb_ref, s0_ref, o_ref, sn_ref, s_scr):
    t = pl.program_id(1)
    tc = qkv_ref.shape[0]

    @pl.when(t == 0)
    def _():
        s_scr[...] = s0_ref[0]

    probs = _gdn_problems(qkv_ref[...], gb_ref[...], DN_CHUNK)
    a_mat, b_mat, q_t, o_intra, d_last = _gdn_intra(*probs, c=DN_CHUNK, small=False, **_PROMPT_MODES)
    s = s_scr[...]
    for ci in range(tc // DN_CHUNK):
        ps = slice(ci * DN_HEADS, (ci + 1) * DN_HEADS)
        o, s = _gdn_state_step(a_mat[ps], b_mat[ps], q_t[ps], o_intra[ps], d_last[ps], s,
                               _PROMPT_STATE_MODE)
        for h in range(DN_HEADS):
            o_ref[ci * DN_CHUNK:(ci + 1) * DN_CHUNK, h * DN_HEAD_DIM:(h + 1) * DN_HEAD_DIM] = o[h]
    s_scr[...] = s

    @pl.when(t == pl.num_programs(1) - 1)
    def _():
        sn_ref[0] = s


def _delta_prompt(qkv, gb, s0, bsz, seq, tc):
    n = qkv.shape[0]
    nt = seq // tc
    rows = lambda w: pl.BlockSpec((tc, w), lambda b, t: (b * nt + t, 0))
    state = pl.BlockSpec((1, DN_HEADS, DN_HEAD_DIM, DN_HEAD_DIM), lambda b, t: (b, 0, 0, 0))
    return pl.pallas_call(
        _delta_prompt_kernel,
        grid=(bsz, nt),
        in_specs=[rows(3 * DN_WIDTH), rows(LANES), state],
        out_specs=[rows(DN_WIDTH), state],
        out_shape=[
            jax.ShapeDtypeStruct((n, DN_WIDTH), F32),
            jax.ShapeDtypeStruct((bsz, DN_HEADS, DN_HEAD_DIM, DN_HEAD_DIM), F32),
        ],
        scratch_shapes=[pltpu.VMEM((DN_HEADS, DN_HEAD_DIM, DN_HEAD_DIM), F32)],
        compiler_params=_cparams("arbitrary", "arbitrary"),
        name="delta_prompt",
    )(qkv, gb, s0)


def _delta_sample_kernel(qkv_ref, gb_ref, s0_ref, *rest):
    o_ref, sn_ref = rest[-2:]
    bt = qkv_ref.shape[0]
    c = SAMPLE_CHUNK
    probs = _gdn_problems(qkv_ref[...].reshape(bt * c, 3 * DN_WIDTH), gb_ref[...].reshape(bt * c, LANES), c)
    a_mat, b_mat, q_t, o_intra, d_last = _gdn_intra(*probs, c=c, small=True, **_SAMPLE_MODES)
    s = s0_ref[0].reshape(bt * DN_HEADS, DN_HEAD_DIM, DN_HEAD_DIM)
    o, s = _gdn_state_step(a_mat, b_mat, q_t, o_intra, d_last, s, _SAMPLE_STATE_MODE)
    for b in range(bt):
        for h in range(DN_HEADS):
            o_ref[b, :, h * DN_HEAD_DIM:(h + 1) * DN_HEAD_DIM] = o[b * DN_HEADS + h]
    sn_ref[0] = s.reshape(bt, DN_HEADS, DN_HEAD_DIM, DN_HEAD_DIM)


def _delta_sample(qkv_b, gb_b, state_all, layer, carried, bt):
    bsz = qkv_b.shape[0]
    blk = lambda w: pl.BlockSpec((bt, SAMPLE_CHUNK, w), lambda i: (i, 0, 0))
    state = pl.BlockSpec((1, bt, DN_HEADS, DN_HEAD_DIM, DN_HEAD_DIM), lambda i: (layer, i, 0, 0, 0))
    in_specs, args, aliases = [blk(3 * DN_WIDTH), blk(LANES), state], [qkv_b, gb_b, state_all], {}
    if carried is not None:
        in_specs.append(pl.BlockSpec(memory_space=pl.ANY))
        args.append(carried)
        aliases = {3: 1}
    return pl.pallas_call(
        _delta_sample_kernel,
        grid=(bsz // bt,),
        in_specs=in_specs,
        out_specs=[blk(DN_WIDTH), state],
        out_shape=[
            jax.ShapeDtypeStruct((bsz, SAMPLE_CHUNK, DN_WIDTH), F32),
            jax.ShapeDtypeStruct(state_all.shape, F32),
        ],
        input_output_aliases=aliases,
        compiler_params=_cparams("arbitrary"),
        name="delta_sample",
    )(*args)


def _outproj_kernel(yc_ref, o_ref, z_ref, x_ref, g1_ref, nw_ref, w_ref, lg_ref, lb_ref, out_ref):
    o = o_ref[...]
    parts = []
    for h in range(DN_HEADS):
        oh = o[:, h * DN_HEAD_DIM:(h + 1) * DN_HEAD_DIM]
        parts.append(oh * lax.rsqrt(jnp.mean(oh * oh, -1, keepdims=True) + RMS_EPS) * nw_ref[...])
    og = jnp.concatenate(parts, axis=-1) * _silu(z_ref[...])
    mixed = jnp.concatenate([yc_ref[...], og], axis=-1).astype(BF16)
    m = jnp.dot(mixed, w_ref[...], preferred_element_type=F32)
    r = DEEPNORM_ALPHA * x_ref[...] + (1.0 + g1_ref[0]) * m
    out_ref[...] = _layer_norm(r, lg_ref[...], lb_ref[...])


def _outproj(yc, o, pm, x, mod3, tpg, norm_w, w_out, ln_g, ln_b, tm):
    n = x.shape[0]
    vec = lambda w: pl.BlockSpec((1, w), lambda i: (0, 0))
    return pl.pallas_call(
        _outproj_kernel,
        grid=(n // tm,),
        in_specs=[
            pl.BlockSpec((tm, CONV_WIDTH), lambda i: (i, 0)),
            pl.BlockSpec((tm, DN_WIDTH), lambda i: (i, 0)),
            pl.BlockSpec((tm, DN_WIDTH), lambda i: (i, OFF_Z // DN_WIDTH)),
            pl.BlockSpec((tm, D_MODEL), lambda i: (i, 0)),
            _mod_spec(mod3, 2, tpg),
            vec(DN_HEAD_DIM),
            pl.BlockSpec((D_MODEL, D_MODEL), lambda i: (0, 0)),
            vec(D_MODEL), vec(D_MODEL),
        ],
        out_specs=pl.BlockSpec((tm, D_MODEL), lambda i: (i, 0)),
        out_shape=jax.ShapeDtypeStruct((n, D_MODEL), F32),
        compiler_params=_cparams("arbitrary"),
        name="outproj",
    )(yc, o, pm, x, mod3, norm_w, w_out, ln_g, ln_b)


def _router_kernel(x_ref, sc_ref, sh_ref, wr_ref, br_ref, gates_ref):
    u = x_ref[...] * (1.0 + sc_ref[0]) + sh_ref[0]
    logits = _mm(u, wr_ref[...], mode="x3") + br_ref[...]
    lane = lax.broadcasted_iota(jnp.int32, logits.shape, 1)
    valid = lane < N_EXPERTS
    logits = jnp.where(valid, logits, -jnp.inf)
    ex = jnp.exp(logits - jnp.max(logits, -1, keepdims=True))
    probs = jnp.where(valid, ex / jnp.sum(ex, -1, keepdims=True), -1.0)
    p1 = jnp.max(probs, -1, keepdims=True)
    i1 = jnp.min(jnp.where(probs == p1, lane, LANES), -1, keepdims=True)
    rest = jnp.where(lane == i1, -1.0, probs)
    p2 = jnp.max(rest, -1, keepdims=True)
    i2 = jnp.min(jnp.where(rest == p2, lane, LANES), -1, keepdims=True)
    tot = p1 + p2
    gates_ref[...] = jnp.where(lane == i1, p1 / tot, 0.0) + jnp.where(lane == i2, p2 / tot, 0.0)


def _router(x, mod3, tpg, w_router, b_router, tm):
    n = x.shape[0]
    return pl.pallas_call(
        _router_kernel,
        grid=(n // tm,),
        in_specs=[
            pl.BlockSpec((tm, D_MODEL), lambda i: (i, 0)),
            _mod_spec(mod3, 4, tpg),
            _mod_spec(mod3, 3, tpg),
            pl.BlockSpec((D_MODEL, LANES), lambda i: (0, 0)),
            pl.BlockSpec((1, LANES), lambda i: (0, 0)),
        ],
        out_specs=pl.BlockSpec((tm, LANES), lambda i: (i, 0)),
        out_shape=jax.ShapeDtypeStruct((n, LANES), F32),
        compiler_params=_cparams("arbitrary"),
        name="router",
    )(x, mod3, mod3, w_router, b_router)


def _ffn_kernel(x_ref, sc_ref, sh_ref, g2_ref, gates_ref, wg_ref, wu_ref, wd_ref, lg_ref, lb_ref,
                out_ref, u_scr, acc_scr, *, gated):
    e = pl.program_id(1)
    f = pl.program_id(2)
    first = jnp.logical_and(e == 0, f == 0)
    last = jnp.logical_and(e == pl.num_programs(1) - 1, f == pl.num_programs(2) - 1)

    @pl.when(first)
    def _():
        u_scr[...] = (x_ref[...] * (1.0 + sc_ref[0]) + sh_ref[0]).astype(BF16)
        acc_scr[...] = jnp.zeros_like(acc_scr)

    u = u_scr[...]
    h = _silu(jnp.dot(u, wg_ref[0], preferred_element_type=F32)) * jnp.dot(
        u, wu_ref[0], preferred_element_type=F32)
    y = jnp.dot(h.astype(BF16), wd_ref[0], preferred_element_type=F32)
    if gated:
        lane = lax.broadcasted_iota(jnp.int32, gates_ref.shape, 1)
        y = jnp.sum(jnp.where(lane == e, gates_ref[...], 0.0), -1, keepdims=True) * y
    acc_scr[...] += y

    @pl.when(last)
    def _():
        r = DEEPNORM_ALPHA * x_ref[...] + (1.0 + g2_ref[0]) * acc_scr[...]
        out_ref[...] = _layer_norm(r, lg_ref[...], lb_ref[...])


def _ffn(x, mod3, tpg, gates, w_gate, w_up, w_down, ln_g, ln_b, tm, tf, gated):
    n = x.shape[0]
    n_e, _, d_ff = w_gate.shape
    vec = lambda w: pl.BlockSpec((1, w), lambda i, e, f: (0, 0))
    return pl.pallas_call(
        functools.partial(_ffn_kernel, gated=gated),
        grid=(n // tm, n_e, d_ff // tf),
        in_specs=[
            pl.BlockSpec((tm, D_MODEL), lambda i, e, f: (i, 0)),
            _mod_spec(mod3, 4, tpg),
            _mod_spec(mod3, 3, tpg),
            _mod_spec(mod3, 5, tpg),
            pl.BlockSpec((tm, LANES), lambda i, e, f: (i, 0)),
            pl.BlockSpec((1, D_MODEL, tf), lambda i, e, f: (e, 0, f)),
            pl.BlockSpec((1, D_MODEL, tf), lambda i, e, f: (e, 0, f)),
            pl.BlockSpec((1, tf, D_MODEL), lambda i, e, f: (e, f, 0)),
            vec(D_MODEL), vec(D_MODEL),
        ],
        out_specs=pl.BlockSpec((tm, D_MODEL), lambda i, e, f: (i, 0)),
        out_shape=jax.ShapeDtypeStruct((n, D_MODEL), F32),
        scratch_shapes=[pltpu.VMEM((tm, D_MODEL), BF16), pltpu.VMEM((tm, D_MODEL), F32)],
        compiler_params=_cparams("arbitrary", "arbitrary", "arbitrary"),
        name="ffn_gated" if gated else "ffn_dense",
    )(x, mod3, mod3, mod3, gates, w_gate, w_up, w_down, ln_g, ln_b)


def _trunk(x, mod3s, tpg, prm, tm, mixer_fn):
    n = x.shape[0]
    states = []
    for l in range(DEPTH):
        mod3 = mod3s[l]
        pm, ps = _inproj(x, mod3, tpg, prm["w_main"][l], prm["w_small"][l], tm)
        yc, o, st = mixer_fn(l, pm, ps)
        states.append(st)
        x = _outproj(yc, o, pm, x, mod3, tpg, prm["norm_w"][l], prm["w_out"][l],
                     prm["ln_g"][l, 0:1], prm["ln_b"][l, 0:1], tm)
        j = l // 2
        if l % 2 == 0:
            ones = jnp.ones((n, LANES), F32)
            x = _ffn(x, mod3, tpg, ones, prm["w_ff_gate"][j:j + 1], prm["w_ff_up"][j:j + 1],
                     prm["w_ff_down"][j:j + 1], prm["ln_g"][l, 1:2], prm["ln_b"][l, 1:2],
                     tm, prm["w_ff_gate"].shape[-1] // 2, False)
        else:
            gates = _router(x, mod3, tpg, prm["w_router"][j], prm["b_router"][j], tm)
            x = _ffn(x, mod3, tpg, gates, prm["w_exp_gate"][j], prm["w_exp_up"][j],
                     prm["w_exp_down"][j], prm["ln_g"][l, 1:2], prm["ln_b"][l, 1:2],
                     tm, prm["w_exp_gate"].shape[-1], True)
    return x, states


def kernel(x_prompt, x_sample, state_conv, state_dn_conv, state_dn, c_prompt, c_sample, w_in, conv_w,
           dn_conv_w, a_log, dt_bias, dn_norm_w, w_out, w_ada, b_ada, ln_g, ln_b, w_ff_gate, w_ff_up,
           w_ff_down, w_router, b_router, w_exp_gate, w_exp_up, w_exp_down):
    bp, seq, d = x_prompt.shape
    bs, steps, _ = x_sample.shape
    pad_lanes = LANES - N_EXPERTS
    hp = jnp.zeros((DEPTH, 2, LANES), F32)
    hp = hp.at[:, 0, G_LANE:].set(a_log).at[:, 1, G_LANE:].set(dt_bias)
    prm = {
        "w_main": w_in[:, :, :P_MAIN].astype(BF16),
        "w_small": w_in[:, :, w_in.shape[-1] - LANES:],
        "norm_w": dn_norm_w.reshape(DEPTH, 1, DN_HEAD_DIM),
        "w_out": w_out.astype(BF16),
        "ln_g": ln_g, "ln_b": ln_b,
        "w_ff_gate": w_ff_gate.astype(BF16), "w_ff_up": w_ff_up.astype(BF16),
        "w_ff_down": w_ff_down.astype(BF16),
        "w_router": jnp.pad(w_router, ((0, 0), (0, 0), (0, pad_lanes))),
        "b_router": jnp.pad(b_router, ((0, 0), (0, pad_lanes)))[:, None, :],
        "w_exp_gate": w_exp_gate.astype(BF16), "w_exp_up": w_exp_up.astype(BF16),
        "w_exp_down": w_exp_down.astype(BF16),
    }

    mod = _ada(jnp.concatenate([c_prompt, c_sample], axis=0), w_ada, b_ada)
    mod_p = [mod[l, :bp].reshape(bp, 1, 6 * d) for l in range(DEPTH)]
    mod_s = [mod[l, bp:].reshape(1, bs, 6 * d) for l in range(DEPTH)]

    tm_p, tc_pre, tc_delta = 512, 256, 256
    zc = jnp.zeros((bp, CONV_K - 1, CONV_WIDTH), F32)
    zd = jnp.zeros((bp, DN_CONV_K - 1, 3 * DN_WIDTH), F32)
    zs = jnp.zeros((bp, DN_HEADS, DN_HEAD_DIM, DN_HEAD_DIM), F32)

    def mixer_prompt(l, pm, ps):
        yc, qkv, gb, ncb, ndb = _pre_prompt(pm, ps, conv_w[l], dn_conv_w[l], hp[l], zc, zd, bp, seq, tc_pre)
        o, s_new = _delta_prompt(qkv, gb, zs, bp, seq, tc_delta)
        return yc, o, (ncb, ndb, s_new)

    y_p, st_p = _trunk(x_prompt.reshape(bp * seq, d), mod_p, seq // tm_p, prm, tm_p, mixer_prompt)

    bt_pre, bt_delta = 32, 8
    dn_states = []

    def mixer_sample(l, pm, ps):
        cb_tm = jnp.transpose(state_conv[l], (1, 0, 2))
        db_tm = jnp.transpose(state_dn_conv[l], (1, 0, 2))
        yc, qkv, gb, ncb, ndb = _pre_sample(pm, ps, conv_w[l], dn_conv_w[l], hp[l], cb_tm, db_tm,
                                            bs, steps, bt_pre)
        to_b = lambda a: jnp.pad(jnp.transpose(a, (1, 0, 2)), ((0, 0), (0, SAMPLE_CHUNK - steps), (0, 0)))
        o_b, s_all = _delta_sample(to_b(qkv), to_b(gb), state_dn, l, dn_states[-1] if dn_states else None,
                                   bt_delta)
        dn_states.append(s_all)
        o = jnp.transpose(o_b[:, :steps], (1, 0, 2)).reshape(steps * bs, DN_WIDTH)
        return (yc.reshape(steps * bs, CONV_WIDTH), o,
                (jnp.transpose(ncb, (1, 0, 2)), jnp.transpose(ndb, (1, 0, 2))))

    x_s = jnp.transpose(x_sample, (1, 0, 2)).reshape(steps * bs, d)
    y_s, st_s = _trunk(x_s, mod_s, steps, prm, bs, mixer_sample)
    y_s = jnp.transpose(y_s.reshape(steps, bs, d), (1, 0, 2))

    stack = lambda sts, i: jnp.stack([s[i] for s in sts])
    return (y_p.reshape(bp, seq, d), y_s,
            stack(st_p, 0), stack(st_p, 1), stack(st_p, 2),
            stack(st_s, 0), stack(st_s, 1), dn_states[-1])
```

```python
import functools
import math

import jax
import jax.numpy as jnp
from jax import lax
from jax.experimental import pallas as pl
from jax.experimental.pallas import tpu as pltpu

F32 = jnp.float32
BF16 = jnp.bfloat16

D_MODEL = 1024
DEPTH = 2
CONV_WIDTH = 512
CONV_K = 3
DN_HEADS = 4
DN_HEAD_DIM = 128
DN_WIDTH = DN_HEADS * DN_HEAD_DIM
DN_CONV_K = 4
DN_CHUNK = 64
N_EXPERTS = 8
DEEPNORM_ALPHA = (2.0 * DEPTH) ** 0.25
LN_EPS = 1e-5
RMS_EPS = 1e-6

OFF_CC = CONV_WIDTH
OFF_CH = 2 * CONV_WIDTH
OFF_QKV = 3 * CONV_WIDTH
OFF_Z = OFF_QKV + 3 * DN_WIDTH
P_MAIN = OFF_Z + DN_WIDTH
LANES = 128
BETA_LANE = LANES - 2 * DN_HEADS
G_LANE = LANES - DN_HEADS
SAMPLE_CHUNK = 8

VMEM_LIMIT = 56 * 1024 * 1024


def _cparams(*sem):
    return pltpu.CompilerParams(dimension_semantics=sem, vmem_limit_bytes=VMEM_LIMIT)


def _silu(x):
    return x * jax.nn.sigmoid(x)


def _layer_norm(r, g, b):
    mu = jnp.mean(r, -1, keepdims=True)
    xc = r - mu
    var = jnp.mean(xc * xc, -1, keepdims=True)
    return xc * lax.rsqrt(var + LN_EPS) * g + b


def _split_bf16(a):
    hi = a.astype(BF16)
    lo = (a - hi.astype(F32)).astype(BF16)
    return hi, lo


_NN = (((1,), (0,)), ((), ()))
_BNN = (((2,), (1,)), ((0,), (0,)))
_BNT = (((2,), (2,)), ((0,), (0,)))
_BTN = (((1,), (1,)), ((0,), (0,)))


def _mm(a, b, dims=_NN, mode="bf16"):
    if mode == "f32":
        return lax.dot_general(a, b, dims, precision=lax.Precision.HIGHEST, preferred_element_type=F32)
    if mode == "bf16":
        return lax.dot_general(a.astype(BF16), b.astype(BF16), dims, preferred_element_type=F32)
    ah, al = _split_bf16(a)
    bh, bl = _split_bf16(b)
    d = functools.partial(lax.dot_general, dimension_numbers=dims, preferred_element_type=F32)
    return d(ah, bh) + (d(ah, bl) + d(al, bh))


def _ada_kernel(c_ref, w_ref, b_ref, o_ref):
    s = _silu(c_ref[...]).astype(BF16)
    o_ref[0] = jnp.dot(s, w_ref[0].astype(BF16), preferred_element_type=F32) + b_ref[0]


def _ada(c_all, w_ada, b_ada):
    rows = c_all.shape[0]
    tn = 1536
    return pl.pallas_call(
        _ada_kernel,
        grid=(DEPTH, 6 * D_MODEL // tn),
        in_specs=[
            pl.BlockSpec((rows, D_MODEL), lambda l, j: (0, 0)),
            pl.BlockSpec((1, D_MODEL, tn), lambda l, j: (l, 0, j)),
            pl.BlockSpec((1, 1, tn), lambda l, j: (l, 0, j)),
        ],
        out_specs=pl.BlockSpec((1, rows, tn), lambda l, j: (l, 0, j)),
        out_shape=jax.ShapeDtypeStruct((DEPTH, rows, 6 * D_MODEL), F32),
        compiler_params=_cparams("arbitrary", "arbitrary"),
        name="ada",
    )(c_all, w_ada, b_ada.reshape(DEPTH, 1, 6 * D_MODEL))


def _mod_spec(mod3, chunk, tiles_per_group):
    r = mod3.shape[1]
    return pl.BlockSpec((1, r, D_MODEL), lambda i, *_: (i // tiles_per_group, 0, chunk))


def _inproj_kernel(x_ref, sc_ref, sh_ref, wm_ref, ws_ref, pm_ref, ps_ref):
    u = x_ref[...] * (1.0 + sc_ref[0]) + sh_ref[0]
    pm_ref[...] = jnp.dot(u.astype(BF16), wm_ref[...], preferred_element_type=F32)
    ps_ref[...] = _mm(u, ws_ref[...], mode="x3")


def _inproj(x, mod3, tpg, w_main, w_small, tm):
    n = x.shape[0]
    return pl.pallas_call(
        _inproj_kernel,
        grid=(n // tm,),
        in_specs=[
            pl.BlockSpec((tm, D_MODEL), lambda i: (i, 0)),
            _mod_spec(mod3, 1, tpg),
            _mod_spec(mod3, 0, tpg),
            pl.BlockSpec((D_MODEL, P_MAIN), lambda i: (0, 0)),
            pl.BlockSpec((D_MODEL, LANES), lambda i: (0, 0)),
        ],
        out_specs=[
            pl.BlockSpec((tm, P_MAIN), lambda i: (i, 0)),
            pl.BlockSpec((tm, LANES), lambda i: (i, 0)),
        ],
        out_shape=[
            jax.ShapeDtypeStruct((n, P_MAIN), F32),
            jax.ShapeDtypeStruct((n, LANES), F32),
        ],
        compiler_params=_cparams("arbitrary"),
        name="inproj",
    )(x, mod3, mod3, w_main, w_small)


def _gates(ps, hp):
    lane = lax.broadcasted_iota(jnp.int32, ps.shape, 1)
    beta = jax.nn.sigmoid(ps)
    g = -jnp.exp(hp[0:1, :]) * jax.nn.softplus(ps + hp[1:2, :])
    return jnp.where(lane >= G_LANE, g, jnp.where(lane >= BETA_LANE, beta, 0.0))


def _qkv_finish(y, out_ref, idx):
    y = _silu(y)
    for h in range(3 * DN_HEADS):
        seg = y[:, h * DN_HEAD_DIM:(h + 1) * DN_HEAD_DIM]
        if h < 2 * DN_HEADS:
            seg = seg * lax.rsqrt(jnp.sum(seg * seg, -1, keepdims=True) + RMS_EPS)
            if h < DN_HEADS:
                seg = seg * (DN_HEAD_DIM ** -0.5)
        out_ref[idx + (slice(None), slice(h * DN_HEAD_DIM, (h + 1) * DN_HEAD_DIM))] = seg


def _shifted(x, tail_ref, i, row):
    s = pltpu.roll(x, i, 0)
    for r in range(i):
        s = jnp.where(row == r, tail_ref[8 - i + r:8 - i + r + 1, :], s)
    return s


def _pre_prompt_kernel(pm_ref, ps_ref, cw_ref, dw_ref, hp_ref, cb_ref, db_ref,
                       yc_ref, qkv_ref, gb_ref, ncb_ref, ndb_ref, tc_scr, td_scr):
    t = pl.program_id(1)
    tc = pm_ref.shape[0]

    @pl.when(t == 0)
    def _():
        tc_scr[8 - (CONV_K - 1):8, :] = cb_ref[0]
        td_scr[8 - (DN_CONV_K - 1):8, :] = db_ref[0]

    row_c = lax.broadcasted_iota(jnp.int32, (tc, CONV_WIDTH), 0)
    gate_b = pm_ref[:, 0:OFF_CC]
    cgh = pm_ref[:, OFF_CC:OFF_CH] * pm_ref[:, OFF_CH:OFF_QKV]
    y = _shifted(cgh, tc_scr, 2, row_c) * cw_ref[0:1, :]
    y = y + _shifted(cgh, tc_scr, 1, row_c) * cw_ref[1:2, :]
    y = y + cgh * cw_ref[2:3, :]
    yc_ref[...] = gate_b * y
    last_c = cgh[tc - (CONV_K - 1):tc, :]

    row_d = lax.broadcasted_iota(jnp.int32, (tc, 3 * DN_WIDTH), 0)
    xq = pm_ref[:, OFF_QKV:OFF_Z]
    yq = _shifted(xq, td_scr, 3, row_d) * dw_ref[0:1, :]
    yq = yq + _shifted(xq, td_scr, 2, row_d) * dw_ref[1:2, :]
    yq = yq + _shifted(xq, td_scr, 1, row_d) * dw_ref[2:3, :]
    yq = yq + xq * dw_ref[3:4, :]
    _qkv_finish(yq, qkv_ref, ())
    last_d = xq[tc - (DN_CONV_K - 1):tc, :]

    gb_ref[...] = _gates(ps_ref[...], hp_ref[...])

    tc_scr[8 - (CONV_K - 1):8, :] = last_c
    td_scr[8 - (DN_CONV_K - 1):8, :] = last_d

    @pl.when(t == pl.num_programs(1) - 1)
    def _():
        ncb_ref[0] = last_c
        ndb_ref[0] = last_d


def _pre_prompt(pm, ps, conv_w, dn_conv_w, hp, conv_buf, dn_buf, bsz, seq, tc):
    n = pm.shape[0]
    nt = seq // tc
    rows = lambda w: pl.BlockSpec((tc, w), lambda b, t: (b * nt + t, 0))
    full = lambda a: pl.BlockSpec(a.shape, lambda b, t: (0,) * a.ndim)
    per_b = lambda k, w: pl.BlockSpec((1, k, w), lambda b, t: (b, 0, 0))
    return pl.pallas_call(
        _pre_prompt_kernel,
        grid=(bsz, nt),
        in_specs=[rows(P_MAIN), rows(LANES), full(conv_w), full(dn_conv_w), full(hp),
                  per_b(CONV_K - 1, CONV_WIDTH), per_b(DN_CONV_K - 1, 3 * DN_WIDTH)],
        out_specs=[rows(CONV_WIDTH), rows(3 * DN_WIDTH), rows(LANES),
                   per_b(CONV_K - 1, CONV_WIDTH), per_b(DN_CONV_K - 1, 3 * DN_WIDTH)],
        out_shape=[
            jax.ShapeDtypeStruct((n, CONV_WIDTH), F32),
            jax.ShapeDtypeStruct((n, 3 * DN_WIDTH), F32),
            jax.ShapeDtypeStruct((n, LANES), F32),
            jax.ShapeDtypeStruct((bsz, CONV_K - 1, CONV_WIDTH), F32),
            jax.ShapeDtypeStruct((bsz, DN_CONV_K - 1, 3 * DN_WIDTH), F32),
        ],
        scratch_shapes=[pltpu.VMEM((8, CONV_WIDTH), F32), pltpu.VMEM((8, 3 * DN_WIDTH), F32)],
        compiler_params=_cparams("arbitrary", "arbitrary"),
        name="pre_prompt",
    )(pm, ps, conv_w, dn_conv_w, hp, conv_buf, dn_buf)


def _pre_sample_kernel(pm_ref, ps_ref, cw_ref, dw_ref, hp_ref, cb_ref, db_ref,
                       yc_ref, qkv_ref, gb_ref, ncb_ref, ndb_ref):
    steps = pm_ref.shape[0]
    cgh = [pm_ref[t, :, OFF_CC:OFF_CH] * pm_ref[t, :, OFF_CH:OFF_QKV] for t in range(steps)]
    ext = [cb_ref[i] for i in range(CONV_K - 1)] + cgh
    for t in range(steps):
        y = ext[t] * cw_ref[0:1, :]
        for i in range(1, CONV_K):
            y = y + ext[t + i] * cw_ref[i:i + 1, :]
        yc_ref[t] = pm_ref[t, :, 0:OFF_CC] * y
    for i in range(CONV_K - 1):
        ncb_ref[i] = ext[len(ext) - (CONV_K - 1) + i]

    xq = [pm_ref[t, :, OFF_QKV:OFF_Z] for t in range(steps)]
    extq = [db_ref[i] for i in range(DN_CONV_K - 1)] + xq
    for t in range(steps):
        y = extq[t] * dw_ref[0:1, :]
        for i in range(1, DN_CONV_K):
            y = y + extq[t + i] * dw_ref[i:i + 1, :]
        _qkv_finish(y, qkv_ref, (t,))
        gb_ref[t] = _gates(ps_ref[t], hp_ref[...])
    for i in range(DN_CONV_K - 1):
        ndb_ref[i] = extq[len(extq) - (DN_CONV_K - 1) + i]


def _pre_sample(pm, ps, conv_w, dn_conv_w, hp, conv_buf_tm, dn_buf_tm, bsz, steps, bt):
    slab = lambda k, w: pl.BlockSpec((k, bt, w), lambda i: (0, i, 0))
    full = lambda a: pl.BlockSpec(a.shape, lambda i: (0,) * a.ndim)
    return pl.pallas_call(
        _pre_sample_kernel,
        grid=(bsz // bt,),
        in_specs=[slab(steps, P_MAIN), slab(steps, LANES), full(conv_w), full(dn_conv_w), full(hp),
                  slab(CONV_K - 1, CONV_WIDTH), slab(DN_CONV_K - 1, 3 * DN_WIDTH)],
        out_specs=[slab(steps, CONV_WIDTH), slab(steps, 3 * DN_WIDTH), slab(steps, LANES),
                   slab(CONV_K - 1, CONV_WIDTH), slab(DN_CONV_K - 1, 3 * DN_WIDTH)],
        out_shape=[
            jax.ShapeDtypeStruct((steps, bsz, CONV_WIDTH), F32),
            jax.ShapeDtypeStruct((steps, bsz, 3 * DN_WIDTH), F32),
            jax.ShapeDtypeStruct((steps, bsz, LANES), F32),
            jax.ShapeDtypeStruct((CONV_K - 1, bsz, CONV_WIDTH), F32),
            jax.ShapeDtypeStruct((DN_CONV_K - 1, bsz, 3 * DN_WIDTH), F32),
        ],
        compiler_params=_cparams("arbitrary"),
        name="pre_sample",
    )(pm.reshape(steps, bsz, P_MAIN), ps.reshape(steps, bsz, LANES), conv_w, dn_conv_w, hp,
      conv_buf_tm, dn_buf_tm)


def _cumsum_chunks(x, c):
    row = lax.broadcasted_iota(jnp.int32, x.shape, 0) & (c - 1)
    s = 1
    while s < c:
        x = x + jnp.where(row >= s, pltpu.roll(x, s, 0), 0.0)
        s *= 2
    return x


def _gdn_problems(qkv, gb, c):
    n = qkv.shape[0] // c
    gc = _cumsum_chunks(gb, c)
    q, k, v, g_b, beta_b = [], [], [], [], []
    for ci in range(n):
        rs = slice(ci * c, (ci + 1) * c)
        for h in range(DN_HEADS):
            ls = lambda base: slice(base + h * DN_HEAD_DIM, base + (h + 1) * DN_HEAD_DIM)
            q.append(qkv[rs, ls(0)])
            k.append(qkv[rs, ls(DN_WIDTH)])
            v.append(qkv[rs, ls(2 * DN_WIDTH)])
            g_b.append(jnp.broadcast_to(gc[rs, G_LANE + h:G_LANE + h + 1], (c, DN_HEAD_DIM)))
            beta_b.append(jnp.broadcast_to(gb[rs, BETA_LANE + h:BETA_LANE + h + 1], (c, DN_HEAD_DIM)))
    return tuple(jnp.stack(a) for a in (q, k, v, g_b, beta_b))


def _unit_lower_solve(m, rhs, c, small, mode):
    if small:
        sol = rhs
        for j in range(c - 1):
            sol = sol - m[:, :, j:j + 1] * sol[:, j:j + 1, :]
        return sol
    row = lax.broadcasted_iota(jnp.int32, m.shape, 1)
    col = lax.broadcasted_iota(jnp.int32, m.shape, 2)
    p = -m
    t = jnp.where(row == col, 1.0, 0.0) + p
    p = _mm(p, p, _BNN, "bf16")
    levels = int(math.log2(c))
    for lvl in range(1, levels):
        if lvl < levels - 1:
            y = _mm(jnp.concatenate([t, p], axis=1), p, _BNN, "bf16")
            t = t + y[:, :c]
            p = y[:, c:]
        else:
            t = t + _mm(t, p, _BNN, "bf16")
    x = _mm(t, rhs, _BNN, "bf16")
    resid = rhs - x - _mm(m, x, _BNN, mode)
    return x + _mm(t, resid, _BNN, "bf16")


def _gdn_intra(q, k, v, g_b, beta_b, *, c, small, mode_gram, mode_solve, mode_apply):
    n = q.shape[0]
    row = lax.broadcasted_iota(jnp.int32, (n, c, c), 1)
    col = lax.broadcasted_iota(jnp.int32, (n, c, c), 2)
    g_cc = g_b[:, :, :c]
    g_row = jnp.sum(jnp.where(row == col, g_cc, 0.0), axis=1, keepdims=True)
    gamma = jnp.exp(jnp.where(row >= col, g_cc - g_row, -jnp.inf))
    kbeta = k * beta_b
    exp_g = jnp.exp(g_b)
    gram = _mm(jnp.concatenate([kbeta, q], axis=1), k, _BNT, mode_gram)
    m = jnp.where(row > col, gram[:, :c] * gamma, 0.0)
    attn = gram[:, c:] * gamma
    rhs = jnp.concatenate([kbeta * exp_g, v * beta_b], axis=-1)
    sol = _unit_lower_solve(m, rhs, c, small, mode_solve)
    g_last = g_b[:, c - 1:c, :]
    k_dec = k * jnp.exp(g_last - g_b)
    kd = _mm(k_dec, sol, _BTN, mode_apply)
    at = _mm(attn, sol, _BNN, mode_apply)
    q_t = q * exp_g - at[..., :DN_HEAD_DIM]
    return kd[..., :DN_HEAD_DIM], kd[..., DN_HEAD_DIM:], q_t, at[..., DN_HEAD_DIM:], jnp.exp(g_last)


def _gdn_state_step(a_mat, b_mat, q_t, o_intra, d_last, s, mode):
    r = _mm(jnp.concatenate([a_mat, q_t], axis=1), s, _BNN, mode)
    s_new = s * d_last - r[:, :DN_HEAD_DIM] + b_mat
    return r[:, DN_HEAD_DIM:] + o_intra, s_new


_PROMPT_MODES = dict(mode_gram="x3", mode_solve="x3", mode_apply="bf16")
_PROMPT_STATE_MODE = "bf16"
_SAMPLE_MODES = dict(mode_gram="bf16", mode_solve="bf16", mode_apply="bf16")
_SAMPLE_STATE_MODE = "bf16"


def _delta_prompt_kernel(qkv_ref, gb_ref, s0_ref, o_ref, sn_ref, s_scr):
    t = pl.program_id(1)
    tc = qkv_ref.shape[0]

    @pl.when(t == 0)
    def _():
        s_scr[...] = s0_ref[0]

    probs = _gdn_problems(qkv_ref[...], gb_ref[...], DN_CHUNK)
    a_mat, b_mat, q_t, o_intra, d_last = _gdn_intra(*probs, c=DN_CHUNK, small=False, **_PROMPT_MODES)
    s = s_scr[...]
    for ci in range(tc // DN_CHUNK):
        ps = slice(ci * DN_HEADS, (ci + 1) * DN_HEADS)
        o, s = _gdn_state_step(a_mat[ps], b_mat[ps], q_t[ps], o_intra[ps], d_last[ps], s,
                               _PROMPT_STATE_MODE)
        for h in range(DN_HEADS):
            o_ref[ci * DN_CHUNK:(ci + 1) * DN_CHUNK, h * DN_HEAD_DIM:(h + 1) * DN_HEAD_DIM] = o[h]
    s_scr[...] = s

    @pl.when(t == pl.num_programs(1) - 1)
    def _():
        sn_ref[0] = s


def _delta_prompt(qkv, gb, s0, bsz, seq, tc):
    n = qkv.shape[0]
    nt = seq // tc
    rows = lambda w: pl.BlockSpec((tc, w), lambda b, t: (b * nt + t, 0))
    state = pl.BlockSpec((1, DN_HEADS, DN_HEAD_DIM, DN_HEAD_DIM), lambda b, t: (b, 0, 0, 0))
    return pl.pallas_call(
        _delta_prompt_kernel,
        grid=(bsz, nt),
        in_specs=[rows(3 * DN_WIDTH), rows(LANES), state],
        out_specs=[rows(DN_WIDTH), state],
        out_shape=[
            jax.ShapeDtypeStruct((n, DN_WIDTH), F32),
            jax.ShapeDtypeStruct((bsz, DN_HEADS, DN_HEAD_DIM, DN_HEAD_DIM), F32),
        ],
        scratch_shapes=[pltpu.VMEM((DN_HEADS, DN_HEAD_DIM, DN_HEAD_DIM), F32)],
        compiler_params=_cparams("arbitrary", "arbitrary"),
        name="delta_prompt",
    )(qkv, gb, s0)


def _delta_sample_kernel(qkv_ref, gb_ref, s0_ref, *rest):
    o_ref, sn_ref = rest[-2:]
    bt = qkv_ref.shape[0]
    c = SAMPLE_CHUNK
    probs = _gdn_problems(qkv_ref[...].reshape(bt * c, 3 * DN_WIDTH), gb_ref[...].reshape(bt * c, LANES), c)
    a_mat, b_mat, q_t, o_intra, d_last = _gdn_intra(*probs, c=c, small=True, **_SAMPLE_MODES)
    s = s0_ref[0].reshape(bt * DN_HEADS, DN_HEAD_DIM, DN_HEAD_DIM)
    o, s = _gdn_state_step(a_mat, b_mat, q_t, o_intra, d_last, s, _SAMPLE_STATE_MODE)
    for b in range(bt):
        for h in range(DN_HEADS):
            o_ref[b, :, h * DN_HEAD_DIM:(h + 1) * DN_HEAD_DIM] = o[b * DN_HEADS + h]
    sn_ref[0] = s.reshape(bt, DN_HEADS, DN_HEAD_DIM, DN_HEAD_DIM)


def _delta_sample(qkv_b, gb_b, state_all, layer, carried, bt):
    bsz = qkv_b.shape[0]
    blk = lambda w: pl.BlockSpec((bt, SAMPLE_CHUNK, w), lambda i: (i, 0, 0))
    state = pl.BlockSpec((1, bt, DN_HEADS, DN_HEAD_DIM, DN_HEAD_DIM), lambda i: (layer, i, 0, 0, 0))
    in_specs, args, aliases = [blk(3 * DN_WIDTH), blk(LANES), state], [qkv_b, gb_b, state_all], {}
    if carried is not None:
        in_specs.append(pl.BlockSpec(memory_space=pl.ANY))
        args.append(carried)
        aliases = {3: 1}
    return pl.pallas_call(
        _delta_sample_kernel,
        grid=(bsz // bt,),
        in_specs=in_specs,
        out_specs=[blk(DN_WIDTH), state],
        out_shape=[
            jax.ShapeDtypeStruct((bsz, SAMPLE_CHUNK, DN_WIDTH), F32),
            jax.ShapeDtypeStruct(state_all.shape, F32),
        ],
        input_output_aliases=aliases,
        compiler_params=_cparams("arbitrary"),
        name="delta_sample",
    )(*args)


def _outproj_kernel(yc_ref, o_ref, z_ref, x_ref, g1_ref, nw_ref, w_ref, lg_ref, lb_ref, out_ref):
    o = o_ref[...]
    parts = []
    for h in range(DN_HEADS):
        oh = o[:, h * DN_HEAD_DIM:(h + 1) * DN_HEAD_DIM]
        parts.append(oh * lax.rsqrt(jnp.mean(oh * oh, -1, keepdims=True) + RMS_EPS) * nw_ref[...])
    og = jnp.concatenate(parts, axis=-1) * _silu(z_ref[...])
    mixed = jnp.concatenate([yc_ref[...], og], axis=-1).astype(BF16)
    m = jnp.dot(mixed, w_ref[...], preferred_element_type=F32)
    r = DEEPNORM_ALPHA * x_ref[...] + (1.0 + g1_ref[0]) * m
    out_ref[...] = _layer_norm(r, lg_ref[...], lb_ref[...])


def _outproj(yc, o, pm, x, mod3, tpg, norm_w, w_out, ln_g, ln_b, tm):
    n = x.shape[0]
    vec = lambda w: pl.BlockSpec((1, w), lambda i: (0, 0))
    return pl.pallas_call(
        _outproj_kernel,
        grid=(n // tm,),
        in_specs=[
            pl.BlockSpec((tm, CONV_WIDTH), lambda i: (i, 0)),
            pl.BlockSpec((tm, DN_WIDTH), lambda i: (i, 0)),
            pl.BlockSpec((tm, DN_WIDTH), lambda i: (i, OFF_Z // DN_WIDTH)),
            pl.BlockSpec((tm, D_MODEL), lambda i: (i, 0)),
            _mod_spec(mod3, 2, tpg),
            vec(DN_HEAD_DIM),
            pl.BlockSpec((D_MODEL, D_MODEL), lambda i: (0, 0)),
            vec(D_MODEL), vec(D_MODEL),
        ],
        out_specs=pl.BlockSpec((tm, D_MODEL), lambda i: (i, 0)),
        out_shape=jax.ShapeDtypeStruct((n, D_MODEL), F32),
        compiler_params=_cparams("arbitrary"),
        name="outproj",
    )(yc, o, pm, x, mod3, norm_w, w_out, ln_g, ln_b)


R_E1, R_E2, R_G1, R_G2, R_R1, R_R2 = range(6)


def _router_kernel(x_ref, sc_ref, sh_ref, wr_ref, br_ref, route_ref, cnt_ref):
    tm = x_ref.shape[0]
    u = x_ref[...] * (1.0 + sc_ref[0]) + sh_ref[0]
    logits = _mm(u, wr_ref[...], mode="x3") + br_ref[...]
    lane = lax.broadcasted_iota(jnp.int32, logits.shape, 1)
    valid = lane < N_EXPERTS
    logits = jnp.where(valid, logits, -jnp.inf)
    ex = jnp.exp(logits - jnp.max(logits, -1, keepdims=True))
    probs = jnp.where(valid, ex / jnp.sum(ex, -1, keepdims=True), -1.0)
    p1 = jnp.max(probs, -1, keepdims=True)
    i1 = jnp.min(jnp.where(probs == p1, lane, LANES), -1, keepdims=True)
    rest = jnp.where(lane == i1, -1.0, probs)
    p2 = jnp.max(rest, -1, keepdims=True)
    i2 = jnp.min(jnp.where(rest == p2, lane, LANES), -1, keepdims=True)
    tot = p1 + p2
    sel = jnp.where(lane == i1, 1.0, 0.0) + jnp.where(lane == i2, 1.0, 0.0)
    incl = _cumsum_chunks(sel, tm)
    excl = incl - sel
    r1 = jnp.sum(jnp.where(lane == i1, excl, 0.0), -1, keepdims=True)
    r2 = jnp.sum(jnp.where(lane == i2, excl, 0.0), -1, keepdims=True)
    cnt_ref[0] = incl[tm - 1:tm, :]
    route = jnp.zeros_like(logits)
    for j, val in enumerate((i1.astype(F32), i2.astype(F32), p1 / tot, p2 / tot, r1, r2)):
        route = jnp.where(lane == j, val, route)
    route_ref[...] = route


def _router(x, mod3, tpg, w_router, b_router, tm):
    n = x.shape[0]
    return pl.pallas_call(
        _router_kernel,
        grid=(n // tm,),
        in_specs=[
            pl.BlockSpec((tm, D_MODEL), lambda i: (i, 0)),
            _mod_spec(mod3, 4, tpg),
            _mod_spec(mod3, 3, tpg),
            pl.BlockSpec((D_MODEL, LANES), lambda i: (0, 0)),
            pl.BlockSpec((1, LANES), lambda i: (0, 0)),
        ],
        out_specs=[pl.BlockSpec((tm, LANES), lambda i: (i, 0)),
                   pl.BlockSpec((1, 1, LANES), lambda i: (i, 0, 0))],
        out_shape=[jax.ShapeDtypeStruct((n, LANES), F32),
                   jax.ShapeDtypeStruct((n // tm, 1, LANES), F32)],
        compiler_params=_cparams("arbitrary"),
        name="router",
    )(x, mod3, mod3, w_router, b_router)


def _dispatch_plan(route, counts, tm, tme):
    n = route.shape[0]
    cnt = counts[:, 0, :N_EXPERTS].astype(jnp.int32)
    tile_off = jnp.cumsum(cnt, axis=0) - cnt
    tot = jnp.sum(cnt, axis=0)
    gsz = (tot + tme - 1) // tme * tme
    gend = jnp.cumsum(gsz)
    goff = gend - gsz
    base = jnp.repeat(goff[None, :] + tile_off, tm, axis=0)
    ids = jnp.arange(N_EXPERTS, dtype=jnp.int32)[None, :]
    col = lambda j: route[:, j].astype(jnp.int32)
    pick = lambda e: jnp.sum(jnp.where(ids == e[:, None], base, 0), axis=1)
    slot1 = pick(col(R_E1)) + col(R_R1)
    slot2 = pick(col(R_E2)) + col(R_R2)
    n_tiles = 2 * n // tme + N_EXPERTS
    start = jnp.arange(n_tiles, dtype=jnp.int32) * tme
    valid = (start < gend[-1]).astype(jnp.int32)
    blk = jnp.minimum(jnp.arange(n_tiles, dtype=jnp.int32), gend[-1] // tme - 1)
    tile_e = jnp.sum((blk * tme)[:, None] >= gend[None, :], axis=1).astype(jnp.int32)
    zoff = jnp.where(gsz > 0, gend - tme, (n_tiles + jnp.arange(N_EXPERTS, dtype=jnp.int32)) * tme)
    return slot1, slot2, zoff, tile_e, valid, blk, n_tiles


def _dispatch_kernel(s1_ref, s2_ref, zoff_ref, x_ref, sc_ref, sh_ref, ug_hbm, ubuf, zbuf, sem):
    i = pl.program_id(0)
    tm = x_ref.shape[0]
    tme = zbuf.shape[0]

    @pl.when(i == 0)
    def _():
        zbuf[...] = jnp.zeros_like(zbuf)
        fill = lambda e: pltpu.make_async_copy(
            zbuf, ug_hbm.at[pl.ds(pl.multiple_of(zoff_ref[e], 8), tme)], sem.at[2])
        for e in range(N_EXPERTS):
            fill(e).start()
        for e in range(N_EXPERTS):
            fill(e).wait()

    ubuf[...] = x_ref[...] * (1.0 + sc_ref[0]) + sh_ref[0]
    base = i * tm

    def issue(r, carry):
        row = ubuf.at[pl.ds(r, 1)]
        pltpu.make_async_copy(row, ug_hbm.at[pl.ds(s1_ref[base + r], 1)], sem.at[0]).start()
        pltpu.make_async_copy(row, ug_hbm.at[pl.ds(s2_ref[base + r], 1)], sem.at[1]).start()
        return carry

    lax.fori_loop(0, tm, issue, 0, unroll=8)
    pltpu.make_async_copy(ubuf, ug_hbm.at[pl.ds(0, tm)], sem.at[0]).wait()
    pltpu.make_async_copy(ubuf, ug_hbm.at[pl.ds(0, tm)], sem.at[1]).wait()


def _dispatch(x, mod3, tpg, slot1, slot2, zoff, tm, tme, rows):
    n = x.shape[0]
    return pl.pallas_call(
        _dispatch_kernel,
        grid_spec=pltpu.PrefetchScalarGridSpec(
            num_scalar_prefetch=3,
            grid=(n // tm,),
            in_specs=[
                pl.BlockSpec((tm, D_MODEL), lambda i, *_: (i, 0)),
                _mod_spec(mod3, 4, tpg),
                _mod_spec(mod3, 3, tpg),
            ],
            out_specs=pl.BlockSpec(memory_space=pl.ANY),
            scratch_shapes=[pltpu.VMEM((tm, D_MODEL), F32), pltpu.VMEM((tme, D_MODEL), F32),
                            pltpu.SemaphoreType.DMA((3,))],
        ),
        out_shape=jax.ShapeDtypeStruct((rows, D_MODEL), F32),
        compiler_params=_cparams("arbitrary"),
        name="dispatch",
    )(slot1, slot2, zoff, x, mod3, mod3)


def _expert_kernel(te_ref, tv_ref, tb_ref, ug_ref, wg_ref, wu_ref, wd_ref, ys_ref):
    i = pl.program_id(0)

    @pl.when(tv_ref[i] == 1)
    def _():
        u = ug_ref[...].astype(BF16)
        h = _silu(jnp.dot(u, wg_ref[0], preferred_element_type=F32)) * jnp.dot(
            u, wu_ref[0], preferred_element_type=F32)
        ys_ref[...] = jnp.dot(h.astype(BF16), wd_ref[0], preferred_element_type=F32)

    @pl.when(tv_ref[i] == 0)
    def _():
        ys_ref[...] = jnp.zeros_like(ys_ref)


def _experts(ug, tile_e, valid, blk, w_gate, w_up, w_down, tme, n_tiles):
    d_e = w_gate.shape[-1]
    return pl.pallas_call(
        _expert_kernel,
        grid_spec=pltpu.PrefetchScalarGridSpec(
            num_scalar_prefetch=3,
            grid=(n_tiles,),
            in_specs=[
                pl.BlockSpec((tme, D_MODEL), lambda i, te, tv, tb: (tb[i], 0)),
                pl.BlockSpec((1, D_MODEL, d_e), lambda i, te, tv, tb: (te[i], 0, 0)),
                pl.BlockSpec((1, D_MODEL, d_e), lambda i, te, tv, tb: (te[i], 0, 0)),
                pl.BlockSpec((1, d_e, D_MODEL), lambda i, te, tv, tb: (te[i], 0, 0)),
            ],
            out_specs=pl.BlockSpec((tme, D_MODEL), lambda i, te, tv, tb: (i, 0)),
        ),
        out_shape=jax.ShapeDtypeStruct((n_tiles * tme, D_MODEL), F32),
        compiler_params=_cparams("arbitrary"),
        name="experts",
    )(tile_e, valid, blk, ug, w_gate, w_up, w_down)


def _combine_kernel(s1_ref, s2_ref, ys_hbm, route_ref, x_ref, g2_ref, lg_ref, lb_ref, out_ref,
                    y1buf, y2buf, sem):
    i = pl.program_id(0)
    tm = x_ref.shape[0]
    base = i * tm

    def issue(r, carry):
        pltpu.make_async_copy(ys_hbm.at[pl.ds(s1_ref[base + r], 1)], y1buf.at[pl.ds(r, 1)], sem.at[0]).start()
        pltpu.make_async_copy(ys_hbm.at[pl.ds(s2_ref[base + r], 1)], y2buf.at[pl.ds(r, 1)], sem.at[1]).start()
        return carry

    lax.fori_loop(0, tm, issue, 0, unroll=8)
    pltpu.make_async_copy(ys_hbm.at[pl.ds(0, tm)], y1buf, sem.at[0]).wait()
    pltpu.make_async_copy(ys_hbm.at[pl.ds(0, tm)], y2buf, sem.at[1]).wait()
    route = route_ref[...]
    f = route[:, R_G1:R_G1 + 1] * y1buf[...] + route[:, R_G2:R_G2 + 1] * y2buf[...]
    r = DEEPNORM_ALPHA * x_ref[...] + (1.0 + g2_ref[0]) * f
    out_ref[...] = _layer_norm(r, lg_ref[...], lb_ref[...])


def _combine(ys, slot1, slot2, route, x, mod3, tpg, ln_g, ln_b, tm):
    n = x.shape[0]
    vec = lambda w: pl.BlockSpec((1, w), lambda i, *_: (0, 0))
    return pl.pallas_call(
        _combine_kernel,
        grid_spec=pltpu.PrefetchScalarGridSpec(
            num_scalar_prefetch=2,
            grid=(n // tm,),
            in_specs=[
                pl.BlockSpec(memory_space=pl.ANY),
                pl.BlockSpec((tm, LANES), lambda i, *_: (i, 0)),
                pl.BlockSpec((tm, D_MODEL), lambda i, *_: (i, 0)),
                _mod_spec(mod3, 5, tpg),
                vec(D_MODEL), vec(D_MODEL),
            ],
            out_specs=pl.BlockSpec((tm, D_MODEL), lambda i, *_: (i, 0)),
            scratch_shapes=[pltpu.VMEM((tm, D_MODEL), F32), pltpu.VMEM((tm, D_MODEL), F32),
                            pltpu.SemaphoreType.DMA((2,))],
        ),
        out_shape=jax.ShapeDtypeStruct((n, D_MODEL), F32),
        compiler_params=_cparams("arbitrary"),
        name="combine",
    )(slot1, slot2, ys, route, x, mod3, ln_g, ln_b)


def _moe(x, mod3, tpg, w_router, b_router, w_gate, w_up, w_down, ln_g, ln_b, tm, tme):
    route, counts = _router(x, mod3, tpg, w_router, b_router, tm)
    slot1, slot2, zoff, tile_e, valid, blk, n_tiles = _dispatch_plan(route, counts, tm, tme)
    ug = _dispatch(x, mod3, tpg, slot1, slot2, zoff, tm, tme, (n_tiles + N_EXPERTS) * tme)
    ys = _experts(ug, tile_e, valid, blk, w_gate, w_up, w_down, tme, n_tiles)
    return _combine(ys, slot1, slot2, route, x, mod3, tpg, ln_g, ln_b, tm)


def _ffn_kernel(x_ref, sc_ref, sh_ref, g2_ref, wg_ref, wu_ref, wd_ref, lg_ref, lb_ref,
                out_ref, u_scr, acc_scr):
    f = pl.program_id(1)

    @pl.when(f == 0)
    def _():
        u_scr[...] = (x_ref[...] * (1.0 + sc_ref[0]) + sh_ref[0]).astype(BF16)
        acc_scr[...] = jnp.zeros_like(acc_scr)

    u = u_scr[...]
    h = _silu(jnp.dot(u, wg_ref[...], preferred_element_type=F32)) * jnp.dot(
        u, wu_ref[...], preferred_element_type=F32)
    acc_scr[...] += jnp.dot(h.astype(BF16), wd_ref[...], preferred_element_type=F32)

    @pl.when(f == pl.num_programs(1) - 1)
    def _():
        r = DEEPNORM_ALPHA * x_ref[...] + (1.0 + g2_ref[0]) * acc_scr[...]
        out_ref[...] = _layer_norm(r, lg_ref[...], lb_ref[...])


def _ffn(x, mod3, tpg, w_gate, w_up, w_down, ln_g, ln_b, tm, tf):
    n = x.shape[0]
    d_ff = w_gate.shape[-1]
    vec = lambda w: pl.BlockSpec((1, w), lambda i, f: (0, 0))
    return pl.pallas_call(
        _ffn_kernel,
        grid=(n // tm, d_ff // tf),
        in_specs=[
            pl.BlockSpec((tm, D_MODEL), lambda i, f: (i, 0)),
            _mod_spec(mod3, 4, tpg),
            _mod_spec(mod3, 3, tpg),
            _mod_spec(mod3, 5, tpg),
            pl.BlockSpec((D_MODEL, tf), lambda i, f: (0, f)),
            pl.BlockSpec((D_MODEL, tf), lambda i, f: (0, f)),
            pl.BlockSpec((tf, D_MODEL), lambda i, f: (f, 0)),
            vec(D_MODEL), vec(D_MODEL),
        ],
        out_specs=pl.BlockSpec((tm, D_MODEL), lambda i, f: (i, 0)),
        out_shape=jax.ShapeDtypeStruct((n, D_MODEL), F32),
        scratch_shapes=[pltpu.VMEM((tm, D_MODEL), BF16), pltpu.VMEM((tm, D_MODEL), F32)],
        compiler_params=_cparams("arbitrary", "arbitrary"),
        name="ffn_dense",
    )(x, mod3, mod3, mod3, w_gate, w_up, w_down, ln_g, ln_b)


def _trunk(x, mod3s, tpg, prm, tm, tme, mixer_fn):
    states = []
    for l in range(DEPTH):
        mod3 = mod3s[l]
        pm, ps = _inproj(x, mod3, tpg, prm["w_main"][l], prm["w_small"][l], tm)
        yc, o, st = mixer_fn(l, pm, ps)
        states.append(st)
        x = _outproj(yc, o, pm, x, mod3, tpg, prm["norm_w"][l], prm["w_out"][l],
                     prm["ln_g"][l, 0:1], prm["ln_b"][l, 0:1], tm)
        j = l // 2
        if l % 2 == 0:
            x = _ffn(x, mod3, tpg, prm["w_ff_gate"][j], prm["w_ff_up"][j], prm["w_ff_down"][j],
                     prm["ln_g"][l, 1:2], prm["ln_b"][l, 1:2], tm, prm["w_ff_gate"].shape[-1] // 2)
        else:
            x = _moe(x, mod3, tpg, prm["w_router"][j], prm["b_router"][j], prm["w_exp_gate"][j],
                     prm["w_exp_up"][j], prm["w_exp_down"][j], prm["ln_g"][l, 1:2], prm["ln_b"][l, 1:2],
                     tm, tme)
    return x, states


def kernel(x_prompt, x_sample, state_conv, state_dn_conv, state_dn, c_prompt, c_sample, w_in, conv_w,
           dn_conv_w, a_log, dt_bias, dn_norm_w, w_out, w_ada, b_ada, ln_g, ln_b, w_ff_gate, w_ff_up,
           w_ff_down, w_router, b_router, w_exp_gate, w_exp_up, w_exp_down):
    bp, seq, d = x_prompt.shape
    bs, steps, _ = x_sample.shape
    pad_lanes = LANES - N_EXPERTS
    hp = jnp.zeros((DEPTH, 2, LANES), F32)
    hp = hp.at[:, 0, G_LANE:].set(a_log).at[:, 1, G_LANE:].set(dt_bias)
    prm = {
        "w_main": w_in[:, :, :P_MAIN].astype(BF16),
        "w_small": w_in[:, :, w_in.shape[-1] - LANES:],
        "norm_w": dn_norm_w.reshape(DEPTH, 1, DN_HEAD_DIM),
        "w_out": w_out.astype(BF16),
        "ln_g": ln_g, "ln_b": ln_b,
        "w_ff_gate": w_ff_gate.astype(BF16), "w_ff_up": w_ff_up.astype(BF16),
        "w_ff_down": w_ff_down.astype(BF16),
        "w_router": jnp.pad(w_router, ((0, 0), (0, 0), (0, pad_lanes))),
        "b_router": jnp.pad(b_router, ((0, 0), (0, pad_lanes)))[:, None, :],
        "w_exp_gate": w_exp_gate.astype(BF16), "w_exp_up": w_exp_up.astype(BF16),
        "w_exp_down": w_exp_down.astype(BF16),
    }

    mod = _ada(jnp.concatenate([c_prompt, c_sample], axis=0), w_ada, b_ada)
    mod_p = [mod[l, :bp].reshape(bp, 1, 6 * d) for l in range(DEPTH)]
    mod_s = [mod[l, bp:].reshape(1, bs, 6 * d) for l in range(DEPTH)]

    tm_p, tc_pre, tc_delta = 512, 256, 256
    zc = jnp.zeros((bp, CONV_K - 1, CONV_WIDTH), F32)
    zd = jnp.zeros((bp, DN_CONV_K - 1, 3 * DN_WIDTH), F32)
    zs = jnp.zeros((bp, DN_HEADS, DN_HEAD_DIM, DN_HEAD_DIM), F32)

    def mixer_prompt(l, pm, ps):
        yc, qkv, gb, ncb, ndb = _pre_prompt(pm, ps, conv_w[l], dn_conv_w[l], hp[l], zc, zd, bp, seq, tc_pre)
        o, s_new = _delta_prompt(qkv, gb, zs, bp, seq, tc_delta)
        return yc, o, (ncb, ndb, s_new)

    y_p, st_p = _trunk(x_prompt.reshape(bp * seq, d), mod_p, seq // tm_p, prm, tm_p, 512, mixer_prompt)

    bt_pre, bt_delta = 32, 8
    dn_states = []

    def mixer_sample(l, pm, ps):
        cb_tm = jnp.transpose(state_conv[l], (1, 0, 2))
        db_tm = jnp.transpose(state_dn_conv[l], (1, 0, 2))
        yc, qkv, gb, ncb, ndb = _pre_sample(pm, ps, conv_w[l], dn_conv_w[l], hp[l], cb_tm, db_tm,
                                            bs, steps, bt_pre)
        to_b = lambda a: jnp.pad(jnp.transpose(a, (1, 0, 2)), ((0, 0), (0, SAMPLE_CHUNK - steps), (0, 0)))
        o_b, s_all = _delta_sample(to_b(qkv), to_b(gb), state_dn, l, dn_states[-1] if dn_states else None,
                                   bt_delta)
        dn_states.append(s_all)
        o = jnp.transpose(o_b[:, :steps], (1, 0, 2)).reshape(steps * bs, DN_WIDTH)
        return (yc.reshape(steps * bs, CONV_WIDTH), o,
                (jnp.transpose(ncb, (1, 0, 2)), jnp.transpose(ndb, (1, 0, 2))))

    x_s = jnp.transpose(x_sample, (1, 0, 2)).reshape(steps * bs, d)
    y_s, st_s = _trunk(x_s, mod_s, steps, prm, bs, 128, mixer_sample)
    y_s = jnp.transpose(y_s.reshape(steps, bs, d), (1, 0, 2))

    stack = lambda sts, i: jnp.stack([s[i] for s in sts])
    return (y_p.reshape(bp, seq, d), y_s,
            stack(st_p, 0), stack(st_p, 1), stack(st_p, 2),
            stack(st_s, 0), stack(st_s, 1), dn_states[-1])
```

```python
import functools
import math

import jax
import jax.numpy as jnp
from jax import lax
from jax.experimental import pallas as pl
from jax.experimental.pallas import tpu as pltpu

F32 = jnp.float32
BF16 = jnp.bfloat16

D_MODEL = 1024
DEPTH = 2
CONV_WIDTH = 512
CONV_K = 3
DN_HEADS = 4
DN_HEAD_DIM = 128
DN_WIDTH = DN_HEADS * DN_HEAD_DIM
DN_CONV_K = 4
DN_CHUNK = 64
N_EXPERTS = 8
DEEPNORM_ALPHA = (2.0 * DEPTH) ** 0.25
LN_EPS = 1e-5
RMS_EPS = 1e-6

OFF_CC = CONV_WIDTH
OFF_CH = 2 * CONV_WIDTH
OFF_QKV = 3 * CONV_WIDTH
OFF_Z = OFF_QKV + 3 * DN_WIDTH
P_MAIN = OFF_Z + DN_WIDTH
LANES = 128
SUBLANES = 8
BETA_LANE = LANES - 2 * DN_HEADS
G_LANE = LANES - DN_HEADS
SAMPLE_CHUNK = 8

VMEM_LIMIT = 56 * 1024 * 1024


def _cparams(*sem):
    return pltpu.CompilerParams(dimension_semantics=sem, vmem_limit_bytes=VMEM_LIMIT)


def _silu(x):
    return x * jax.nn.sigmoid(x)


def _layer_norm(r, g, b):
    mu = jnp.mean(r, -1, keepdims=True)
    xc = r - mu
    var = jnp.mean(xc * xc, -1, keepdims=True)
    return xc * lax.rsqrt(var + LN_EPS) * g + b


def _split_bf16(a):
    hi = a.astype(BF16)
    lo = (a - hi.astype(F32)).astype(BF16)
    return hi, lo


_NN = (((1,), (0,)), ((), ()))
_NT = (((1,), (1,)), ((), ()))
_BNN = (((2,), (1,)), ((0,), (0,)))
_BNT = (((2,), (2,)), ((0,), (0,)))
_BTN = (((1,), (1,)), ((0,), (0,)))


def _mm(a, b, dims=_NN, mode="bf16"):
    if mode == "f32":
        return lax.dot_general(a, b, dims, precision=lax.Precision.HIGHEST, preferred_element_type=F32)
    if mode == "bf16":
        return lax.dot_general(a.astype(BF16), b.astype(BF16), dims, preferred_element_type=F32)
    ah, al = _split_bf16(a)
    bh, bl = _split_bf16(b)
    d = functools.partial(lax.dot_general, dimension_numbers=dims, preferred_element_type=F32)
    return d(ah, bh) + (d(ah, bl) + d(al, bh))


def _ada_kernel(c_ref, w_ref, b_ref, o_ref):
    s = _silu(c_ref[...]).astype(BF16)
    o_ref[0] = jnp.dot(s, w_ref[0].astype(BF16), preferred_element_type=F32) + b_ref[0]


def _ada(c_all, w_ada, b_ada):
    rows = c_all.shape[0]
    tn = 1536
    return pl.pallas_call(
        _ada_kernel,
        grid=(DEPTH, 6 * D_MODEL // tn),
        in_specs=[
            pl.BlockSpec((rows, D_MODEL), lambda l, j: (0, 0)),
            pl.BlockSpec((1, D_MODEL, tn), lambda l, j: (l, 0, j)),
            pl.BlockSpec((1, 1, tn), lambda l, j: (l, 0, j)),
        ],
        out_specs=pl.BlockSpec((1, rows, tn), lambda l, j: (l, 0, j)),
        out_shape=jax.ShapeDtypeStruct((DEPTH, rows, 6 * D_MODEL), F32),
        compiler_params=_cparams("arbitrary", "arbitrary"),
        name="ada",
    )(c_all, w_ada, b_ada.reshape(DEPTH, 1, 6 * D_MODEL))


def _mod_spec(mod3, chunk, tiles_per_group):
    r = mod3.shape[1]
    return pl.BlockSpec((1, r, D_MODEL), lambda i, *_: (i // tiles_per_group, 0, chunk))


def _inproj_kernel(x_ref, sc_ref, sh_ref, wm_ref, ws_ref, pm_ref, ps_ref):
    u = x_ref[...] * (1.0 + sc_ref[0]) + sh_ref[0]
    pm_ref[...] = lax.dot_general(u.astype(BF16), wm_ref[...], _NT, preferred_element_type=F32)
    ps_ref[...] = _mm(u, ws_ref[...], _NT, mode="x3")


def _inproj(x, mod3, tpg, w_main, w_small, tm):
    n = x.shape[0]
    return pl.pallas_call(
        _inproj_kernel,
        grid=(n // tm,),
        in_specs=[
            pl.BlockSpec((tm, D_MODEL), lambda i: (i, 0)),
            _mod_spec(mod3, 1, tpg),
            _mod_spec(mod3, 0, tpg),
            pl.BlockSpec((P_MAIN, D_MODEL), lambda i: (0, 0)),
            pl.BlockSpec((LANES, D_MODEL), lambda i: (0, 0)),
        ],
        out_specs=[
            pl.BlockSpec((tm, P_MAIN), lambda i: (i, 0)),
            pl.BlockSpec((tm, LANES), lambda i: (i, 0)),
        ],
        out_shape=[
            jax.ShapeDtypeStruct((n, P_MAIN), F32),
            jax.ShapeDtypeStruct((n, LANES), F32),
        ],
        compiler_params=_cparams("arbitrary"),
        name="inproj",
    )(x, mod3, mod3, w_main, w_small)


def _gates(ps, hp):
    lane = lax.broadcasted_iota(jnp.int32, ps.shape, 1)
    beta = jax.nn.sigmoid(ps)
    g = -jnp.exp(hp[0:1, :]) * jax.nn.softplus(ps + hp[1:2, :])
    return jnp.where(lane >= G_LANE, g, jnp.where(lane >= BETA_LANE, beta, 0.0))


def _qkv_finish(y, out_ref, idx):
    y = _silu(y)
    for h in range(3 * DN_HEADS):
        seg = y[:, h * DN_HEAD_DIM:(h + 1) * DN_HEAD_DIM]
        if h < 2 * DN_HEADS:
            seg = seg * lax.rsqrt(jnp.sum(seg * seg, -1, keepdims=True) + RMS_EPS)
            if h < DN_HEADS:
                seg = seg * (DN_HEAD_DIM ** -0.5)
        out_ref[idx + (slice(None), slice(h * DN_HEAD_DIM, (h + 1) * DN_HEAD_DIM))] = seg


def _shifted(x, tail_ref, i, row):
    s = pltpu.roll(x, i, 0)
    for r in range(i):
        s = jnp.where(row == r, tail_ref[8 - i + r:8 - i + r + 1, :], s)
    return s


def _pre_prompt_kernel(pm_ref, ps_ref, cw_ref, dw_ref, hp_ref, cb_ref, db_ref,
                       yc_ref, qkv_ref, gb_ref, ncb_ref, ndb_ref, tc_scr, td_scr):
    t = pl.program_id(1)
    tc = pm_ref.shape[0]

    @pl.when(t == 0)
    def _():
        tc_scr[8 - (CONV_K - 1):8, :] = cb_ref[0]
        td_scr[8 - (DN_CONV_K - 1):8, :] = db_ref[0]

    row_c = lax.broadcasted_iota(jnp.int32, (tc, CONV_WIDTH), 0)
    gate_b = pm_ref[:, 0:OFF_CC]
    cgh = pm_ref[:, OFF_CC:OFF_CH] * pm_ref[:, OFF_CH:OFF_QKV]
    y = _shifted(cgh, tc_scr, 2, row_c) * cw_ref[0:1, :]
    y = y + _shifted(cgh, tc_scr, 1, row_c) * cw_ref[1:2, :]
    y = y + cgh * cw_ref[2:3, :]
    yc_ref[...] = gate_b * y
    last_c = cgh[tc - (CONV_K - 1):tc, :]

    row_d = lax.broadcasted_iota(jnp.int32, (tc, 3 * DN_WIDTH), 0)
    xq = pm_ref[:, OFF_QKV:OFF_Z]
    yq = _shifted(xq, td_scr, 3, row_d) * dw_ref[0:1, :]
    yq = yq + _shifted(xq, td_scr, 2, row_d) * dw_ref[1:2, :]
    yq = yq + _shifted(xq, td_scr, 1, row_d) * dw_ref[2:3, :]
    yq = yq + xq * dw_ref[3:4, :]
    _qkv_finish(yq, qkv_ref, ())
    last_d = xq[tc - (DN_CONV_K - 1):tc, :]

    gb_ref[...] = _gates(ps_ref[...], hp_ref[...])

    tc_scr[8 - (CONV_K - 1):8, :] = last_c
    td_scr[8 - (DN_CONV_K - 1):8, :] = last_d

    @pl.when(t == pl.num_programs(1) - 1)
    def _():
        ncb_ref[0] = last_c
        ndb_ref[0] = last_d


def _pre_prompt(pm, ps, conv_w, dn_conv_w, hp, conv_buf, dn_buf, bsz, seq, tc):
    n = pm.shape[0]
    nt = seq // tc
    rows = lambda w: pl.BlockSpec((tc, w), lambda b, t: (b * nt + t, 0))
    full = lambda a: pl.BlockSpec(a.shape, lambda b, t: (0,) * a.ndim)
    per_b = lambda k, w: pl.BlockSpec((1, k, w), lambda b, t: (b, 0, 0))
    return pl.pallas_call(
        _pre_prompt_kernel,
        grid=(bsz, nt),
        in_specs=[rows(P_MAIN), rows(LANES), full(conv_w), full(dn_conv_w), full(hp),
                  per_b(CONV_K - 1, CONV_WIDTH), per_b(DN_CONV_K - 1, 3 * DN_WIDTH)],
        out_specs=[rows(CONV_WIDTH), rows(3 * DN_WIDTH), rows(LANES),
                   per_b(CONV_K - 1, CONV_WIDTH), per_b(DN_CONV_K - 1, 3 * DN_WIDTH)],
        out_shape=[
            jax.ShapeDtypeStruct((n, CONV_WIDTH), F32),
            jax.ShapeDtypeStruct((n, 3 * DN_WIDTH), F32),
            jax.ShapeDtypeStruct((n, LANES), F32),
            jax.ShapeDtypeStruct((bsz, CONV_K - 1, CONV_WIDTH), F32),
            jax.ShapeDtypeStruct((bsz, DN_CONV_K - 1, 3 * DN_WIDTH), F32),
        ],
        scratch_shapes=[pltpu.VMEM((8, CONV_WIDTH), F32), pltpu.VMEM((8, 3 * DN_WIDTH), F32)],
        compiler_params=_cparams("arbitrary", "arbitrary"),
        name="pre_prompt",
    )(pm, ps, conv_w, dn_conv_w, hp, conv_buf, dn_buf)


def _pre_sample_kernel(pm_ref, ps_ref, cw_ref, dw_ref, hp_ref, cb_ref, db_ref,
                       yc_ref, qkv_ref, gb_ref, ncb_ref, ndb_ref):
    steps = pm_ref.shape[0]
    cgh = [pm_ref[t, :, OFF_CC:OFF_CH] * pm_ref[t, :, OFF_CH:OFF_QKV] for t in range(steps)]
    ext = [cb_ref[i] for i in range(CONV_K - 1)] + cgh
    for t in range(steps):
        y = ext[t] * cw_ref[0:1, :]
        for i in range(1, CONV_K):
            y = y + ext[t + i] * cw_ref[i:i + 1, :]
        yc_ref[t] = pm_ref[t, :, 0:OFF_CC] * y
    for i in range(CONV_K - 1):
        ncb_ref[i] = ext[len(ext) - (CONV_K - 1) + i]

    xq = [pm_ref[t, :, OFF_QKV:OFF_Z] for t in range(steps)]
    extq = [db_ref[i] for i in range(DN_CONV_K - 1)] + xq
    for t in range(steps):
        y = extq[t] * dw_ref[0:1, :]
        for i in range(1, DN_CONV_K):
            y = y + extq[t + i] * dw_ref[i:i + 1, :]
        _qkv_finish(y, qkv_ref, (t,))
        gb_ref[t] = _gates(ps_ref[t], hp_ref[...])
    for i in range(DN_CONV_K - 1):
        ndb_ref[i] = extq[len(extq) - (DN_CONV_K - 1) + i]


def _pre_sample(pm, ps, conv_w, dn_conv_w, hp, conv_buf_tm, dn_buf_tm, bsz, steps, bt):
    slab = lambda k, w: pl.BlockSpec((k, bt, w), lambda i: (0, i, 0))
    full = lambda a: pl.BlockSpec(a.shape, lambda i: (0,) * a.ndim)
    return pl.pallas_call(
        _pre_sample_kernel,
        grid=(bsz // bt,),
        in_specs=[slab(steps, P_MAIN), slab(steps, LANES), full(conv_w), full(dn_conv_w), full(hp),
                  slab(CONV_K - 1, CONV_WIDTH), slab(DN_CONV_K - 1, 3 * DN_WIDTH)],
        out_specs=[slab(steps, CONV_WIDTH), slab(steps, 3 * DN_WIDTH), slab(steps, LANES),
                   slab(CONV_K - 1, CONV_WIDTH), slab(DN_CONV_K - 1, 3 * DN_WIDTH)],
        out_shape=[
            jax.ShapeDtypeStruct((steps, bsz, CONV_WIDTH), F32),
            jax.ShapeDtypeStruct((steps, bsz, 3 * DN_WIDTH), F32),
            jax.ShapeDtypeStruct((steps, bsz, LANES), F32),
            jax.ShapeDtypeStruct((CONV_K - 1, bsz, CONV_WIDTH), F32),
            jax.ShapeDtypeStruct((DN_CONV_K - 1, bsz, 3 * DN_WIDTH), F32),
        ],
        compiler_params=_cparams("arbitrary"),
        name="pre_sample",
    )(pm.reshape(steps, bsz, P_MAIN), ps.reshape(steps, bsz, LANES), conv_w, dn_conv_w, hp,
      conv_buf_tm, dn_buf_tm)


def _cumsum_chunks(x, c):
    row = lax.broadcasted_iota(jnp.int32, x.shape, 0) & (c - 1)
    s = 1
    while s < c:
        x = x + jnp.where(row >= s, pltpu.roll(x, s, 0), 0.0)
        s *= 2
    return x


def _gdn_problems(qkv, gb, c):
    n = qkv.shape[0] // c
    gc = _cumsum_chunks(gb, c)
    q, k, v, g_b, beta_b = [], [], [], [], []
    for ci in range(n):
        rs = slice(ci * c, (ci + 1) * c)
        for h in range(DN_HEADS):
            ls = lambda base: slice(base + h * DN_HEAD_DIM, base + (h + 1) * DN_HEAD_DIM)
            q.append(qkv[rs, ls(0)])
            k.append(qkv[rs, ls(DN_WIDTH)])
            v.append(qkv[rs, ls(2 * DN_WIDTH)])
            g_b.append(jnp.broadcast_to(gc[rs, G_LANE + h:G_LANE + h + 1], (c, DN_HEAD_DIM)))
            beta_b.append(jnp.broadcast_to(gb[rs, BETA_LANE + h:BETA_LANE + h + 1], (c, DN_HEAD_DIM)))
    return tuple(jnp.stack(a) for a in (q, k, v, g_b, beta_b))


def _unit_lower_solve(m, rhs, c, small, mode):
    if small:
        sol = rhs
        for j in range(c - 1):
            sol = sol - m[:, :, j:j + 1] * sol[:, j:j + 1, :]
        return sol
    row = lax.broadcasted_iota(jnp.int32, m.shape, 1)
    col = lax.broadcasted_iota(jnp.int32, m.shape, 2)
    p = -m
    t = jnp.where(row == col, 1.0, 0.0) + p
    p = _mm(p, p, _BNN, "bf16")
    levels = int(math.log2(c))
    for lvl in range(1, levels):
        if lvl < levels - 1:
            y = _mm(jnp.concatenate([t, p], axis=1), p, _BNN, "bf16")
            t = t + y[:, :c]
            p = y[:, c:]
        else:
            t = t + _mm(t, p, _BNN, "bf16")
    x = _mm(t, rhs, _BNN, "bf16")
    resid = rhs - x - _mm(m, x, _BNN, mode)
    return x + _mm(t, resid, _BNN, "bf16")


def _gdn_intra(q, k, v, g_b, beta_b, *, c, small, mode_gram, mode_solve, mode_apply):
    n = q.shape[0]
    row = lax.broadcasted_iota(jnp.int32, (n, c, c), 1)
    col = lax.broadcasted_iota(jnp.int32, (n, c, c), 2)
    g_cc = g_b[:, :, :c]
    g_row = jnp.sum(jnp.where(row == col, g_cc, 0.0), axis=1, keepdims=True)
    gamma = jnp.exp(jnp.where(row >= col, g_cc - g_row, -jnp.inf))
    kbeta = k * beta_b
    exp_g = jnp.exp(g_b)
    gram = _mm(jnp.concatenate([kbeta, q], axis=1), k, _BNT, mode_gram)
    m = jnp.where(row > col, gram[:, :c] * gamma, 0.0)
    attn = gram[:, c:] * gamma
    rhs = jnp.concatenate([kbeta * exp_g, v * beta_b], axis=-1)
    sol = _unit_lower_solve(m, rhs, c, small, mode_solve)
    g_last = g_b[:, c - 1:c, :]
    k_dec = k * jnp.exp(g_last - g_b)
    kd = _mm(k_dec, sol, _BTN, mode_apply)
    at = _mm(attn, sol, _BNN, mode_apply)
    q_t = q * exp_g - at[..., :DN_HEAD_DIM]
    return kd[..., :DN_HEAD_DIM], kd[..., DN_HEAD_DIM:], q_t, at[..., DN_HEAD_DIM:], jnp.exp(g_last)


def _gdn_state_step(a_mat, b_mat, q_t, o_intra, d_last, s, mode):
    r = _mm(jnp.concatenate([a_mat, q_t], axis=1), s, _BNN, mode)
    s_new = s * d_last - r[:, :DN_HEAD_DIM] + b_mat
    return r[:, DN_HEAD_DIM:] + o_intra, s_new


_PROMPT_MODES = dict(mode_gram="x3", mode_solve="x3", mode_apply="bf16")
_PROMPT_STATE_MODE = "bf16"
_SAMPLE_MODES = dict(mode_gram="bf16", mode_solve="bf16", mode_apply="bf16")
_SAMPLE_STATE_MODE = "bf16"


def _delta_prompt_kernel(qkv_ref, gb_ref, s0_ref, o_ref, sn_ref, s_scr):
    t = pl.program_id(1)
    tc = qkv_ref.shape[0]

    @pl.when(t == 0)
    def _():
        s_scr[...] = s0_ref[0]

    probs = _gdn_problems(qkv_ref[...], gb_ref[...], DN_CHUNK)
    a_mat, b_mat, q_t, o_intra, d_last = _gdn_intra(*probs, c=DN_CHUNK, small=False, **_PROMPT_MODES)
    s = s_scr[...]
    for ci in range(tc // DN_CHUNK):
        ps = slice(ci * DN_HEADS, (ci + 1) * DN_HEADS)
        o, s = _gdn_state_step(a_mat[ps], b_mat[ps], q_t[ps], o_intra[ps], d_last[ps], s,
                               _PROMPT_STATE_MODE)
        for h in range(DN_HEADS):
            o_ref[ci * DN_CHUNK:(ci + 1) * DN_CHUNK, h * DN_HEAD_DIM:(h + 1) * DN_HEAD_DIM] = o[h]
    s_scr[...] = s

    @pl.when(t == pl.num_programs(1) - 1)
    def _():
        sn_ref[0] = s


def _delta_prompt(qkv, gb, s0, bsz, seq, tc):
    n = qkv.shape[0]
    nt = seq // tc
    rows = lambda w: pl.BlockSpec((tc, w), lambda b, t: (b * nt + t, 0))
    state = pl.BlockSpec((1, DN_HEADS, DN_HEAD_DIM, DN_HEAD_DIM), lambda b, t: (b, 0, 0, 0))
    return pl.pallas_call(
        _delta_prompt_kernel,
        grid=(bsz, nt),
        in_specs=[rows(3 * DN_WIDTH), rows(LANES), state],
        out_specs=[rows(DN_WIDTH), state],
        out_shape=[
            jax.ShapeDtypeStruct((n, DN_WIDTH), F32),
            jax.ShapeDtypeStruct((bsz, DN_HEADS, DN_HEAD_DIM, DN_HEAD_DIM), F32),
        ],
        scratch_shapes=[pltpu.VMEM((DN_HEADS, DN_HEAD_DIM, DN_HEAD_DIM), F32)],
        compiler_params=_cparams("arbitrary", "arbitrary"),
        name="delta_prompt",
    )(qkv, gb, s0)


def _delta_sample_kernel(qkv_ref, gb_ref, s0_ref, *rest):
    o_ref, sn_ref = rest[-2:]
    bt = qkv_ref.shape[0]
    c = SAMPLE_CHUNK
    probs = _gdn_problems(qkv_ref[...].reshape(bt * c, 3 * DN_WIDTH), gb_ref[...].reshape(bt * c, LANES), c)
    a_mat, b_mat, q_t, o_intra, d_last = _gdn_intra(*probs, c=c, small=True, **_SAMPLE_MODES)
    s = s0_ref[0].reshape(bt * DN_HEADS, DN_HEAD_DIM, DN_HEAD_DIM)
    o, s = _gdn_state_step(a_mat, b_mat, q_t, o_intra, d_last, s, _SAMPLE_STATE_MODE)
    for b in range(bt):
        for h in range(DN_HEADS):
            o_ref[b, :, h * DN_HEAD_DIM:(h + 1) * DN_HEAD_DIM] = o[b * DN_HEADS + h]
    sn_ref[0] = s.reshape(bt, DN_HEADS, DN_HEAD_DIM, DN_HEAD_DIM)


def _delta_sample(qkv_b, gb_b, state_all, layer, carried, bt):
    bsz = qkv_b.shape[0]
    blk = lambda w: pl.BlockSpec((bt, SAMPLE_CHUNK, w), lambda i: (i, 0, 0))
    state = pl.BlockSpec((1, bt, DN_HEADS, DN_HEAD_DIM, DN_HEAD_DIM), lambda i: (layer, i, 0, 0, 0))
    in_specs, args, aliases = [blk(3 * DN_WIDTH), blk(LANES), state], [qkv_b, gb_b, state_all], {}
    if carried is not None:
        in_specs.append(pl.BlockSpec(memory_space=pl.ANY))
        args.append(carried)
        aliases = {3: 1}
    return pl.pallas_call(
        _delta_sample_kernel,
        grid=(bsz // bt,),
        in_specs=in_specs,
        out_specs=[blk(DN_WIDTH), state],
        out_shape=[
            jax.ShapeDtypeStruct((bsz, SAMPLE_CHUNK, DN_WIDTH), F32),
            jax.ShapeDtypeStruct(state_all.shape, F32),
        ],
        input_output_aliases=aliases,
        compiler_params=_cparams("arbitrary"),
        name="delta_sample",
    )(*args)


def _outproj_kernel(yc_ref, o_ref, z_ref, x_ref, g1_ref, nw_ref, w_ref, lg_ref, lb_ref, out_ref):
    o = o_ref[...]
    parts = []
    for h in range(DN_HEADS):
        oh = o[:, h * DN_HEAD_DIM:(h + 1) * DN_HEAD_DIM]
        parts.append(oh * lax.rsqrt(jnp.mean(oh * oh, -1, keepdims=True) + RMS_EPS) * nw_ref[...])
    og = jnp.concatenate(parts, axis=-1) * _silu(z_ref[...])
    mixed = jnp.concatenate([yc_ref[...], og], axis=-1).astype(BF16)
    m = jnp.dot(mixed, w_ref[...], preferred_element_type=F32)
    r = DEEPNORM_ALPHA * x_ref[...] + (1.0 + g1_ref[0]) * m
    out_ref[...] = _layer_norm(r, lg_ref[...], lb_ref[...])


def _outproj(yc, o, pm, x, mod3, tpg, norm_w, w_out, ln_g, ln_b, tm):
    n = x.shape[0]
    vec = lambda w: pl.BlockSpec((1, w), lambda i: (0, 0))
    return pl.pallas_call(
        _outproj_kernel,
        grid=(n // tm,),
        in_specs=[
            pl.BlockSpec((tm, CONV_WIDTH), lambda i: (i, 0)),
            pl.BlockSpec((tm, DN_WIDTH), lambda i: (i, 0)),
            pl.BlockSpec((tm, DN_WIDTH), lambda i: (i, OFF_Z // DN_WIDTH)),
            pl.BlockSpec((tm, D_MODEL), lambda i: (i, 0)),
            _mod_spec(mod3, 2, tpg),
            vec(DN_HEAD_DIM),
            pl.BlockSpec((D_MODEL, D_MODEL), lambda i: (0, 0)),
            vec(D_MODEL), vec(D_MODEL),
        ],
        out_specs=pl.BlockSpec((tm, D_MODEL), lambda i: (i, 0)),
        out_shape=jax.ShapeDtypeStruct((n, D_MODEL), F32),
        compiler_params=_cparams("arbitrary"),
        name="outproj",
    )(yc, o, pm, x, mod3, norm_w, w_out, ln_g, ln_b)


R_E1, R_E2, R_G1, R_G2, R_R1, R_R2 = range(6)


def _router_kernel(x_ref, sc_ref, sh_ref, wr_ref, br_ref, route_ref, route_t_ref, cnt_ref):
    tm = x_ref.shape[0]
    u = x_ref[...] * (1.0 + sc_ref[0]) + sh_ref[0]
    logits = _mm(u, wr_ref[...], mode="x3") + br_ref[...]
    lane = lax.broadcasted_iota(jnp.int32, logits.shape, 1)
    valid = lane < N_EXPERTS
    logits = jnp.where(valid, logits, -jnp.inf)
    ex = jnp.exp(logits - jnp.max(logits, -1, keepdims=True))
    probs = jnp.where(valid, ex / jnp.sum(ex, -1, keepdims=True), -1.0)
    p1 = jnp.max(probs, -1, keepdims=True)
    i1 = jnp.min(jnp.where(probs == p1, lane, LANES), -1, keepdims=True)
    rest = jnp.where(lane == i1, -1.0, probs)
    p2 = jnp.max(rest, -1, keepdims=True)
    i2 = jnp.min(jnp.where(rest == p2, lane, LANES), -1, keepdims=True)
    tot = p1 + p2
    sel = jnp.where(lane == i1, 1.0, 0.0) + jnp.where(lane == i2, 1.0, 0.0)
    incl = _cumsum_chunks(sel, tm)
    excl = incl - sel
    r1 = jnp.sum(jnp.where(lane == i1, excl, 0.0), -1, keepdims=True)
    r2 = jnp.sum(jnp.where(lane == i2, excl, 0.0), -1, keepdims=True)
    cnt_ref[0] = incl[tm - 1:tm, :]
    route = jnp.zeros_like(logits)
    for j, val in enumerate((i1.astype(F32), i2.astype(F32), p1 / tot, p2 / tot, r1, r2)):
        route = jnp.where(lane == j, val, route)
    route_ref[...] = route
    route_t_ref[0] = jnp.transpose(route)[:SUBLANES, :]


def _router(x, mod3, tpg, w_router, b_router, tm):
    n = x.shape[0]
    return pl.pallas_call(
        _router_kernel,
        grid=(n // tm,),
        in_specs=[
            pl.BlockSpec((tm, D_MODEL), lambda i: (i, 0)),
            _mod_spec(mod3, 4, tpg),
            _mod_spec(mod3, 3, tpg),
            pl.BlockSpec((D_MODEL, LANES), lambda i: (0, 0)),
            pl.BlockSpec((1, LANES), lambda i: (0, 0)),
        ],
        out_specs=[pl.BlockSpec((tm, LANES), lambda i: (i, 0)),
                   pl.BlockSpec((1, SUBLANES, tm), lambda i: (i, 0, 0)),
                   pl.BlockSpec((1, 1, LANES), lambda i: (i, 0, 0))],
        out_shape=[jax.ShapeDtypeStruct((n, LANES), F32),
                   jax.ShapeDtypeStruct((n // tm, SUBLANES, tm), F32),
                   jax.ShapeDtypeStruct((n // tm, 1, LANES), F32)],
        compiler_params=_cparams("arbitrary"),
        name="router",
    )(x, mod3, mod3, w_router, b_router)


def _dispatch_plan(route_t, counts, tme):
    n_rt, _, tm = route_t.shape
    n = n_rt * tm
    cnt = counts[:, 0, :N_EXPERTS].astype(jnp.int32)
    tile_off = jnp.cumsum(cnt, axis=0) - cnt
    tot = jnp.sum(cnt, axis=0)
    gsz = (tot + tme - 1) // tme * tme
    gend = jnp.cumsum(gsz)
    goff = gend - gsz
    base = (goff[None, :] + tile_off)[:, :, None]
    ids = jnp.arange(N_EXPERTS, dtype=jnp.int32)[None, :, None]
    field = lambda j: route_t[:, j:j + 1, :].astype(jnp.int32)
    pick = lambda e: jnp.sum(jnp.where(ids == e, base, 0), axis=1, keepdims=True)
    slot1 = (pick(field(R_E1)) + field(R_R1)).reshape(n)
    slot2 = (pick(field(R_E2)) + field(R_R2)).reshape(n)
    n_tiles = 2 * n // tme + N_EXPERTS
    start = jnp.arange(n_tiles, dtype=jnp.int32) * tme
    valid = (start < gend[-1]).astype(jnp.int32)
    blk = jnp.minimum(jnp.arange(n_tiles, dtype=jnp.int32), gend[-1] // tme - 1)
    tile_e = jnp.sum((blk * tme)[:, None] >= gend[None, :], axis=1).astype(jnp.int32)
    zoff = jnp.where(gsz > 0, gend - tme, (n_tiles + jnp.arange(N_EXPERTS, dtype=jnp.int32)) * tme)
    return slot1, slot2, zoff, tile_e, valid, blk, n_tiles


def _dispatch_kernel(s1_ref, s2_ref, zoff_ref, x_ref, sc_ref, sh_ref, ug_hbm, ubuf, zbuf, sem):
    i = pl.program_id(0)
    tm = x_ref.shape[0]
    tme = zbuf.shape[0]

    @pl.when(i == 0)
    def _():
        zbuf[...] = jnp.zeros_like(zbuf)
        fill = lambda e: pltpu.make_async_copy(
            zbuf, ug_hbm.at[pl.ds(pl.multiple_of(zoff_ref[e], 8), tme)], sem.at[2])
        for e in range(N_EXPERTS):
            fill(e).start()
        for e in range(N_EXPERTS):
            fill(e).wait()

    ubuf[...] = (x_ref[...] * (1.0 + sc_ref[0]) + sh_ref[0]).reshape(ubuf.shape)
    base = i * tm

    def issue(j, carry):
        for k in range(SUBLANES):
            row = ubuf.at[j, pl.ds(k, 1)]
            t = base + j * SUBLANES + k
            pltpu.make_async_copy(row, ug_hbm.at[pl.ds(s1_ref[t], 1)], sem.at[0]).start()
            pltpu.make_async_copy(row, ug_hbm.at[pl.ds(s2_ref[t], 1)], sem.at[1]).start()
        return carry

    lax.fori_loop(0, tm // SUBLANES, issue, 0)
    pltpu.make_async_copy(ubuf, ubuf, sem.at[0]).wait()
    pltpu.make_async_copy(ubuf, ubuf, sem.at[1]).wait()


def _dispatch(x, mod3, tpg, slot1, slot2, zoff, tm, tme, rows):
    n = x.shape[0]
    return pl.pallas_call(
        _dispatch_kernel,
        grid_spec=pltpu.PrefetchScalarGridSpec(
            num_scalar_prefetch=3,
            grid=(n // tm,),
            in_specs=[
                pl.BlockSpec((tm, D_MODEL), lambda i, *_: (i, 0)),
                _mod_spec(mod3, 4, tpg),
                _mod_spec(mod3, 3, tpg),
            ],
            out_specs=pl.BlockSpec(memory_space=pl.ANY),
            scratch_shapes=[pltpu.VMEM((tm // SUBLANES, SUBLANES, D_MODEL), F32),
                            pltpu.VMEM((tme, D_MODEL), F32), pltpu.SemaphoreType.DMA((3,))],
        ),
        out_shape=jax.ShapeDtypeStruct((rows, D_MODEL), F32),
        compiler_params=_cparams("arbitrary"),
        name="dispatch",
    )(slot1, slot2, zoff, x, mod3, mod3)


def _expert_kernel(te_ref, tv_ref, tb_ref, ug_ref, wg_ref, wu_ref, wd_ref, ys_ref):
    i = pl.program_id(0)

    @pl.when(tv_ref[i] == 1)
    def _():
        u = ug_ref[...].astype(BF16)
        h = _silu(jnp.dot(u, wg_ref[0], preferred_element_type=F32)) * jnp.dot(
            u, wu_ref[0], preferred_element_type=F32)
        ys_ref[...] = jnp.dot(h.astype(BF16), wd_ref[0], preferred_element_type=F32)

    @pl.when(tv_ref[i] == 0)
    def _():
        ys_ref[...] = jnp.zeros_like(ys_ref)


def _experts(ug, tile_e, valid, blk, w_gate, w_up, w_down, tme, n_tiles):
    d_e = w_gate.shape[-1]
    return pl.pallas_call(
        _expert_kernel,
        grid_spec=pltpu.PrefetchScalarGridSpec(
            num_scalar_prefetch=3,
            grid=(n_tiles,),
            in_specs=[
                pl.BlockSpec((tme, D_MODEL), lambda i, te, tv, tb: (tb[i], 0)),
                pl.BlockSpec((1, D_MODEL, d_e), lambda i, te, tv, tb: (te[i], 0, 0)),
                pl.BlockSpec((1, D_MODEL, d_e), lambda i, te, tv, tb: (te[i], 0, 0)),
                pl.BlockSpec((1, d_e, D_MODEL), lambda i, te, tv, tb: (te[i], 0, 0)),
            ],
            out_specs=pl.BlockSpec((tme, D_MODEL), lambda i, te, tv, tb: (i, 0)),
        ),
        out_shape=jax.ShapeDtypeStruct((n_tiles * tme, D_MODEL), F32),
        compiler_params=_cparams("arbitrary"),
        name="experts",
    )(tile_e, valid, blk, ug, w_gate, w_up, w_down)


def _combine_kernel(s1_ref, s2_ref, ys_hbm, route_ref, x_ref, g2_ref, lg_ref, lb_ref, out_ref,
                    y1buf, y2buf, sem):
    i = pl.program_id(0)
    tm = x_ref.shape[0]
    base = i * tm

    def issue(j, carry):
        for k in range(SUBLANES):
            t = base + j * SUBLANES + k
            pltpu.make_async_copy(ys_hbm.at[pl.ds(s1_ref[t], 1)], y1buf.at[j, pl.ds(k, 1)], sem.at[0]).start()
            pltpu.make_async_copy(ys_hbm.at[pl.ds(s2_ref[t], 1)], y2buf.at[j, pl.ds(k, 1)], sem.at[1]).start()
        return carry

    lax.fori_loop(0, tm // SUBLANES, issue, 0)
    pltpu.make_async_copy(y1buf, y1buf, sem.at[0]).wait()
    pltpu.make_async_copy(y2buf, y2buf, sem.at[1]).wait()
    route = route_ref[...]
    y1 = y1buf[...].reshape(tm, D_MODEL)
    y2 = y2buf[...].reshape(tm, D_MODEL)
    f = route[:, R_G1:R_G1 + 1] * y1 + route[:, R_G2:R_G2 + 1] * y2
    r = DEEPNORM_ALPHA * x_ref[...] + (1.0 + g2_ref[0]) * f
    out_ref[...] = _layer_norm(r, lg_ref[...], lb_ref[...])


def _combine(ys, slot1, slot2, route, x, mod3, tpg, ln_g, ln_b, tm):
    n = x.shape[0]
    vec = lambda w: pl.BlockSpec((1, w), lambda i, *_: (0, 0))
    return pl.pallas_call(
        _combine_kernel,
        grid_spec=pltpu.PrefetchScalarGridSpec(
            num_scalar_prefetch=2,
            grid=(n // tm,),
            in_specs=[
                pl.BlockSpec(memory_space=pl.ANY),
                pl.BlockSpec((tm, LANES), lambda i, *_: (i, 0)),
                pl.BlockSpec((tm, D_MODEL), lambda i, *_: (i, 0)),
                _mod_spec(mod3, 5, tpg),
                vec(D_MODEL), vec(D_MODEL),
            ],
            out_specs=pl.BlockSpec((tm, D_MODEL), lambda i, *_: (i, 0)),
            scratch_shapes=[pltpu.VMEM((tm // SUBLANES, SUBLANES, D_MODEL), F32),
                            pltpu.VMEM((tm // SUBLANES, SUBLANES, D_MODEL), F32),
                            pltpu.SemaphoreType.DMA((2,))],
        ),
        out_shape=jax.ShapeDtypeStruct((n, D_MODEL), F32),
        compiler_params=_cparams("arbitrary"),
        name="combine",
    )(slot1, slot2, ys, route, x, mod3, ln_g, ln_b)


def _moe(x, mod3, tpg, w_router, b_router, w_gate, w_up, w_down, ln_g, ln_b, tm, tme):
    route, route_t, counts = _router(x, mod3, tpg, w_router, b_router, tm)
    slot1, slot2, zoff, tile_e, valid, blk, n_tiles = _dispatch_plan(route_t, counts, tme)
    ug = _dispatch(x, mod3, tpg, slot1, slot2, zoff, tm, tme, (n_tiles + N_EXPERTS) * tme)
    ys = _experts(ug, tile_e, valid, blk, w_gate, w_up, w_down, tme, n_tiles)
    return _combine(ys, slot1, slot2, route, x, mod3, tpg, ln_g, ln_b, tm)


def _ffn_kernel(x_ref, sc_ref, sh_ref, g2_ref, wg_ref, wu_ref, wd_ref, lg_ref, lb_ref,
                out_ref, u_scr, acc_scr):
    f = pl.program_id(1)

    @pl.when(f == 0)
    def _():
        u_scr[...] = (x_ref[...] * (1.0 + sc_ref[0]) + sh_ref[0]).astype(BF16)
        acc_scr[...] = jnp.zeros_like(acc_scr)

    u = u_scr[...]
    h = _silu(jnp.dot(u, wg_ref[...], preferred_element_type=F32)) * jnp.dot(
        u, wu_ref[...], preferred_element_type=F32)
    acc_scr[...] += jnp.dot(h.astype(BF16), wd_ref[...], preferred_element_type=F32)

    @pl.when(f == pl.num_programs(1) - 1)
    def _():
        r = DEEPNORM_ALPHA * x_ref[...] + (1.0 + g2_ref[0]) * acc_scr[...]
        out_ref[...] = _layer_norm(r, lg_ref[...], lb_ref[...])


def _ffn(x, mod3, tpg, w_gate, w_up, w_down, ln_g, ln_b, tm, tf):
    n = x.shape[0]
    d_ff = w_gate.shape[-1]
    vec = lambda w: pl.BlockSpec((1, w), lambda i, f: (0, 0))
    return pl.pallas_call(
        _ffn_kernel,
        grid=(n // tm, d_ff // tf),
        in_specs=[
            pl.BlockSpec((tm, D_MODEL), lambda i, f: (i, 0)),
            _mod_spec(mod3, 4, tpg),
            _mod_spec(mod3, 3, tpg),
            _mod_spec(mod3, 5, tpg),
            pl.BlockSpec((D_MODEL, tf), lambda i, f: (0, f)),
            pl.BlockSpec((D_MODEL, tf), lambda i, f: (0, f)),
            pl.BlockSpec((tf, D_MODEL), lambda i, f: (f, 0)),
            vec(D_MODEL), vec(D_MODEL),
        ],
        out_specs=pl.BlockSpec((tm, D_MODEL), lambda i, f: (i, 0)),
        out_shape=jax.ShapeDtypeStruct((n, D_MODEL), F32),
        scratch_shapes=[pltpu.VMEM((tm, D_MODEL), BF16), pltpu.VMEM((tm, D_MODEL), F32)],
        compiler_params=_cparams("arbitrary", "arbitrary"),
        name="ffn_dense",
    )(x, mod3, mod3, mod3, w_gate, w_up, w_down, ln_g, ln_b)


def _trunk(x, mod3s, rows_per_group, prm, tiles, mixer_fn):
    tm, tme = tiles["tm"], tiles["tme"]
    tpg = rows_per_group // tm
    states = []
    for l in range(DEPTH):
        mod3 = mod3s[l]
        pm, ps = _inproj(x, mod3, tpg, prm["w_main"][l], prm["w_small"][l], tm)
        yc, o, st = mixer_fn(l, pm, ps)
        states.append(st)
        x = _outproj(yc, o, pm, x, mod3, tpg, prm["norm_w"][l], prm["w_out"][l],
                     prm["ln_g"][l, 0:1], prm["ln_b"][l, 0:1], tm)
        j = l // 2
        if l % 2 == 0:
            x = _ffn(x, mod3, rows_per_group // tiles["tm_ffn"], prm["w_ff_gate"][j], prm["w_ff_up"][j],
                     prm["w_ff_down"][j], prm["ln_g"][l, 1:2], prm["ln_b"][l, 1:2],
                     tiles["tm_ffn"], tiles["tf"])
        else:
            x = _moe(x, mod3, tpg, prm["w_router"][j], prm["b_router"][j], prm["w_exp_gate"][j],
                     prm["w_exp_up"][j], prm["w_exp_down"][j], prm["ln_g"][l, 1:2], prm["ln_b"][l, 1:2],
                     tm, tme)
    return x, states


def kernel(x_prompt, x_sample, state_conv, state_dn_conv, state_dn, c_prompt, c_sample, w_in, conv_w,
           dn_conv_w, a_log, dt_bias, dn_norm_w, w_out, w_ada, b_ada, ln_g, ln_b, w_ff_gate, w_ff_up,
           w_ff_down, w_router, b_router, w_exp_gate, w_exp_up, w_exp_down):
    bp, seq, d = x_prompt.shape
    bs, steps, _ = x_sample.shape
    pad_lanes = LANES - N_EXPERTS
    hp = jnp.zeros((DEPTH, 2, LANES), F32)
    hp = hp.at[:, 0, G_LANE:].set(a_log).at[:, 1, G_LANE:].set(dt_bias)
    w_in_t = jnp.swapaxes(w_in, 1, 2)
    prm = {
        "w_main": w_in_t[:, :P_MAIN].astype(BF16),
        "w_small": w_in_t[:, w_in_t.shape[1] - LANES:],
        "norm_w": dn_norm_w.reshape(DEPTH, 1, DN_HEAD_DIM),
        "w_out": w_out.astype(BF16),
        "ln_g": ln_g, "ln_b": ln_b,
        "w_ff_gate": w_ff_gate.astype(BF16), "w_ff_up": w_ff_up.astype(BF16),
        "w_ff_down": w_ff_down.astype(BF16),
        "w_router": jnp.pad(w_router, ((0, 0), (0, 0), (0, pad_lanes))),
        "b_router": jnp.pad(b_router, ((0, 0), (0, pad_lanes)))[:, None, :],
        "w_exp_gate": w_exp_gate.astype(BF16), "w_exp_up": w_exp_up.astype(BF16),
        "w_exp_down": w_exp_down.astype(BF16),
    }

    mod = _ada(jnp.concatenate([c_prompt, c_sample], axis=0), w_ada, b_ada)
    mod_p = [mod[l, :bp].reshape(bp, 1, 6 * d) for l in range(DEPTH)]
    mod_s = [mod[l, bp:].reshape(1, bs, 6 * d) for l in range(DEPTH)]

    tm_p, tc_pre, tc_delta = 512, 256, 256
    zc = jnp.zeros((bp, CONV_K - 1, CONV_WIDTH), F32)
    zd = jnp.zeros((bp, DN_CONV_K - 1, 3 * DN_WIDTH), F32)
    zs = jnp.zeros((bp, DN_HEADS, DN_HEAD_DIM, DN_HEAD_DIM), F32)

    def mixer_prompt(l, pm, ps):
        yc, qkv, gb, ncb, ndb = _pre_prompt(pm, ps, conv_w[l], dn_conv_w[l], hp[l], zc, zd, bp, seq, tc_pre)
        o, s_new = _delta_prompt(qkv, gb, zs, bp, seq, tc_delta)
        return yc, o, (ncb, ndb, s_new)

    y_p, st_p = _trunk(x_prompt.reshape(bp * seq, d), mod_p, seq, prm,
                       dict(tm=512, tm_ffn=1024, tf=256, tme=512), mixer_prompt)

    bt_pre, bt_delta = 32, 8
    dn_states = []

    def mixer_sample(l, pm, ps):
        cb_tm = jnp.transpose(state_conv[l], (1, 0, 2))
        db_tm = jnp.transpose(state_dn_conv[l], (1, 0, 2))
        yc, qkv, gb, ncb, ndb = _pre_sample(pm, ps, conv_w[l], dn_conv_w[l], hp[l], cb_tm, db_tm,
                                            bs, steps, bt_pre)
        to_b = lambda a: jnp.pad(jnp.transpose(a, (1, 0, 2)), ((0, 0), (0, SAMPLE_CHUNK - steps), (0, 0)))
        o_b, s_all = _delta_sample(to_b(qkv), to_b(gb), state_dn, l, dn_states[-1] if dn_states else None,
                                   bt_delta)
        dn_states.append(s_all)
        o = jnp.transpose(o_b[:, :steps], (1, 0, 2)).reshape(steps * bs, DN_WIDTH)
        return (yc.reshape(steps * bs, CONV_WIDTH), o,
                (jnp.transpose(ncb, (1, 0, 2)), jnp.transpose(ndb, (1, 0, 2))))

    x_s = jnp.transpose(x_sample, (1, 0, 2)).reshape(steps * bs, d)
    y_s, st_s = _trunk(x_s, mod_s, steps * bs, prm,
                       dict(tm=bs, tm_ffn=bs, tf=w_ff_gate.shape[-1] // 2, tme=128), mixer_sample)
    y_s = jnp.transpose(y_s.reshape(steps, bs, d), (1, 0, 2))

    stack = lambda sts, i: jnp.stack([s[i] for s in sts])
    return (y_p.reshape(bp, seq, d), y_s,
            stack(st_p, 0), stack(st_p, 1), stack(st_p, 2),
            stack(st_s, 0), stack(st_s, 1), dn_states[-1])
```

```python
import functools
import math

import jax
import jax.numpy as jnp
from jax import lax
from jax.experimental import pallas as pl
from jax.experimental.pallas import tpu as pltpu

F32 = jnp.float32
BF16 = jnp.bfloat16

D_MODEL = 1024
DEPTH = 2
CONV_WIDTH = 512
CONV_K = 3
DN_HEADS = 4
DN_HEAD_DIM = 128
DN_WIDTH = DN_HEADS * DN_HEAD_DIM
DN_CONV_K = 4
DN_CHUNK = 64
N_EXPERTS = 8
DEEPNORM_ALPHA = (2.0 * DEPTH) ** 0.25
LN_EPS = 1e-5
RMS_EPS = 1e-6

OFF_CC = CONV_WIDTH
OFF_CH = 2 * CONV_WIDTH
OFF_QKV = 3 * CONV_WIDTH
OFF_Z = OFF_QKV + 3 * DN_WIDTH
P_MAIN = OFF_Z + DN_WIDTH
LANES = 128
SUBLANES = 8
BETA_LANE = LANES - 2 * DN_HEADS
G_LANE = LANES - DN_HEADS
SAMPLE_CHUNK = 8

VMEM_LIMIT = 56 * 1024 * 1024


def _cparams(*sem):
    return pltpu.CompilerParams(dimension_semantics=sem, vmem_limit_bytes=VMEM_LIMIT)


def _silu(x):
    return x * jax.nn.sigmoid(x)


def _layer_norm(r, g, b):
    mu = jnp.mean(r, -1, keepdims=True)
    xc = r - mu
    var = jnp.mean(xc * xc, -1, keepdims=True)
    return xc * lax.rsqrt(var + LN_EPS) * g + b


def _split_bf16(a):
    hi = a.astype(BF16)
    lo = (a - hi.astype(F32)).astype(BF16)
    return hi, lo


_NN = (((1,), (0,)), ((), ()))
_NT = (((1,), (1,)), ((), ()))
_BNN = (((2,), (1,)), ((0,), (0,)))
_BNT = (((2,), (2,)), ((0,), (0,)))
_BTN = (((1,), (1,)), ((0,), (0,)))


def _mm(a, b, dims=_NN, mode="bf16"):
    if mode == "f32":
        return lax.dot_general(a, b, dims, precision=lax.Precision.HIGHEST, preferred_element_type=F32)
    if mode == "bf16":
        return lax.dot_general(a.astype(BF16), b.astype(BF16), dims, preferred_element_type=F32)
    ah, al = _split_bf16(a)
    bh, bl = _split_bf16(b)
    d = functools.partial(lax.dot_general, dimension_numbers=dims, preferred_element_type=F32)
    return d(ah, bh) + (d(ah, bl) + d(al, bh))


def _ada_kernel(c_ref, w_ref, b_ref, o_ref):
    s = _silu(c_ref[...]).astype(BF16)
    o_ref[0] = jnp.dot(s, w_ref[0].astype(BF16), preferred_element_type=F32) + b_ref[0]


def _ada(c_all, w_ada, b_ada):
    rows = c_all.shape[0]
    tn = 1536
    return pl.pallas_call(
        _ada_kernel,
        grid=(DEPTH, 6 * D_MODEL // tn),
        in_specs=[
            pl.BlockSpec((rows, D_MODEL), lambda l, j: (0, 0)),
            pl.BlockSpec((1, D_MODEL, tn), lambda l, j: (l, 0, j)),
            pl.BlockSpec((1, 1, tn), lambda l, j: (l, 0, j)),
        ],
        out_specs=pl.BlockSpec((1, rows, tn), lambda l, j: (l, 0, j)),
        out_shape=jax.ShapeDtypeStruct((DEPTH, rows, 6 * D_MODEL), F32),
        compiler_params=_cparams("arbitrary", "arbitrary"),
        name="ada",
    )(c_all, w_ada, b_ada.reshape(DEPTH, 1, 6 * D_MODEL))


def _mod_spec(mod3, chunk, tiles_per_group):
    r = mod3.shape[1]
    return pl.BlockSpec((1, r, D_MODEL), lambda i, *_: (i // tiles_per_group, 0, chunk))


def _inproj_kernel(x_ref, sc_ref, sh_ref, wm_ref, ws_ref, pm_ref, ps_ref):
    u = x_ref[...] * (1.0 + sc_ref[0]) + sh_ref[0]
    pm_ref[...] = lax.dot_general(u.astype(BF16), wm_ref[...], _NT, preferred_element_type=F32)
    ps_ref[...] = _mm(u, ws_ref[...], _NT, mode="x3")


def _inproj(x, mod3, tpg, w_main, w_small, tm):
    n = x.shape[0]
    return pl.pallas_call(
        _inproj_kernel,
        grid=(n // tm,),
        in_specs=[
            pl.BlockSpec((tm, D_MODEL), lambda i: (i, 0)),
            _mod_spec(mod3, 1, tpg),
            _mod_spec(mod3, 0, tpg),
            pl.BlockSpec((P_MAIN, D_MODEL), lambda i: (0, 0)),
            pl.BlockSpec((LANES, D_MODEL), lambda i: (0, 0)),
        ],
        out_specs=[
            pl.BlockSpec((tm, P_MAIN), lambda i: (i, 0)),
            pl.BlockSpec((tm, LANES), lambda i: (i, 0)),
        ],
        out_shape=[
            jax.ShapeDtypeStruct((n, P_MAIN), F32),
            jax.ShapeDtypeStruct((n, LANES), F32),
        ],
        compiler_params=_cparams("arbitrary"),
        name="inproj",
    )(x, mod3, mod3, w_main, w_small)


def _gates(ps, hp):
    lane = lax.broadcasted_iota(jnp.int32, ps.shape, 1)
    beta = jax.nn.sigmoid(ps)
    g = -jnp.exp(hp[0:1, :]) * jax.nn.softplus(ps + hp[1:2, :])
    return jnp.where(lane >= G_LANE, g, jnp.where(lane >= BETA_LANE, beta, 0.0))


def _qkv_finish(y, out_ref, idx):
    y = _silu(y)
    for h in range(3 * DN_HEADS):
        seg = y[:, h * DN_HEAD_DIM:(h + 1) * DN_HEAD_DIM]
        if h < 2 * DN_HEADS:
            seg = seg * lax.rsqrt(jnp.sum(seg * seg, -1, keepdims=True) + RMS_EPS)
            if h < DN_HEADS:
                seg = seg * (DN_HEAD_DIM ** -0.5)
        out_ref[idx + (slice(None), slice(h * DN_HEAD_DIM, (h + 1) * DN_HEAD_DIM))] = seg


def _shifted(x, tail_ref, i, row):
    s = pltpu.roll(x, i, 0)
    for r in range(i):
        s = jnp.where(row == r, tail_ref[8 - i + r:8 - i + r + 1, :], s)
    return s


def _inproj_pre_prompt_kernel(x_ref, sc_ref, sh_ref, wm_ref, ws_ref, cw_ref, dw_ref, hp_ref, cb_ref, db_ref,
                              yc_ref, qkv_ref, z_ref, gb_ref, ncb_ref, ndb_ref, tc_scr, td_scr):
    t = pl.program_id(1)
    tm = x_ref.shape[0]

    @pl.when(t == 0)
    def _():
        tc_scr[8 - (CONV_K - 1):8, :] = cb_ref[0]
        td_scr[8 - (DN_CONV_K - 1):8, :] = db_ref[0]

    u_f32 = x_ref[...] * (1.0 + sc_ref[0]) + sh_ref[0]
    u = u_f32.astype(BF16)
    proj = lambda lo, hi: lax.dot_general(u, wm_ref[lo:hi, :], _NT, preferred_element_type=F32)

    row_d = lax.broadcasted_iota(jnp.int32, (tm, 3 * DN_WIDTH), 0)
    xq = proj(OFF_QKV, OFF_Z)
    yq = _shifted(xq, td_scr, 3, row_d) * dw_ref[0:1, :]
    yq = yq + _shifted(xq, td_scr, 2, row_d) * dw_ref[1:2, :]
    yq = yq + _shifted(xq, td_scr, 1, row_d) * dw_ref[2:3, :]
    yq = yq + xq * dw_ref[3:4, :]
    _qkv_finish(yq, qkv_ref, ())
    last_d = xq[tm - (DN_CONV_K - 1):tm, :]

    row_c = lax.broadcasted_iota(jnp.int32, (tm, CONV_WIDTH), 0)
    cgh = proj(OFF_CC, OFF_CH) * proj(OFF_CH, OFF_QKV)
    y = _shifted(cgh, tc_scr, 2, row_c) * cw_ref[0:1, :]
    y = y + _shifted(cgh, tc_scr, 1, row_c) * cw_ref[1:2, :]
    y = y + cgh * cw_ref[2:3, :]
    yc_ref[...] = proj(0, OFF_CC) * y
    last_c = cgh[tm - (CONV_K - 1):tm, :]

    z_ref[...] = proj(OFF_Z, P_MAIN)
    gb_ref[...] = _gates(_mm(u_f32, ws_ref[...], _NT, mode="x3"), hp_ref[...])

    tc_scr[8 - (CONV_K - 1):8, :] = last_c
    td_scr[8 - (DN_CONV_K - 1):8, :] = last_d

    @pl.when(t == pl.num_programs(1) - 1)
    def _():
        ncb_ref[0] = last_c
        ndb_ref[0] = last_d


def _inproj_pre_prompt(x, mod3, w_main, w_small, conv_w, dn_conv_w, hp, conv_buf, dn_buf, bsz, seq, tm):
    n = x.shape[0]
    nt = seq // tm
    rows = lambda w: pl.BlockSpec((tm, w), lambda b, t: (b * nt + t, 0))
    full = lambda a: pl.BlockSpec(a.shape, lambda b, t: (0,) * a.ndim)
    per_b = lambda k, w: pl.BlockSpec((1, k, w), lambda b, t: (b, 0, 0))
    mod = lambda chunk: pl.BlockSpec((1, 1, D_MODEL), lambda b, t: (b, 0, chunk))
    return pl.pallas_call(
        _inproj_pre_prompt_kernel,
        grid=(bsz, nt),
        in_specs=[rows(D_MODEL), mod(1), mod(0), full(w_main), full(w_small),
                  full(conv_w), full(dn_conv_w), full(hp),
                  per_b(CONV_K - 1, CONV_WIDTH), per_b(DN_CONV_K - 1, 3 * DN_WIDTH)],
        out_specs=[rows(CONV_WIDTH), rows(3 * DN_WIDTH), rows(DN_WIDTH), rows(LANES),
                   per_b(CONV_K - 1, CONV_WIDTH), per_b(DN_CONV_K - 1, 3 * DN_WIDTH)],
        out_shape=[
            jax.ShapeDtypeStruct((n, CONV_WIDTH), F32),
            jax.ShapeDtypeStruct((n, 3 * DN_WIDTH), F32),
            jax.ShapeDtypeStruct((n, DN_WIDTH), F32),
            jax.ShapeDtypeStruct((n, LANES), F32),
            jax.ShapeDtypeStruct((bsz, CONV_K - 1, CONV_WIDTH), F32),
            jax.ShapeDtypeStruct((bsz, DN_CONV_K - 1, 3 * DN_WIDTH), F32),
        ],
        scratch_shapes=[pltpu.VMEM((8, CONV_WIDTH), F32), pltpu.VMEM((8, 3 * DN_WIDTH), F32)],
        compiler_params=_cparams("arbitrary", "arbitrary"),
        name="inproj_pre_prompt",
    )(x, mod3, mod3, w_main, w_small, conv_w, dn_conv_w, hp, conv_buf, dn_buf)


def _pre_sample_kernel(pm_ref, ps_ref, cw_ref, dw_ref, hp_ref, cb_ref, db_ref,
                       yc_ref, qkv_ref, gb_ref, ncb_ref, ndb_ref):
    steps = pm_ref.shape[0]
    cgh = [pm_ref[t, :, OFF_CC:OFF_CH] * pm_ref[t, :, OFF_CH:OFF_QKV] for t in range(steps)]
    ext = [cb_ref[i] for i in range(CONV_K - 1)] + cgh
    for t in range(steps):
        y = ext[t] * cw_ref[0:1, :]
        for i in range(1, CONV_K):
            y = y + ext[t + i] * cw_ref[i:i + 1, :]
        yc_ref[t] = pm_ref[t, :, 0:OFF_CC] * y
    for i in range(CONV_K - 1):
        ncb_ref[i] = ext[len(ext) - (CONV_K - 1) + i]

    xq = [pm_ref[t, :, OFF_QKV:OFF_Z] for t in range(steps)]
    extq = [db_ref[i] for i in range(DN_CONV_K - 1)] + xq
    for t in range(steps):
        y = extq[t] * dw_ref[0:1, :]
        for i in range(1, DN_CONV_K):
            y = y + extq[t + i] * dw_ref[i:i + 1, :]
        _qkv_finish(y, qkv_ref, (t,))
        gb_ref[t] = _gates(ps_ref[t], hp_ref[...])
    for i in range(DN_CONV_K - 1):
        ndb_ref[i] = extq[len(extq) - (DN_CONV_K - 1) + i]


def _pre_sample(pm, ps, conv_w, dn_conv_w, hp, conv_buf_tm, dn_buf_tm, bsz, steps, bt):
    slab = lambda k, w: pl.BlockSpec((k, bt, w), lambda i: (0, i, 0))
    full = lambda a: pl.BlockSpec(a.shape, lambda i: (0,) * a.ndim)
    return pl.pallas_call(
        _pre_sample_kernel,
        grid=(bsz // bt,),
        in_specs=[slab(steps, P_MAIN), slab(steps, LANES), full(conv_w), full(dn_conv_w), full(hp),
                  slab(CONV_K - 1, CONV_WIDTH), slab(DN_CONV_K - 1, 3 * DN_WIDTH)],
        out_specs=[slab(steps, CONV_WIDTH), slab(steps, 3 * DN_WIDTH), slab(steps, LANES),
                   slab(CONV_K - 1, CONV_WIDTH), slab(DN_CONV_K - 1, 3 * DN_WIDTH)],
        out_shape=[
            jax.ShapeDtypeStruct((steps, bsz, CONV_WIDTH), F32),
            jax.ShapeDtypeStruct((steps, bsz, 3 * DN_WIDTH), F32),
            jax.ShapeDtypeStruct((steps, bsz, LANES), F32),
            jax.ShapeDtypeStruct((CONV_K - 1, bsz, CONV_WIDTH), F32),
            jax.ShapeDtypeStruct((DN_CONV_K - 1, bsz, 3 * DN_WIDTH), F32),
        ],
        compiler_params=_cparams("arbitrary"),
        name="pre_sample",
    )(pm.reshape(steps, bsz, P_MAIN), ps.reshape(steps, bsz, LANES), conv_w, dn_conv_w, hp,
      conv_buf_tm, dn_buf_tm)


def _cumsum_chunks(x, c):
    row = lax.broadcasted_iota(jnp.int32, x.shape, 0) & (c - 1)
    s = 1
    while s < c:
        x = x + jnp.where(row >= s, pltpu.roll(x, s, 0), 0.0)
        s *= 2
    return x


def _gdn_problems(qkv, gb, c):
    n = qkv.shape[0] // c
    gc = _cumsum_chunks(gb, c)
    q, k, v, g_b, beta_b = [], [], [], [], []
    for ci in range(n):
        rs = slice(ci * c, (ci + 1) * c)
        for h in range(DN_HEADS):
            ls = lambda base: slice(base + h * DN_HEAD_DIM, base + (h + 1) * DN_HEAD_DIM)
            q.append(qkv[rs, ls(0)])
            k.append(qkv[rs, ls(DN_WIDTH)])
            v.append(qkv[rs, ls(2 * DN_WIDTH)])
            g_b.append(jnp.broadcast_to(gc[rs, G_LANE + h:G_LANE + h + 1], (c, DN_HEAD_DIM)))
            beta_b.append(jnp.broadcast_to(gb[rs, BETA_LANE + h:BETA_LANE + h + 1], (c, DN_HEAD_DIM)))
    return tuple(jnp.stack(a) for a in (q, k, v, g_b, beta_b))


def _unit_lower_solve(m, rhs, c, small, mode):
    if small:
        sol = rhs
        for j in range(c - 1):
            sol = sol - m[:, :, j:j + 1] * sol[:, j:j + 1, :]
        return sol
    row = lax.broadcasted_iota(jnp.int32, m.shape, 1)
    col = lax.broadcasted_iota(jnp.int32, m.shape, 2)
    p = -m
    t = jnp.where(row == col, 1.0, 0.0) + p
    p = _mm(p, p, _BNN, "bf16")
    levels = int(math.log2(c))
    for lvl in range(1, levels):
        if lvl < levels - 1:
            y = _mm(jnp.concatenate([t, p], axis=1), p, _BNN, "bf16")
            t = t + y[:, :c]
            p = y[:, c:]
        else:
            t = t + _mm(t, p, _BNN, "bf16")
    x = _mm(t, rhs, _BNN, "bf16")
    resid = rhs - x - _mm(m, x, _BNN, mode)
    return x + _mm(t, resid, _BNN, "bf16")


def _gdn_intra(q, k, v, g_b, beta_b, *, c, small, mode_gram, mode_solve, mode_apply):
    n = q.shape[0]
    row = lax.broadcasted_iota(jnp.int32, (n, c, c), 1)
    col = lax.broadcasted_iota(jnp.int32, (n, c, c), 2)
    g_cc = g_b[:, :, :c]
    g_row = jnp.sum(jnp.where(row == col, g_cc, 0.0), axis=1, keepdims=True)
    gamma = jnp.exp(jnp.where(row >= col, g_cc - g_row, -jnp.inf))
    kbeta = k * beta_b
    exp_g = jnp.exp(g_b)
    gram = _mm(jnp.concatenate([kbeta, q], axis=1), k, _BNT, mode_gram)
    m = jnp.where(row > col, gram[:, :c] * gamma, 0.0)
    attn = gram[:, c:] * gamma
    rhs = jnp.concatenate([kbeta * exp_g, v * beta_b], axis=-1)
    sol = _unit_lower_solve(m, rhs, c, small, mode_solve)
    g_last = g_b[:, c - 1:c, :]
    k_dec = k * jnp.exp(g_last - g_b)
    kd = _mm(k_dec, sol, _BTN, mode_apply)
    at = _mm(attn, sol, _BNN, mode_apply)
    q_t = q * exp_g - at[..., :DN_HEAD_DIM]
    return kd[..., :DN_HEAD_DIM], kd[..., DN_HEAD_DIM:], q_t, at[..., DN_HEAD_DIM:], jnp.exp(g_last)


def _gdn_state_step(a_mat, b_mat, q_t, o_intra, d_last, s, mode):
    r = _mm(jnp.concatenate([a_mat, q_t], axis=1), s, _BNN, mode)
    s_new = s * d_last - r[:, :DN_HEAD_DIM] + b_mat
    return r[:, DN_HEAD_DIM:] + o_intra, s_new


_PROMPT_MODES = dict(mode_gram="bf16", mode_solve="x3", mode_apply="bf16")
_PROMPT_STATE_MODE = "bf16"
_SAMPLE_MODES = dict(mode_gram="bf16", mode_solve="bf16", mode_apply="bf16")
_SAMPLE_STATE_MODE = "bf16"


def _delta_prompt_kernel(qkv_ref, gb_ref, s0_ref, o_ref, sn_ref, s_scr):
    t = pl.program_id(1)
    tc = qkv_ref.shape[0]

    @pl.when(t == 0)
    def _():
        s_scr[...] = s0_ref[0]

    probs = _gdn_problems(qkv_ref[...], gb_ref[...], DN_CHUNK)
    a_mat, b_mat, q_t, o_intra, d_last = _gdn_intra(*probs, c=DN_CHUNK, small=False, **_PROMPT_MODES)
    s = s_scr[...]
    for ci in range(tc // DN_CHUNK):
        ps = slice(ci * DN_HEADS, (ci + 1) * DN_HEADS)
        o, s = _gdn_state_step(a_mat[ps], b_mat[ps], q_t[ps], o_intra[ps], d_last[ps], s,
                               _PROMPT_STATE_MODE)
        for h in range(DN_HEADS):
            o_ref[ci * DN_CHUNK:(ci + 1) * DN_CHUNK, h * DN_HEAD_DIM:(h + 1) * DN_HEAD_DIM] = o[h]
    s_scr[...] = s

    @pl.when(t == pl.num_programs(1) - 1)
    def _():
        sn_ref[0] = s


def _delta_prompt(qkv, gb, s0, bsz, seq, tc):
    n = qkv.shape[0]
    nt = seq // tc
    rows = lambda w: pl.BlockSpec((tc, w), lambda b, t: (b * nt + t, 0))
    state = pl.BlockSpec((1, DN_HEADS, DN_HEAD_DIM, DN_HEAD_DIM), lambda b, t: (b, 0, 0, 0))
    return pl.pallas_call(
        _delta_prompt_kernel,
        grid=(bsz, nt),
        in_specs=[rows(3 * DN_WIDTH), rows(LANES), state],
        out_specs=[rows(DN_WIDTH), state],
        out_shape=[
            jax.ShapeDtypeStruct((n, DN_WIDTH), F32),
            jax.ShapeDtypeStruct((bsz, DN_HEADS, DN_HEAD_DIM, DN_HEAD_DIM), F32),
        ],
        scratch_shapes=[pltpu.VMEM((DN_HEADS, DN_HEAD_DIM, DN_HEAD_DIM), F32)],
        compiler_params=_cparams("arbitrary", "arbitrary"),
        name="delta_prompt",
    )(qkv, gb, s0)


def _delta_sample_kernel(qkv_ref, gb_ref, s0_ref, *rest):
    o_ref, sn_ref = rest[-2:]
    bt = qkv_ref.shape[0]
    c = SAMPLE_CHUNK
    probs = _gdn_problems(qkv_ref[...].reshape(bt * c, 3 * DN_WIDTH), gb_ref[...].reshape(bt * c, LANES), c)
    a_mat, b_mat, q_t, o_intra, d_last = _gdn_intra(*probs, c=c, small=True, **_SAMPLE_MODES)
    s = s0_ref[0].reshape(bt * DN_HEADS, DN_HEAD_DIM, DN_HEAD_DIM)
    o, s = _gdn_state_step(a_mat, b_mat, q_t, o_intra, d_last, s, _SAMPLE_STATE_MODE)
    for b in range(bt):
        for h in range(DN_HEADS):
            o_ref[b, :, h * DN_HEAD_DIM:(h + 1) * DN_HEAD_DIM] = o[b * DN_HEADS + h]
    sn_ref[0] = s.reshape(bt, DN_HEADS, DN_HEAD_DIM, DN_HEAD_DIM)


def _delta_sample(qkv_b, gb_b, state_all, layer, carried, bt):
    bsz = qkv_b.shape[0]
    blk = lambda w: pl.BlockSpec((bt, SAMPLE_CHUNK, w), lambda i: (i, 0, 0))
    state = pl.BlockSpec((1, bt, DN_HEADS, DN_HEAD_DIM, DN_HEAD_DIM), lambda i: (layer, i, 0, 0, 0))
    in_specs, args, aliases = [blk(3 * DN_WIDTH), blk(LANES), state], [qkv_b, gb_b, state_all], {}
    if carried is not None:
        in_specs.append(pl.BlockSpec(memory_space=pl.ANY))
        args.append(carried)
        aliases = {3: 1}
    return pl.pallas_call(
        _delta_sample_kernel,
        grid=(bsz // bt,),
        in_specs=in_specs,
        out_specs=[blk(DN_WIDTH), state],
        out_shape=[
            jax.ShapeDtypeStruct((bsz, SAMPLE_CHUNK, DN_WIDTH), F32),
            jax.ShapeDtypeStruct(state_all.shape, F32),
        ],
        input_output_aliases=aliases,
        compiler_params=_cparams("arbitrary"),
        name="delta_sample",
    )(*args)


def _outproj_kernel(yc_ref, o_ref, z_ref, x_ref, g1_ref, nw_ref, w_ref, lg_ref, lb_ref, out_ref):
    o = o_ref[...]
    parts = []
    for h in range(DN_HEADS):
        oh = o[:, h * DN_HEAD_DIM:(h + 1) * DN_HEAD_DIM]
        parts.append(oh * lax.rsqrt(jnp.mean(oh * oh, -1, keepdims=True) + RMS_EPS) * nw_ref[...])
    og = jnp.concatenate(parts, axis=-1) * _silu(z_ref[...])
    mixed = jnp.concatenate([yc_ref[...], og], axis=-1).astype(BF16)
    m = jnp.dot(mixed, w_ref[...], preferred_element_type=F32)
    r = DEEPNORM_ALPHA * x_ref[...] + (1.0 + g1_ref[0]) * m
    out_ref[...] = _layer_norm(r, lg_ref[...], lb_ref[...])


def _outproj(yc, o, z_src, z_blk, x, mod3, tpg, norm_w, w_out, ln_g, ln_b, tm):
    n = x.shape[0]
    vec = lambda w: pl.BlockSpec((1, w), lambda i: (0, 0))
    return pl.pallas_call(
        _outproj_kernel,
        grid=(n // tm,),
        in_specs=[
            pl.BlockSpec((tm, CONV_WIDTH), lambda i: (i, 0)),
            pl.BlockSpec((tm, DN_WIDTH), lambda i: (i, 0)),
            pl.BlockSpec((tm, DN_WIDTH), lambda i: (i, z_blk)),
            pl.BlockSpec((tm, D_MODEL), lambda i: (i, 0)),
            _mod_spec(mod3, 2, tpg),
            vec(DN_HEAD_DIM),
            pl.BlockSpec((D_MODEL, D_MODEL), lambda i: (0, 0)),
            vec(D_MODEL), vec(D_MODEL),
        ],
        out_specs=pl.BlockSpec((tm, D_MODEL), lambda i: (i, 0)),
        out_shape=jax.ShapeDtypeStruct((n, D_MODEL), F32),
        compiler_params=_cparams("arbitrary"),
        name="outproj",
    )(yc, o, z_src, x, mod3, norm_w, w_out, ln_g, ln_b)


R_E1, R_E2, R_G1, R_G2, R_R1, R_R2 = range(6)


def _router_kernel(x_ref, sc_ref, sh_ref, wr_ref, br_ref, route_ref, route_t_ref, cnt_ref):
    tm = x_ref.shape[0]
    u = x_ref[...] * (1.0 + sc_ref[0]) + sh_ref[0]
    logits = _mm(u, wr_ref[...], mode="x3") + br_ref[...]
    lane = lax.broadcasted_iota(jnp.int32, logits.shape, 1)
    valid = lane < N_EXPERTS
    logits = jnp.where(valid, logits, -jnp.inf)
    ex = jnp.exp(logits - jnp.max(logits, -1, keepdims=True))
    probs = jnp.where(valid, ex / jnp.sum(ex, -1, keepdims=True), -1.0)
    p1 = jnp.max(probs, -1, keepdims=True)
    i1 = jnp.min(jnp.where(probs == p1, lane, LANES), -1, keepdims=True)
    rest = jnp.where(lane == i1, -1.0, probs)
    p2 = jnp.max(rest, -1, keepdims=True)
    i2 = jnp.min(jnp.where(rest == p2, lane, LANES), -1, keepdims=True)
    tot = p1 + p2
    sel = jnp.where(lane == i1, 1.0, 0.0) + jnp.where(lane == i2, 1.0, 0.0)
    incl = _cumsum_chunks(sel, tm)
    excl = incl - sel
    r1 = jnp.sum(jnp.where(lane == i1, excl, 0.0), -1, keepdims=True)
    r2 = jnp.sum(jnp.where(lane == i2, excl, 0.0), -1, keepdims=True)
    cnt_ref[0] = incl[tm - 1:tm, :]
    route = jnp.zeros_like(logits)
    for j, val in enumerate((i1.astype(F32), i2.astype(F32), p1 / tot, p2 / tot, r1, r2)):
        route = jnp.where(lane == j, val, route)
    route_ref[...] = route
    route_t_ref[0] = jnp.transpose(route)[:SUBLANES, :]


def _router(x, mod3, tpg, w_router, b_router, tm):
    n = x.shape[0]
    return pl.pallas_call(
        _router_kernel,
        grid=(n // tm,),
        in_specs=[
            pl.BlockSpec((tm, D_MODEL), lambda i: (i, 0)),
            _mod_spec(mod3, 4, tpg),
            _mod_spec(mod3, 3, tpg),
            pl.BlockSpec((D_MODEL, LANES), lambda i: (0, 0)),
            pl.BlockSpec((1, LANES), lambda i: (0, 0)),
        ],
        out_specs=[pl.BlockSpec((tm, LANES), lambda i: (i, 0)),
                   pl.BlockSpec((1, SUBLANES, tm), lambda i: (i, 0, 0)),
                   pl.BlockSpec((1, 1, LANES), lambda i: (i, 0, 0))],
        out_shape=[jax.ShapeDtypeStruct((n, LANES), F32),
                   jax.ShapeDtypeStruct((n // tm, SUBLANES, tm), F32),
                   jax.ShapeDtypeStruct((n // tm, 1, LANES), F32)],
        compiler_params=_cparams("arbitrary"),
        name="router",
    )(x, mod3, mod3, w_router, b_router)


def _dispatch_plan(route_t, counts, tme):
    n_rt, _, tm = route_t.shape
    n = n_rt * tm
    cnt = counts[:, 0, :N_EXPERTS].astype(jnp.int32)
    tile_off = jnp.cumsum(cnt, axis=0) - cnt
    tot = jnp.sum(cnt, axis=0)
    gsz = (tot + tme - 1) // tme * tme
    gend = jnp.cumsum(gsz)
    goff = gend - gsz
    base = (goff[None, :] + tile_off)[:, :, None]
    ids = jnp.arange(N_EXPERTS, dtype=jnp.int32)[None, :, None]
    field = lambda j: route_t[:, j:j + 1, :].astype(jnp.int32)
    pick = lambda e: jnp.sum(jnp.where(ids == e, base, 0), axis=1, keepdims=True)
    slot1 = (pick(field(R_E1)) + field(R_R1)).reshape(n)
    slot2 = (pick(field(R_E2)) + field(R_R2)).reshape(n)
    n_tiles = 2 * n // tme + N_EXPERTS
    start = jnp.arange(n_tiles, dtype=jnp.int32) * tme
    valid = (start < gend[-1]).astype(jnp.int32)
    blk = jnp.minimum(jnp.arange(n_tiles, dtype=jnp.int32), gend[-1] // tme - 1)
    tile_e = jnp.sum((blk * tme)[:, None] >= gend[None, :], axis=1).astype(jnp.int32)
    zoff = jnp.where(gsz > 0, gend - tme, (n_tiles + jnp.arange(N_EXPERTS, dtype=jnp.int32)) * tme)
    return slot1, slot2, zoff, tile_e, valid, blk, n_tiles


def _dispatch_kernel(s1_ref, s2_ref, zoff_ref, x_ref, sc_ref, sh_ref, ug_hbm, ubuf, zbuf, sem):
    i = pl.program_id(0)
    tm = x_ref.shape[0]
    tme = zbuf.shape[0]

    @pl.when(i == 0)
    def _():
        zbuf[...] = jnp.zeros_like(zbuf)
        fill = lambda e: pltpu.make_async_copy(
            zbuf, ug_hbm.at[pl.ds(pl.multiple_of(zoff_ref[e], 8), tme)], sem.at[2])
        for e in range(N_EXPERTS):
            fill(e).start()
        for e in range(N_EXPERTS):
            fill(e).wait()

    ubuf[...] = (x_ref[...] * (1.0 + sc_ref[0]) + sh_ref[0]).reshape(ubuf.shape)
    base = i * tm

    def issue(j, carry):
        for k in range(SUBLANES):
            row = ubuf.at[j, pl.ds(k, 1)]
            t = base + j * SUBLANES + k
            pltpu.make_async_copy(row, ug_hbm.at[pl.ds(s1_ref[t], 1)], sem.at[0]).start()
            pltpu.make_async_copy(row, ug_hbm.at[pl.ds(s2_ref[t], 1)], sem.at[1]).start()
        return carry

    lax.fori_loop(0, tm // SUBLANES, issue, 0)
    pltpu.make_async_copy(ubuf, ubuf, sem.at[0]).wait()
    pltpu.make_async_copy(ubuf, ubuf, sem.at[1]).wait()


def _dispatch(x, mod3, tpg, slot1, slot2, zoff, tm, tme, rows):
    n = x.shape[0]
    return pl.pallas_call(
        _dispatch_kernel,
        grid_spec=pltpu.PrefetchScalarGridSpec(
            num_scalar_prefetch=3,
            grid=(n // tm,),
            in_specs=[
                pl.BlockSpec((tm, D_MODEL), lambda i, *_: (i, 0)),
                _mod_spec(mod3, 4, tpg),
                _mod_spec(mod3, 3, tpg),
            ],
            out_specs=pl.BlockSpec(memory_space=pl.ANY),
            scratch_shapes=[pltpu.VMEM((tm // SUBLANES, SUBLANES, D_MODEL), F32),
                            pltpu.VMEM((tme, D_MODEL), F32), pltpu.SemaphoreType.DMA((3,))],
        ),
        out_shape=jax.ShapeDtypeStruct((rows, D_MODEL), F32),
        compiler_params=_cparams("arbitrary"),
        name="dispatch",
    )(slot1, slot2, zoff, x, mod3, mod3)


def _expert_kernel(te_ref, tv_ref, tb_ref, ug_ref, wg_ref, wu_ref, wd_ref, ys_ref):
    i = pl.program_id(0)

    @pl.when(tv_ref[i] == 1)
    def _():
        u = ug_ref[...].astype(BF16)
        h = _silu(jnp.dot(u, wg_ref[0], preferred_element_type=F32)) * jnp.dot(
            u, wu_ref[0], preferred_element_type=F32)
        ys_ref[...] = jnp.dot(h.astype(BF16), wd_ref[0], preferred_element_type=F32)

    @pl.when(tv_ref[i] == 0)
    def _():
        ys_ref[...] = jnp.zeros_like(ys_ref)


def _experts(ug, tile_e, valid, blk, w_gate, w_up, w_down, tme, n_tiles):
    d_e = w_gate.shape[-1]
    return pl.pallas_call(
        _expert_kernel,
        grid_spec=pltpu.PrefetchScalarGridSpec(
            num_scalar_prefetch=3,
            grid=(n_tiles,),
            in_specs=[
                pl.BlockSpec((tme, D_MODEL), lambda i, te, tv, tb: (tb[i], 0)),
                pl.BlockSpec((1, D_MODEL, d_e), lambda i, te, tv, tb: (te[i], 0, 0)),
                pl.BlockSpec((1, D_MODEL, d_e), lambda i, te, tv, tb: (te[i], 0, 0)),
                pl.BlockSpec((1, d_e, D_MODEL), lambda i, te, tv, tb: (te[i], 0, 0)),
            ],
            out_specs=pl.BlockSpec((tme, D_MODEL), lambda i, te, tv, tb: (i, 0)),
        ),
        out_shape=jax.ShapeDtypeStruct((n_tiles * tme, D_MODEL), F32),
        compiler_params=_cparams("arbitrary"),
        name="experts",
    )(tile_e, valid, blk, ug, w_gate, w_up, w_down)


def _combine_kernel(s1_ref, s2_ref, ys_hbm, route_ref, x_ref, g2_ref, lg_ref, lb_ref, out_ref,
                    y1buf, y2buf, sem):
    i = pl.program_id(0)
    tm = x_ref.shape[0]
    base = i * tm

    def issue(j, carry):
        for k in range(SUBLANES):
            t = base + j * SUBLANES + k
            pltpu.make_async_copy(ys_hbm.at[pl.ds(s1_ref[t], 1)], y1buf.at[j, pl.ds(k, 1)], sem.at[0]).start()
            pltpu.make_async_copy(ys_hbm.at[pl.ds(s2_ref[t], 1)], y2buf.at[j, pl.ds(k, 1)], sem.at[1]).start()
        return carry

    lax.fori_loop(0, tm // SUBLANES, issue, 0)
    pltpu.make_async_copy(y1buf, y1buf, sem.at[0]).wait()
    pltpu.make_async_copy(y2buf, y2buf, sem.at[1]).wait()
    route = route_ref[...]
    y1 = y1buf[...].reshape(tm, D_MODEL)
    y2 = y2buf[...].reshape(tm, D_MODEL)
    f = route[:, R_G1:R_G1 + 1] * y1 + route[:, R_G2:R_G2 + 1] * y2
    r = DEEPNORM_ALPHA * x_ref[...] + (1.0 + g2_ref[0]) * f
    out_ref[...] = _layer_norm(r, lg_ref[...], lb_ref[...])


def _combine(ys, slot1, slot2, route, x, mod3, tpg, ln_g, ln_b, tm):
    n = x.shape[0]
    vec = lambda w: pl.BlockSpec((1, w), lambda i, *_: (0, 0))
    return pl.pallas_call(
        _combine_kernel,
        grid_spec=pltpu.PrefetchScalarGridSpec(
            num_scalar_prefetch=2,
            grid=(n // tm,),
            in_specs=[
                pl.BlockSpec(memory_space=pl.ANY),
                pl.BlockSpec((tm, LANES), lambda i, *_: (i, 0)),
                pl.BlockSpec((tm, D_MODEL), lambda i, *_: (i, 0)),
                _mod_spec(mod3, 5, tpg),
                vec(D_MODEL), vec(D_MODEL),
            ],
            out_specs=pl.BlockSpec((tm, D_MODEL), lambda i, *_: (i, 0)),
            scratch_shapes=[pltpu.VMEM((tm // SUBLANES, SUBLANES, D_MODEL), F32),
                            pltpu.VMEM((tm // SUBLANES, SUBLANES, D_MODEL), F32),
                            pltpu.SemaphoreType.DMA((2,))],
        ),
        out_shape=jax.ShapeDtypeStruct((n, D_MODEL), F32),
        compiler_params=_cparams("arbitrary"),
        name="combine",
    )(slot1, slot2, ys, route, x, mod3, ln_g, ln_b)


def _moe(x, mod3, tpg, w_router, b_router, w_gate, w_up, w_down, ln_g, ln_b, tm, tme):
    route, route_t, counts = _router(x, mod3, tpg, w_router, b_router, tm)
    slot1, slot2, zoff, tile_e, valid, blk, n_tiles = _dispatch_plan(route_t, counts, tme)
    ug = _dispatch(x, mod3, tpg, slot1, slot2, zoff, tm, tme, (n_tiles + N_EXPERTS) * tme)
    ys = _experts(ug, tile_e, valid, blk, w_gate, w_up, w_down, tme, n_tiles)
    return _combine(ys, slot1, slot2, route, x, mod3, tpg, ln_g, ln_b, tm)


def _ffn_kernel(x_ref, sc_ref, sh_ref, g2_ref, wg_ref, wu_ref, wd_ref, lg_ref, lb_ref,
                out_ref, u_scr, acc_scr):
    f = pl.program_id(1)

    @pl.when(f == 0)
    def _():
        u_scr[...] = (x_ref[...] * (1.0 + sc_ref[0]) + sh_ref[0]).astype(BF16)
        acc_scr[...] = jnp.zeros_like(acc_scr)

    u = u_scr[...]
    h = _silu(jnp.dot(u, wg_ref[...], preferred_element_type=F32)) * jnp.dot(
        u, wu_ref[...], preferred_element_type=F32)
    acc_scr[...] += jnp.dot(h.astype(BF16), wd_ref[...], preferred_element_type=F32)

    @pl.when(f == pl.num_programs(1) - 1)
    def _():
        r = DEEPNORM_ALPHA * x_ref[...] + (1.0 + g2_ref[0]) * acc_scr[...]
        out_ref[...] = _layer_norm(r, lg_ref[...], lb_ref[...])


def _ffn(x, mod3, tpg, w_gate, w_up, w_down, ln_g, ln_b, tm, tf):
    n = x.shape[0]
    d_ff = w_gate.shape[-1]
    vec = lambda w: pl.BlockSpec((1, w), lambda i, f: (0, 0))
    return pl.pallas_call(
        _ffn_kernel,
        grid=(n // tm, d_ff // tf),
        in_specs=[
            pl.BlockSpec((tm, D_MODEL), lambda i, f: (i, 0)),
            _mod_spec(mod3, 4, tpg),
            _mod_spec(mod3, 3, tpg),
            _mod_spec(mod3, 5, tpg),
            pl.BlockSpec((D_MODEL, tf), lambda i, f: (0, f)),
            pl.BlockSpec((D_MODEL, tf), lambda i, f: (0, f)),
            pl.BlockSpec((tf, D_MODEL), lambda i, f: (f, 0)),
            vec(D_MODEL), vec(D_MODEL),
        ],
        out_specs=pl.BlockSpec((tm, D_MODEL), lambda i, f: (i, 0)),
        out_shape=jax.ShapeDtypeStruct((n, D_MODEL), F32),
        scratch_shapes=[pltpu.VMEM((tm, D_MODEL), BF16), pltpu.VMEM((tm, D_MODEL), F32)],
        compiler_params=_cparams("arbitrary", "arbitrary"),
        name="ffn_dense",
    )(x, mod3, mod3, mod3, w_gate, w_up, w_down, ln_g, ln_b)


def _trunk(x, mod3s, rows_per_group, prm, tiles, mixer_fn):
    tm, tme = tiles["tm"], tiles["tme"]
    tpg = rows_per_group // tm
    states = []
    for l in range(DEPTH):
        mod3 = mod3s[l]
        yc, o, z_src, z_blk, st = mixer_fn(l, x, mod3)
        states.append(st)
        x = _outproj(yc, o, z_src, z_blk, x, mod3, tpg, prm["norm_w"][l], prm["w_out"][l],
                     prm["ln_g"][l, 0:1], prm["ln_b"][l, 0:1], tm)
        j = l // 2
        if l % 2 == 0:
            x = _ffn(x, mod3, rows_per_group // tiles["tm_ffn"], prm["w_ff_gate"][j], prm["w_ff_up"][j],
                     prm["w_ff_down"][j], prm["ln_g"][l, 1:2], prm["ln_b"][l, 1:2],
                     tiles["tm_ffn"], tiles["tf"])
        else:
            x = _moe(x, mod3, tpg, prm["w_router"][j], prm["b_router"][j], prm["w_exp_gate"][j],
                     prm["w_exp_up"][j], prm["w_exp_down"][j], prm["ln_g"][l, 1:2], prm["ln_b"][l, 1:2],
                     tm, tme)
    return x, states


def kernel(x_prompt, x_sample, state_conv, state_dn_conv, state_dn, c_prompt, c_sample, w_in, conv_w,
           dn_conv_w, a_log, dt_bias, dn_norm_w, w_out, w_ada, b_ada, ln_g, ln_b, w_ff_gate, w_ff_up,
           w_ff_down, w_router, b_router, w_exp_gate, w_exp_up, w_exp_down):
    bp, seq, d = x_prompt.shape
    bs, steps, _ = x_sample.shape
    pad_lanes = LANES - N_EXPERTS
    hp = jnp.zeros((DEPTH, 2, LANES), F32)
    hp = hp.at[:, 0, G_LANE:].set(a_log).at[:, 1, G_LANE:].set(dt_bias)
    w_in_t = jnp.swapaxes(w_in, 1, 2)
    prm = {
        "w_main": w_in_t[:, :P_MAIN].astype(BF16),
        "w_small": w_in_t[:, w_in_t.shape[1] - LANES:],
        "norm_w": dn_norm_w.reshape(DEPTH, 1, DN_HEAD_DIM),
        "w_out": w_out.astype(BF16),
        "ln_g": ln_g, "ln_b": ln_b,
        "w_ff_gate": w_ff_gate.astype(BF16), "w_ff_up": w_ff_up.astype(BF16),
        "w_ff_down": w_ff_down.astype(BF16),
        "w_router": jnp.pad(w_router, ((0, 0), (0, 0), (0, pad_lanes))),
        "b_router": jnp.pad(b_router, ((0, 0), (0, pad_lanes)))[:, None, :],
        "w_exp_gate": w_exp_gate.astype(BF16), "w_exp_up": w_exp_up.astype(BF16),
        "w_exp_down": w_exp_down.astype(BF16),
    }

    mod = _ada(jnp.concatenate([c_prompt, c_sample], axis=0), w_ada, b_ada)
    mod_p = [mod[l, :bp].reshape(bp, 1, 6 * d) for l in range(DEPTH)]
    mod_s = [mod[l, bp:].reshape(1, bs, 6 * d) for l in range(DEPTH)]

    tm_p, tc_delta = 512, 512
    zc = jnp.zeros((bp, CONV_K - 1, CONV_WIDTH), F32)
    zd = jnp.zeros((bp, DN_CONV_K - 1, 3 * DN_WIDTH), F32)
    zs = jnp.zeros((bp, DN_HEADS, DN_HEAD_DIM, DN_HEAD_DIM), F32)

    def mixer_prompt(l, x, mod3):
        yc, qkv, z, gb, ncb, ndb = _inproj_pre_prompt(
            x, mod3, prm["w_main"][l], prm["w_small"][l], conv_w[l], dn_conv_w[l], hp[l], zc, zd,
            bp, seq, tm_p)
        o, s_new = _delta_prompt(qkv, gb, zs, bp, seq, tc_delta)
        return yc, o, z, 0, (ncb, ndb, s_new)

    y_p, st_p = _trunk(x_prompt.reshape(bp * seq, d), mod_p, seq, prm,
                       dict(tm=tm_p, tm_ffn=512, tf=w_ff_gate.shape[-1] // 2, tme=512), mixer_prompt)

    bt_pre, bt_delta = 32, 8
    dn_states = []

    def mixer_sample(l, x, mod3):
        pm, ps = _inproj(x, mod3, steps, prm["w_main"][l], prm["w_small"][l], bs)
        cb_tm = jnp.transpose(state_conv[l], (1, 0, 2))
        db_tm = jnp.transpose(state_dn_conv[l], (1, 0, 2))
        yc, qkv, gb, ncb, ndb = _pre_sample(pm, ps, conv_w[l], dn_conv_w[l], hp[l], cb_tm, db_tm,
                                            bs, steps, bt_pre)
        to_b = lambda a: jnp.pad(jnp.transpose(a, (1, 0, 2)), ((0, 0), (0, SAMPLE_CHUNK - steps), (0, 0)))
        o_b, s_all = _delta_sample(to_b(qkv), to_b(gb), state_dn, l, dn_states[-1] if dn_states else None,
                                   bt_delta)
        dn_states.append(s_all)
        o = jnp.transpose(o_b[:, :steps], (1, 0, 2)).reshape(steps * bs, DN_WIDTH)
        return (yc.reshape(steps * bs, CONV_WIDTH), o, pm, OFF_Z // DN_WIDTH,
                (jnp.transpose(ncb, (1, 0, 2)), jnp.transpose(ndb, (1, 0, 2))))

    x_s = jnp.transpose(x_sample, (1, 0, 2)).reshape(steps * bs, d)
    y_s, st_s = _trunk(x_s, mod_s, steps * bs, prm,
                       dict(tm=bs, tm_ffn=bs, tf=w_ff_gate.shape[-1] // 2, tme=128), mixer_sample)
    y_s = jnp.transpose(y_s.reshape(steps, bs, d), (1, 0, 2))

    stack = lambda sts, i: jnp.stack([s[i] for s in sts])
    return (y_p.reshape(bp, seq, d), y_s,
            stack(st_p, 0), stack(st_p, 1), stack(st_p, 2),
            stack(st_s, 0), stack(st_s, 1), dn_states[-1])
```

```python
import functools
import math

import jax
import jax.numpy as jnp
from jax import lax
from jax.experimental import pallas as pl
from jax.experimental.pallas import tpu as pltpu

F32 = jnp.float32
BF16 = jnp.bfloat16

D_MODEL = 1024
DEPTH = 2
CONV_WIDTH = 512
CONV_K = 3
DN_HEADS = 4
DN_HEAD_DIM = 128
DN_WIDTH = DN_HEADS * DN_HEAD_DIM
DN_CONV_K = 4
DN_CHUNK = 64
N_EXPERTS = 8
DEEPNORM_ALPHA = (2.0 * DEPTH) ** 0.25
LN_EPS = 1e-5
RMS_EPS = 1e-6

OFF_CC = CONV_WIDTH
OFF_CH = 2 * CONV_WIDTH
OFF_QKV = 3 * CONV_WIDTH
OFF_Z = OFF_QKV + 3 * DN_WIDTH
P_MAIN = OFF_Z + DN_WIDTH
LANES = 128
SUBLANES = 8
BETA_LANE = LANES - 2 * DN_HEADS
G_LANE = LANES - DN_HEADS
SAMPLE_CHUNK = 8

VMEM_LIMIT = 56 * 1024 * 1024


def _cparams(*sem):
    return pltpu.CompilerParams(dimension_semantics=sem, vmem_limit_bytes=VMEM_LIMIT)


def _silu(x):
    return x * jax.nn.sigmoid(x)


def _layer_norm(r, g, b):
    mu = jnp.mean(r, -1, keepdims=True)
    xc = r - mu
    var = jnp.mean(xc * xc, -1, keepdims=True)
    return xc * lax.rsqrt(var + LN_EPS) * g + b


def _split_bf16(a):
    hi = a.astype(BF16)
    lo = (a - hi.astype(F32)).astype(BF16)
    return hi, lo


_NN = (((1,), (0,)), ((), ()))
_NT = (((1,), (1,)), ((), ()))
_BNN = (((2,), (1,)), ((0,), (0,)))
_BNT = (((2,), (2,)), ((0,), (0,)))
_BTN = (((1,), (1,)), ((0,), (0,)))


def _mm(a, b, dims=_NN, mode="bf16"):
    if mode == "f32":
        return lax.dot_general(a, b, dims, precision=lax.Precision.HIGHEST, preferred_element_type=F32)
    if mode == "bf16":
        return lax.dot_general(a.astype(BF16), b.astype(BF16), dims, preferred_element_type=F32)
    ah, al = _split_bf16(a)
    bh, bl = _split_bf16(b)
    d = functools.partial(lax.dot_general, dimension_numbers=dims, preferred_element_type=F32)
    return d(ah, bh) + (d(ah, bl) + d(al, bh))


def _ada_kernel(c_ref, w_ref, b_ref, o_ref):
    s = _silu(c_ref[...]).astype(BF16)
    o_ref[0] = jnp.dot(s, w_ref[0].astype(BF16), preferred_element_type=F32) + b_ref[0]


def _ada(c_all, w_ada, b_ada):
    rows = c_all.shape[0]
    tn = 1536
    return pl.pallas_call(
        _ada_kernel,
        grid=(DEPTH, 6 * D_MODEL // tn),
        in_specs=[
            pl.BlockSpec((rows, D_MODEL), lambda l, j: (0, 0)),
            pl.BlockSpec((1, D_MODEL, tn), lambda l, j: (l, 0, j)),
            pl.BlockSpec((1, 1, tn), lambda l, j: (l, 0, j)),
        ],
        out_specs=pl.BlockSpec((1, rows, tn), lambda l, j: (l, 0, j)),
        out_shape=jax.ShapeDtypeStruct((DEPTH, rows, 6 * D_MODEL), F32),
        compiler_params=_cparams("arbitrary", "arbitrary"),
        name="ada",
    )(c_all, w_ada, b_ada.reshape(DEPTH, 1, 6 * D_MODEL))


def _mod_spec(mod3, chunk, tiles_per_group):
    r = mod3.shape[1]
    return pl.BlockSpec((1, r, D_MODEL), lambda i, *_: (i // tiles_per_group, 0, chunk))


def _inproj_kernel(x_ref, sc_ref, sh_ref, wm_ref, ws_ref, pm_ref, ps_ref):
    u = x_ref[...] * (1.0 + sc_ref[0]) + sh_ref[0]
    pm_ref[...] = lax.dot_general(u.astype(BF16), wm_ref[...], _NT, preferred_element_type=F32)
    ps_ref[...] = _mm(u, ws_ref[...], _NT, mode="x3")


def _inproj(x, mod3, tpg, w_main, w_small, tm):
    n = x.shape[0]
    return pl.pallas_call(
        _inproj_kernel,
        grid=(n // tm,),
        in_specs=[
            pl.BlockSpec((tm, D_MODEL), lambda i: (i, 0)),
            _mod_spec(mod3, 1, tpg),
            _mod_spec(mod3, 0, tpg),
            pl.BlockSpec((P_MAIN, D_MODEL), lambda i: (0, 0)),
            pl.BlockSpec((LANES, D_MODEL), lambda i: (0, 0)),
        ],
        out_specs=[
            pl.BlockSpec((tm, P_MAIN), lambda i: (i, 0)),
            pl.BlockSpec((tm, LANES), lambda i: (i, 0)),
        ],
        out_shape=[
            jax.ShapeDtypeStruct((n, P_MAIN), F32),
            jax.ShapeDtypeStruct((n, LANES), F32),
        ],
        compiler_params=_cparams("arbitrary"),
        name="inproj",
    )(x, mod3, mod3, w_main, w_small)


def _gates(ps, hp):
    lane = lax.broadcasted_iota(jnp.int32, ps.shape, 1)
    beta = jax.nn.sigmoid(ps)
    g = -jnp.exp(hp[0:1, :]) * jax.nn.softplus(ps + hp[1:2, :])
    return jnp.where(lane >= G_LANE, g, jnp.where(lane >= BETA_LANE, beta, 0.0))


def _qkv_finish(y, out_ref, idx):
    y = _silu(y)
    for h in range(3 * DN_HEADS):
        seg = y[:, h * DN_HEAD_DIM:(h + 1) * DN_HEAD_DIM]
        if h < 2 * DN_HEADS:
            seg = seg * lax.rsqrt(jnp.sum(seg * seg, -1, keepdims=True) + RMS_EPS)
            if h < DN_HEADS:
                seg = seg * (DN_HEAD_DIM ** -0.5)
        out_ref[idx + (slice(None), slice(h * DN_HEAD_DIM, (h + 1) * DN_HEAD_DIM))] = seg


def _shifted(x, tail_ref, i, row):
    s = pltpu.roll(x, i, 0)
    for r in range(i):
        s = jnp.where(row == r, tail_ref[8 - i + r:8 - i + r + 1, :], s)
    return s


def _inproj_pre_prompt_kernel(x_ref, sc_ref, sh_ref, wm_ref, ws_ref, cw_ref, dw_ref, hp_ref, cb_ref, db_ref,
                              yc_ref, qkv_ref, z_ref, gb_ref, ncb_ref, ndb_ref, tc_scr, td_scr):
    t = pl.program_id(1)
    tm = x_ref.shape[0]

    @pl.when(t == 0)
    def _():
        tc_scr[8 - (CONV_K - 1):8, :] = cb_ref[0]
        td_scr[8 - (DN_CONV_K - 1):8, :] = db_ref[0]

    u_f32 = x_ref[...] * (1.0 + sc_ref[0]) + sh_ref[0]
    u = u_f32.astype(BF16)
    proj = lambda lo, hi: lax.dot_general(u, wm_ref[lo:hi, :], _NT, preferred_element_type=F32)

    row_d = lax.broadcasted_iota(jnp.int32, (tm, 3 * DN_WIDTH), 0)
    xq = proj(OFF_QKV, OFF_Z)
    yq = _shifted(xq, td_scr, 3, row_d) * dw_ref[0:1, :]
    yq = yq + _shifted(xq, td_scr, 2, row_d) * dw_ref[1:2, :]
    yq = yq + _shifted(xq, td_scr, 1, row_d) * dw_ref[2:3, :]
    yq = yq + xq * dw_ref[3:4, :]
    _qkv_finish(yq, qkv_ref, ())
    last_d = xq[tm - (DN_CONV_K - 1):tm, :]

    row_c = lax.broadcasted_iota(jnp.int32, (tm, CONV_WIDTH), 0)
    cgh = proj(OFF_CC, OFF_CH) * proj(OFF_CH, OFF_QKV)
    y = _shifted(cgh, tc_scr, 2, row_c) * cw_ref[0:1, :]
    y = y + _shifted(cgh, tc_scr, 1, row_c) * cw_ref[1:2, :]
    y = y + cgh * cw_ref[2:3, :]
    yc_ref[...] = (proj(0, OFF_CC) * y).astype(yc_ref.dtype)
    last_c = cgh[tm - (CONV_K - 1):tm, :]

    z_ref[...] = proj(OFF_Z, P_MAIN).astype(z_ref.dtype)
    gb_ref[...] = _gates(_mm(u_f32, ws_ref[...], _NT, mode="x3"), hp_ref[...])

    tc_scr[8 - (CONV_K - 1):8, :] = last_c
    td_scr[8 - (DN_CONV_K - 1):8, :] = last_d

    @pl.when(t == pl.num_programs(1) - 1)
    def _():
        ncb_ref[0] = last_c
        ndb_ref[0] = last_d


def _inproj_pre_prompt(x, mod3, w_main, w_small, conv_w, dn_conv_w, hp, conv_buf, dn_buf, bsz, seq, tm):
    n = x.shape[0]
    nt = seq // tm
    rows = lambda w: pl.BlockSpec((tm, w), lambda b, t: (b * nt + t, 0))
    full = lambda a: pl.BlockSpec(a.shape, lambda b, t: (0,) * a.ndim)
    per_b = lambda k, w: pl.BlockSpec((1, k, w), lambda b, t: (b, 0, 0))
    mod = lambda chunk: pl.BlockSpec((1, 1, D_MODEL), lambda b, t: (b, 0, chunk))
    return pl.pallas_call(
        _inproj_pre_prompt_kernel,
        grid=(bsz, nt),
        in_specs=[rows(D_MODEL), mod(1), mod(0), full(w_main), full(w_small),
                  full(conv_w), full(dn_conv_w), full(hp),
                  per_b(CONV_K - 1, CONV_WIDTH), per_b(DN_CONV_K - 1, 3 * DN_WIDTH)],
        out_specs=[rows(CONV_WIDTH), rows(3 * DN_WIDTH), rows(DN_WIDTH), rows(LANES),
                   per_b(CONV_K - 1, CONV_WIDTH), per_b(DN_CONV_K - 1, 3 * DN_WIDTH)],
        out_shape=[
            jax.ShapeDtypeStruct((n, CONV_WIDTH), BF16),
            jax.ShapeDtypeStruct((n, 3 * DN_WIDTH), F32),
            jax.ShapeDtypeStruct((n, DN_WIDTH), BF16),
            jax.ShapeDtypeStruct((n, LANES), F32),
            jax.ShapeDtypeStruct((bsz, CONV_K - 1, CONV_WIDTH), F32),
            jax.ShapeDtypeStruct((bsz, DN_CONV_K - 1, 3 * DN_WIDTH), F32),
        ],
        scratch_shapes=[pltpu.VMEM((8, CONV_WIDTH), F32), pltpu.VMEM((8, 3 * DN_WIDTH), F32)],
        compiler_params=_cparams("arbitrary", "arbitrary"),
        name="inproj_pre_prompt",
    )(x, mod3, mod3, w_main, w_small, conv_w, dn_conv_w, hp, conv_buf, dn_buf)


def _pre_sample_kernel(pm_ref, ps_ref, cw_ref, dw_ref, hp_ref, cb_ref, db_ref,
                       yc_ref, qkv_ref, gb_ref, ncb_ref, ndb_ref):
    steps = pm_ref.shape[0]
    cgh = [pm_ref[t, :, OFF_CC:OFF_CH] * pm_ref[t, :, OFF_CH:OFF_QKV] for t in range(steps)]
    ext = [cb_ref[i] for i in range(CONV_K - 1)] + cgh
    for t in range(steps):
        y = ext[t] * cw_ref[0:1, :]
        for i in range(1, CONV_K):
            y = y + ext[t + i] * cw_ref[i:i + 1, :]
        yc_ref[t] = pm_ref[t, :, 0:OFF_CC] * y
    for i in range(CONV_K - 1):
        ncb_ref[i] = ext[len(ext) - (CONV_K - 1) + i]

    xq = [pm_ref[t, :, OFF_QKV:OFF_Z] for t in range(steps)]
    extq = [db_ref[i] for i in range(DN_CONV_K - 1)] + xq
    for t in range(steps):
        y = extq[t] * dw_ref[0:1, :]
        for i in range(1, DN_CONV_K):
            y = y + extq[t + i] * dw_ref[i:i + 1, :]
        _qkv_finish(y, qkv_ref, (t,))
        gb_ref[t] = _gates(ps_ref[t], hp_ref[...])
    for i in range(DN_CONV_K - 1):
        ndb_ref[i] = extq[len(extq) - (DN_CONV_K - 1) + i]


def _pre_sample(pm, ps, conv_w, dn_conv_w, hp, conv_buf_tm, dn_buf_tm, bsz, steps, bt):
    slab = lambda k, w: pl.BlockSpec((k, bt, w), lambda i: (0, i, 0))
    full = lambda a: pl.BlockSpec(a.shape, lambda i: (0,) * a.ndim)
    return pl.pallas_call(
        _pre_sample_kernel,
        grid=(bsz // bt,),
        in_specs=[slab(steps, P_MAIN), slab(steps, LANES), full(conv_w), full(dn_conv_w), full(hp),
                  slab(CONV_K - 1, CONV_WIDTH), slab(DN_CONV_K - 1, 3 * DN_WIDTH)],
        out_specs=[slab(steps, CONV_WIDTH), slab(steps, 3 * DN_WIDTH), slab(steps, LANES),
                   slab(CONV_K - 1, CONV_WIDTH), slab(DN_CONV_K - 1, 3 * DN_WIDTH)],
        out_shape=[
            jax.ShapeDtypeStruct((steps, bsz, CONV_WIDTH), F32),
            jax.ShapeDtypeStruct((steps, bsz, 3 * DN_WIDTH), F32),
            jax.ShapeDtypeStruct((steps, bsz, LANES), F32),
            jax.ShapeDtypeStruct((CONV_K - 1, bsz, CONV_WIDTH), F32),
            jax.ShapeDtypeStruct((DN_CONV_K - 1, bsz, 3 * DN_WIDTH), F32),
        ],
        compiler_params=_cparams("arbitrary"),
        name="pre_sample",
    )(pm.reshape(steps, bsz, P_MAIN), ps.reshape(steps, bsz, LANES), conv_w, dn_conv_w, hp,
      conv_buf_tm, dn_buf_tm)


def _cumsum_chunks(x, c):
    row = lax.broadcasted_iota(jnp.int32, x.shape, 0) & (c - 1)
    s = 1
    while s < c:
        x = x + jnp.where(row >= s, pltpu.roll(x, s, 0), 0.0)
        s *= 2
    return x


def _gdn_problems(qkv, gb, c):
    n = qkv.shape[0] // c
    gc = _cumsum_chunks(gb, c)
    q, k, v, g_b, beta_b = [], [], [], [], []
    for ci in range(n):
        rs = slice(ci * c, (ci + 1) * c)
        for h in range(DN_HEADS):
            ls = lambda base: slice(base + h * DN_HEAD_DIM, base + (h + 1) * DN_HEAD_DIM)
            q.append(qkv[rs, ls(0)])
            k.append(qkv[rs, ls(DN_WIDTH)])
            v.append(qkv[rs, ls(2 * DN_WIDTH)])
            g_b.append(jnp.broadcast_to(gc[rs, G_LANE + h:G_LANE + h + 1], (c, DN_HEAD_DIM)))
            beta_b.append(jnp.broadcast_to(gb[rs, BETA_LANE + h:BETA_LANE + h + 1], (c, DN_HEAD_DIM)))
    return tuple(jnp.stack(a) for a in (q, k, v, g_b, beta_b))


def _unit_lower_solve(m, rhs, c, small, mode):
    if small:
        sol = rhs
        for j in range(c - 1):
            sol = sol - m[:, :, j:j + 1] * sol[:, j:j + 1, :]
        return sol
    row = lax.broadcasted_iota(jnp.int32, m.shape, 1)
    col = lax.broadcasted_iota(jnp.int32, m.shape, 2)
    p = -m
    t = jnp.where(row == col, 1.0, 0.0) + p
    p = _mm(p, p, _BNN, "bf16")
    levels = int(math.log2(c))
    for lvl in range(1, levels):
        if lvl < levels - 1:
            y = _mm(jnp.concatenate([t, p], axis=1), p, _BNN, "bf16")
            t = t + y[:, :c]
            p = y[:, c:]
        else:
            t = t + _mm(t, p, _BNN, "bf16")
    x = _mm(t, rhs, _BNN, "bf16")
    resid = rhs - x - _mm(m, x, _BNN, mode)
    return x + _mm(t, resid, _BNN, "bf16")


def _gdn_intra(q, k, v, g_b, beta_b, *, c, small, mode_gram, mode_solve, mode_apply):
    n = q.shape[0]
    row = lax.broadcasted_iota(jnp.int32, (n, c, c), 1)
    col = lax.broadcasted_iota(jnp.int32, (n, c, c), 2)
    g_cc = g_b[:, :, :c]
    g_row = jnp.sum(jnp.where(row == col, g_cc, 0.0), axis=1, keepdims=True)
    gamma = jnp.exp(jnp.where(row >= col, g_cc - g_row, -jnp.inf))
    kbeta = k * beta_b
    exp_g = jnp.exp(g_b)
    gram = _mm(jnp.concatenate([kbeta, q], axis=1), k, _BNT, mode_gram)
    m = jnp.where(row > col, gram[:, :c] * gamma, 0.0)
    attn = gram[:, c:] * gamma
    rhs = jnp.concatenate([kbeta * exp_g, v * beta_b], axis=-1)
    sol = _unit_lower_solve(m, rhs, c, small, mode_solve)
    g_last = g_b[:, c - 1:c, :]
    k_dec = k * jnp.exp(g_last - g_b)
    kd = _mm(k_dec, sol, _BTN, mode_apply)
    at = _mm(attn, sol, _BNN, mode_apply)
    q_t = q * exp_g - at[..., :DN_HEAD_DIM]
    return kd[..., :DN_HEAD_DIM], kd[..., DN_HEAD_DIM:], q_t, at[..., DN_HEAD_DIM:], jnp.exp(g_last)


def _gdn_state_step(a_mat, b_mat, q_t, o_intra, d_last, s, mode):
    r = _mm(jnp.concatenate([a_mat, q_t], axis=1), s, _BNN, mode)
    s_new = s * d_last - r[:, :DN_HEAD_DIM] + b_mat
    return r[:, DN_HEAD_DIM:] + o_intra, s_new


_PROMPT_MODES = dict(mode_gram="bf16", mode_solve="x3", mode_apply="bf16")
_PROMPT_STATE_MODE = "bf16"
_SAMPLE_MODES = dict(mode_gram="bf16", mode_solve="bf16", mode_apply="bf16")
_SAMPLE_STATE_MODE = "bf16"


def _delta_prompt_kernel(qkv_ref, gb_ref, s0_ref, o_ref, sn_ref, s_scr):
    t = pl.program_id(1)
    tc = qkv_ref.shape[0]

    @pl.when(t == 0)
    def _():
        s_scr[...] = s0_ref[0]

    probs = _gdn_problems(qkv_ref[...], gb_ref[...], DN_CHUNK)
    a_mat, b_mat, q_t, o_intra, d_last = _gdn_intra(*probs, c=DN_CHUNK, small=False, **_PROMPT_MODES)
    s = s_scr[...]
    for ci in range(tc // DN_CHUNK):
        ps = slice(ci * DN_HEADS, (ci + 1) * DN_HEADS)
        o, s = _gdn_state_step(a_mat[ps], b_mat[ps], q_t[ps], o_intra[ps], d_last[ps], s,
                               _PROMPT_STATE_MODE)
        for h in range(DN_HEADS):
            o_ref[ci * DN_CHUNK:(ci + 1) * DN_CHUNK, h * DN_HEAD_DIM:(h + 1) * DN_HEAD_DIM] = (
                o[h].astype(o_ref.dtype))
    s_scr[...] = s

    @pl.when(t == pl.num_programs(1) - 1)
    def _():
        sn_ref[0] = s


def _delta_prompt(qkv, gb, s0, bsz, seq, tc):
    n = qkv.shape[0]
    nt = seq // tc
    rows = lambda w: pl.BlockSpec((tc, w), lambda b, t: (b * nt + t, 0))
    state = pl.BlockSpec((1, DN_HEADS, DN_HEAD_DIM, DN_HEAD_DIM), lambda b, t: (b, 0, 0, 0))
    return pl.pallas_call(
        _delta_prompt_kernel,
        grid=(bsz, nt),
        in_specs=[rows(3 * DN_WIDTH), rows(LANES), state],
        out_specs=[rows(DN_WIDTH), state],
        out_shape=[
            jax.ShapeDtypeStruct((n, DN_WIDTH), BF16),
            jax.ShapeDtypeStruct((bsz, DN_HEADS, DN_HEAD_DIM, DN_HEAD_DIM), F32),
        ],
        scratch_shapes=[pltpu.VMEM((DN_HEADS, DN_HEAD_DIM, DN_HEAD_DIM), F32)],
        compiler_params=_cparams("arbitrary", "arbitrary"),
        name="delta_prompt",
    )(qkv, gb, s0)


def _delta_sample_kernel(qkv_ref, gb_ref, s0_ref, *rest):
    o_ref, sn_ref = rest[-2:]
    bt = qkv_ref.shape[0]
    c = SAMPLE_CHUNK
    probs = _gdn_problems(qkv_ref[...].reshape(bt * c, 3 * DN_WIDTH), gb_ref[...].reshape(bt * c, LANES), c)
    a_mat, b_mat, q_t, o_intra, d_last = _gdn_intra(*probs, c=c, small=True, **_SAMPLE_MODES)
    s = s0_ref[0].reshape(bt * DN_HEADS, DN_HEAD_DIM, DN_HEAD_DIM)
    o, s = _gdn_state_step(a_mat, b_mat, q_t, o_intra, d_last, s, _SAMPLE_STATE_MODE)
    for b in range(bt):
        for h in range(DN_HEADS):
            o_ref[b, :, h * DN_HEAD_DIM:(h + 1) * DN_HEAD_DIM] = o[b * DN_HEADS + h]
    sn_ref[0] = s.reshape(bt, DN_HEADS, DN_HEAD_DIM, DN_HEAD_DIM)


def _delta_sample(qkv_b, gb_b, state_all, layer, carried, bt):
    bsz = qkv_b.shape[0]
    blk = lambda w: pl.BlockSpec((bt, SAMPLE_CHUNK, w), lambda i: (i, 0, 0))
    state = pl.BlockSpec((1, bt, DN_HEADS, DN_HEAD_DIM, DN_HEAD_DIM), lambda i: (layer, i, 0, 0, 0))
    in_specs, args, aliases = [blk(3 * DN_WIDTH), blk(LANES), state], [qkv_b, gb_b, state_all], {}
    if carried is not None:
        in_specs.append(pl.BlockSpec(memory_space=pl.ANY))
        args.append(carried)
        aliases = {3: 1}
    return pl.pallas_call(
        _delta_sample_kernel,
        grid=(bsz // bt,),
        in_specs=in_specs,
        out_specs=[blk(DN_WIDTH), state],
        out_shape=[
            jax.ShapeDtypeStruct((bsz, SAMPLE_CHUNK, DN_WIDTH), F32),
            jax.ShapeDtypeStruct(state_all.shape, F32),
        ],
        input_output_aliases=aliases,
        compiler_params=_cparams("arbitrary"),
        name="delta_sample",
    )(*args)


def _outproj_kernel(yc_ref, o_ref, z_ref, x_ref, g1_ref, nw_ref, w_ref, lg_ref, lb_ref, out_ref):
    o = o_ref[...].astype(F32)
    parts = []
    for h in range(DN_HEADS):
        oh = o[:, h * DN_HEAD_DIM:(h + 1) * DN_HEAD_DIM]
        parts.append(oh * lax.rsqrt(jnp.mean(oh * oh, -1, keepdims=True) + RMS_EPS) * nw_ref[...])
    og = jnp.concatenate(parts, axis=-1) * _silu(z_ref[...].astype(F32))
    mixed = jnp.concatenate([yc_ref[...].astype(BF16), og.astype(BF16)], axis=-1)
    m = jnp.dot(mixed, w_ref[...], preferred_element_type=F32)
    r = DEEPNORM_ALPHA * x_ref[...] + (1.0 + g1_ref[0]) * m
    out_ref[...] = _layer_norm(r, lg_ref[...], lb_ref[...])


def _outproj(yc, o, z_src, z_blk, x, mod3, tpg, norm_w, w_out, ln_g, ln_b, tm):
    n = x.shape[0]
    vec = lambda w: pl.BlockSpec((1, w), lambda i: (0, 0))
    return pl.pallas_call(
        _outproj_kernel,
        grid=(n // tm,),
        in_specs=[
            pl.BlockSpec((tm, CONV_WIDTH), lambda i: (i, 0)),
            pl.BlockSpec((tm, DN_WIDTH), lambda i: (i, 0)),
            pl.BlockSpec((tm, DN_WIDTH), lambda i: (i, z_blk)),
            pl.BlockSpec((tm, D_MODEL), lambda i: (i, 0)),
            _mod_spec(mod3, 2, tpg),
            vec(DN_HEAD_DIM),
            pl.BlockSpec((D_MODEL, D_MODEL), lambda i: (0, 0)),
            vec(D_MODEL), vec(D_MODEL),
        ],
        out_specs=pl.BlockSpec((tm, D_MODEL), lambda i: (i, 0)),
        out_shape=jax.ShapeDtypeStruct((n, D_MODEL), F32),
        compiler_params=_cparams("arbitrary"),
        name="outproj",
    )(yc, o, z_src, x, mod3, norm_w, w_out, ln_g, ln_b)


R_E1, R_E2, R_G1, R_G2, R_R1, R_R2 = range(6)


def _router_kernel(x_ref, sc_ref, sh_ref, wr_ref, br_ref, route_ref, route_t_ref, cnt_ref):
    tm = x_ref.shape[0]
    u = x_ref[...] * (1.0 + sc_ref[0]) + sh_ref[0]
    logits = _mm(u, wr_ref[...], mode="x3") + br_ref[...]
    lane = lax.broadcasted_iota(jnp.int32, logits.shape, 1)
    valid = lane < N_EXPERTS
    logits = jnp.where(valid, logits, -jnp.inf)
    ex = jnp.exp(logits - jnp.max(logits, -1, keepdims=True))
    probs = jnp.where(valid, ex / jnp.sum(ex, -1, keepdims=True), -1.0)
    p1 = jnp.max(probs, -1, keepdims=True)
    i1 = jnp.min(jnp.where(probs == p1, lane, LANES), -1, keepdims=True)
    rest = jnp.where(lane == i1, -1.0, probs)
    p2 = jnp.max(rest, -1, keepdims=True)
    i2 = jnp.min(jnp.where(rest == p2, lane, LANES), -1, keepdims=True)
    tot = p1 + p2
    sel = jnp.where(lane == i1, 1.0, 0.0) + jnp.where(lane == i2, 1.0, 0.0)
    incl = _cumsum_chunks(sel, tm)
    excl = incl - sel
    r1 = jnp.sum(jnp.where(lane == i1, excl, 0.0), -1, keepdims=True)
    r2 = jnp.sum(jnp.where(lane == i2, excl, 0.0), -1, keepdims=True)
    cnt_ref[0] = incl[tm - 1:tm, :]
    route = jnp.zeros_like(logits)
    for j, val in enumerate((i1.astype(F32), i2.astype(F32), p1 / tot, p2 / tot, r1, r2)):
        route = jnp.where(lane == j, val, route)
    route_ref[...] = route
    route_t_ref[0] = jnp.transpose(route)[:SUBLANES, :]


def _router(x, mod3, tpg, w_router, b_router, tm):
    n = x.shape[0]
    return pl.pallas_call(
        _router_kernel,
        grid=(n // tm,),
        in_specs=[
            pl.BlockSpec((tm, D_MODEL), lambda i: (i, 0)),
            _mod_spec(mod3, 4, tpg),
            _mod_spec(mod3, 3, tpg),
            pl.BlockSpec((D_MODEL, LANES), lambda i: (0, 0)),
            pl.BlockSpec((1, LANES), lambda i: (0, 0)),
        ],
        out_specs=[pl.BlockSpec((tm, LANES), lambda i: (i, 0)),
                   pl.BlockSpec((1, SUBLANES, tm), lambda i: (i, 0, 0)),
                   pl.BlockSpec((1, 1, LANES), lambda i: (i, 0, 0))],
        out_shape=[jax.ShapeDtypeStruct((n, LANES), F32),
                   jax.ShapeDtypeStruct((n // tm, SUBLANES, tm), F32),
                   jax.ShapeDtypeStruct((n // tm, 1, LANES), F32)],
        compiler_params=_cparams("arbitrary"),
        name="router",
    )(x, mod3, mod3, w_router, b_router)


def _dispatch_plan(route_t, counts, tme):
    n_rt, _, tm = route_t.shape
    n = n_rt * tm
    cnt = counts[:, 0, :N_EXPERTS].astype(jnp.int32)
    tile_off = jnp.cumsum(cnt, axis=0) - cnt
    tot = jnp.sum(cnt, axis=0)
    gsz = (tot + tme - 1) // tme * tme
    gend = jnp.cumsum(gsz)
    goff = gend - gsz
    base = (goff[None, :] + tile_off)[:, :, None]
    ids = jnp.arange(N_EXPERTS, dtype=jnp.int32)[None, :, None]
    field = lambda j: route_t[:, j:j + 1, :].astype(jnp.int32)
    pick = lambda e: jnp.sum(jnp.where(ids == e, base, 0), axis=1, keepdims=True)
    slot1 = (pick(field(R_E1)) + field(R_R1)).reshape(n)
    slot2 = (pick(field(R_E2)) + field(R_R2)).reshape(n)
    n_tiles = 2 * n // tme + N_EXPERTS
    start = jnp.arange(n_tiles, dtype=jnp.int32) * tme
    valid = (start < gend[-1]).astype(jnp.int32)
    blk = jnp.minimum(jnp.arange(n_tiles, dtype=jnp.int32), gend[-1] // tme - 1)
    tile_e = jnp.sum((blk * tme)[:, None] >= gend[None, :], axis=1).astype(jnp.int32)
    zoff = jnp.where(gsz > 0, gend - tme, (n_tiles + jnp.arange(N_EXPERTS, dtype=jnp.int32)) * tme)
    return slot1, slot2, zoff, tile_e, valid, blk, n_tiles


def _dispatch_kernel(s1_ref, s2_ref, zoff_ref, x_ref, sc_ref, sh_ref, ug_hbm, ubuf, zbuf, sem):
    i = pl.program_id(0)
    tm = x_ref.shape[0]
    tme = zbuf.shape[0]

    @pl.when(i == 0)
    def _():
        zbuf[...] = jnp.zeros_like(zbuf)
        fill = lambda e: pltpu.make_async_copy(
            zbuf, ug_hbm.at[pl.ds(pl.multiple_of(zoff_ref[e], 8), tme)], sem.at[2])
        for e in range(N_EXPERTS):
            fill(e).start()
        for e in range(N_EXPERTS):
            fill(e).wait()

    ubuf[...] = (x_ref[...] * (1.0 + sc_ref[0]) + sh_ref[0]).reshape(ubuf.shape)
    base = i * tm

    def issue(j, carry):
        for k in range(SUBLANES):
            row = ubuf.at[j, pl.ds(k, 1)]
            t = base + j * SUBLANES + k
            pltpu.make_async_copy(row, ug_hbm.at[pl.ds(s1_ref[t], 1)], sem.at[0]).start()
            pltpu.make_async_copy(row, ug_hbm.at[pl.ds(s2_ref[t], 1)], sem.at[1]).start()
        return carry

    lax.fori_loop(0, tm // SUBLANES, issue, 0)
    pltpu.make_async_copy(ubuf, ubuf, sem.at[0]).wait()
    pltpu.make_async_copy(ubuf, ubuf, sem.at[1]).wait()


def _dispatch(x, mod3, tpg, slot1, slot2, zoff, tm, tme, rows):
    n = x.shape[0]
    return pl.pallas_call(
        _dispatch_kernel,
        grid_spec=pltpu.PrefetchScalarGridSpec(
            num_scalar_prefetch=3,
            grid=(n // tm,),
            in_specs=[
                pl.BlockSpec((tm, D_MODEL), lambda i, *_: (i, 0)),
                _mod_spec(mod3, 4, tpg),
                _mod_spec(mod3, 3, tpg),
            ],
            out_specs=pl.BlockSpec(memory_space=pl.ANY),
            scratch_shapes=[pltpu.VMEM((tm // SUBLANES, SUBLANES, D_MODEL), F32),
                            pltpu.VMEM((tme, D_MODEL), F32), pltpu.SemaphoreType.DMA((3,))],
        ),
        out_shape=jax.ShapeDtypeStruct((rows, D_MODEL), F32),
        compiler_params=_cparams("arbitrary"),
        name="dispatch",
    )(slot1, slot2, zoff, x, mod3, mod3)


def _expert_kernel(te_ref, tv_ref, tb_ref, ug_ref, wg_ref, wu_ref, wd_ref, ys_ref):
    i = pl.program_id(0)

    @pl.when(tv_ref[i] == 1)
    def _():
        u = ug_ref[...].astype(BF16)
        h = _silu(jnp.dot(u, wg_ref[0].astype(BF16), preferred_element_type=F32)) * jnp.dot(
            u, wu_ref[0].astype(BF16), preferred_element_type=F32)
        ys_ref[...] = jnp.dot(h.astype(BF16), wd_ref[0].astype(BF16), preferred_element_type=F32)

    @pl.when(tv_ref[i] == 0)
    def _():
        ys_ref[...] = jnp.zeros_like(ys_ref)


def _experts(ug, tile_e, valid, blk, w_gate, w_up, w_down, tme, n_tiles):
    d_e = w_gate.shape[-1]
    return pl.pallas_call(
        _expert_kernel,
        grid_spec=pltpu.PrefetchScalarGridSpec(
            num_scalar_prefetch=3,
            grid=(n_tiles,),
            in_specs=[
                pl.BlockSpec((tme, D_MODEL), lambda i, te, tv, tb: (tb[i], 0)),
                pl.BlockSpec((1, D_MODEL, d_e), lambda i, te, tv, tb: (te[i], 0, 0)),
                pl.BlockSpec((1, D_MODEL, d_e), lambda i, te, tv, tb: (te[i], 0, 0)),
                pl.BlockSpec((1, d_e, D_MODEL), lambda i, te, tv, tb: (te[i], 0, 0)),
            ],
            out_specs=pl.BlockSpec((tme, D_MODEL), lambda i, te, tv, tb: (i, 0)),
        ),
        out_shape=jax.ShapeDtypeStruct((n_tiles * tme, D_MODEL), F32),
        compiler_params=_cparams("arbitrary"),
        name="experts",
    )(tile_e, valid, blk, ug, w_gate, w_up, w_down)


def _combine_kernel(s1_ref, s2_ref, ys_hbm, route_ref, x_ref, g2_ref, lg_ref, lb_ref, out_ref,
                    y1buf, y2buf, sem):
    i = pl.program_id(0)
    tm = x_ref.shape[0]
    base = i * tm

    def issue(j, carry):
        for k in range(SUBLANES):
            t = base + j * SUBLANES + k
            pltpu.make_async_copy(ys_hbm.at[pl.ds(s1_ref[t], 1)], y1buf.at[j, pl.ds(k, 1)], sem.at[0]).start()
            pltpu.make_async_copy(ys_hbm.at[pl.ds(s2_ref[t], 1)], y2buf.at[j, pl.ds(k, 1)], sem.at[1]).start()
        return carry

    lax.fori_loop(0, tm // SUBLANES, issue, 0)
    pltpu.make_async_copy(y1buf, y1buf, sem.at[0]).wait()
    pltpu.make_async_copy(y2buf, y2buf, sem.at[1]).wait()
    route = route_ref[...]
    y1 = y1buf[...].reshape(tm, D_MODEL)
    y2 = y2buf[...].reshape(tm, D_MODEL)
    f = route[:, R_G1:R_G1 + 1] * y1 + route[:, R_G2:R_G2 + 1] * y2
    r = DEEPNORM_ALPHA * x_ref[...] + (1.0 + g2_ref[0]) * f
    out_ref[...] = _layer_norm(r, lg_ref[...], lb_ref[...])


def _combine(ys, slot1, slot2, route, x, mod3, tpg, ln_g, ln_b, tm):
    n = x.shape[0]
    vec = lambda w: pl.BlockSpec((1, w), lambda i, *_: (0, 0))
    return pl.pallas_call(
        _combine_kernel,
        grid_spec=pltpu.PrefetchScalarGridSpec(
            num_scalar_prefetch=2,
            grid=(n // tm,),
            in_specs=[
                pl.BlockSpec(memory_space=pl.ANY),
                pl.BlockSpec((tm, LANES), lambda i, *_: (i, 0)),
                pl.BlockSpec((tm, D_MODEL), lambda i, *_: (i, 0)),
                _mod_spec(mod3, 5, tpg),
                vec(D_MODEL), vec(D_MODEL),
            ],
            out_specs=pl.BlockSpec((tm, D_MODEL), lambda i, *_: (i, 0)),
            scratch_shapes=[pltpu.VMEM((tm // SUBLANES, SUBLANES, D_MODEL), F32),
                            pltpu.VMEM((tm // SUBLANES, SUBLANES, D_MODEL), F32),
                            pltpu.SemaphoreType.DMA((2,))],
        ),
        out_shape=jax.ShapeDtypeStruct((n, D_MODEL), F32),
        compiler_params=_cparams("arbitrary"),
        name="combine",
    )(slot1, slot2, ys, route, x, mod3, ln_g, ln_b)


def _moe(x, mod3, tpg, w_router, b_router, w_gate, w_up, w_down, ln_g, ln_b, tm, tme):
    route, route_t, counts = _router(x, mod3, tpg, w_router, b_router, tm)
    slot1, slot2, zoff, tile_e, valid, blk, n_tiles = _dispatch_plan(route_t, counts, tme)
    ug = _dispatch(x, mod3, tpg, slot1, slot2, zoff, tm, tme, (n_tiles + N_EXPERTS) * tme)
    ys = _experts(ug, tile_e, valid, blk, w_gate, w_up, w_down, tme, n_tiles)
    return _combine(ys, slot1, slot2, route, x, mod3, tpg, ln_g, ln_b, tm)


def _ffn_kernel(x_ref, sc_ref, sh_ref, g2_ref, wg_ref, wu_ref, wd_ref, lg_ref, lb_ref,
                out_ref, u_scr, acc_scr):
    f = pl.program_id(1)

    @pl.when(f == 0)
    def _():
        u_scr[...] = (x_ref[...] * (1.0 + sc_ref[0]) + sh_ref[0]).astype(BF16)
        acc_scr[...] = jnp.zeros_like(acc_scr)

    u = u_scr[...]
    h = _silu(jnp.dot(u, wg_ref[...], preferred_element_type=F32)) * jnp.dot(
        u, wu_ref[...], preferred_element_type=F32)
    acc_scr[...] += jnp.dot(h.astype(BF16), wd_ref[...], preferred_element_type=F32)

    @pl.when(f == pl.num_programs(1) - 1)
    def _():
        r = DEEPNORM_ALPHA * x_ref[...] + (1.0 + g2_ref[0]) * acc_scr[...]
        out_ref[...] = _layer_norm(r, lg_ref[...], lb_ref[...])


def _ffn(x, mod3, tpg, w_gate, w_up, w_down, ln_g, ln_b, tm, tf):
    n = x.shape[0]
    d_ff = w_gate.shape[-1]
    vec = lambda w: pl.BlockSpec((1, w), lambda i, f: (0, 0))
    return pl.pallas_call(
        _ffn_kernel,
        grid=(n // tm, d_ff // tf),
        in_specs=[
            pl.BlockSpec((tm, D_MODEL), lambda i, f: (i, 0)),
            _mod_spec(mod3, 4, tpg),
            _mod_spec(mod3, 3, tpg),
            _mod_spec(mod3, 5, tpg),
            pl.BlockSpec((D_MODEL, tf), lambda i, f: (0, f)),
            pl.BlockSpec((D_MODEL, tf), lambda i, f: (0, f)),
            pl.BlockSpec((tf, D_MODEL), lambda i, f: (f, 0)),
            vec(D_MODEL), vec(D_MODEL),
        ],
        out_specs=pl.BlockSpec((tm, D_MODEL), lambda i, f: (i, 0)),
        out_shape=jax.ShapeDtypeStruct((n, D_MODEL), F32),
        scratch_shapes=[pltpu.VMEM((tm, D_MODEL), BF16), pltpu.VMEM((tm, D_MODEL), F32)],
        compiler_params=_cparams("arbitrary", "arbitrary"),
        name="ffn_dense",
    )(x, mod3, mod3, mod3, w_gate, w_up, w_down, ln_g, ln_b)


def _trunk(x, mod3s, rows_per_group, prm, tiles, mixer_fn):
    tm, tme = tiles["tm"], tiles["tme"]
    tpg = rows_per_group // tm
    states = []
    for l in range(DEPTH):
        mod3 = mod3s[l]
        yc, o, z_src, z_blk, st = mixer_fn(l, x, mod3)
        states.append(st)
        x = _outproj(yc, o, z_src, z_blk, x, mod3, tpg, prm["norm_w"][l], prm["w_out"][l],
                     prm["ln_g"][l, 0:1], prm["ln_b"][l, 0:1], tm)
        j = l // 2
        if l % 2 == 0:
            x = _ffn(x, mod3, rows_per_group // tiles["tm_ffn"], prm["w_ff_gate"][j], prm["w_ff_up"][j],
                     prm["w_ff_down"][j], prm["ln_g"][l, 1:2], prm["ln_b"][l, 1:2],
                     tiles["tm_ffn"], tiles["tf"])
        else:
            x = _moe(x, mod3, tpg, prm["w_router"][j], prm["b_router"][j], prm["w_exp_gate"][j],
                     prm["w_exp_up"][j], prm["w_exp_down"][j], prm["ln_g"][l, 1:2], prm["ln_b"][l, 1:2],
                     tm, tme)
    return x, states


def kernel(x_prompt, x_sample, state_conv, state_dn_conv, state_dn, c_prompt, c_sample, w_in, conv_w,
           dn_conv_w, a_log, dt_bias, dn_norm_w, w_out, w_ada, b_ada, ln_g, ln_b, w_ff_gate, w_ff_up,
           w_ff_down, w_router, b_router, w_exp_gate, w_exp_up, w_exp_down):
    bp, seq, d = x_prompt.shape
    bs, steps, _ = x_sample.shape
    pad_lanes = LANES - N_EXPERTS
    hp = jnp.zeros((DEPTH, 2, LANES), F32)
    hp = hp.at[:, 0, G_LANE:].set(a_log).at[:, 1, G_LANE:].set(dt_bias)
    w_in_t = jnp.swapaxes(w_in, 1, 2)
    prm = {
        "w_main": w_in_t[:, :P_MAIN].astype(BF16),
        "w_small": w_in_t[:, w_in_t.shape[1] - LANES:],
        "norm_w": dn_norm_w.reshape(DEPTH, 1, DN_HEAD_DIM),
        "w_out": w_out.astype(BF16),
        "ln_g": ln_g, "ln_b": ln_b,
        "w_ff_gate": w_ff_gate.astype(BF16), "w_ff_up": w_ff_up.astype(BF16),
        "w_ff_down": w_ff_down.astype(BF16),
        "w_router": jnp.pad(w_router, ((0, 0), (0, 0), (0, pad_lanes))),
        "b_router": jnp.pad(b_router, ((0, 0), (0, pad_lanes)))[:, None, :],
        "w_exp_gate": w_exp_gate, "w_exp_up": w_exp_up, "w_exp_down": w_exp_down,
    }

    mod = _ada(jnp.concatenate([c_prompt, c_sample], axis=0), w_ada, b_ada)
    mod_p = [mod[l, :bp].reshape(bp, 1, 6 * d) for l in range(DEPTH)]
    mod_s = [mod[l, bp:].reshape(1, bs, 6 * d) for l in range(DEPTH)]

    tm_p, tc_delta = 512, 512
    zc = jnp.zeros((bp, CONV_K - 1, CONV_WIDTH), F32)
    zd = jnp.zeros((bp, DN_CONV_K - 1, 3 * DN_WIDTH), F32)
    zs = jnp.zeros((bp, DN_HEADS, DN_HEAD_DIM, DN_HEAD_DIM), F32)

    def mixer_prompt(l, x, mod3):
        yc, qkv, z, gb, ncb, ndb = _inproj_pre_prompt(
            x, mod3, prm["w_main"][l], prm["w_small"][l], conv_w[l], dn_conv_w[l], hp[l], zc, zd,
            bp, seq, tm_p)
        o, s_new = _delta_prompt(qkv, gb, zs, bp, seq, tc_delta)
        return yc, o, z, 0, (ncb, ndb, s_new)

    y_p, st_p = _trunk(x_prompt.reshape(bp * seq, d), mod_p, seq, prm,
                       dict(tm=tm_p, tm_ffn=512, tf=w_ff_gate.shape[-1] // 2, tme=512), mixer_prompt)

    bt_pre, bt_delta = 32, 8
    dn_states = []

    def mixer_sample(l, x, mod3):
        pm, ps = _inproj(x, mod3, steps, prm["w_main"][l], prm["w_small"][l], bs)
        cb_tm = jnp.transpose(state_conv[l], (1, 0, 2))
        db_tm = jnp.transpose(state_dn_conv[l], (1, 0, 2))
        yc, qkv, gb, ncb, ndb = _pre_sample(pm, ps, conv_w[l], dn_conv_w[l], hp[l], cb_tm, db_tm,
                                            bs, steps, bt_pre)
        to_b = lambda a: jnp.pad(jnp.transpose(a, (1, 0, 2)), ((0, 0), (0, SAMPLE_CHUNK - steps), (0, 0)))
        o_b, s_all = _delta_sample(to_b(qkv), to_b(gb), state_dn, l, dn_states[-1] if dn_states else None,
                                   bt_delta)
        dn_states.append(s_all)
        o = jnp.transpose(o_b[:, :steps], (1, 0, 2)).reshape(steps * bs, DN_WIDTH)
        return (yc.reshape(steps * bs, CONV_WIDTH), o, pm, OFF_Z // DN_WIDTH,
                (jnp.transpose(ncb, (1, 0, 2)), jnp.transpose(ndb, (1, 0, 2))))

    x_s = jnp.transpose(x_sample, (1, 0, 2)).reshape(steps * bs, d)
    y_s, st_s = _trunk(x_s, mod_s, steps * bs, prm,
                       dict(tm=bs, tm_ffn=bs, tf=w_ff_gate.shape[-1] // 2, tme=128), mixer_sample)
    y_s = jnp.transpose(y_s.reshape(steps, bs, d), (1, 0, 2))

    stack = lambda sts, i: jnp.stack([s[i] for s in sts])
    return (y_p.reshape(bp, seq, d), y_s,
            stack(st_p, 0), stack(st_p, 1), stack(st_p, 2),
            stack(st_s, 0), stack(st_s, 1), dn_states[-1])
```

```python
import functools
import math

import jax
import jax.numpy as jnp
from jax import lax
from jax.experimental import pallas as pl
from jax.experimental.pallas import tpu as pltpu

F32 = jnp.float32
BF16 = jnp.bfloat16

D_MODEL = 1024
DEPTH = 2
CONV_WIDTH = 512
CONV_K = 3
DN_HEADS = 4
DN_HEAD_DIM = 128
DN_WIDTH = DN_HEADS * DN_HEAD_DIM
DN_CONV_K = 4
DN_CHUNK = 64
N_EXPERTS = 8
DEEPNORM_ALPHA = (2.0 * DEPTH) ** 0.25
LN_EPS = 1e-5
RMS_EPS = 1e-6

OFF_CC = CONV_WIDTH
OFF_CH = 2 * CONV_WIDTH
OFF_QKV = 3 * CONV_WIDTH
OFF_Z = OFF_QKV + 3 * DN_WIDTH
P_MAIN = OFF_Z + DN_WIDTH
LANES = 128
SUBLANES = 8
BETA_LANE = LANES - 2 * DN_HEADS
G_LANE = LANES - DN_HEADS
SAMPLE_CHUNK = 8

VMEM_LIMIT = 56 * 1024 * 1024


def _cparams(*sem):
    return pltpu.CompilerParams(dimension_semantics=sem, vmem_limit_bytes=VMEM_LIMIT)


def _silu(x):
    return x * jax.nn.sigmoid(x)


def _layer_norm(r, g, b):
    mu = jnp.mean(r, -1, keepdims=True)
    xc = r - mu
    var = jnp.mean(xc * xc, -1, keepdims=True)
    return xc * lax.rsqrt(var + LN_EPS) * g + b


def _split_bf16(a):
    hi = a.astype(BF16)
    lo = (a - hi.astype(F32)).astype(BF16)
    return hi, lo


_NN = (((1,), (0,)), ((), ()))
_NT = (((1,), (1,)), ((), ()))
_BNN = (((2,), (1,)), ((0,), (0,)))
_BNT = (((2,), (2,)), ((0,), (0,)))
_BTN = (((1,), (1,)), ((0,), (0,)))


def _mm(a, b, dims=_NN, mode="bf16"):
    if mode == "f32":
        return lax.dot_general(a, b, dims, precision=lax.Precision.HIGHEST, preferred_element_type=F32)
    if mode == "bf16":
        return lax.dot_general(a.astype(BF16), b.astype(BF16), dims, preferred_element_type=F32)
    ah, al = _split_bf16(a)
    bh, bl = _split_bf16(b)
    d = functools.partial(lax.dot_general, dimension_numbers=dims, preferred_element_type=F32)
    return d(ah, bh) + (d(ah, bl) + d(al, bh))


def _ada_kernel(c_ref, w_ref, b_ref, o_ref):
    s = _silu(c_ref[...]).astype(BF16)
    o_ref[0] = jnp.dot(s, w_ref[0].astype(BF16), preferred_element_type=F32) + b_ref[0]


def _ada(c_all, w_ada, b_ada):
    rows = c_all.shape[0]
    tn = 1536
    return pl.pallas_call(
        _ada_kernel,
        grid=(DEPTH, 6 * D_MODEL // tn),
        in_specs=[
            pl.BlockSpec((rows, D_MODEL), lambda l, j: (0, 0)),
            pl.BlockSpec((1, D_MODEL, tn), lambda l, j: (l, 0, j)),
            pl.BlockSpec((1, 1, tn), lambda l, j: (l, 0, j)),
        ],
        out_specs=pl.BlockSpec((1, rows, tn), lambda l, j: (l, 0, j)),
        out_shape=jax.ShapeDtypeStruct((DEPTH, rows, 6 * D_MODEL), F32),
        compiler_params=_cparams("arbitrary", "arbitrary"),
        name="ada",
    )(c_all, w_ada, b_ada.reshape(DEPTH, 1, 6 * D_MODEL))


def _mod_spec(mod3, chunk, tiles_per_group):
    r = mod3.shape[1]
    return pl.BlockSpec((1, r, D_MODEL), lambda i, *_: (i // tiles_per_group, 0, chunk))


def _inproj_kernel(x_ref, sc_ref, sh_ref, wm_ref, ws_ref, pm_ref, ps_ref):
    u = x_ref[...] * (1.0 + sc_ref[0]) + sh_ref[0]
    pm_ref[...] = lax.dot_general(u.astype(BF16), wm_ref[...], _NT, preferred_element_type=F32)
    ps_ref[...] = _mm(u, ws_ref[...], _NT, mode="x3")


def _inproj(x, mod3, tpg, w_main, w_small, tm):
    n = x.shape[0]
    return pl.pallas_call(
        _inproj_kernel,
        grid=(n // tm,),
        in_specs=[
            pl.BlockSpec((tm, D_MODEL), lambda i: (i, 0)),
            _mod_spec(mod3, 1, tpg),
            _mod_spec(mod3, 0, tpg),
            pl.BlockSpec((P_MAIN, D_MODEL), lambda i: (0, 0)),
            pl.BlockSpec((LANES, D_MODEL), lambda i: (0, 0)),
        ],
        out_specs=[
            pl.BlockSpec((tm, P_MAIN), lambda i: (i, 0)),
            pl.BlockSpec((tm, LANES), lambda i: (i, 0)),
        ],
        out_shape=[
            jax.ShapeDtypeStruct((n, P_MAIN), F32),
            jax.ShapeDtypeStruct((n, LANES), F32),
        ],
        compiler_params=_cparams("arbitrary"),
        name="inproj",
    )(x, mod3, mod3, w_main, w_small)


def _gates(ps, hp):
    lane = lax.broadcasted_iota(jnp.int32, ps.shape, 1)
    beta = jax.nn.sigmoid(ps)
    g = -jnp.exp(hp[0:1, :]) * jax.nn.softplus(ps + hp[1:2, :])
    return jnp.where(lane >= G_LANE, g, jnp.where(lane >= BETA_LANE, beta, 0.0))


def _qkv_finish(y, out_ref, idx):
    y = _silu(y)
    for h in range(3 * DN_HEADS):
        seg = y[:, h * DN_HEAD_DIM:(h + 1) * DN_HEAD_DIM]
        if h < 2 * DN_HEADS:
            seg = seg * lax.rsqrt(jnp.sum(seg * seg, -1, keepdims=True) + RMS_EPS)
            if h < DN_HEADS:
                seg = seg * (DN_HEAD_DIM ** -0.5)
        out_ref[idx + (slice(None), slice(h * DN_HEAD_DIM, (h + 1) * DN_HEAD_DIM))] = seg


def _shifted(x, tail_ref, i, row):
    s = pltpu.roll(x, i, 0)
    top = s[0:SUBLANES]
    for r in range(i):
        top = jnp.where(row == r, tail_ref[8 - i + r:8 - i + r + 1, :], top)
    return jnp.concatenate([top, s[SUBLANES:]], axis=0)


def _inproj_pre_prompt_kernel(x_ref, sc_ref, sh_ref, wm_ref, ws_ref, cw_ref, dw_ref, hp_ref, cb_ref, db_ref,
                              yc_ref, qkv_ref, z_ref, gb_ref, ncb_ref, ndb_ref, tc_scr, td_scr):
    t = pl.program_id(1)
    tm = x_ref.shape[0]

    @pl.when(t == 0)
    def _():
        tc_scr[8 - (CONV_K - 1):8, :] = cb_ref[0]
        td_scr[8 - (DN_CONV_K - 1):8, :] = db_ref[0]

    u_f32 = x_ref[...] * (1.0 + sc_ref[0]) + sh_ref[0]
    u = u_f32.astype(BF16)
    proj = lambda lo, hi: lax.dot_general(u, wm_ref[lo:hi, :], _NT, preferred_element_type=F32)

    row_d = lax.broadcasted_iota(jnp.int32, (SUBLANES, 3 * DN_WIDTH), 0)
    xq = proj(OFF_QKV, OFF_Z)
    yq = _shifted(xq, td_scr, 3, row_d) * dw_ref[0:1, :]
    yq = yq + _shifted(xq, td_scr, 2, row_d) * dw_ref[1:2, :]
    yq = yq + _shifted(xq, td_scr, 1, row_d) * dw_ref[2:3, :]
    yq = yq + xq * dw_ref[3:4, :]
    _qkv_finish(yq, qkv_ref, ())
    last_d = xq[tm - (DN_CONV_K - 1):tm, :]

    row_c = lax.broadcasted_iota(jnp.int32, (SUBLANES, CONV_WIDTH), 0)
    cgh = proj(OFF_CC, OFF_CH) * proj(OFF_CH, OFF_QKV)
    y = _shifted(cgh, tc_scr, 2, row_c) * cw_ref[0:1, :]
    y = y + _shifted(cgh, tc_scr, 1, row_c) * cw_ref[1:2, :]
    y = y + cgh * cw_ref[2:3, :]
    yc_ref[...] = (proj(0, OFF_CC) * y).astype(yc_ref.dtype)
    last_c = cgh[tm - (CONV_K - 1):tm, :]

    z_ref[...] = proj(OFF_Z, P_MAIN).astype(z_ref.dtype)
    gb_ref[...] = _gates(_mm(u_f32, ws_ref[...], _NT, mode="x3"), hp_ref[...])

    tc_scr[8 - (CONV_K - 1):8, :] = last_c
    td_scr[8 - (DN_CONV_K - 1):8, :] = last_d

    @pl.when(t == pl.num_programs(1) - 1)
    def _():
        ncb_ref[0] = last_c
        ndb_ref[0] = last_d


def _inproj_pre_prompt(x, mod3, w_main, w_small, conv_w, dn_conv_w, hp, conv_buf, dn_buf, bsz, seq, tm):
    n = x.shape[0]
    nt = seq // tm
    rows = lambda w: pl.BlockSpec((tm, w), lambda b, t: (b * nt + t, 0))
    full = lambda a: pl.BlockSpec(a.shape, lambda b, t: (0,) * a.ndim)
    per_b = lambda k, w: pl.BlockSpec((1, k, w), lambda b, t: (b, 0, 0))
    mod = lambda chunk: pl.BlockSpec((1, 1, D_MODEL), lambda b, t: (b, 0, chunk))
    return pl.pallas_call(
        _inproj_pre_prompt_kernel,
        grid=(bsz, nt),
        in_specs=[rows(D_MODEL), mod(1), mod(0), full(w_main), full(w_small),
                  full(conv_w), full(dn_conv_w), full(hp),
                  per_b(CONV_K - 1, CONV_WIDTH), per_b(DN_CONV_K - 1, 3 * DN_WIDTH)],
        out_specs=[rows(CONV_WIDTH), rows(3 * DN_WIDTH), rows(DN_WIDTH), rows(LANES),
                   per_b(CONV_K - 1, CONV_WIDTH), per_b(DN_CONV_K - 1, 3 * DN_WIDTH)],
        out_shape=[
            jax.ShapeDtypeStruct((n, CONV_WIDTH), BF16),
            jax.ShapeDtypeStruct((n, 3 * DN_WIDTH), F32),
            jax.ShapeDtypeStruct((n, DN_WIDTH), BF16),
            jax.ShapeDtypeStruct((n, LANES), F32),
            jax.ShapeDtypeStruct((bsz, CONV_K - 1, CONV_WIDTH), F32),
            jax.ShapeDtypeStruct((bsz, DN_CONV_K - 1, 3 * DN_WIDTH), F32),
        ],
        scratch_shapes=[pltpu.VMEM((8, CONV_WIDTH), F32), pltpu.VMEM((8, 3 * DN_WIDTH), F32)],
        compiler_params=_cparams("arbitrary", "arbitrary"),
        name="inproj_pre_prompt",
    )(x, mod3, mod3, w_main, w_small, conv_w, dn_conv_w, hp, conv_buf, dn_buf)


def _pre_sample_kernel(pm_ref, ps_ref, cw_ref, dw_ref, hp_ref, cb_ref, db_ref,
                       yc_ref, qkv_ref, gb_ref, ncb_ref, ndb_ref):
    steps = pm_ref.shape[0]
    cgh = [pm_ref[t, :, OFF_CC:OFF_CH] * pm_ref[t, :, OFF_CH:OFF_QKV] for t in range(steps)]
    ext = [cb_ref[i] for i in range(CONV_K - 1)] + cgh
    for t in range(steps):
        y = ext[t] * cw_ref[0:1, :]
        for i in range(1, CONV_K):
            y = y + ext[t + i] * cw_ref[i:i + 1, :]
        yc_ref[t] = pm_ref[t, :, 0:OFF_CC] * y
    for i in range(CONV_K - 1):
        ncb_ref[i] = ext[len(ext) - (CONV_K - 1) + i]

    xq = [pm_ref[t, :, OFF_QKV:OFF_Z] for t in range(steps)]
    extq = [db_ref[i] for i in range(DN_CONV_K - 1)] + xq
    for t in range(steps):
        y = extq[t] * dw_ref[0:1, :]
        for i in range(1, DN_CONV_K):
            y = y + extq[t + i] * dw_ref[i:i + 1, :]
        _qkv_finish(y, qkv_ref, (t,))
        gb_ref[t] = _gates(ps_ref[t], hp_ref[...])
    for i in range(DN_CONV_K - 1):
        ndb_ref[i] = extq[len(extq) - (DN_CONV_K - 1) + i]


def _pre_sample(pm, ps, conv_w, dn_conv_w, hp, conv_buf_tm, dn_buf_tm, bsz, steps, bt):
    slab = lambda k, w: pl.BlockSpec((k, bt, w), lambda i: (0, i, 0))
    full = lambda a: pl.BlockSpec(a.shape, lambda i: (0,) * a.ndim)
    return pl.pallas_call(
        _pre_sample_kernel,
        grid=(bsz // bt,),
        in_specs=[slab(steps, P_MAIN), slab(steps, LANES), full(conv_w), full(dn_conv_w), full(hp),
                  slab(CONV_K - 1, CONV_WIDTH), slab(DN_CONV_K - 1, 3 * DN_WIDTH)],
        out_specs=[slab(steps, CONV_WIDTH), slab(steps, 3 * DN_WIDTH), slab(steps, LANES),
                   slab(CONV_K - 1, CONV_WIDTH), slab(DN_CONV_K - 1, 3 * DN_WIDTH)],
        out_shape=[
            jax.ShapeDtypeStruct((steps, bsz, CONV_WIDTH), F32),
            jax.ShapeDtypeStruct((steps, bsz, 3 * DN_WIDTH), F32),
            jax.ShapeDtypeStruct((steps, bsz, LANES), F32),
            jax.ShapeDtypeStruct((CONV_K - 1, bsz, CONV_WIDTH), F32),
            jax.ShapeDtypeStruct((DN_CONV_K - 1, bsz, 3 * DN_WIDTH), F32),
        ],
        compiler_params=_cparams("arbitrary"),
        name="pre_sample",
    )(pm.reshape(steps, bsz, P_MAIN), ps.reshape(steps, bsz, LANES), conv_w, dn_conv_w, hp,
      conv_buf_tm, dn_buf_tm)


def _cumsum_chunks(x, c):
    row = lax.broadcasted_iota(jnp.int32, x.shape, 0) & (c - 1)
    s = 1
    while s < c:
        x = x + jnp.where(row >= s, pltpu.roll(x, s, 0), 0.0)
        s *= 2
    return x


def _gdn_problems(qkv, gb, c):
    n = qkv.shape[0] // c
    gc = _cumsum_chunks(gb, c)
    q, k, v, g_b, beta_b = [], [], [], [], []
    for ci in range(n):
        rs = slice(ci * c, (ci + 1) * c)
        for h in range(DN_HEADS):
            ls = lambda base: slice(base + h * DN_HEAD_DIM, base + (h + 1) * DN_HEAD_DIM)
            q.append(qkv[rs, ls(0)])
            k.append(qkv[rs, ls(DN_WIDTH)])
            v.append(qkv[rs, ls(2 * DN_WIDTH)])
            g_b.append(jnp.broadcast_to(gc[rs, G_LANE + h:G_LANE + h + 1], (c, DN_HEAD_DIM)))
            beta_b.append(jnp.broadcast_to(gb[rs, BETA_LANE + h:BETA_LANE + h + 1], (c, DN_HEAD_DIM)))
    return tuple(jnp.stack(a) for a in (q, k, v, g_b, beta_b))


def _unit_lower_solve(m, rhs, c, small, mode):
    if small:
        sol = rhs
        for j in range(c - 1):
            sol = sol - m[:, :, j:j + 1] * sol[:, j:j + 1, :]
        return sol
    row = lax.broadcasted_iota(jnp.int32, m.shape, 1)
    col = lax.broadcasted_iota(jnp.int32, m.shape, 2)
    p = -m
    t = jnp.where(row == col, 1.0, 0.0) + p
    p = _mm(p, p, _BNN, "bf16")
    levels = int(math.log2(c))
    for lvl in range(1, levels):
        if lvl < levels - 1:
            y = _mm(jnp.concatenate([t, p], axis=1), p, _BNN, "bf16")
            t = t + y[:, :c]
            p = y[:, c:]
        else:
            t = t + _mm(t, p, _BNN, "bf16")
    x = _mm(t, rhs, _BNN, "bf16")
    resid = rhs - x - _mm(m, x, _BNN, mode)
    return x + _mm(t, resid, _BNN, "bf16")


def _gdn_intra(q, k, v, g_b, beta_b, *, c, small, mode_gram, mode_solve, mode_apply):
    n = q.shape[0]
    row = lax.broadcasted_iota(jnp.int32, (n, c, c), 1)
    col = lax.broadcasted_iota(jnp.int32, (n, c, c), 2)
    g_cc = g_b[:, :, :c]
    g_row = jnp.sum(jnp.where(row == col, g_cc, 0.0), axis=1, keepdims=True)
    gamma = jnp.exp(jnp.where(row >= col, g_cc - g_row, -jnp.inf))
    kbeta = k * beta_b
    exp_g = jnp.exp(g_b)
    gram = _mm(jnp.concatenate([kbeta, q], axis=1), k, _BNT, mode_gram)
    m = jnp.where(row > col, gram[:, :c] * gamma, 0.0)
    attn = gram[:, c:] * gamma
    rhs = jnp.concatenate([kbeta * exp_g, v * beta_b], axis=-1)
    sol = _unit_lower_solve(m, rhs, c, small, mode_solve)
    g_last = g_b[:, c - 1:c, :]
    k_dec = k * jnp.exp(g_last - g_b)
    kd = _mm(k_dec, sol, _BTN, mode_apply)
    at = _mm(attn, sol, _BNN, mode_apply)
    q_t = q * exp_g - at[..., :DN_HEAD_DIM]
    return kd[..., :DN_HEAD_DIM], kd[..., DN_HEAD_DIM:], q_t, at[..., DN_HEAD_DIM:], jnp.exp(g_last)


def _gdn_state_step(a_mat, b_mat, q_t, o_intra, d_last, s, mode):
    r = _mm(jnp.concatenate([a_mat, q_t], axis=1), s, _BNN, mode)
    s_new = s * d_last - r[:, :DN_HEAD_DIM] + b_mat
    return r[:, DN_HEAD_DIM:] + o_intra, s_new


_PROMPT_MODES = dict(mode_gram="bf16", mode_solve="x3", mode_apply="bf16")
_PROMPT_STATE_MODE = "bf16"
_SAMPLE_MODES = dict(mode_gram="bf16", mode_solve="bf16", mode_apply="bf16")
_SAMPLE_STATE_MODE = "bf16"


def _delta_prompt_kernel(qkv_ref, gb_ref, s0_ref, o_ref, sn_ref, s_scr):
    t = pl.program_id(1)
    tc = qkv_ref.shape[0]

    @pl.when(t == 0)
    def _():
        s_scr[...] = s0_ref[0]

    probs = _gdn_problems(qkv_ref[...], gb_ref[...], DN_CHUNK)
    a_mat, b_mat, q_t, o_intra, d_last = _gdn_intra(*probs, c=DN_CHUNK, small=False, **_PROMPT_MODES)
    s = s_scr[...]
    for ci in range(tc // DN_CHUNK):
        ps = slice(ci * DN_HEADS, (ci + 1) * DN_HEADS)
        o, s = _gdn_state_step(a_mat[ps], b_mat[ps], q_t[ps], o_intra[ps], d_last[ps], s,
                               _PROMPT_STATE_MODE)
        for h in range(DN_HEADS):
            o_ref[ci * DN_CHUNK:(ci + 1) * DN_CHUNK, h * DN_HEAD_DIM:(h + 1) * DN_HEAD_DIM] = (
                o[h].astype(o_ref.dtype))
    s_scr[...] = s

    @pl.when(t == pl.num_programs(1) - 1)
    def _():
        sn_ref[0] = s


def _delta_prompt(qkv, gb, s0, bsz, seq, tc):
    n = qkv.shape[0]
    nt = seq // tc
    rows = lambda w: pl.BlockSpec((tc, w), lambda b, t: (b * nt + t, 0))
    state = pl.BlockSpec((1, DN_HEADS, DN_HEAD_DIM, DN_HEAD_DIM), lambda b, t: (b, 0, 0, 0))
    return pl.pallas_call(
        _delta_prompt_kernel,
        grid=(bsz, nt),
        in_specs=[rows(3 * DN_WIDTH), rows(LANES), state],
        out_specs=[rows(DN_WIDTH), state],
        out_shape=[
            jax.ShapeDtypeStruct((n, DN_WIDTH), BF16),
            jax.ShapeDtypeStruct((bsz, DN_HEADS, DN_HEAD_DIM, DN_HEAD_DIM), F32),
        ],
        scratch_shapes=[pltpu.VMEM((DN_HEADS, DN_HEAD_DIM, DN_HEAD_DIM), F32)],
        compiler_params=_cparams("arbitrary", "arbitrary"),
        name="delta_prompt",
    )(qkv, gb, s0)


def _delta_sample_kernel(qkv_ref, gb_ref, s0_ref, *rest):
    o_ref, sn_ref = rest[-2:]
    bt = qkv_ref.shape[0]
    c = SAMPLE_CHUNK
    probs = _gdn_problems(qkv_ref[...].reshape(bt * c, 3 * DN_WIDTH), gb_ref[...].reshape(bt * c, LANES), c)
    a_mat, b_mat, q_t, o_intra, d_last = _gdn_intra(*probs, c=c, small=True, **_SAMPLE_MODES)
    s = s0_ref[0].reshape(bt * DN_HEADS, DN_HEAD_DIM, DN_HEAD_DIM)
    o, s = _gdn_state_step(a_mat, b_mat, q_t, o_intra, d_last, s, _SAMPLE_STATE_MODE)
    for b in range(bt):
        for h in range(DN_HEADS):
            o_ref[b, :, h * DN_HEAD_DIM:(h + 1) * DN_HEAD_DIM] = o[b * DN_HEADS + h]
    sn_ref[0] = s.reshape(bt, DN_HEADS, DN_HEAD_DIM, DN_HEAD_DIM)


def _delta_sample(qkv_b, gb_b, state_all, layer, carried, bt):
    bsz = qkv_b.shape[0]
    blk = lambda w: pl.BlockSpec((bt, SAMPLE_CHUNK, w), lambda i: (i, 0, 0))
    state = pl.BlockSpec((1, bt, DN_HEADS, DN_HEAD_DIM, DN_HEAD_DIM), lambda i: (layer, i, 0, 0, 0))
    in_specs, args, aliases = [blk(3 * DN_WIDTH), blk(LANES), state], [qkv_b, gb_b, state_all], {}
    if carried is not None:
        in_specs.append(pl.BlockSpec(memory_space=pl.ANY))
        args.append(carried)
        aliases = {3: 1}
    return pl.pallas_call(
        _delta_sample_kernel,
        grid=(bsz // bt,),
        in_specs=in_specs,
        out_specs=[blk(DN_WIDTH), state],
        out_shape=[
            jax.ShapeDtypeStruct((bsz, SAMPLE_CHUNK, DN_WIDTH), F32),
            jax.ShapeDtypeStruct(state_all.shape, F32),
        ],
        input_output_aliases=aliases,
        compiler_params=_cparams("arbitrary"),
        name="delta_sample",
    )(*args)


def _outproj_kernel(yc_ref, o_ref, z_ref, x_ref, g1_ref, nw_ref, w_ref, lg_ref, lb_ref, out_ref):
    o = o_ref[...].astype(F32)
    parts = []
    for h in range(DN_HEADS):
        oh = o[:, h * DN_HEAD_DIM:(h + 1) * DN_HEAD_DIM]
        parts.append(oh * lax.rsqrt(jnp.mean(oh * oh, -1, keepdims=True) + RMS_EPS) * nw_ref[...])
    og = jnp.concatenate(parts, axis=-1) * _silu(z_ref[...].astype(F32))
    mixed = jnp.concatenate([yc_ref[...].astype(BF16), og.astype(BF16)], axis=-1)
    m = jnp.dot(mixed, w_ref[...], preferred_element_type=F32)
    r = DEEPNORM_ALPHA * x_ref[...] + (1.0 + g1_ref[0]) * m
    out_ref[...] = _layer_norm(r, lg_ref[...], lb_ref[...])


def _outproj(yc, o, z_src, z_blk, x, mod3, tpg, norm_w, w_out, ln_g, ln_b, tm):
    n = x.shape[0]
    vec = lambda w: pl.BlockSpec((1, w), lambda i: (0, 0))
    return pl.pallas_call(
        _outproj_kernel,
        grid=(n // tm,),
        in_specs=[
            pl.BlockSpec((tm, CONV_WIDTH), lambda i: (i, 0)),
            pl.BlockSpec((tm, DN_WIDTH), lambda i: (i, 0)),
            pl.BlockSpec((tm, DN_WIDTH), lambda i: (i, z_blk)),
            pl.BlockSpec((tm, D_MODEL), lambda i: (i, 0)),
            _mod_spec(mod3, 2, tpg),
            vec(DN_HEAD_DIM),
            pl.BlockSpec((D_MODEL, D_MODEL), lambda i: (0, 0)),
            vec(D_MODEL), vec(D_MODEL),
        ],
        out_specs=pl.BlockSpec((tm, D_MODEL), lambda i: (i, 0)),
        out_shape=jax.ShapeDtypeStruct((n, D_MODEL), F32),
        compiler_params=_cparams("arbitrary"),
        name="outproj",
    )(yc, o, z_src, x, mod3, norm_w, w_out, ln_g, ln_b)


R_E1, R_E2, R_G1, R_G2, R_R1, R_R2 = range(6)


def _router_kernel(x_ref, sc_ref, sh_ref, wr_ref, br_ref, route_ref, route_t_ref, cnt_ref):
    tm = x_ref.shape[0]
    u = x_ref[...] * (1.0 + sc_ref[0]) + sh_ref[0]
    logits = _mm(u, wr_ref[...], mode="x3") + br_ref[...]
    lane = lax.broadcasted_iota(jnp.int32, logits.shape, 1)
    valid = lane < N_EXPERTS
    logits = jnp.where(valid, logits, -jnp.inf)
    ex = jnp.exp(logits - jnp.max(logits, -1, keepdims=True))
    probs = jnp.where(valid, ex / jnp.sum(ex, -1, keepdims=True), -1.0)
    p1 = jnp.max(probs, -1, keepdims=True)
    i1 = jnp.min(jnp.where(probs == p1, lane, LANES), -1, keepdims=True)
    rest = jnp.where(lane == i1, -1.0, probs)
    p2 = jnp.max(rest, -1, keepdims=True)
    i2 = jnp.min(jnp.where(rest == p2, lane, LANES), -1, keepdims=True)
    tot = p1 + p2
    sel = jnp.where(lane == i1, 1.0, 0.0) + jnp.where(lane == i2, 1.0, 0.0)
    incl = _cumsum_chunks(sel, tm)
    excl = incl - sel
    r1 = jnp.sum(jnp.where(lane == i1, excl, 0.0), -1, keepdims=True)
    r2 = jnp.sum(jnp.where(lane == i2, excl, 0.0), -1, keepdims=True)
    cnt_ref[0] = incl[tm - 1:tm, :]
    route = jnp.zeros_like(logits)
    for j, val in enumerate((i1.astype(F32), i2.astype(F32), p1 / tot, p2 / tot, r1, r2)):
        route = jnp.where(lane == j, val, route)
    route_ref[...] = route
    route_t_ref[0] = jnp.transpose(route)[:SUBLANES, :]


def _router(x, mod3, tpg, w_router, b_router, tm):
    n = x.shape[0]
    return pl.pallas_call(
        _router_kernel,
        grid=(n // tm,),
        in_specs=[
            pl.BlockSpec((tm, D_MODEL), lambda i: (i, 0)),
            _mod_spec(mod3, 4, tpg),
            _mod_spec(mod3, 3, tpg),
            pl.BlockSpec((D_MODEL, LANES), lambda i: (0, 0)),
            pl.BlockSpec((1, LANES), lambda i: (0, 0)),
        ],
        out_specs=[pl.BlockSpec((tm, LANES), lambda i: (i, 0)),
                   pl.BlockSpec((1, SUBLANES, tm), lambda i: (i, 0, 0)),
                   pl.BlockSpec((1, 1, LANES), lambda i: (i, 0, 0))],
        out_shape=[jax.ShapeDtypeStruct((n, LANES), F32),
                   jax.ShapeDtypeStruct((n // tm, SUBLANES, tm), F32),
                   jax.ShapeDtypeStruct((n // tm, 1, LANES), F32)],
        compiler_params=_cparams("arbitrary"),
        name="router",
    )(x, mod3, mod3, w_router, b_router)


def _dispatch_plan(route_t, counts, tme):
    n_rt, _, tm = route_t.shape
    n = n_rt * tm
    cnt = counts[:, 0, :N_EXPERTS].astype(jnp.int32)
    tile_off = jnp.cumsum(cnt, axis=0) - cnt
    tot = jnp.sum(cnt, axis=0)
    gsz = (tot + tme - 1) // tme * tme
    gend = jnp.cumsum(gsz)
    goff = gend - gsz
    base = (goff[None, :] + tile_off)[:, :, None]
    ids = jnp.arange(N_EXPERTS, dtype=jnp.int32)[None, :, None]
    field = lambda j: route_t[:, j:j + 1, :].astype(jnp.int32)
    pick = lambda e: jnp.sum(jnp.where(ids == e, base, 0), axis=1, keepdims=True)
    slot1 = (pick(field(R_E1)) + field(R_R1)).reshape(n)
    slot2 = (pick(field(R_E2)) + field(R_R2)).reshape(n)
    n_tiles = 2 * n // tme + N_EXPERTS
    start = jnp.arange(n_tiles, dtype=jnp.int32) * tme
    valid = (start < gend[-1]).astype(jnp.int32)
    blk = jnp.minimum(jnp.arange(n_tiles, dtype=jnp.int32), gend[-1] // tme - 1)
    tile_e = jnp.sum((blk * tme)[:, None] >= gend[None, :], axis=1).astype(jnp.int32)
    zoff = jnp.where(gsz > 0, gend - tme, (n_tiles + jnp.arange(N_EXPERTS, dtype=jnp.int32)) * tme)
    return slot1, slot2, zoff, tile_e, valid, blk, n_tiles


def _dispatch_kernel(s1_ref, s2_ref, zoff_ref, x_ref, sc_ref, sh_ref, ug_hbm, ubuf, zbuf, sem):
    i = pl.program_id(0)
    tm = x_ref.shape[0]
    tme = zbuf.shape[0]

    @pl.when(i == 0)
    def _():
        zbuf[...] = jnp.zeros_like(zbuf)
        fill = lambda e: pltpu.make_async_copy(
            zbuf, ug_hbm.at[pl.ds(pl.multiple_of(zoff_ref[e], 8), tme)], sem.at[2])
        for e in range(N_EXPERTS):
            fill(e).start()
        for e in range(N_EXPERTS):
            fill(e).wait()

    ubuf[...] = (x_ref[...] * (1.0 + sc_ref[0]) + sh_ref[0]).reshape(ubuf.shape)
    base = i * tm

    def issue(j, carry):
        for k in range(SUBLANES):
            row = ubuf.at[j, pl.ds(k, 1)]
            t = base + j * SUBLANES + k
            pltpu.make_async_copy(row, ug_hbm.at[pl.ds(s1_ref[t], 1)], sem.at[0]).start()
            pltpu.make_async_copy(row, ug_hbm.at[pl.ds(s2_ref[t], 1)], sem.at[1]).start()
        return carry

    lax.fori_loop(0, tm // SUBLANES, issue, 0)
    pltpu.make_async_copy(ubuf, ubuf, sem.at[0]).wait()
    pltpu.make_async_copy(ubuf, ubuf, sem.at[1]).wait()


def _dispatch(x, mod3, tpg, slot1, slot2, zoff, tm, tme, rows):
    n = x.shape[0]
    return pl.pallas_call(
        _dispatch_kernel,
        grid_spec=pltpu.PrefetchScalarGridSpec(
            num_scalar_prefetch=3,
            grid=(n // tm,),
            in_specs=[
                pl.BlockSpec((tm, D_MODEL), lambda i, *_: (i, 0)),
                _mod_spec(mod3, 4, tpg),
                _mod_spec(mod3, 3, tpg),
            ],
            out_specs=pl.BlockSpec(memory_space=pl.ANY),
            scratch_shapes=[pltpu.VMEM((tm // SUBLANES, SUBLANES, D_MODEL), F32),
                            pltpu.VMEM((tme, D_MODEL), F32), pltpu.SemaphoreType.DMA((3,))],
        ),
        out_shape=jax.ShapeDtypeStruct((rows, D_MODEL), F32),
        compiler_params=_cparams("arbitrary"),
        name="dispatch",
    )(slot1, slot2, zoff, x, mod3, mod3)


def _expert_kernel(te_ref, tv_ref, tb_ref, ug_ref, wg_ref, wu_ref, wd_ref, ys_ref):
    i = pl.program_id(0)

    @pl.when(tv_ref[i] == 1)
    def _():
        u = ug_ref[...].astype(BF16)
        h = _silu(jnp.dot(u, wg_ref[0].astype(BF16), preferred_element_type=F32)) * jnp.dot(
            u, wu_ref[0].astype(BF16), preferred_element_type=F32)
        ys_ref[...] = jnp.dot(h.astype(BF16), wd_ref[0].astype(BF16), preferred_element_type=F32)

    @pl.when(tv_ref[i] == 0)
    def _():
        ys_ref[...] = jnp.zeros_like(ys_ref)


def _experts(ug, tile_e, valid, blk, w_gate, w_up, w_down, tme, n_tiles):
    d_e = w_gate.shape[-1]
    return pl.pallas_call(
        _expert_kernel,
        grid_spec=pltpu.PrefetchScalarGridSpec(
            num_scalar_prefetch=3,
            grid=(n_tiles,),
            in_specs=[
                pl.BlockSpec((tme, D_MODEL), lambda i, te, tv, tb: (tb[i], 0)),
                pl.BlockSpec((1, D_MODEL, d_e), lambda i, te, tv, tb: (te[i], 0, 0)),
                pl.BlockSpec((1, D_MODEL, d_e), lambda i, te, tv, tb: (te[i], 0, 0)),
                pl.BlockSpec((1, d_e, D_MODEL), lambda i, te, tv, tb: (te[i], 0, 0)),
            ],
            out_specs=pl.BlockSpec((tme, D_MODEL), lambda i, te, tv, tb: (i, 0)),
        ),
        out_shape=jax.ShapeDtypeStruct((n_tiles * tme, D_MODEL), F32),
        compiler_params=_cparams("arbitrary"),
        name="experts",
    )(tile_e, valid, blk, ug, w_gate, w_up, w_down)


def _combine_kernel(s1_ref, s2_ref, ys_hbm, route_ref, x_ref, g2_ref, lg_ref, lb_ref, out_ref,
                    y1buf, y2buf, sem):
    i = pl.program_id(0)
    tm = x_ref.shape[0]
    base = i * tm

    def issue(j, carry):
        for k in range(SUBLANES):
            t = base + j * SUBLANES + k
            pltpu.make_async_copy(ys_hbm.at[pl.ds(s1_ref[t], 1)], y1buf.at[j, pl.ds(k, 1)], sem.at[0]).start()
            pltpu.make_async_copy(ys_hbm.at[pl.ds(s2_ref[t], 1)], y2buf.at[j, pl.ds(k, 1)], sem.at[1]).start()
        return carry

    lax.fori_loop(0, tm // SUBLANES, issue, 0)
    pltpu.make_async_copy(y1buf, y1buf, sem.at[0]).wait()
    pltpu.make_async_copy(y2buf, y2buf, sem.at[1]).wait()
    route = route_ref[...]
    y1 = y1buf[...].reshape(tm, D_MODEL)
    y2 = y2buf[...].reshape(tm, D_MODEL)
    f = route[:, R_G1:R_G1 + 1] * y1 + route[:, R_G2:R_G2 + 1] * y2
    r = DEEPNORM_ALPHA * x_ref[...] + (1.0 + g2_ref[0]) * f
    out_ref[...] = _layer_norm(r, lg_ref[...], lb_ref[...])


def _combine(ys, slot1, slot2, route, x, mod3, tpg, ln_g, ln_b, tm):
    n = x.shape[0]
    vec = lambda w: pl.BlockSpec((1, w), lambda i, *_: (0, 0))
    return pl.pallas_call(
        _combine_kernel,
        grid_spec=pltpu.PrefetchScalarGridSpec(
            num_scalar_prefetch=2,
            grid=(n // tm,),
            in_specs=[
                pl.BlockSpec(memory_space=pl.ANY),
                pl.BlockSpec((tm, LANES), lambda i, *_: (i, 0)),
                pl.BlockSpec((tm, D_MODEL), lambda i, *_: (i, 0)),
                _mod_spec(mod3, 5, tpg),
                vec(D_MODEL), vec(D_MODEL),
            ],
            out_specs=pl.BlockSpec((tm, D_MODEL), lambda i, *_: (i, 0)),
            scratch_shapes=[pltpu.VMEM((tm // SUBLANES, SUBLANES, D_MODEL), F32),
                            pltpu.VMEM((tm // SUBLANES, SUBLANES, D_MODEL), F32),
                            pltpu.SemaphoreType.DMA((2,))],
        ),
        out_shape=jax.ShapeDtypeStruct((n, D_MODEL), F32),
        compiler_params=_cparams("arbitrary"),
        name="combine",
    )(slot1, slot2, ys, route, x, mod3, ln_g, ln_b)


def _moe(x, mod3, tpg, w_router, b_router, w_gate, w_up, w_down, ln_g, ln_b, tm, tme):
    route, route_t, counts = _router(x, mod3, tpg, w_router, b_router, tm)
    slot1, slot2, zoff, tile_e, valid, blk, n_tiles = _dispatch_plan(route_t, counts, tme)
    ug = _dispatch(x, mod3, tpg, slot1, slot2, zoff, tm, tme, (n_tiles + N_EXPERTS) * tme)
    ys = _experts(ug, tile_e, valid, blk, w_gate, w_up, w_down, tme, n_tiles)
    return _combine(ys, slot1, slot2, route, x, mod3, tpg, ln_g, ln_b, tm)


def _ffn_kernel(x_ref, sc_ref, sh_ref, g2_ref, wg_ref, wu_ref, wd_ref, lg_ref, lb_ref, out_ref, *, tf):
    x = x_ref[...]
    u = (x * (1.0 + sc_ref[0]) + sh_ref[0]).astype(BF16)
    y = None
    for f0 in range(0, wg_ref.shape[1], tf):
        h = _silu(jnp.dot(u, wg_ref[:, f0:f0 + tf], preferred_element_type=F32)) * jnp.dot(
            u, wu_ref[:, f0:f0 + tf], preferred_element_type=F32)
        part = jnp.dot(h.astype(BF16), wd_ref[f0:f0 + tf, :], preferred_element_type=F32)
        y = part if y is None else y + part
    r = DEEPNORM_ALPHA * x + (1.0 + g2_ref[0]) * y
    out_ref[...] = _layer_norm(r, lg_ref[...], lb_ref[...])


def _ffn(x, mod3, tpg, w_gate, w_up, w_down, ln_g, ln_b, tm, tf):
    n = x.shape[0]
    vec = lambda w: pl.BlockSpec((1, w), lambda i: (0, 0))
    resident = lambda a: pl.BlockSpec(a.shape, lambda i: (0, 0), pipeline_mode=pl.Buffered(1))
    return pl.pallas_call(
        functools.partial(_ffn_kernel, tf=tf),
        grid=(n // tm,),
        in_specs=[
            pl.BlockSpec((tm, D_MODEL), lambda i: (i, 0)),
            _mod_spec(mod3, 4, tpg),
            _mod_spec(mod3, 3, tpg),
            _mod_spec(mod3, 5, tpg),
            resident(w_gate), resident(w_up), resident(w_down),
            vec(D_MODEL), vec(D_MODEL),
        ],
        out_specs=pl.BlockSpec((tm, D_MODEL), lambda i: (i, 0)),
        out_shape=jax.ShapeDtypeStruct((n, D_MODEL), F32),
        compiler_params=_cparams("arbitrary"),
        name="ffn_dense",
    )(x, mod3, mod3, mod3, w_gate, w_up, w_down, ln_g, ln_b)


def _trunk(x, mod3s, rows_per_group, prm, tiles, mixer_fn):
    tm, tme = tiles["tm"], tiles["tme"]
    tpg = rows_per_group // tm
    states = []
    for l in range(DEPTH):
        mod3 = mod3s[l]
        yc, o, z_src, z_blk, st = mixer_fn(l, x, mod3)
        states.append(st)
        x = _outproj(yc, o, z_src, z_blk, x, mod3, tpg, prm["norm_w"][l], prm["w_out"][l],
                     prm["ln_g"][l, 0:1], prm["ln_b"][l, 0:1], tm)
        j = l // 2
        if l % 2 == 0:
            x = _ffn(x, mod3, rows_per_group // tiles["tm_ffn"], prm["w_ff_gate"][j], prm["w_ff_up"][j],
                     prm["w_ff_down"][j], prm["ln_g"][l, 1:2], prm["ln_b"][l, 1:2],
                     tiles["tm_ffn"], tiles["tf"])
        else:
            x = _moe(x, mod3, tpg, prm["w_router"][j], prm["b_router"][j], prm["w_exp_gate"][j],
                     prm["w_exp_up"][j], prm["w_exp_down"][j], prm["ln_g"][l, 1:2], prm["ln_b"][l, 1:2],
                     tm, tme)
    return x, states


def kernel(x_prompt, x_sample, state_conv, state_dn_conv, state_dn, c_prompt, c_sample, w_in, conv_w,
           dn_conv_w, a_log, dt_bias, dn_norm_w, w_out, w_ada, b_ada, ln_g, ln_b, w_ff_gate, w_ff_up,
           w_ff_down, w_router, b_router, w_exp_gate, w_exp_up, w_exp_down):
    bp, seq, d = x_prompt.shape
    bs, steps, _ = x_sample.shape
    pad_lanes = LANES - N_EXPERTS
    hp = jnp.zeros((DEPTH, 2, LANES), F32)
    hp = hp.at[:, 0, G_LANE:].set(a_log).at[:, 1, G_LANE:].set(dt_bias)
    w_in_t = jnp.swapaxes(w_in, 1, 2)
    prm = {
        "w_main": w_in_t[:, :P_MAIN].astype(BF16),
        "w_small": w_in_t[:, w_in_t.shape[1] - LANES:],
        "norm_w": dn_norm_w.reshape(DEPTH, 1, DN_HEAD_DIM),
        "w_out": w_out.astype(BF16),
        "ln_g": ln_g, "ln_b": ln_b,
        "w_ff_gate": w_ff_gate.astype(BF16), "w_ff_up": w_ff_up.astype(BF16),
        "w_ff_down": w_ff_down.astype(BF16),
        "w_router": jnp.pad(w_router, ((0, 0), (0, 0), (0, pad_lanes))),
        "b_router": jnp.pad(b_router, ((0, 0), (0, pad_lanes)))[:, None, :],
        "w_exp_gate": w_exp_gate, "w_exp_up": w_exp_up, "w_exp_down": w_exp_down,
    }

    mod = _ada(jnp.concatenate([c_prompt, c_sample], axis=0), w_ada, b_ada)
    mod_p = [mod[l, :bp].reshape(bp, 1, 6 * d) for l in range(DEPTH)]
    mod_s = [mod[l, bp:].reshape(1, bs, 6 * d) for l in range(DEPTH)]

    tm_p, tc_delta = 512, 512
    zc = jnp.zeros((bp, CONV_K - 1, CONV_WIDTH), F32)
    zd = jnp.zeros((bp, DN_CONV_K - 1, 3 * DN_WIDTH), F32)
    zs = jnp.zeros((bp, DN_HEADS, DN_HEAD_DIM, DN_HEAD_DIM), F32)

    def mixer_prompt(l, x, mod3):
        yc, qkv, z, gb, ncb, ndb = _inproj_pre_prompt(
            x, mod3, prm["w_main"][l], prm["w_small"][l], conv_w[l], dn_conv_w[l], hp[l], zc, zd,
            bp, seq, tm_p)
        o, s_new = _delta_prompt(qkv, gb, zs, bp, seq, tc_delta)
        return yc, o, z, 0, (ncb, ndb, s_new)

    y_p, st_p = _trunk(x_prompt.reshape(bp * seq, d), mod_p, seq, prm,
                       dict(tm=tm_p, tm_ffn=512, tf=256, tme=512), mixer_prompt)

    bt_pre, bt_delta = 32, 8
    dn_states = []

    def mixer_sample(l, x, mod3):
        pm, ps = _inproj(x, mod3, steps, prm["w_main"][l], prm["w_small"][l], bs)
        cb_tm = jnp.transpose(state_conv[l], (1, 0, 2))
        db_tm = jnp.transpose(state_dn_conv[l], (1, 0, 2))
        yc, qkv, gb, ncb, ndb = _pre_sample(pm, ps, conv_w[l], dn_conv_w[l], hp[l], cb_tm, db_tm,
                                            bs, steps, bt_pre)
        to_b = lambda a: jnp.pad(jnp.transpose(a, (1, 0, 2)), ((0, 0), (0, SAMPLE_CHUNK - steps), (0, 0)))
        o_b, s_all = _delta_sample(to_b(qkv), to_b(gb), state_dn, l, dn_states[-1] if dn_states else None,
                                   bt_delta)
        dn_states.append(s_all)
        o = jnp.transpose(o_b[:, :steps], (1, 0, 2)).reshape(steps * bs, DN_WIDTH)
        return (yc.reshape(steps * bs, CONV_WIDTH), o, pm, OFF_Z // DN_WIDTH,
                (jnp.transpose(ncb, (1, 0, 2)), jnp.transpose(ndb, (1, 0, 2))))

    x_s = jnp.transpose(x_sample, (1, 0, 2)).reshape(steps * bs, d)
    y_s, st_s = _trunk(x_s, mod_s, steps * bs, prm,
                       dict(tm=bs, tm_ffn=bs, tf=w_ff_gate.shape[-1],tme=128), mixer_sample)
    y_s = jnp.transpose(y_s.reshape(steps, bs, d), (1, 0, 2))

    stack = lambda sts, i: jnp.stack([s[i] for s in sts])
    return (y_p.reshape(bp, seq, d), y_s,
            stack(st_p, 0), stack(st_p, 1), stack(st_p, 2),
            stack(st_s, 0), stack(st_s, 1), dn_states[-1])
```

```python
import functools
import math

import jax
import jax.numpy as jnp
from jax import lax
from jax.experimental import pallas as pl
from jax.experimental.pallas import tpu as pltpu

F32 = jnp.float32
BF16 = jnp.bfloat16

D_MODEL = 1024
DEPTH = 2
CONV_WIDTH = 512
CONV_K = 3
DN_HEADS = 4
DN_HEAD_DIM = 128
DN_WIDTH = DN_HEADS * DN_HEAD_DIM
DN_CONV_K = 4
DN_CHUNK = 64
N_EXPERTS = 8
DEEPNORM_ALPHA = (2.0 * DEPTH) ** 0.25
LN_EPS = 1e-5
RMS_EPS = 1e-6

OFF_CC = CONV_WIDTH
OFF_CH = 2 * CONV_WIDTH
OFF_QKV = 3 * CONV_WIDTH
OFF_Z = OFF_QKV + 3 * DN_WIDTH
P_MAIN = OFF_Z + DN_WIDTH
LANES = 128
SUBLANES = 8
BETA_LANE = LANES - 2 * DN_HEADS
G_LANE = LANES - DN_HEADS
SAMPLE_CHUNK = 8

VMEM_LIMIT = 56 * 1024 * 1024


def _cparams(*sem):
    return pltpu.CompilerParams(dimension_semantics=sem, vmem_limit_bytes=VMEM_LIMIT)


def _silu(x):
    return x * jax.nn.sigmoid(x)


def _layer_norm(r, g, b):
    mu = jnp.mean(r, -1, keepdims=True)
    xc = r - mu
    var = jnp.mean(xc * xc, -1, keepdims=True)
    return xc * lax.rsqrt(var + LN_EPS) * g + b


def _split_bf16(a):
    hi = a.astype(BF16)
    lo = (a - hi.astype(F32)).astype(BF16)
    return hi, lo


_NN = (((1,), (0,)), ((), ()))
_NT = (((1,), (1,)), ((), ()))
_BNN = (((2,), (1,)), ((0,), (0,)))
_BNT = (((2,), (2,)), ((0,), (0,)))
_BTN = (((1,), (1,)), ((0,), (0,)))


def _mm(a, b, dims=_NN, mode="bf16"):
    if mode == "f32":
        return lax.dot_general(a, b, dims, precision=lax.Precision.HIGHEST, preferred_element_type=F32)
    if mode == "bf16":
        return lax.dot_general(a.astype(BF16), b.astype(BF16), dims, preferred_element_type=F32)
    ah, al = _split_bf16(a)
    bh, bl = _split_bf16(b)
    d = functools.partial(lax.dot_general, dimension_numbers=dims, preferred_element_type=F32)
    return d(ah, bh) + (d(ah, bl) + d(al, bh))


def _ada_kernel(c_ref, w_ref, b_ref, o_ref):
    s = _silu(c_ref[...]).astype(BF16)
    o_ref[0] = jnp.dot(s, w_ref[0].astype(BF16), preferred_element_type=F32) + b_ref[0]


def _ada(c_all, w_ada, b_ada):
    rows = c_all.shape[0]
    tn = 1536
    return pl.pallas_call(
        _ada_kernel,
        grid=(DEPTH, 6 * D_MODEL // tn),
        in_specs=[
            pl.BlockSpec((rows, D_MODEL), lambda l, j: (0, 0)),
            pl.BlockSpec((1, D_MODEL, tn), lambda l, j: (l, 0, j)),
            pl.BlockSpec((1, 1, tn), lambda l, j: (l, 0, j)),
        ],
        out_specs=pl.BlockSpec((1, rows, tn), lambda l, j: (l, 0, j)),
        out_shape=jax.ShapeDtypeStruct((DEPTH, rows, 6 * D_MODEL), F32),
        compiler_params=_cparams("arbitrary", "arbitrary"),
        name="ada",
    )(c_all, w_ada, b_ada.reshape(DEPTH, 1, 6 * D_MODEL))


def _mod_spec(mod3, chunk, tiles_per_group):
    r = mod3.shape[1]
    return pl.BlockSpec((1, r, D_MODEL), lambda i, *_: (i // tiles_per_group, 0, chunk))


def _inproj_kernel(x_ref, sc_ref, sh_ref, wm_ref, ws_ref, pm_ref, ps_ref):
    u = x_ref[...] * (1.0 + sc_ref[0]) + sh_ref[0]
    pm_ref[...] = lax.dot_general(u.astype(BF16), wm_ref[...], _NT, preferred_element_type=F32)
    ps_ref[...] = _mm(u, ws_ref[...], _NT, mode="x3")


def _inproj(x, mod3, tpg, w_main, w_small, tm):
    n = x.shape[0]
    return pl.pallas_call(
        _inproj_kernel,
        grid=(n // tm,),
        in_specs=[
            pl.BlockSpec((tm, D_MODEL), lambda i: (i, 0)),
            _mod_spec(mod3, 1, tpg),
            _mod_spec(mod3, 0, tpg),
            pl.BlockSpec((P_MAIN, D_MODEL), lambda i: (0, 0)),
            pl.BlockSpec((LANES, D_MODEL), lambda i: (0, 0)),
        ],
        out_specs=[
            pl.BlockSpec((tm, P_MAIN), lambda i: (i, 0)),
            pl.BlockSpec((tm, LANES), lambda i: (i, 0)),
        ],
        out_shape=[
            jax.ShapeDtypeStruct((n, P_MAIN), F32),
            jax.ShapeDtypeStruct((n, LANES), F32),
        ],
        compiler_params=_cparams("arbitrary"),
        name="inproj",
    )(x, mod3, mod3, w_main, w_small)


def _gates(ps, hp):
    lane = lax.broadcasted_iota(jnp.int32, ps.shape, 1)
    beta = jax.nn.sigmoid(ps)
    g = -jnp.exp(hp[0:1, :]) * jax.nn.softplus(ps + hp[1:2, :])
    return jnp.where(lane >= G_LANE, g, jnp.where(lane >= BETA_LANE, beta, 0.0))


def _qkv_finish(y, out_ref, idx, first_head=0):
    y = _silu(y)
    for j in range(y.shape[1] // DN_HEAD_DIM):
        h = first_head + j
        seg = y[:, j * DN_HEAD_DIM:(j + 1) * DN_HEAD_DIM]
        if h < 2 * DN_HEADS:
            seg = seg * lax.rsqrt(jnp.sum(seg * seg, -1, keepdims=True) + RMS_EPS)
            if h < DN_HEADS:
                seg = seg * (DN_HEAD_DIM ** -0.5)
        out_ref[idx + (slice(None), slice(h * DN_HEAD_DIM, (h + 1) * DN_HEAD_DIM))] = seg


def _shifted(x, tail_ref, i, row):
    s = pltpu.roll(x, i, 0)
    top = s[0:SUBLANES]
    for r in range(i):
        top = jnp.where(row == r, tail_ref[8 - i + r:8 - i + r + 1, :], top)
    return jnp.concatenate([top, s[SUBLANES:]], axis=0)


def _inproj_pre_prompt_kernel(x_ref, sc_ref, sh_ref, wm_ref, ws_ref, cw_ref, dw_ref, hp_ref, cb_ref, db_ref,
                              yc_ref, qkv_ref, z_ref, gb_ref, ncb_ref, ndb_ref, tc_scr, td_scr):
    t = pl.program_id(1)
    tm = x_ref.shape[0]

    @pl.when(t == 0)
    def _():
        tc_scr[8 - (CONV_K - 1):8, :] = cb_ref[0]
        td_scr[8 - (DN_CONV_K - 1):8, :] = db_ref[0]

    u_f32 = x_ref[...] * (1.0 + sc_ref[0]) + sh_ref[0]
    u = u_f32.astype(BF16)
    proj = lambda lo, hi: lax.dot_general(u, wm_ref[lo:hi, :], _NT, preferred_element_type=F32)

    row = lax.broadcasted_iota(jnp.int32, (SUBLANES, DN_WIDTH), 0)

    def dn_part(xp, part):
        cols = slice(part * DN_WIDTH, (part + 1) * DN_WIDTH)
        tail = td_scr.at[:, cols]
        yp = _shifted(xp, tail, 3, row) * dw_ref[0:1, cols]
        yp = yp + _shifted(xp, tail, 2, row) * dw_ref[1:2, cols]
        yp = yp + _shifted(xp, tail, 1, row) * dw_ref[2:3, cols]
        yp = yp + xp * dw_ref[3:4, cols]
        _qkv_finish(yp, qkv_ref, (), first_head=part * DN_HEADS)
        return xp[tm - (DN_CONV_K - 1):tm, :]

    qkv_cols = lambda part: (OFF_QKV + part * DN_WIDTH, OFF_QKV + (part + 1) * DN_WIDTH)
    x_q = proj(*qkv_cols(0))
    x_k = proj(*qkv_cols(1))
    last_q = dn_part(x_q, 0)
    x_v = proj(*qkv_cols(2))
    last_k = dn_part(x_k, 1)
    p_c = proj(OFF_CC, OFF_CH)
    last_v = dn_part(x_v, 2)
    p_h = proj(OFF_CH, OFF_QKV)
    cgh = p_c * p_h
    p_b = proj(0, OFF_CC)
    y = _shifted(cgh, tc_scr, 2, row) * cw_ref[0:1, :]
    y = y + _shifted(cgh, tc_scr, 1, row) * cw_ref[1:2, :]
    y = y + cgh * cw_ref[2:3, :]
    p_z = proj(OFF_Z, P_MAIN)
    yc_ref[...] = (p_b * y).astype(yc_ref.dtype)
    last_c = cgh[tm - (CONV_K - 1):tm, :]
    small = _mm(u_f32, ws_ref[...], _NT, mode="x3")
    z_ref[...] = p_z.astype(z_ref.dtype)
    gb_ref[...] = _gates(small, hp_ref[...])

    last_d = jnp.concatenate([last_q, last_k, last_v], axis=-1)
    tc_scr[8 - (CONV_K - 1):8, :] = last_c
    td_scr[8 - (DN_CONV_K - 1):8, :] = last_d

    @pl.when(t == pl.num_programs(1) - 1)
    def _():
        ncb_ref[0] = last_c
        ndb_ref[0] = last_d


def _inproj_pre_prompt(x, mod3, w_main, w_small, conv_w, dn_conv_w, hp, conv_buf, dn_buf, bsz, seq, tm):
    n = x.shape[0]
    nt = seq // tm
    rows = lambda w: pl.BlockSpec((tm, w), lambda b, t: (b * nt + t, 0))
    full = lambda a: pl.BlockSpec(a.shape, lambda b, t: (0,) * a.ndim)
    per_b = lambda k, w: pl.BlockSpec((1, k, w), lambda b, t: (b, 0, 0))
    mod = lambda chunk: pl.BlockSpec((1, 1, D_MODEL), lambda b, t: (b, 0, chunk))
    return pl.pallas_call(
        _inproj_pre_prompt_kernel,
        grid=(bsz, nt),
        in_specs=[rows(D_MODEL), mod(1), mod(0), full(w_main), full(w_small),
                  full(conv_w), full(dn_conv_w), full(hp),
                  per_b(CONV_K - 1, CONV_WIDTH), per_b(DN_CONV_K - 1, 3 * DN_WIDTH)],
        out_specs=[rows(CONV_WIDTH), rows(3 * DN_WIDTH), rows(DN_WIDTH), rows(LANES),
                   per_b(CONV_K - 1, CONV_WIDTH), per_b(DN_CONV_K - 1, 3 * DN_WIDTH)],
        out_shape=[
            jax.ShapeDtypeStruct((n, CONV_WIDTH), BF16),
            jax.ShapeDtypeStruct((n, 3 * DN_WIDTH), F32),
            jax.ShapeDtypeStruct((n, DN_WIDTH), BF16),
            jax.ShapeDtypeStruct((n, LANES), F32),
            jax.ShapeDtypeStruct((bsz, CONV_K - 1, CONV_WIDTH), F32),
            jax.ShapeDtypeStruct((bsz, DN_CONV_K - 1, 3 * DN_WIDTH), F32),
        ],
        scratch_shapes=[pltpu.VMEM((8, CONV_WIDTH), F32), pltpu.VMEM((8, 3 * DN_WIDTH), F32)],
        compiler_params=_cparams("arbitrary", "arbitrary"),
        name="inproj_pre_prompt",
    )(x, mod3, mod3, w_main, w_small, conv_w, dn_conv_w, hp, conv_buf, dn_buf)


def _pre_sample_kernel(pm_ref, ps_ref, cw_ref, dw_ref, hp_ref, cb_ref, db_ref,
                       yc_ref, qkv_ref, gb_ref, ncb_ref, ndb_ref):
    steps = pm_ref.shape[0]
    cgh = [pm_ref[t, :, OFF_CC:OFF_CH] * pm_ref[t, :, OFF_CH:OFF_QKV] for t in range(steps)]
    ext = [cb_ref[i] for i in range(CONV_K - 1)] + cgh
    for t in range(steps):
        y = ext[t] * cw_ref[0:1, :]
        for i in range(1, CONV_K):
            y = y + ext[t + i] * cw_ref[i:i + 1, :]
        yc_ref[t] = pm_ref[t, :, 0:OFF_CC] * y
    for i in range(CONV_K - 1):
        ncb_ref[i] = ext[len(ext) - (CONV_K - 1) + i]

    xq = [pm_ref[t, :, OFF_QKV:OFF_Z] for t in range(steps)]
    extq = [db_ref[i] for i in range(DN_CONV_K - 1)] + xq
    for t in range(steps):
        y = extq[t] * dw_ref[0:1, :]
        for i in range(1, DN_CONV_K):
            y = y + extq[t + i] * dw_ref[i:i + 1, :]
        _qkv_finish(y, qkv_ref, (t,))
        gb_ref[t] = _gates(ps_ref[t], hp_ref[...])
    for i in range(DN_CONV_K - 1):
        ndb_ref[i] = extq[len(extq) - (DN_CONV_K - 1) + i]


def _pre_sample(pm, ps, conv_w, dn_conv_w, hp, conv_buf_tm, dn_buf_tm, bsz, steps, bt):
    slab = lambda k, w: pl.BlockSpec((k, bt, w), lambda i: (0, i, 0))
    full = lambda a: pl.BlockSpec(a.shape, lambda i: (0,) * a.ndim)
    return pl.pallas_call(
        _pre_sample_kernel,
        grid=(bsz // bt,),
        in_specs=[slab(steps, P_MAIN), slab(steps, LANES), full(conv_w), full(dn_conv_w), full(hp),
                  slab(CONV_K - 1, CONV_WIDTH), slab(DN_CONV_K - 1, 3 * DN_WIDTH)],
        out_specs=[slab(steps, CONV_WIDTH), slab(steps, 3 * DN_WIDTH), slab(steps, LANES),
                   slab(CONV_K - 1, CONV_WIDTH), slab(DN_CONV_K - 1, 3 * DN_WIDTH)],
        out_shape=[
            jax.ShapeDtypeStruct((steps, bsz, CONV_WIDTH), F32),
            jax.ShapeDtypeStruct((steps, bsz, 3 * DN_WIDTH), F32),
            jax.ShapeDtypeStruct((steps, bsz, LANES), F32),
            jax.ShapeDtypeStruct((CONV_K - 1, bsz, CONV_WIDTH), F32),
            jax.ShapeDtypeStruct((DN_CONV_K - 1, bsz, 3 * DN_WIDTH), F32),
        ],
        compiler_params=_cparams("arbitrary"),
        name="pre_sample",
    )(pm.reshape(steps, bsz, P_MAIN), ps.reshape(steps, bsz, LANES), conv_w, dn_conv_w, hp,
      conv_buf_tm, dn_buf_tm)


def _cumsum_chunks(x, c):
    row = lax.broadcasted_iota(jnp.int32, x.shape, 0) & (c - 1)
    s = 1
    while s < c:
        x = x + jnp.where(row >= s, pltpu.roll(x, s, 0), 0.0)
        s *= 2
    return x


def _gdn_problems(qkv, gb, c):
    n = qkv.shape[0] // c
    gc = _cumsum_chunks(gb, c)
    q, k, v, g_b, beta_b = [], [], [], [], []
    for ci in range(n):
        rs = slice(ci * c, (ci + 1) * c)
        for h in range(DN_HEADS):
            ls = lambda base: slice(base + h * DN_HEAD_DIM, base + (h + 1) * DN_HEAD_DIM)
            q.append(qkv[rs, ls(0)])
            k.append(qkv[rs, ls(DN_WIDTH)])
            v.append(qkv[rs, ls(2 * DN_WIDTH)])
            g_b.append(jnp.broadcast_to(gc[rs, G_LANE + h:G_LANE + h + 1], (c, DN_HEAD_DIM)))
            beta_b.append(jnp.broadcast_to(gb[rs, BETA_LANE + h:BETA_LANE + h + 1], (c, DN_HEAD_DIM)))
    return tuple(jnp.stack(a) for a in (q, k, v, g_b, beta_b))


def _unit_lower_solve(m, rhs, c, small, mode):
    if small:
        sol = rhs
        for j in range(c - 1):
            sol = sol - m[:, :, j:j + 1] * sol[:, j:j + 1, :]
        return sol
    row = lax.broadcasted_iota(jnp.int32, m.shape, 1)
    col = lax.broadcasted_iota(jnp.int32, m.shape, 2)
    p = -m
    t = jnp.where(row == col, 1.0, 0.0) + p
    p = _mm(p, p, _BNN, "bf16")
    levels = int(math.log2(c))
    for lvl in range(1, levels):
        if lvl < levels - 1:
            y = _mm(jnp.concatenate([t, p], axis=1), p, _BNN, "bf16")
            t = t + y[:, :c]
            p = y[:, c:]
        else:
            t = t + _mm(t, p, _BNN, "bf16")
    x = _mm(t, rhs, _BNN, "bf16")
    resid = rhs - x - _mm(m, x, _BNN, mode)
    return x + _mm(t, resid, _BNN, "bf16")


def _gdn_intra(q, k, v, g_b, beta_b, *, c, small, mode_gram, mode_solve, mode_apply):
    n = q.shape[0]
    row = lax.broadcasted_iota(jnp.int32, (n, c, c), 1)
    col = lax.broadcasted_iota(jnp.int32, (n, c, c), 2)
    g_cc = g_b[:, :, :c]
    g_row = jnp.sum(jnp.where(row == col, g_cc, 0.0), axis=1, keepdims=True)
    gamma = jnp.exp(jnp.where(row >= col, g_cc - g_row, -jnp.inf))
    kbeta = k * beta_b
    exp_g = jnp.exp(g_b)
    gram = _mm(jnp.concatenate([kbeta, q], axis=1), k, _BNT, mode_gram)
    m = jnp.where(row > col, gram[:, :c] * gamma, 0.0)
    attn = gram[:, c:] * gamma
    rhs = jnp.concatenate([kbeta * exp_g, v * beta_b], axis=-1)
    sol = _unit_lower_solve(m, rhs, c, small, mode_solve)
    g_last = g_b[:, c - 1:c, :]
    k_dec = k * jnp.exp(g_last - g_b)
    kd = _mm(k_dec, sol, _BTN, mode_apply)
    at = _mm(attn, sol, _BNN, mode_apply)
    q_t = q * exp_g - at[..., :DN_HEAD_DIM]
    return kd[..., :DN_HEAD_DIM], kd[..., DN_HEAD_DIM:], q_t, at[..., DN_HEAD_DIM:], jnp.exp(g_last)


def _gdn_state_step(a_mat, b_mat, q_t, o_intra, d_last, s, mode):
    r = _mm(jnp.concatenate([a_mat, q_t], axis=1), s, _BNN, mode)
    s_new = s * d_last - r[:, :DN_HEAD_DIM] + b_mat
    return r[:, DN_HEAD_DIM:] + o_intra, s_new


_PROMPT_MODES = dict(mode_gram="bf16", mode_solve="x3", mode_apply="bf16")
_PROMPT_STATE_MODE = "bf16"
_SAMPLE_MODES = dict(mode_gram="bf16", mode_solve="bf16", mode_apply="bf16")
_SAMPLE_STATE_MODE = "bf16"


def _delta_prompt_kernel(qkv_ref, gb_ref, s0_ref, o_ref, sn_ref, s_scr):
    t = pl.program_id(1)
    tc = qkv_ref.shape[0]

    @pl.when(t == 0)
    def _():
        s_scr[...] = s0_ref[0]

    probs = _gdn_problems(qkv_ref[...], gb_ref[...], DN_CHUNK)
    a_mat, b_mat, q_t, o_intra, d_last = _gdn_intra(*probs, c=DN_CHUNK, small=False, **_PROMPT_MODES)
    s = s_scr[...]
    for ci in range(tc // DN_CHUNK):
        ps = slice(ci * DN_HEADS, (ci + 1) * DN_HEADS)
        o, s = _gdn_state_step(a_mat[ps], b_mat[ps], q_t[ps], o_intra[ps], d_last[ps], s,
                               _PROMPT_STATE_MODE)
        for h in range(DN_HEADS):
            o_ref[ci * DN_CHUNK:(ci + 1) * DN_CHUNK, h * DN_HEAD_DIM:(h + 1) * DN_HEAD_DIM] = (
                o[h].astype(o_ref.dtype))
    s_scr[...] = s

    @pl.when(t == pl.num_programs(1) - 1)
    def _():
        sn_ref[0] = s


def _delta_prompt(qkv, gb, s0, bsz, seq, tc):
    n = qkv.shape[0]
    nt = seq // tc
    rows = lambda w: pl.BlockSpec((tc, w), lambda b, t: (b * nt + t, 0))
    state = pl.BlockSpec((1, DN_HEADS, DN_HEAD_DIM, DN_HEAD_DIM), lambda b, t: (b, 0, 0, 0))
    return pl.pallas_call(
        _delta_prompt_kernel,
        grid=(bsz, nt),
        in_specs=[rows(3 * DN_WIDTH), rows(LANES), state],
        out_specs=[rows(DN_WIDTH), state],
        out_shape=[
            jax.ShapeDtypeStruct((n, DN_WIDTH), BF16),
            jax.ShapeDtypeStruct((bsz, DN_HEADS, DN_HEAD_DIM, DN_HEAD_DIM), F32),
        ],
        scratch_shapes=[pltpu.VMEM((DN_HEADS, DN_HEAD_DIM, DN_HEAD_DIM), F32)],
        compiler_params=_cparams("arbitrary", "arbitrary"),
        name="delta_prompt",
    )(qkv, gb, s0)


def _delta_sample_kernel(qkv_ref, gb_ref, s0_ref, *rest):
    o_ref, sn_ref = rest[-2:]
    bt = qkv_ref.shape[0]
    c = SAMPLE_CHUNK
    probs = _gdn_problems(qkv_ref[...].reshape(bt * c, 3 * DN_WIDTH), gb_ref[...].reshape(bt * c, LANES), c)
    a_mat, b_mat, q_t, o_intra, d_last = _gdn_intra(*probs, c=c, small=True, **_SAMPLE_MODES)
    s = s0_ref[0].reshape(bt * DN_HEADS, DN_HEAD_DIM, DN_HEAD_DIM)
    o, s = _gdn_state_step(a_mat, b_mat, q_t, o_intra, d_last, s, _SAMPLE_STATE_MODE)
    for b in range(bt):
        for h in range(DN_HEADS):
            o_ref[b, :, h * DN_HEAD_DIM:(h + 1) * DN_HEAD_DIM] = o[b * DN_HEADS + h]
    sn_ref[0] = s.reshape(bt, DN_HEADS, DN_HEAD_DIM, DN_HEAD_DIM)


def _delta_sample(qkv_b, gb_b, state_all, layer, carried, bt):
    bsz = qkv_b.shape[0]
    blk = lambda w: pl.BlockSpec((bt, SAMPLE_CHUNK, w), lambda i: (i, 0, 0))
    state = pl.BlockSpec((1, bt, DN_HEADS, DN_HEAD_DIM, DN_HEAD_DIM), lambda i: (layer, i, 0, 0, 0))
    in_specs, args, aliases = [blk(3 * DN_WIDTH), blk(LANES), state], [qkv_b, gb_b, state_all], {}
    if carried is not None:
        in_specs.append(pl.BlockSpec(memory_space=pl.ANY))
        args.append(carried)
        aliases = {3: 1}
    return pl.pallas_call(
        _delta_sample_kernel,
        grid=(bsz // bt,),
        in_specs=in_specs,
        out_specs=[blk(DN_WIDTH), state],
        out_shape=[
            jax.ShapeDtypeStruct((bsz, SAMPLE_CHUNK, DN_WIDTH), F32),
            jax.ShapeDtypeStruct(state_all.shape, F32),
        ],
        input_output_aliases=aliases,
        compiler_params=_cparams("arbitrary"),
        name="delta_sample",
    )(*args)


def _outproj_kernel(yc_ref, o_ref, z_ref, x_ref, g1_ref, nw_ref, w_ref, lg_ref, lb_ref, out_ref):
    o = o_ref[...].astype(F32)
    parts = []
    for h in range(DN_HEADS):
        oh = o[:, h * DN_HEAD_DIM:(h + 1) * DN_HEAD_DIM]
        parts.append(oh * lax.rsqrt(jnp.mean(oh * oh, -1, keepdims=True) + RMS_EPS) * nw_ref[...])
    og = jnp.concatenate(parts, axis=-1) * _silu(z_ref[...].astype(F32))
    mixed = jnp.concatenate([yc_ref[...].astype(BF16), og.astype(BF16)], axis=-1)
    m = jnp.dot(mixed, w_ref[...], preferred_element_type=F32)
    r = DEEPNORM_ALPHA * x_ref[...] + (1.0 + g1_ref[0]) * m
    out_ref[...] = _layer_norm(r, lg_ref[...], lb_ref[...])


def _outproj(yc, o, z_src, z_blk, x, mod3, tpg, norm_w, w_out, ln_g, ln_b, tm):
    n = x.shape[0]
    vec = lambda w: pl.BlockSpec((1, w), lambda i: (0, 0))
    return pl.pallas_call(
        _outproj_kernel,
        grid=(n // tm,),
        in_specs=[
            pl.BlockSpec((tm, CONV_WIDTH), lambda i: (i, 0)),
            pl.BlockSpec((tm, DN_WIDTH), lambda i: (i, 0)),
            pl.BlockSpec((tm, DN_WIDTH), lambda i: (i, z_blk)),
            pl.BlockSpec((tm, D_MODEL), lambda i: (i, 0)),
            _mod_spec(mod3, 2, tpg),
            vec(DN_HEAD_DIM),
            pl.BlockSpec((D_MODEL, D_MODEL), lambda i: (0, 0)),
            vec(D_MODEL), vec(D_MODEL),
        ],
        out_specs=pl.BlockSpec((tm, D_MODEL), lambda i: (i, 0)),
        out_shape=jax.ShapeDtypeStruct((n, D_MODEL), F32),
        compiler_params=_cparams("arbitrary"),
        name="outproj",
    )(yc, o, z_src, x, mod3, norm_w, w_out, ln_g, ln_b)


R_E1, R_E2, R_G1, R_G2, R_R1, R_R2 = range(6)


def _router_kernel(x_ref, sc_ref, sh_ref, wr_ref, br_ref, route_ref, route_t_ref, cnt_ref):
    tm = x_ref.shape[0]
    u = x_ref[...] * (1.0 + sc_ref[0]) + sh_ref[0]
    logits = _mm(u, wr_ref[...], mode="x3") + br_ref[...]
    lane = lax.broadcasted_iota(jnp.int32, logits.shape, 1)
    valid = lane < N_EXPERTS
    logits = jnp.where(valid, logits, -jnp.inf)
    ex = jnp.exp(logits - jnp.max(logits, -1, keepdims=True))
    probs = jnp.where(valid, ex / jnp.sum(ex, -1, keepdims=True), -1.0)
    p1 = jnp.max(probs, -1, keepdims=True)
    i1 = jnp.min(jnp.where(probs == p1, lane, LANES), -1, keepdims=True)
    rest = jnp.where(lane == i1, -1.0, probs)
    p2 = jnp.max(rest, -1, keepdims=True)
    i2 = jnp.min(jnp.where(rest == p2, lane, LANES), -1, keepdims=True)
    tot = p1 + p2
    sel = jnp.where(lane == i1, 1.0, 0.0) + jnp.where(lane == i2, 1.0, 0.0)
    incl = _cumsum_chunks(sel, tm)
    excl = incl - sel
    r1 = jnp.sum(jnp.where(lane == i1, excl, 0.0), -1, keepdims=True)
    r2 = jnp.sum(jnp.where(lane == i2, excl, 0.0), -1, keepdims=True)
    cnt_ref[0] = incl[tm - 1:tm, :]
    route = jnp.zeros_like(logits)
    for j, val in enumerate((i1.astype(F32), i2.astype(F32), p1 / tot, p2 / tot, r1, r2)):
        route = jnp.where(lane == j, val, route)
    route_ref[...] = route
    route_t_ref[0] = jnp.transpose(route)[:SUBLANES, :]


def _router(x, mod3, tpg, w_router, b_router, tm):
    n = x.shape[0]
    return pl.pallas_call(
        _router_kernel,
        grid=(n // tm,),
        in_specs=[
            pl.BlockSpec((tm, D_MODEL), lambda i: (i, 0)),
            _mod_spec(mod3, 4, tpg),
            _mod_spec(mod3, 3, tpg),
            pl.BlockSpec((D_MODEL, LANES), lambda i: (0, 0)),
            pl.BlockSpec((1, LANES), lambda i: (0, 0)),
        ],
        out_specs=[pl.BlockSpec((tm, LANES), lambda i: (i, 0)),
                   pl.BlockSpec((1, SUBLANES, tm), lambda i: (i, 0, 0)),
                   pl.BlockSpec((1, 1, LANES), lambda i: (i, 0, 0))],
        out_shape=[jax.ShapeDtypeStruct((n, LANES), F32),
                   jax.ShapeDtypeStruct((n // tm, SUBLANES, tm), F32),
                   jax.ShapeDtypeStruct((n // tm, 1, LANES), F32)],
        compiler_params=_cparams("arbitrary"),
        name="router",
    )(x, mod3, mod3, w_router, b_router)


def _dispatch_plan(route_t, counts, tme):
    n_rt, _, tm = route_t.shape
    n = n_rt * tm
    cnt = counts[:, 0, :N_EXPERTS].astype(jnp.int32)
    tile_off = jnp.cumsum(cnt, axis=0) - cnt
    tot = jnp.sum(cnt, axis=0)
    gsz = (tot + tme - 1) // tme * tme
    gend = jnp.cumsum(gsz)
    goff = gend - gsz
    base = (goff[None, :] + tile_off)[:, :, None]
    ids = jnp.arange(N_EXPERTS, dtype=jnp.int32)[None, :, None]
    field = lambda j: route_t[:, j:j + 1, :].astype(jnp.int32)
    pick = lambda e: jnp.sum(jnp.where(ids == e, base, 0), axis=1, keepdims=True)
    slot1 = (pick(field(R_E1)) + field(R_R1)).reshape(n)
    slot2 = (pick(field(R_E2)) + field(R_R2)).reshape(n)
    n_tiles = 2 * n // tme + N_EXPERTS
    start = jnp.arange(n_tiles, dtype=jnp.int32) * tme
    valid = (start < gend[-1]).astype(jnp.int32)
    blk = jnp.minimum(jnp.arange(n_tiles, dtype=jnp.int32), gend[-1] // tme - 1)
    tile_e = jnp.sum((blk * tme)[:, None] >= gend[None, :], axis=1).astype(jnp.int32)
    zoff = jnp.where(gsz > 0, gend - tme, (n_tiles + jnp.arange(N_EXPERTS, dtype=jnp.int32)) * tme)
    return slot1, slot2, zoff, tile_e, valid, blk, n_tiles


def _dispatch_kernel(s1_ref, s2_ref, zoff_ref, x_ref, sc_ref, sh_ref, ug_hbm, ubuf, zbuf, sem):
    i = pl.program_id(0)
    tm = x_ref.shape[0]
    tme = zbuf.shape[0]

    @pl.when(i == 0)
    def _():
        zbuf[...] = jnp.zeros_like(zbuf)
        fill = lambda e: pltpu.make_async_copy(
            zbuf, ug_hbm.at[pl.ds(pl.multiple_of(zoff_ref[e], 8), tme)], sem.at[2])
        for e in range(N_EXPERTS):
            fill(e).start()
        for e in range(N_EXPERTS):
            fill(e).wait()

    ubuf[...] = (x_ref[...] * (1.0 + sc_ref[0]) + sh_ref[0]).reshape(ubuf.shape)
    base = i * tm

    def issue(j, carry):
        for k in range(SUBLANES):
            row = ubuf.at[j, pl.ds(k, 1)]
            t = base + j * SUBLANES + k
            pltpu.make_async_copy(row, ug_hbm.at[pl.ds(s1_ref[t], 1)], sem.at[0]).start()
            pltpu.make_async_copy(row, ug_hbm.at[pl.ds(s2_ref[t], 1)], sem.at[1]).start(priority=1)
        return carry

    lax.fori_loop(0, tm // SUBLANES, issue, 0)
    pltpu.make_async_copy(ubuf, ubuf, sem.at[0]).wait()
    pltpu.make_async_copy(ubuf, ubuf, sem.at[1]).wait()


def _dispatch(x, mod3, tpg, slot1, slot2, zoff, tm, tme, rows):
    n = x.shape[0]
    return pl.pallas_call(
        _dispatch_kernel,
        grid_spec=pltpu.PrefetchScalarGridSpec(
            num_scalar_prefetch=3,
            grid=(n // tm,),
            in_specs=[
                pl.BlockSpec((tm, D_MODEL), lambda i, *_: (i, 0)),
                _mod_spec(mod3, 4, tpg),
                _mod_spec(mod3, 3, tpg),
            ],
            out_specs=pl.BlockSpec(memory_space=pl.ANY),
            scratch_shapes=[pltpu.VMEM((tm // SUBLANES, SUBLANES, D_MODEL), F32),
                            pltpu.VMEM((tme, D_MODEL), F32), pltpu.SemaphoreType.DMA((3,))],
        ),
        out_shape=jax.ShapeDtypeStruct((rows, D_MODEL), F32),
        compiler_params=_cparams("arbitrary"),
        name="dispatch",
    )(slot1, slot2, zoff, x, mod3, mod3)


def _expert_kernel(te_ref, tv_ref, tb_ref, ug_ref, wg_ref, wu_ref, wd_ref, ys_ref):
    i = pl.program_id(0)

    @pl.when(tv_ref[i] == 1)
    def _():
        u = ug_ref[...].astype(BF16)
        h = _silu(jnp.dot(u, wg_ref[0].astype(BF16), preferred_element_type=F32)) * jnp.dot(
            u, wu_ref[0].astype(BF16), preferred_element_type=F32)
        ys_ref[...] = jnp.dot(h.astype(BF16), wd_ref[0].astype(BF16), preferred_element_type=F32)

    @pl.when(tv_ref[i] == 0)
    def _():
        ys_ref[...] = jnp.zeros_like(ys_ref)


def _experts(ug, tile_e, valid, blk, w_gate, w_up, w_down, tme, n_tiles):
    d_e = w_gate.shape[-1]
    return pl.pallas_call(
        _expert_kernel,
        grid_spec=pltpu.PrefetchScalarGridSpec(
            num_scalar_prefetch=3,
            grid=(n_tiles,),
            in_specs=[
                pl.BlockSpec((tme, D_MODEL), lambda i, te, tv, tb: (tb[i], 0)),
                pl.BlockSpec((1, D_MODEL, d_e), lambda i, te, tv, tb: (te[i], 0, 0)),
                pl.BlockSpec((1, D_MODEL, d_e), lambda i, te, tv, tb: (te[i], 0, 0)),
                pl.BlockSpec((1, d_e, D_MODEL), lambda i, te, tv, tb: (te[i], 0, 0)),
            ],
            out_specs=pl.BlockSpec((tme, D_MODEL), lambda i, te, tv, tb: (i, 0)),
        ),
        out_shape=jax.ShapeDtypeStruct((n_tiles * tme, D_MODEL), F32),
        compiler_params=_cparams("arbitrary"),
        name="experts",
    )(tile_e, valid, blk, ug, w_gate, w_up, w_down)


def _combine_kernel(s1_ref, s2_ref, ys_hbm, route_ref, x_ref, g2_ref, lg_ref, lb_ref, out_ref,
                    y1buf, y2buf, sem):
    i = pl.program_id(0)
    tm = x_ref.shape[0]
    base = i * tm

    def issue(j, carry):
        for k in range(SUBLANES):
            t = base + j * SUBLANES + k
            pltpu.make_async_copy(ys_hbm.at[pl.ds(s1_ref[t], 1)], y1buf.at[j, pl.ds(k, 1)], sem.at[0]).start()
            pltpu.make_async_copy(ys_hbm.at[pl.ds(s2_ref[t], 1)], y2buf.at[j, pl.ds(k, 1)],
                                  sem.at[1]).start(priority=1)
        return carry

    lax.fori_loop(0, tm // SUBLANES, issue, 0)
    pltpu.make_async_copy(y1buf, y1buf, sem.at[0]).wait()
    pltpu.make_async_copy(y2buf, y2buf, sem.at[1]).wait()
    route = route_ref[...]
    y1 = y1buf[...].reshape(tm, D_MODEL)
    y2 = y2buf[...].reshape(tm, D_MODEL)
    f = route[:, R_G1:R_G1 + 1] * y1 + route[:, R_G2:R_G2 + 1] * y2
    r = DEEPNORM_ALPHA * x_ref[...] + (1.0 + g2_ref[0]) * f
    out_ref[...] = _layer_norm(r, lg_ref[...], lb_ref[...])


def _combine(ys, slot1, slot2, route, x, mod3, tpg, ln_g, ln_b, tm):
    n = x.shape[0]
    vec = lambda w: pl.BlockSpec((1, w), lambda i, *_: (0, 0))
    return pl.pallas_call(
        _combine_kernel,
        grid_spec=pltpu.PrefetchScalarGridSpec(
            num_scalar_prefetch=2,
            grid=(n // tm,),
            in_specs=[
                pl.BlockSpec(memory_space=pl.ANY),
                pl.BlockSpec((tm, LANES), lambda i, *_: (i, 0)),
                pl.BlockSpec((tm, D_MODEL), lambda i, *_: (i, 0)),
                _mod_spec(mod3, 5, tpg),
                vec(D_MODEL), vec(D_MODEL),
            ],
            out_specs=pl.BlockSpec((tm, D_MODEL), lambda i, *_: (i, 0)),
            scratch_shapes=[pltpu.VMEM((tm // SUBLANES, SUBLANES, D_MODEL), F32),
                            pltpu.VMEM((tm // SUBLANES, SUBLANES, D_MODEL), F32),
                            pltpu.SemaphoreType.DMA((2,))],
        ),
        out_shape=jax.ShapeDtypeStruct((n, D_MODEL), F32),
        compiler_params=_cparams("arbitrary"),
        name="combine",
    )(slot1, slot2, ys, route, x, mod3, ln_g, ln_b)


def _moe(x, mod3, tpg, w_router, b_router, w_gate, w_up, w_down, ln_g, ln_b, tm, tme):
    route, route_t, counts = _router(x, mod3, tpg, w_router, b_router, tm)
    slot1, slot2, zoff, tile_e, valid, blk, n_tiles = _dispatch_plan(route_t, counts, tme)
    ug = _dispatch(x, mod3, tpg, slot1, slot2, zoff, tm, tme, (n_tiles + N_EXPERTS) * tme)
    ys = _experts(ug, tile_e, valid, blk, w_gate, w_up, w_down, tme, n_tiles)
    return _combine(ys, slot1, slot2, route, x, mod3, tpg, ln_g, ln_b, tm)


def _ffn_kernel(x_ref, sc_ref, sh_ref, g2_ref, wg_ref, wu_ref, wd_ref, lg_ref, lb_ref, out_ref, *, tf):
    x = x_ref[...]
    u = (x * (1.0 + sc_ref[0]) + sh_ref[0]).astype(BF16)
    y = None
    for f0 in range(0, wg_ref.shape[1], tf):
        h = _silu(jnp.dot(u, wg_ref[:, f0:f0 + tf], preferred_element_type=F32)) * jnp.dot(
            u, wu_ref[:, f0:f0 + tf], preferred_element_type=F32)
        part = jnp.dot(h.astype(BF16), wd_ref[f0:f0 + tf, :], preferred_element_type=F32)
        y = part if y is None else y + part
    r = DEEPNORM_ALPHA * x + (1.0 + g2_ref[0]) * y
    out_ref[...] = _layer_norm(r, lg_ref[...], lb_ref[...])


def _ffn(x, mod3, tpg, w_gate, w_up, w_down, ln_g, ln_b, tm, tf):
    n = x.shape[0]
    vec = lambda w: pl.BlockSpec((1, w), lambda i: (0, 0))
    resident = lambda a: pl.BlockSpec(a.shape, lambda i: (0, 0), pipeline_mode=pl.Buffered(1))
    return pl.pallas_call(
        functools.partial(_ffn_kernel, tf=tf),
        grid=(n // tm,),
        in_specs=[
            pl.BlockSpec((tm, D_MODEL), lambda i: (i, 0)),
            _mod_spec(mod3, 4, tpg),
            _mod_spec(mod3, 3, tpg),
            _mod_spec(mod3, 5, tpg),
            resident(w_gate), resident(w_up), resident(w_down),
            vec(D_MODEL), vec(D_MODEL),
        ],
        out_specs=pl.BlockSpec((tm, D_MODEL), lambda i: (i, 0)),
        out_shape=jax.ShapeDtypeStruct((n, D_MODEL), F32),
        compiler_params=_cparams("arbitrary"),
        name="ffn_dense",
    )(x, mod3, mod3, mod3, w_gate, w_up, w_down, ln_g, ln_b)


def _trunk(x, mod3s, rows_per_group, prm, tiles, mixer_fn):
    tm, tme = tiles["tm"], tiles["tme"]
    tpg = rows_per_group // tm
    states = []
    for l in range(DEPTH):
        mod3 = mod3s[l]
        yc, o, z_src, z_blk, st = mixer_fn(l, x, mod3)
        states.append(st)
        x = _outproj(yc, o, z_src, z_blk, x, mod3, tpg, prm["norm_w"][l], prm["w_out"][l],
                     prm["ln_g"][l, 0:1], prm["ln_b"][l, 0:1], tm)
        j = l // 2
        if l % 2 == 0:
            x = _ffn(x, mod3, rows_per_group // tiles["tm_ffn"], prm["w_ff_gate"][j], prm["w_ff_up"][j],
                     prm["w_ff_down"][j], prm["ln_g"][l, 1:2], prm["ln_b"][l, 1:2],
                     tiles["tm_ffn"], tiles["tf"])
        else:
            x = _moe(x, mod3, tpg, prm["w_router"][j], prm["b_router"][j], prm["w_exp_gate"][j],
                     prm["w_exp_up"][j], prm["w_exp_down"][j], prm["ln_g"][l, 1:2], prm["ln_b"][l, 1:2],
                     tm, tme)
    return x, states


def kernel(x_prompt, x_sample, state_conv, state_dn_conv, state_dn, c_prompt, c_sample, w_in, conv_w,
           dn_conv_w, a_log, dt_bias, dn_norm_w, w_out, w_ada, b_ada, ln_g, ln_b, w_ff_gate, w_ff_up,
           w_ff_down, w_router, b_router, w_exp_gate, w_exp_up, w_exp_down):
    bp, seq, d = x_prompt.shape
    bs, steps, _ = x_sample.shape
    pad_lanes = LANES - N_EXPERTS
    hp = jnp.zeros((DEPTH, 2, LANES), F32)
    hp = hp.at[:, 0, G_LANE:].set(a_log).at[:, 1, G_LANE:].set(dt_bias)
    w_in_t = jnp.swapaxes(w_in, 1, 2)
    prm = {
        "w_main": w_in_t[:, :P_MAIN].astype(BF16),
        "w_small": w_in_t[:, w_in_t.shape[1] - LANES:],
        "norm_w": dn_norm_w.reshape(DEPTH, 1, DN_HEAD_DIM),
        "w_out": w_out.astype(BF16),
        "ln_g": ln_g, "ln_b": ln_b,
        "w_ff_gate": w_ff_gate.astype(BF16), "w_ff_up": w_ff_up.astype(BF16),
        "w_ff_down": w_ff_down.astype(BF16),
        "w_router": jnp.pad(w_router, ((0, 0), (0, 0), (0, pad_lanes))),
        "b_router": jnp.pad(b_router, ((0, 0), (0, pad_lanes)))[:, None, :],
        "w_exp_gate": w_exp_gate, "w_exp_up": w_exp_up, "w_exp_down": w_exp_down,
    }

    mod = _ada(jnp.concatenate([c_prompt, c_sample], axis=0), w_ada, b_ada)
    mod_p = [mod[l, :bp].reshape(bp, 1, 6 * d) for l in range(DEPTH)]
    mod_s = [mod[l, bp:].reshape(1, bs, 6 * d) for l in range(DEPTH)]

    tm_p, tc_delta = 512, 512
    zc = jnp.zeros((bp, CONV_K - 1, CONV_WIDTH), F32)
    zd = jnp.zeros((bp, DN_CONV_K - 1, 3 * DN_WIDTH), F32)
    zs = jnp.zeros((bp, DN_HEADS, DN_HEAD_DIM, DN_HEAD_DIM), F32)

    def mixer_prompt(l, x, mod3):
        yc, qkv, z, gb, ncb, ndb = _inproj_pre_prompt(
            x, mod3, prm["w_main"][l], prm["w_small"][l], conv_w[l], dn_conv_w[l], hp[l], zc, zd,
            bp, seq, tm_p)
        o, s_new = _delta_prompt(qkv, gb, zs, bp, seq, tc_delta)
        return yc, o, z, 0, (ncb, ndb, s_new)

    y_p, st_p = _trunk(x_prompt.reshape(bp * seq, d), mod_p, seq, prm,
                       dict(tm=tm_p, tm_ffn=512, tf=256, tme=512), mixer_prompt)

    bt_pre, bt_delta = 32, 16
    dn_states = []

    def mixer_sample(l, x, mod3):
        pm, ps = _inproj(x, mod3, steps, prm["w_main"][l], prm["w_small"][l], bs)
        cb_tm = jnp.transpose(state_conv[l], (1, 0, 2))
        db_tm = jnp.transpose(state_dn_conv[l], (1, 0, 2))
        yc, qkv, gb, ncb, ndb = _pre_sample(pm, ps, conv_w[l], dn_conv_w[l], hp[l], cb_tm, db_tm,
                                            bs, steps, bt_pre)
        to_b = lambda a: jnp.pad(jnp.transpose(a, (1, 0, 2)), ((0, 0), (0, SAMPLE_CHUNK - steps), (0, 0)))
        o_b, s_all = _delta_sample(to_b(qkv), to_b(gb), state_dn, l, dn_states[-1] if dn_states else None,
                                   bt_delta)
        dn_states.append(s_all)
        o = jnp.transpose(o_b[:, :steps], (1, 0, 2)).reshape(steps * bs, DN_WIDTH)
        return (yc.reshape(steps * bs, CONV_WIDTH), o, pm, OFF_Z // DN_WIDTH,
                (jnp.transpose(ncb, (1, 0, 2)), jnp.transpose(ndb, (1, 0, 2))))

    x_s = jnp.transpose(x_sample, (1, 0, 2)).reshape(steps * bs, d)
    y_s, st_s = _trunk(x_s, mod_s, steps * bs, prm,
                       dict(tm=bs, tm_ffn=bs, tf=w_ff_gate.shape[-1],tme=128), mixer_sample)
    y_s = jnp.transpose(y_s.reshape(steps, bs, d), (1, 0, 2))

    stack = lambda sts, i: jnp.stack([s[i] for s in sts])
    return (y_p.reshape(bp, seq, d), y_s,
            stack(st_p, 0), stack(st_p, 1), stack(st_p, 2),
            stack(st_s, 0), stack(st_s, 1), dn_states[-1])
```

```python
import functools
import math

import jax
import jax.numpy as jnp
from jax import lax
from jax.experimental import pallas as pl
from jax.experimental.pallas import tpu as pltpu

F32 = jnp.float32
BF16 = jnp.bfloat16

D_MODEL = 1024
DEPTH = 2
CONV_WIDTH = 512
CONV_K = 3
DN_HEADS = 4
DN_HEAD_DIM = 128
DN_WIDTH = DN_HEADS * DN_HEAD_DIM
DN_CONV_K = 4
DN_CHUNK = 64
N_EXPERTS = 8
DEEPNORM_ALPHA = (2.0 * DEPTH) ** 0.25
LN_EPS = 1e-5
RMS_EPS = 1e-6

OFF_CC = CONV_WIDTH
OFF_CH = 2 * CONV_WIDTH
OFF_QKV = 3 * CONV_WIDTH
OFF_Z = OFF_QKV + 3 * DN_WIDTH
P_MAIN = OFF_Z + DN_WIDTH
LANES = 128
SUBLANES = 8
BETA_LANE = LANES - 2 * DN_HEADS
G_LANE = LANES - DN_HEADS
SAMPLE_CHUNK = 8

VMEM_LIMIT = 56 * 1024 * 1024


def _cparams(*sem):
    return pltpu.CompilerParams(dimension_semantics=sem, vmem_limit_bytes=VMEM_LIMIT)


def _silu(x):
    return x * jax.nn.sigmoid(x)


def _layer_norm(r, g, b):
    mu = jnp.mean(r, -1, keepdims=True)
    xc = r - mu
    var = jnp.mean(xc * xc, -1, keepdims=True)
    return xc * lax.rsqrt(var + LN_EPS) * g + b


def _split_bf16(a):
    hi = a.astype(BF16)
    lo = (a - hi.astype(F32)).astype(BF16)
    return hi, lo


_NN = (((1,), (0,)), ((), ()))
_NT = (((1,), (1,)), ((), ()))
_BNN = (((2,), (1,)), ((0,), (0,)))
_BNT = (((2,), (2,)), ((0,), (0,)))
_BTN = (((1,), (1,)), ((0,), (0,)))


def _mm(a, b, dims=_NN, mode="bf16"):
    if mode == "f32":
        return lax.dot_general(a, b, dims, precision=lax.Precision.HIGHEST, preferred_element_type=F32)
    if mode == "bf16":
        return lax.dot_general(a.astype(BF16), b.astype(BF16), dims, preferred_element_type=F32)
    ah, al = _split_bf16(a)
    bh, bl = _split_bf16(b)
    d = functools.partial(lax.dot_general, dimension_numbers=dims, preferred_element_type=F32)
    return d(ah, bh) + (d(ah, bl) + d(al, bh))


def _ada_kernel(c_ref, w_ref, b_ref, o_ref):
    s = _silu(c_ref[...]).astype(BF16)
    o_ref[0] = jnp.dot(s, w_ref[0].astype(BF16), preferred_element_type=F32) + b_ref[0]


def _ada(c_all, w_ada, b_ada):
    rows = c_all.shape[0]
    tn = 1536
    return pl.pallas_call(
        _ada_kernel,
        grid=(DEPTH, 6 * D_MODEL // tn),
        in_specs=[
            pl.BlockSpec((rows, D_MODEL), lambda l, j: (0, 0)),
            pl.BlockSpec((1, D_MODEL, tn), lambda l, j: (l, 0, j)),
            pl.BlockSpec((1, 1, tn), lambda l, j: (l, 0, j)),
        ],
        out_specs=pl.BlockSpec((1, rows, tn), lambda l, j: (l, 0, j)),
        out_shape=jax.ShapeDtypeStruct((DEPTH, rows, 6 * D_MODEL), F32),
        compiler_params=_cparams("arbitrary", "arbitrary"),
        name="ada",
    )(c_all, w_ada, b_ada.reshape(DEPTH, 1, 6 * D_MODEL))


def _mod_spec(mod3, chunk, tiles_per_group):
    r = mod3.shape[1]
    return pl.BlockSpec((1, r, D_MODEL), lambda i, *_: (i // tiles_per_group, 0, chunk))


def _inproj_kernel(x_ref, sc_ref, sh_ref, wm_ref, ws_ref, pm_ref, ps_ref):
    u = x_ref[...] * (1.0 + sc_ref[0]) + sh_ref[0]
    pm_ref[...] = lax.dot_general(u.astype(BF16), wm_ref[...], _NT, preferred_element_type=F32)
    ps_ref[...] = _mm(u, ws_ref[...], _NT, mode="x3")


def _inproj(x, mod3, tpg, w_main, w_small, tm):
    n = x.shape[0]
    return pl.pallas_call(
        _inproj_kernel,
        grid=(n // tm,),
        in_specs=[
            pl.BlockSpec((tm, D_MODEL), lambda i: (i, 0)),
            _mod_spec(mod3, 1, tpg),
            _mod_spec(mod3, 0, tpg),
            pl.BlockSpec((P_MAIN, D_MODEL), lambda i: (0, 0)),
            pl.BlockSpec((LANES, D_MODEL), lambda i: (0, 0)),
        ],
        out_specs=[
            pl.BlockSpec((tm, P_MAIN), lambda i: (i, 0)),
            pl.BlockSpec((tm, LANES), lambda i: (i, 0)),
        ],
        out_shape=[
            jax.ShapeDtypeStruct((n, P_MAIN), F32),
            jax.ShapeDtypeStruct((n, LANES), F32),
        ],
        compiler_params=_cparams("arbitrary"),
        name="inproj",
    )(x, mod3, mod3, w_main, w_small)


def _gates(ps, hp):
    lane = lax.broadcasted_iota(jnp.int32, ps.shape, 1)
    beta = jax.nn.sigmoid(ps)
    g = -jnp.exp(hp[0:1, :]) * jax.nn.softplus(ps + hp[1:2, :])
    return jnp.where(lane >= G_LANE, g, jnp.where(lane >= BETA_LANE, beta, 0.0))


def _qkv_finish(y, out_ref, idx, first_head=0):
    y = _silu(y)
    for j in range(y.shape[1] // DN_HEAD_DIM):
        h = first_head + j
        seg = y[:, j * DN_HEAD_DIM:(j + 1) * DN_HEAD_DIM]
        if h < 2 * DN_HEADS:
            seg = seg * lax.rsqrt(jnp.sum(seg * seg, -1, keepdims=True) + RMS_EPS)
            if h < DN_HEADS:
                seg = seg * (DN_HEAD_DIM ** -0.5)
        out_ref[idx + (slice(None), slice(h * DN_HEAD_DIM, (h + 1) * DN_HEAD_DIM))] = seg


def _shifted(x, tail_ref, i, row):
    s = pltpu.roll(x, i, 0)
    top = s[0:SUBLANES]
    for r in range(i):
        top = jnp.where(row == r, tail_ref[8 - i + r:8 - i + r + 1, :], top)
    return jnp.concatenate([top, s[SUBLANES:]], axis=0)


def _inproj_pre_prompt_kernel(x_ref, sc_ref, sh_ref, wm_ref, ws_ref, cw_ref, dw_ref, hp_ref, cb_ref, db_ref,
                              yc_ref, qkv_ref, z_ref, gb_ref, ncb_ref, ndb_ref, tc_scr, td_scr):
    t = pl.program_id(1)
    tm = x_ref.shape[0]

    @pl.when(t == 0)
    def _():
        tc_scr[8 - (CONV_K - 1):8, :] = cb_ref[0]
        td_scr[8 - (DN_CONV_K - 1):8, :] = db_ref[0]

    u_f32 = x_ref[...] * (1.0 + sc_ref[0]) + sh_ref[0]
    u = u_f32.astype(BF16)
    proj = lambda lo, hi: lax.dot_general(u, wm_ref[lo:hi, :], _NT, preferred_element_type=F32)

    row = lax.broadcasted_iota(jnp.int32, (SUBLANES, DN_WIDTH), 0)

    def dn_part(xp, part):
        cols = slice(part * DN_WIDTH, (part + 1) * DN_WIDTH)
        tail = td_scr.at[:, cols]
        yp = _shifted(xp, tail, 3, row) * dw_ref[0:1, cols]
        yp = yp + _shifted(xp, tail, 2, row) * dw_ref[1:2, cols]
        yp = yp + _shifted(xp, tail, 1, row) * dw_ref[2:3, cols]
        yp = yp + xp * dw_ref[3:4, cols]
        _qkv_finish(yp, qkv_ref, (), first_head=part * DN_HEADS)
        return xp[tm - (DN_CONV_K - 1):tm, :]

    qkv_cols = lambda part: (OFF_QKV + part * DN_WIDTH, OFF_QKV + (part + 1) * DN_WIDTH)
    x_q = proj(*qkv_cols(0))
    x_k = proj(*qkv_cols(1))
    last_q = dn_part(x_q, 0)
    x_v = proj(*qkv_cols(2))
    last_k = dn_part(x_k, 1)
    p_c = proj(OFF_CC, OFF_CH)
    last_v = dn_part(x_v, 2)
    p_h = proj(OFF_CH, OFF_QKV)
    cgh = p_c * p_h
    p_b = proj(0, OFF_CC)
    y = _shifted(cgh, tc_scr, 2, row) * cw_ref[0:1, :]
    y = y + _shifted(cgh, tc_scr, 1, row) * cw_ref[1:2, :]
    y = y + cgh * cw_ref[2:3, :]
    p_z = proj(OFF_Z, P_MAIN)
    yc_ref[...] = (p_b * y).astype(yc_ref.dtype)
    last_c = cgh[tm - (CONV_K - 1):tm, :]
    small = _mm(u_f32, ws_ref[...], _NT, mode="bf16")
    z_ref[...] = p_z.astype(z_ref.dtype)
    gb_ref[...] = _gates(small, hp_ref[...])

    last_d = jnp.concatenate([last_q, last_k, last_v], axis=-1)
    tc_scr[8 - (CONV_K - 1):8, :] = last_c
    td_scr[8 - (DN_CONV_K - 1):8, :] = last_d

    @pl.when(t == pl.num_programs(1) - 1)
    def _():
        ncb_ref[0] = last_c
        ndb_ref[0] = last_d


def _inproj_pre_prompt(x, mod3, w_main, w_small, conv_w, dn_conv_w, hp, conv_buf, dn_buf, bsz, seq, tm):
    n = x.shape[0]
    nt = seq // tm
    rows = lambda w: pl.BlockSpec((tm, w), lambda b, t: (b * nt + t, 0))
    full = lambda a: pl.BlockSpec(a.shape, lambda b, t: (0,) * a.ndim)
    per_b = lambda k, w: pl.BlockSpec((1, k, w), lambda b, t: (b, 0, 0))
    mod = lambda chunk: pl.BlockSpec((1, 1, D_MODEL), lambda b, t: (b, 0, chunk))
    return pl.pallas_call(
        _inproj_pre_prompt_kernel,
        grid=(bsz, nt),
        in_specs=[rows(D_MODEL), mod(1), mod(0), full(w_main), full(w_small),
                  full(conv_w), full(dn_conv_w), full(hp),
                  per_b(CONV_K - 1, CONV_WIDTH), per_b(DN_CONV_K - 1, 3 * DN_WIDTH)],
        out_specs=[rows(CONV_WIDTH), rows(3 * DN_WIDTH), rows(DN_WIDTH), rows(LANES),
                   per_b(CONV_K - 1, CONV_WIDTH), per_b(DN_CONV_K - 1, 3 * DN_WIDTH)],
        out_shape=[
            jax.ShapeDtypeStruct((n, CONV_WIDTH), BF16),
            jax.ShapeDtypeStruct((n, 3 * DN_WIDTH), F32),
            jax.ShapeDtypeStruct((n, DN_WIDTH), BF16),
            jax.ShapeDtypeStruct((n, LANES), F32),
            jax.ShapeDtypeStruct((bsz, CONV_K - 1, CONV_WIDTH), F32),
            jax.ShapeDtypeStruct((bsz, DN_CONV_K - 1, 3 * DN_WIDTH), F32),
        ],
        scratch_shapes=[pltpu.VMEM((8, CONV_WIDTH), F32), pltpu.VMEM((8, 3 * DN_WIDTH), F32)],
        compiler_params=_cparams("arbitrary", "arbitrary"),
        name="inproj_pre_prompt",
    )(x, mod3, mod3, w_main, w_small, conv_w, dn_conv_w, hp, conv_buf, dn_buf)


def _pre_sample_kernel(pm_ref, ps_ref, cw_ref, dw_ref, hp_ref, cb_ref, db_ref,
                       yc_ref, qkv_ref, gb_ref, ncb_ref, ndb_ref):
    steps = pm_ref.shape[0]
    cgh = [pm_ref[t, :, OFF_CC:OFF_CH] * pm_ref[t, :, OFF_CH:OFF_QKV] for t in range(steps)]
    ext = [cb_ref[i] for i in range(CONV_K - 1)] + cgh
    for t in range(steps):
        y = ext[t] * cw_ref[0:1, :]
        for i in range(1, CONV_K):
            y = y + ext[t + i] * cw_ref[i:i + 1, :]
        yc_ref[t] = pm_ref[t, :, 0:OFF_CC] * y
    for i in range(CONV_K - 1):
        ncb_ref[i] = ext[len(ext) - (CONV_K - 1) + i]

    xq = [pm_ref[t, :, OFF_QKV:OFF_Z] for t in range(steps)]
    extq = [db_ref[i] for i in range(DN_CONV_K - 1)] + xq
    for t in range(steps):
        y = extq[t] * dw_ref[0:1, :]
        for i in range(1, DN_CONV_K):
            y = y + extq[t + i] * dw_ref[i:i + 1, :]
        _qkv_finish(y, qkv_ref, (t,))
        gb_ref[t] = _gates(ps_ref[t], hp_ref[...])
    for i in range(DN_CONV_K - 1):
        ndb_ref[i] = extq[len(extq) - (DN_CONV_K - 1) + i]


def _pre_sample(pm, ps, conv_w, dn_conv_w, hp, conv_buf_tm, dn_buf_tm, bsz, steps, bt):
    slab = lambda k, w: pl.BlockSpec((k, bt, w), lambda i: (0, i, 0))
    full = lambda a: pl.BlockSpec(a.shape, lambda i: (0,) * a.ndim)
    return pl.pallas_call(
        _pre_sample_kernel,
        grid=(bsz // bt,),
        in_specs=[slab(steps, P_MAIN), slab(steps, LANES), full(conv_w), full(dn_conv_w), full(hp),
                  slab(CONV_K - 1, CONV_WIDTH), slab(DN_CONV_K - 1, 3 * DN_WIDTH)],
        out_specs=[slab(steps, CONV_WIDTH), slab(steps, 3 * DN_WIDTH), slab(steps, LANES),
                   slab(CONV_K - 1, CONV_WIDTH), slab(DN_CONV_K - 1, 3 * DN_WIDTH)],
        out_shape=[
            jax.ShapeDtypeStruct((steps, bsz, CONV_WIDTH), F32),
            jax.ShapeDtypeStruct((steps, bsz, 3 * DN_WIDTH), F32),
            jax.ShapeDtypeStruct((steps, bsz, LANES), F32),
            jax.ShapeDtypeStruct((CONV_K - 1, bsz, CONV_WIDTH), F32),
            jax.ShapeDtypeStruct((DN_CONV_K - 1, bsz, 3 * DN_WIDTH), F32),
        ],
        compiler_params=_cparams("arbitrary"),
        name="pre_sample",
    )(pm.reshape(steps, bsz, P_MAIN), ps.reshape(steps, bsz, LANES), conv_w, dn_conv_w, hp,
      conv_buf_tm, dn_buf_tm)


def _cumsum_chunks(x, c):
    row = lax.broadcasted_iota(jnp.int32, x.shape, 0) & (c - 1)
    s = 1
    while s < c:
        x = x + jnp.where(row >= s, pltpu.roll(x, s, 0), 0.0)
        s *= 2
    return x


def _gdn_problems(qkv, gb, c):
    n = qkv.shape[0] // c
    gc = _cumsum_chunks(gb, c)
    q, k, v, g_b, beta_b = [], [], [], [], []
    for ci in range(n):
        rs = slice(ci * c, (ci + 1) * c)
        for h in range(DN_HEADS):
            ls = lambda base: slice(base + h * DN_HEAD_DIM, base + (h + 1) * DN_HEAD_DIM)
            q.append(qkv[rs, ls(0)])
            k.append(qkv[rs, ls(DN_WIDTH)])
            v.append(qkv[rs, ls(2 * DN_WIDTH)])
            g_b.append(jnp.broadcast_to(gc[rs, G_LANE + h:G_LANE + h + 1], (c, DN_HEAD_DIM)))
            beta_b.append(jnp.broadcast_to(gb[rs, BETA_LANE + h:BETA_LANE + h + 1], (c, DN_HEAD_DIM)))
    return tuple(jnp.stack(a) for a in (q, k, v, g_b, beta_b))


def _unit_lower_solve(m, rhs, c, small, mode):
    if small:
        sol = rhs
        for j in range(c - 1):
            sol = sol - m[:, :, j:j + 1] * sol[:, j:j + 1, :]
        return sol
    row = lax.broadcasted_iota(jnp.int32, m.shape, 1)
    col = lax.broadcasted_iota(jnp.int32, m.shape, 2)
    p = -m
    t = jnp.where(row == col, 1.0, 0.0) + p
    p = _mm(p, p, _BNN, "bf16")
    levels = int(math.log2(c))
    for lvl in range(1, levels):
        if lvl < levels - 1:
            y = _mm(jnp.concatenate([t, p], axis=1), p, _BNN, "bf16")
            t = t + y[:, :c]
            p = y[:, c:]
        else:
            t = t + _mm(t, p, _BNN, "bf16")
    x = _mm(t, rhs, _BNN, "bf16")
    resid = rhs - x - _mm(m, x, _BNN, mode)
    return x + _mm(t, resid, _BNN, "bf16")


def _gdn_intra(q, k, v, g_b, beta_b, *, c, small, mode_gram, mode_solve, mode_apply):
    n = q.shape[0]
    row = lax.broadcasted_iota(jnp.int32, (n, c, c), 1)
    col = lax.broadcasted_iota(jnp.int32, (n, c, c), 2)
    g_cc = g_b[:, :, :c]
    g_row = jnp.sum(jnp.where(row == col, g_cc, 0.0), axis=1, keepdims=True)
    gamma = jnp.exp(jnp.where(row >= col, g_cc - g_row, -jnp.inf))
    kbeta = k * beta_b
    exp_g = jnp.exp(g_b)
    gram = _mm(jnp.concatenate([kbeta, q], axis=1), k, _BNT, mode_gram)
    m = jnp.where(row > col, gram[:, :c] * gamma, 0.0)
    attn = gram[:, c:] * gamma
    rhs = jnp.concatenate([kbeta * exp_g, v * beta_b], axis=-1)
    sol = _unit_lower_solve(m, rhs, c, small, mode_solve)
    g_last = g_b[:, c - 1:c, :]
    k_dec = k * jnp.exp(g_last - g_b)
    kd = _mm(k_dec, sol, _BTN, mode_apply)
    at = _mm(attn, sol, _BNN, mode_apply)
    q_t = q * exp_g - at[..., :DN_HEAD_DIM]
    return kd[..., :DN_HEAD_DIM], kd[..., DN_HEAD_DIM:], q_t, at[..., DN_HEAD_DIM:], jnp.exp(g_last)


def _gdn_state_step(a_mat, b_mat, q_t, o_intra, d_last, s, mode):
    r = _mm(jnp.concatenate([a_mat, q_t], axis=1), s, _BNN, mode)
    s_new = s * d_last - r[:, :DN_HEAD_DIM] + b_mat
    return r[:, DN_HEAD_DIM:] + o_intra, s_new


_PROMPT_MODES = dict(mode_gram="bf16", mode_solve="x3", mode_apply="bf16")
_PROMPT_STATE_MODE = "bf16"
_SAMPLE_MODES = dict(mode_gram="bf16", mode_solve="bf16", mode_apply="bf16")
_SAMPLE_STATE_MODE = "bf16"


def _delta_prompt_kernel(qkv_ref, gb_ref, s0_ref, o_ref, sn_ref, s_scr):
    t = pl.program_id(1)
    tc = qkv_ref.shape[0]

    @pl.when(t == 0)
    def _():
        s_scr[...] = s0_ref[0]

    probs = _gdn_problems(qkv_ref[...], gb_ref[...], DN_CHUNK)
    a_mat, b_mat, q_t, o_intra, d_last = _gdn_intra(*probs, c=DN_CHUNK, small=False, **_PROMPT_MODES)
    s = s_scr[...]
    for ci in range(tc // DN_CHUNK):
        ps = slice(ci * DN_HEADS, (ci + 1) * DN_HEADS)
        o, s = _gdn_state_step(a_mat[ps], b_mat[ps], q_t[ps], o_intra[ps], d_last[ps], s,
                               _PROMPT_STATE_MODE)
        for h in range(DN_HEADS):
            o_ref[ci * DN_CHUNK:(ci + 1) * DN_CHUNK, h * DN_HEAD_DIM:(h + 1) * DN_HEAD_DIM] = (
                o[h].astype(o_ref.dtype))
    s_scr[...] = s

    @pl.when(t == pl.num_programs(1) - 1)
    def _():
        sn_ref[0] = s


def _delta_prompt(qkv, gb, s0, bsz, seq, tc):
    n = qkv.shape[0]
    nt = seq // tc
    rows = lambda w: pl.BlockSpec((tc, w), lambda b, t: (b * nt + t, 0))
    state = pl.BlockSpec((1, DN_HEADS, DN_HEAD_DIM, DN_HEAD_DIM), lambda b, t: (b, 0, 0, 0))
    return pl.pallas_call(
        _delta_prompt_kernel,
        grid=(bsz, nt),
        in_specs=[rows(3 * DN_WIDTH), rows(LANES), state],
        out_specs=[rows(DN_WIDTH), state],
        out_shape=[
            jax.ShapeDtypeStruct((n, DN_WIDTH), BF16),
            jax.ShapeDtypeStruct((bsz, DN_HEADS, DN_HEAD_DIM, DN_HEAD_DIM), F32),
        ],
        scratch_shapes=[pltpu.VMEM((DN_HEADS, DN_HEAD_DIM, DN_HEAD_DIM), F32)],
        compiler_params=_cparams("arbitrary", "arbitrary"),
        name="delta_prompt",
    )(qkv, gb, s0)


def _delta_sample_kernel(qkv_ref, gb_ref, s0_ref, *rest):
    o_ref, sn_ref = rest[-2:]
    bt = qkv_ref.shape[0]
    c = SAMPLE_CHUNK
    probs = _gdn_problems(qkv_ref[...].reshape(bt * c, 3 * DN_WIDTH), gb_ref[...].reshape(bt * c, LANES), c)
    a_mat, b_mat, q_t, o_intra, d_last = _gdn_intra(*probs, c=c, small=True, **_SAMPLE_MODES)
    s = s0_ref[0].reshape(bt * DN_HEADS, DN_HEAD_DIM, DN_HEAD_DIM)
    o, s = _gdn_state_step(a_mat, b_mat, q_t, o_intra, d_last, s, _SAMPLE_STATE_MODE)
    for b in range(bt):
        for h in range(DN_HEADS):
            o_ref[b, :, h * DN_HEAD_DIM:(h + 1) * DN_HEAD_DIM] = o[b * DN_HEADS + h]
    sn_ref[0] = s.reshape(bt, DN_HEADS, DN_HEAD_DIM, DN_HEAD_DIM)


def _delta_sample(qkv_b, gb_b, state_all, layer, carried, bt):
    bsz = qkv_b.shape[0]
    blk = lambda w: pl.BlockSpec((bt, SAMPLE_CHUNK, w), lambda i: (i, 0, 0))
    state = pl.BlockSpec((1, bt, DN_HEADS, DN_HEAD_DIM, DN_HEAD_DIM), lambda i: (layer, i, 0, 0, 0))
    in_specs, args, aliases = [blk(3 * DN_WIDTH), blk(LANES), state], [qkv_b, gb_b, state_all], {}
    if carried is not None:
        in_specs.append(pl.BlockSpec(memory_space=pl.ANY))
        args.append(carried)
        aliases = {3: 1}
    return pl.pallas_call(
        _delta_sample_kernel,
        grid=(bsz // bt,),
        in_specs=in_specs,
        out_specs=[blk(DN_WIDTH), state],
        out_shape=[
            jax.ShapeDtypeStruct((bsz, SAMPLE_CHUNK, DN_WIDTH), F32),
            jax.ShapeDtypeStruct(state_all.shape, F32),
        ],
        input_output_aliases=aliases,
        compiler_params=_cparams("arbitrary"),
        name="delta_sample",
    )(*args)


def _outproj_kernel(yc_ref, o_ref, z_ref, x_ref, g1_ref, nw_ref, w_ref, lg_ref, lb_ref, out_ref):
    o = o_ref[...].astype(F32)
    parts = []
    for h in range(DN_HEADS):
        oh = o[:, h * DN_HEAD_DIM:(h + 1) * DN_HEAD_DIM]
        parts.append(oh * lax.rsqrt(jnp.mean(oh * oh, -1, keepdims=True) + RMS_EPS) * nw_ref[...])
    og = jnp.concatenate(parts, axis=-1) * _silu(z_ref[...].astype(F32))
    mixed = jnp.concatenate([yc_ref[...].astype(BF16), og.astype(BF16)], axis=-1)
    m = jnp.dot(mixed, w_ref[...], preferred_element_type=F32)
    r = DEEPNORM_ALPHA * x_ref[...] + (1.0 + g1_ref[0]) * m
    out_ref[...] = _layer_norm(r, lg_ref[...], lb_ref[...])


def _outproj(yc, o, z_src, z_blk, x, mod3, tpg, norm_w, w_out, ln_g, ln_b, tm):
    n = x.shape[0]
    vec = lambda w: pl.BlockSpec((1, w), lambda i: (0, 0))
    return pl.pallas_call(
        _outproj_kernel,
        grid=(n // tm,),
        in_specs=[
            pl.BlockSpec((tm, CONV_WIDTH), lambda i: (i, 0)),
            pl.BlockSpec((tm, DN_WIDTH), lambda i: (i, 0)),
            pl.BlockSpec((tm, DN_WIDTH), lambda i: (i, z_blk)),
            pl.BlockSpec((tm, D_MODEL), lambda i: (i, 0)),
            _mod_spec(mod3, 2, tpg),
            vec(DN_HEAD_DIM),
            pl.BlockSpec((D_MODEL, D_MODEL), lambda i: (0, 0)),
            vec(D_MODEL), vec(D_MODEL),
        ],
        out_specs=pl.BlockSpec((tm, D_MODEL), lambda i: (i, 0)),
        out_shape=jax.ShapeDtypeStruct((n, D_MODEL), F32),
        compiler_params=_cparams("arbitrary"),
        name="outproj",
    )(yc, o, z_src, x, mod3, norm_w, w_out, ln_g, ln_b)


R_E1, R_E2, R_G1, R_G2, R_R1, R_R2 = range(6)


def _joint_specs(xp, xs, mod_p, mod_s, chunks, tm, seq):
    last = xp.shape[0] // tm - 1
    tpg = seq // tm
    assert xs.shape[0] == tm
    specs = [pl.BlockSpec((tm, D_MODEL), lambda i, *_: (jnp.minimum(i, last), 0)),
             pl.BlockSpec((tm, D_MODEL), lambda i, *_: (0, 0))]
    for c in chunks:
        specs.append(pl.BlockSpec((1, 1, D_MODEL), lambda i, *_, c=c: (jnp.minimum(i, last) // tpg, 0, c)))
        specs.append(pl.BlockSpec((1, mod_s.shape[1], D_MODEL), lambda i, *_, c=c: (0, 0, c)))
    return specs


def _tile_rows(mod_ref, rows):
    m = mod_ref[0]
    return jnp.concatenate([m] * (rows // m.shape[0]), axis=0)


def _joint_modulated(i, n_prompt_tiles, xp_ref, xs_ref, scp_ref, scs_ref, shp_ref, shs_ref, u_ref):
    tm = xp_ref.shape[0]

    @pl.when(i < n_prompt_tiles)
    def _():
        u_ref[...] = (xp_ref[...] * (1.0 + scp_ref[0]) + shp_ref[0]).reshape(u_ref.shape)

    @pl.when(i >= n_prompt_tiles)
    def _():
        u_ref[...] = (xs_ref[...] * (1.0 + _tile_rows(scs_ref, tm)) + _tile_rows(shs_ref, tm)).reshape(
            u_ref.shape)


def _router_kernel(xp_ref, xs_ref, scp_ref, scs_ref, shp_ref, shs_ref, wr_ref, br_ref,
                   route_ref, route_t_ref, cnt_ref, u_scr, *, n_prompt_tiles):
    tm = xp_ref.shape[0]
    _joint_modulated(pl.program_id(0), n_prompt_tiles, xp_ref, xs_ref, scp_ref, scs_ref, shp_ref, shs_ref,
                     u_scr)
    u = u_scr[...]
    logits = _mm(u, wr_ref[...], mode="x3") + br_ref[...]
    lane = lax.broadcasted_iota(jnp.int32, logits.shape, 1)
    valid = lane < N_EXPERTS
    logits = jnp.where(valid, logits, -jnp.inf)
    ex = jnp.exp(logits - jnp.max(logits, -1, keepdims=True))
    probs = jnp.where(valid, ex / jnp.sum(ex, -1, keepdims=True), -1.0)
    p1 = jnp.max(probs, -1, keepdims=True)
    i1 = jnp.min(jnp.where(probs == p1, lane, LANES), -1, keepdims=True)
    rest = jnp.where(lane == i1, -1.0, probs)
    p2 = jnp.max(rest, -1, keepdims=True)
    i2 = jnp.min(jnp.where(rest == p2, lane, LANES), -1, keepdims=True)
    tot = p1 + p2
    sel = jnp.where(lane == i1, 1.0, 0.0) + jnp.where(lane == i2, 1.0, 0.0)
    incl = _cumsum_chunks(sel, tm)
    excl = incl - sel
    r1 = jnp.sum(jnp.where(lane == i1, excl, 0.0), -1, keepdims=True)
    r2 = jnp.sum(jnp.where(lane == i2, excl, 0.0), -1, keepdims=True)
    cnt_ref[0] = incl[tm - 1:tm, :]
    route = jnp.zeros_like(logits)
    for j, val in enumerate((i1.astype(F32), i2.astype(F32), p1 / tot, p2 / tot, r1, r2)):
        route = jnp.where(lane == j, val, route)
    route_ref[...] = route
    route_t_ref[0] = jnp.transpose(route)[:SUBLANES, :]


def _router(xp, xs, mod_p, mod_s, seq, w_router, b_router, tm):
    ntp = xp.shape[0] // tm
    nt = ntp + 1
    return pl.pallas_call(
        functools.partial(_router_kernel, n_prompt_tiles=ntp),
        grid=(nt,),
        in_specs=_joint_specs(xp, xs, mod_p, mod_s, (4, 3), tm, seq) + [
            pl.BlockSpec((D_MODEL, LANES), lambda i: (0, 0)),
            pl.BlockSpec((1, LANES), lambda i: (0, 0)),
        ],
        out_specs=[pl.BlockSpec((tm, LANES), lambda i: (i, 0)),
                   pl.BlockSpec((1, SUBLANES, tm), lambda i: (i, 0, 0)),
                   pl.BlockSpec((1, 1, LANES), lambda i: (i, 0, 0))],
        out_shape=[jax.ShapeDtypeStruct((nt * tm, LANES), F32),
                   jax.ShapeDtypeStruct((nt, SUBLANES, tm), F32),
                   jax.ShapeDtypeStruct((nt, 1, LANES), F32)],
        scratch_shapes=[pltpu.VMEM((tm, D_MODEL), F32)],
        compiler_params=_cparams("arbitrary"),
        name="router",
    )(xp, xs, mod_p, mod_s, mod_p, mod_s, w_router, b_router)


def _dispatch_plan(route_t, counts, tme):
    n_rt, _, tm = route_t.shape
    n = n_rt * tm
    cnt = counts[:, 0, :N_EXPERTS].astype(jnp.int32)
    tile_off = jnp.cumsum(cnt, axis=0) - cnt
    tot = jnp.sum(cnt, axis=0)
    gsz = (tot + tme - 1) // tme * tme
    gend = jnp.cumsum(gsz)
    goff = gend - gsz
    base = (goff[None, :] + tile_off)[:, :, None]
    ids = jnp.arange(N_EXPERTS, dtype=jnp.int32)[None, :, None]
    field = lambda j: route_t[:, j:j + 1, :].astype(jnp.int32)
    pick = lambda e: jnp.sum(jnp.where(ids == e, base, 0), axis=1, keepdims=True)
    slot1 = (pick(field(R_E1)) + field(R_R1)).reshape(n)
    slot2 = (pick(field(R_E2)) + field(R_R2)).reshape(n)
    n_tiles = 2 * n // tme + N_EXPERTS
    start = jnp.arange(n_tiles, dtype=jnp.int32) * tme
    valid = (start < gend[-1]).astype(jnp.int32)
    blk = jnp.minimum(jnp.arange(n_tiles, dtype=jnp.int32), gend[-1] // tme - 1)
    tile_e = jnp.sum((blk * tme)[:, None] >= gend[None, :], axis=1).astype(jnp.int32)
    zoff = jnp.where(gsz > 0, gend - tme, (n_tiles + jnp.arange(N_EXPERTS, dtype=jnp.int32)) * tme)
    return slot1, slot2, zoff, tile_e, valid, blk, n_tiles


def _dispatch_kernel(s1_ref, s2_ref, zoff_ref, xp_ref, xs_ref, scp_ref, scs_ref, shp_ref, shs_ref,
                     ug_hbm, ubuf, zbuf, sem, *, n_prompt_tiles):
    i = pl.program_id(0)
    tm = xp_ref.shape[0]
    tme = zbuf.shape[0]

    @pl.when(i == 0)
    def _():
        zbuf[...] = jnp.zeros_like(zbuf)
        fill = lambda e: pltpu.make_async_copy(
            zbuf, ug_hbm.at[pl.ds(pl.multiple_of(zoff_ref[e], 8), tme)], sem.at[2])
        for e in range(N_EXPERTS):
            fill(e).start()
        for e in range(N_EXPERTS):
            fill(e).wait()

    _joint_modulated(i, n_prompt_tiles, xp_ref, xs_ref, scp_ref, scs_ref, shp_ref, shs_ref, ubuf)
    base = i * tm

    def issue(j, carry):
        for k in range(SUBLANES):
            row = ubuf.at[j, pl.ds(k, 1)]
            t = base + j * SUBLANES + k
            pltpu.make_async_copy(row, ug_hbm.at[pl.ds(s1_ref[t], 1)], sem.at[0]).start()
            pltpu.make_async_copy(row, ug_hbm.at[pl.ds(s2_ref[t], 1)], sem.at[1]).start(priority=1)
        return carry

    lax.fori_loop(0, tm // SUBLANES, issue, 0)
    pltpu.make_async_copy(ubuf, ubuf, sem.at[0]).wait()
    pltpu.make_async_copy(ubuf, ubuf, sem.at[1]).wait()


def _dispatch(xp, xs, mod_p, mod_s, seq, slot1, slot2, zoff, tm, tme, rows):
    ntp = xp.shape[0] // tm
    return pl.pallas_call(
        functools.partial(_dispatch_kernel, n_prompt_tiles=ntp),
        grid_spec=pltpu.PrefetchScalarGridSpec(
            num_scalar_prefetch=3,
            grid=(ntp + 1,),
            in_specs=_joint_specs(xp, xs, mod_p, mod_s, (4, 3), tm, seq),
            out_specs=pl.BlockSpec(memory_space=pl.ANY),
            scratch_shapes=[pltpu.VMEM((tm // SUBLANES, SUBLANES, D_MODEL), F32),
                            pltpu.VMEM((tme, D_MODEL), F32), pltpu.SemaphoreType.DMA((3,))],
        ),
        out_shape=jax.ShapeDtypeStruct((rows, D_MODEL), F32),
        compiler_params=_cparams("arbitrary"),
        name="dispatch",
    )(slot1, slot2, zoff, xp, xs, mod_p, mod_s, mod_p, mod_s)


def _expert_kernel(te_ref, tv_ref, tb_ref, ug_ref, wg_ref, wu_ref, wd_ref, ys_ref):
    i = pl.program_id(0)

    @pl.when(tv_ref[i] == 1)
    def _():
        u = ug_ref[...].astype(BF16)
        h = _silu(jnp.dot(u, wg_ref[0].astype(BF16), preferred_element_type=F32)) * jnp.dot(
            u, wu_ref[0].astype(BF16), preferred_element_type=F32)
        ys_ref[...] = jnp.dot(h.astype(BF16), wd_ref[0].astype(BF16), preferred_element_type=F32)

    @pl.when(tv_ref[i] == 0)
    def _():
        ys_ref[...] = jnp.zeros_like(ys_ref)


def _experts(ug, tile_e, valid, blk, w_gate, w_up, w_down, tme, n_tiles):
    d_e = w_gate.shape[-1]
    return pl.pallas_call(
        _expert_kernel,
        grid_spec=pltpu.PrefetchScalarGridSpec(
            num_scalar_prefetch=3,
            grid=(n_tiles,),
            in_specs=[
                pl.BlockSpec((tme, D_MODEL), lambda i, te, tv, tb: (tb[i], 0)),
                pl.BlockSpec((1, D_MODEL, d_e), lambda i, te, tv, tb: (te[i], 0, 0)),
                pl.BlockSpec((1, D_MODEL, d_e), lambda i, te, tv, tb: (te[i], 0, 0)),
                pl.BlockSpec((1, d_e, D_MODEL), lambda i, te, tv, tb: (te[i], 0, 0)),
            ],
            out_specs=pl.BlockSpec((tme, D_MODEL), lambda i, te, tv, tb: (i, 0)),
        ),
        out_shape=jax.ShapeDtypeStruct((n_tiles * tme, D_MODEL), F32),
        compiler_params=_cparams("arbitrary"),
        name="experts",
    )(tile_e, valid, blk, ug, w_gate, w_up, w_down)


def _combine_kernel(s1_ref, s2_ref, ys_hbm, route_ref, xp_ref, xs_ref, g2p_ref, g2s_ref, lg_ref, lb_ref,
                    outp_ref, outs_ref, y1buf, y2buf, sem, *, n_prompt_tiles):
    i = pl.program_id(0)
    tm = xp_ref.shape[0]
    base = i * tm

    def issue(j, carry):
        for k in range(SUBLANES):
            t = base + j * SUBLANES + k
            pltpu.make_async_copy(ys_hbm.at[pl.ds(s1_ref[t], 1)], y1buf.at[j, pl.ds(k, 1)], sem.at[0]).start()
            pltpu.make_async_copy(ys_hbm.at[pl.ds(s2_ref[t], 1)], y2buf.at[j, pl.ds(k, 1)],
                                  sem.at[1]).start(priority=1)
        return carry

    lax.fori_loop(0, tm // SUBLANES, issue, 0)
    pltpu.make_async_copy(y1buf, y1buf, sem.at[0]).wait()
    pltpu.make_async_copy(y2buf, y2buf, sem.at[1]).wait()
    route = route_ref[...]
    y1 = y1buf[...].reshape(tm, D_MODEL)
    y2 = y2buf[...].reshape(tm, D_MODEL)
    f = route[:, R_G1:R_G1 + 1] * y1 + route[:, R_G2:R_G2 + 1] * y2

    @pl.when(i < n_prompt_tiles)
    def _():
        r = DEEPNORM_ALPHA * xp_ref[...] + (1.0 + g2p_ref[0]) * f
        outp_ref[...] = _layer_norm(r, lg_ref[...], lb_ref[...])

    @pl.when(i >= n_prompt_tiles)
    def _():
        r = DEEPNORM_ALPHA * xs_ref[...] + (1.0 + _tile_rows(g2s_ref, tm)) * f
        outs_ref[...] = _layer_norm(r, lg_ref[...], lb_ref[...])


def _combine(ys, slot1, slot2, route, xp, xs, mod_p, mod_s, seq, ln_g, ln_b, tm):
    ntp = xp.shape[0] // tm
    vec = lambda w: pl.BlockSpec((1, w), lambda i, *_: (0, 0))
    x_specs = _joint_specs(xp, xs, mod_p, mod_s, (5,), tm, seq)
    return pl.pallas_call(
        functools.partial(_combine_kernel, n_prompt_tiles=ntp),
        grid_spec=pltpu.PrefetchScalarGridSpec(
            num_scalar_prefetch=2,
            grid=(ntp + 1,),
            in_specs=[
                pl.BlockSpec(memory_space=pl.ANY),
                pl.BlockSpec((tm, LANES), lambda i, *_: (i, 0)),
            ] + x_specs + [vec(D_MODEL), vec(D_MODEL)],
            out_specs=[x_specs[0], x_specs[1]],
            scratch_shapes=[pltpu.VMEM((tm // SUBLANES, SUBLANES, D_MODEL), F32),
                            pltpu.VMEM((tm // SUBLANES, SUBLANES, D_MODEL), F32),
                            pltpu.SemaphoreType.DMA((2,))],
        ),
        out_shape=[jax.ShapeDtypeStruct(xp.shape, F32), jax.ShapeDtypeStruct(xs.shape, F32)],
        compiler_params=_cparams("arbitrary"),
        name="combine",
    )(slot1, slot2, ys, route, xp, xs, mod_p, mod_s, ln_g, ln_b)


def _moe(xp, xs, mod_p, mod_s, seq, w_router, b_router, w_gate, w_up, w_down, ln_g, ln_b, tm, tme):
    route, route_t, counts = _router(xp, xs, mod_p, mod_s, seq, w_router, b_router, tm)
    slot1, slot2, zoff, tile_e, valid, blk, n_tiles = _dispatch_plan(route_t, counts, tme)
    ug = _dispatch(xp, xs, mod_p, mod_s, seq, slot1, slot2, zoff, tm, tme, (n_tiles + N_EXPERTS) * tme)
    ys = _experts(ug, tile_e, valid, blk, w_gate, w_up, w_down, tme, n_tiles)
    return _combine(ys, slot1, slot2, route, xp, xs, mod_p, mod_s, seq, ln_g, ln_b, tm)


def _ffn_kernel(x_ref, sc_ref, sh_ref, g2_ref, wg_ref, wu_ref, wd_ref, lg_ref, lb_ref, out_ref, *, tf):
    x = x_ref[...]
    u = (x * (1.0 + sc_ref[0]) + sh_ref[0]).astype(BF16)
    y = None
    for f0 in range(0, wg_ref.shape[1], tf):
        h = _silu(jnp.dot(u, wg_ref[:, f0:f0 + tf], preferred_element_type=F32)) * jnp.dot(
            u, wu_ref[:, f0:f0 + tf], preferred_element_type=F32)
        part = jnp.dot(h.astype(BF16), wd_ref[f0:f0 + tf, :], preferred_element_type=F32)
        y = part if y is None else y + part
    r = DEEPNORM_ALPHA * x + (1.0 + g2_ref[0]) * y
    out_ref[...] = _layer_norm(r, lg_ref[...], lb_ref[...])


def _ffn(x, mod3, tpg, w_gate, w_up, w_down, ln_g, ln_b, tm, tf):
    n = x.shape[0]
    vec = lambda w: pl.BlockSpec((1, w), lambda i: (0, 0))
    resident = lambda a: pl.BlockSpec(a.shape, lambda i: (0, 0), pipeline_mode=pl.Buffered(1))
    return pl.pallas_call(
        functools.partial(_ffn_kernel, tf=tf),
        grid=(n // tm,),
        in_specs=[
            pl.BlockSpec((tm, D_MODEL), lambda i: (i, 0)),
            _mod_spec(mod3, 4, tpg),
            _mod_spec(mod3, 3, tpg),
            _mod_spec(mod3, 5, tpg),
            resident(w_gate), resident(w_up), resident(w_down),
            vec(D_MODEL), vec(D_MODEL),
        ],
        out_specs=pl.BlockSpec((tm, D_MODEL), lambda i: (i, 0)),
        out_shape=jax.ShapeDtypeStruct((n, D_MODEL), F32),
        compiler_params=_cparams("arbitrary"),
        name="ffn_dense",
    )(x, mod3, mod3, mod3, w_gate, w_up, w_down, ln_g, ln_b)


def _token_mixer(l, x, mod3, rows_per_group, prm, tm, mixer_fn):
    yc, o, z_src, z_blk, st = mixer_fn(l, x, mod3)
    x = _outproj(yc, o, z_src, z_blk, x, mod3, rows_per_group // tm, prm["norm_w"][l], prm["w_out"][l],
                 prm["ln_g"][l, 0:1], prm["ln_b"][l, 0:1], tm)
    return x, st


def kernel(x_prompt, x_sample, state_conv, state_dn_conv, state_dn, c_prompt, c_sample, w_in, conv_w,
           dn_conv_w, a_log, dt_bias, dn_norm_w, w_out, w_ada, b_ada, ln_g, ln_b, w_ff_gate, w_ff_up,
           w_ff_down, w_router, b_router, w_exp_gate, w_exp_up, w_exp_down):
    bp, seq, d = x_prompt.shape
    bs, steps, _ = x_sample.shape
    pad_lanes = LANES - N_EXPERTS
    hp = jnp.zeros((DEPTH, 2, LANES), F32)
    hp = hp.at[:, 0, G_LANE:].set(a_log).at[:, 1, G_LANE:].set(dt_bias)
    w_in_t = jnp.swapaxes(w_in, 1, 2)
    prm = {
        "w_main": w_in_t[:, :P_MAIN].astype(BF16),
        "w_small": w_in_t[:, w_in_t.shape[1] - LANES:],
        "norm_w": dn_norm_w.reshape(DEPTH, 1, DN_HEAD_DIM),
        "w_out": w_out.astype(BF16),
        "ln_g": ln_g, "ln_b": ln_b,
        "w_ff_gate": w_ff_gate.astype(BF16), "w_ff_up": w_ff_up.astype(BF16),
        "w_ff_down": w_ff_down.astype(BF16),
        "w_router": jnp.pad(w_router, ((0, 0), (0, 0), (0, pad_lanes))),
        "b_router": jnp.pad(b_router, ((0, 0), (0, pad_lanes)))[:, None, :],
        "w_exp_gate": w_exp_gate, "w_exp_up": w_exp_up, "w_exp_down": w_exp_down,
    }

    mod = _ada(jnp.concatenate([c_prompt, c_sample], axis=0), w_ada, b_ada)
    mod_p = [mod[l, :bp].reshape(bp, 1, 6 * d) for l in range(DEPTH)]
    mod_s = [mod[l, bp:].reshape(1, bs, 6 * d) for l in range(DEPTH)]

    tm_p, tc_delta = 512, 512
    zc = jnp.zeros((bp, CONV_K - 1, CONV_WIDTH), F32)
    zd = jnp.zeros((bp, DN_CONV_K - 1, 3 * DN_WIDTH), F32)
    zs = jnp.zeros((bp, DN_HEADS, DN_HEAD_DIM, DN_HEAD_DIM), F32)

    def mixer_prompt(l, x, mod3):
        yc, qkv, z, gb, ncb, ndb = _inproj_pre_prompt(
            x, mod3, prm["w_main"][l], prm["w_small"][l], conv_w[l], dn_conv_w[l], hp[l], zc, zd,
            bp, seq, tm_p)
        o, s_new = _delta_prompt(qkv, gb, zs, bp, seq, tc_delta)
        return yc, o, z, 0, (ncb, ndb, s_new)

    bt_pre, bt_delta = 32, 16
    dn_states = []

    def mixer_sample(l, x, mod3):
        pm, ps = _inproj(x, mod3, steps, prm["w_main"][l], prm["w_small"][l], bs)
        cb_tm = jnp.transpose(state_conv[l], (1, 0, 2))
        db_tm = jnp.transpose(state_dn_conv[l], (1, 0, 2))
        yc, qkv, gb, ncb, ndb = _pre_sample(pm, ps, conv_w[l], dn_conv_w[l], hp[l], cb_tm, db_tm,
                                            bs, steps, bt_pre)
        to_b = lambda a: jnp.pad(jnp.transpose(a, (1, 0, 2)), ((0, 0), (0, SAMPLE_CHUNK - steps), (0, 0)))
        o_b, s_all = _delta_sample(to_b(qkv), to_b(gb), state_dn, l, dn_states[-1] if dn_states else None,
                                   bt_delta)
        dn_states.append(s_all)
        o = jnp.transpose(o_b[:, :steps], (1, 0, 2)).reshape(steps * bs, DN_WIDTH)
        return (yc.reshape(steps * bs, CONV_WIDTH), o, pm, OFF_Z // DN_WIDTH,
                (jnp.transpose(ncb, (1, 0, 2)), jnp.transpose(ndb, (1, 0, 2))))

    x_p = x_prompt.reshape(bp * seq, d)
    x_s = jnp.transpose(x_sample, (1, 0, 2)).reshape(steps * bs, d)
    n_s = steps * bs
    st_p, st_s = [], []
    for l in range(DEPTH):
        x_p, st = _token_mixer(l, x_p, mod_p[l], seq, prm, tm_p, mixer_prompt)
        st_p.append(st)
        x_s, st = _token_mixer(l, x_s, mod_s[l], n_s, prm, bs, mixer_sample)
        st_s.append(st)
        j = l // 2
        ln = (prm["ln_g"][l, 1:2], prm["ln_b"][l, 1:2])
        if l % 2 == 0:
            ffn_w = (prm["w_ff_gate"][j], prm["w_ff_up"][j], prm["w_ff_down"][j])
            x_p = _ffn(x_p, mod_p[l], seq // tm_p, *ffn_w, *ln, tm_p, 256)
            x_s = _ffn(x_s, mod_s[l], n_s // bs, *ffn_w, *ln, bs, w_ff_gate.shape[-1])
        else:
            x_p, x_s = _moe(x_p, x_s, mod_p[l], mod_s[l], seq, prm["w_router"][j], prm["b_router"][j],
                            prm["w_exp_gate"][j], prm["w_exp_up"][j], prm["w_exp_down"][j], *ln, tm_p, 512)
    y_p = x_p
    y_s = jnp.transpose(x_s.reshape(steps, bs, d), (1, 0, 2))

    stack = lambda sts, i: jnp.stack([s[i] for s in sts])
    return (y_p.reshape(bp, seq, d), y_s,
            stack(st_p, 0), stack(st_p, 1), stack(st_p, 2),
            stack(st_s, 0), stack(st_s, 1), dn_states[-1])
```

```python
import functools
import math

import jax
import jax.numpy as jnp
from jax import lax
from jax.experimental import pallas as pl
from jax.experimental.pallas import tpu as pltpu

F32 = jnp.float32
BF16 = jnp.bfloat16

D_MODEL = 1024
DEPTH = 2
CONV_WIDTH = 512
CONV_K = 3
DN_HEADS = 4
DN_HEAD_DIM = 128
DN_WIDTH = DN_HEADS * DN_HEAD_DIM
DN_CONV_K = 4
DN_CHUNK = 64
N_EXPERTS = 8
DEEPNORM_ALPHA = (2.0 * DEPTH) ** 0.25
LN_EPS = 1e-5
RMS_EPS = 1e-6

OFF_CC = CONV_WIDTH
OFF_CH = 2 * CONV_WIDTH
OFF_QKV = 3 * CONV_WIDTH
OFF_Z = OFF_QKV + 3 * DN_WIDTH
P_MAIN = OFF_Z + DN_WIDTH
LANES = 128
SUBLANES = 8
BETA_LANE = LANES - 2 * DN_HEADS
G_LANE = LANES - DN_HEADS
SAMPLE_CHUNK = 8

VMEM_LIMIT = 56 * 1024 * 1024


def _cparams(*sem):
    return pltpu.CompilerParams(dimension_semantics=sem, vmem_limit_bytes=VMEM_LIMIT)


def _silu(x):
    return x * jax.nn.sigmoid(x)


def _layer_norm(r, g, b):
    mu = jnp.mean(r, -1, keepdims=True)
    xc = r - mu
    var = jnp.mean(xc * xc, -1, keepdims=True)
    return xc * lax.rsqrt(var + LN_EPS) * g + b


def _split_bf16(a):
    hi = a.astype(BF16)
    lo = (a - hi.astype(F32)).astype(BF16)
    return hi, lo


_NN = (((1,), (0,)), ((), ()))
_NT = (((1,), (1,)), ((), ()))
_BNN = (((2,), (1,)), ((0,), (0,)))
_BNT = (((2,), (2,)), ((0,), (0,)))
_BTN = (((1,), (1,)), ((0,), (0,)))


def _mm(a, b, dims=_NN, mode="bf16"):
    if mode == "f32":
        return lax.dot_general(a, b, dims, precision=lax.Precision.HIGHEST, preferred_element_type=F32)
    if mode == "bf16":
        return lax.dot_general(a.astype(BF16), b.astype(BF16), dims, preferred_element_type=F32)
    ah, al = _split_bf16(a)
    bh, bl = _split_bf16(b)
    d = functools.partial(lax.dot_general, dimension_numbers=dims, preferred_element_type=F32)
    return d(ah, bh) + (d(ah, bl) + d(al, bh))


def _ada_kernel(c_ref, w_ref, b_ref, o_ref):
    s = _silu(c_ref[...]).astype(BF16)
    o_ref[0] = jnp.dot(s, w_ref[0].astype(BF16), preferred_element_type=F32) + b_ref[0]


def _ada(c_all, w_ada, b_ada):
    rows = c_all.shape[0]
    tn = 1536
    return pl.pallas_call(
        _ada_kernel,
        grid=(DEPTH, 6 * D_MODEL // tn),
        in_specs=[
            pl.BlockSpec((rows, D_MODEL), lambda l, j: (0, 0)),
            pl.BlockSpec((1, D_MODEL, tn), lambda l, j: (l, 0, j)),
            pl.BlockSpec((1, 1, tn), lambda l, j: (l, 0, j)),
        ],
        out_specs=pl.BlockSpec((1, rows, tn), lambda l, j: (l, 0, j)),
        out_shape=jax.ShapeDtypeStruct((DEPTH, rows, 6 * D_MODEL), F32),
        compiler_params=_cparams("arbitrary", "arbitrary"),
        name="ada",
    )(c_all, w_ada, b_ada.reshape(DEPTH, 1, 6 * D_MODEL))


def _mod_spec(mod3, chunk, tiles_per_group):
    r = mod3.shape[1]
    return pl.BlockSpec((1, r, D_MODEL), lambda i, *_: (i // tiles_per_group, 0, chunk))


def _inproj_kernel(x_ref, sc_ref, sh_ref, wm_ref, ws_ref, pm_ref, ps_ref):
    u = x_ref[...] * (1.0 + sc_ref[0]) + sh_ref[0]
    pm_ref[...] = lax.dot_general(u.astype(BF16), wm_ref[...], _NT, preferred_element_type=F32)
    ps_ref[...] = _mm(u, ws_ref[...], _NT, mode="x3")


def _inproj(x, mod3, tpg, w_main, w_small, tm):
    n = x.shape[0]
    return pl.pallas_call(
        _inproj_kernel,
        grid=(n // tm,),
        in_specs=[
            pl.BlockSpec((tm, D_MODEL), lambda i: (i, 0)),
            _mod_spec(mod3, 1, tpg),
            _mod_spec(mod3, 0, tpg),
            pl.BlockSpec((P_MAIN, D_MODEL), lambda i: (0, 0)),
            pl.BlockSpec((LANES, D_MODEL), lambda i: (0, 0)),
        ],
        out_specs=[
            pl.BlockSpec((tm, P_MAIN), lambda i: (i, 0)),
            pl.BlockSpec((tm, LANES), lambda i: (i, 0)),
        ],
        out_shape=[
            jax.ShapeDtypeStruct((n, P_MAIN), F32),
            jax.ShapeDtypeStruct((n, LANES), F32),
        ],
        compiler_params=_cparams("arbitrary"),
        name="inproj",
    )(x, mod3, mod3, w_main, w_small)


def _gates(ps, hp):
    lane = lax.broadcasted_iota(jnp.int32, ps.shape, 1)
    beta = jax.nn.sigmoid(ps)
    g = -jnp.exp(hp[0:1, :]) * jax.nn.softplus(ps + hp[1:2, :])
    return jnp.where(lane >= G_LANE, g, jnp.where(lane >= BETA_LANE, beta, 0.0))


def _qkv_finish(y, out_ref, idx, first_head=0):
    y = _silu(y)
    for j in range(y.shape[1] // DN_HEAD_DIM):
        h = first_head + j
        seg = y[:, j * DN_HEAD_DIM:(j + 1) * DN_HEAD_DIM]
        if h < 2 * DN_HEADS:
            seg = seg * lax.rsqrt(jnp.sum(seg * seg, -1, keepdims=True) + RMS_EPS)
            if h < DN_HEADS:
                seg = seg * (DN_HEAD_DIM ** -0.5)
        out_ref[idx + (slice(None), slice(h * DN_HEAD_DIM, (h + 1) * DN_HEAD_DIM))] = seg


def _shifted(x, tail_ref, i, row):
    s = pltpu.roll(x, i, 0)
    top = s[0:SUBLANES]
    for r in range(i):
        top = jnp.where(row == r, tail_ref[8 - i + r:8 - i + r + 1, :], top)
    return jnp.concatenate([top, s[SUBLANES:]], axis=0)


def _inproj_pre_prompt_kernel(x_ref, sc_ref, sh_ref, wm_ref, ws_ref, cw_ref, dw_ref, hp_ref, cb_ref, db_ref,
                              yc_ref, qkv_ref, z_ref, gb_ref, ncb_ref, ndb_ref, tc_scr, td_scr):
    t = pl.program_id(1)
    tm = x_ref.shape[0]

    @pl.when(t == 0)
    def _():
        tc_scr[8 - (CONV_K - 1):8, :] = cb_ref[0]
        td_scr[8 - (DN_CONV_K - 1):8, :] = db_ref[0]

    u_f32 = x_ref[...] * (1.0 + sc_ref[0]) + sh_ref[0]
    u = u_f32.astype(BF16)
    proj = lambda lo, hi: lax.dot_general(u, wm_ref[lo:hi, :], _NT, preferred_element_type=F32)

    row = lax.broadcasted_iota(jnp.int32, (SUBLANES, DN_WIDTH), 0)

    def dn_part(xp, part):
        cols = slice(part * DN_WIDTH, (part + 1) * DN_WIDTH)
        tail = td_scr.at[:, cols]
        yp = _shifted(xp, tail, 3, row) * dw_ref[0:1, cols]
        yp = yp + _shifted(xp, tail, 2, row) * dw_ref[1:2, cols]
        yp = yp + _shifted(xp, tail, 1, row) * dw_ref[2:3, cols]
        yp = yp + xp * dw_ref[3:4, cols]
        _qkv_finish(yp, qkv_ref, (), first_head=part * DN_HEADS)
        return xp[tm - (DN_CONV_K - 1):tm, :]

    qkv_cols = lambda part: (OFF_QKV + part * DN_WIDTH, OFF_QKV + (part + 1) * DN_WIDTH)
    x_q = proj(*qkv_cols(0))
    x_k = proj(*qkv_cols(1))
    last_q = dn_part(x_q, 0)
    x_v = proj(*qkv_cols(2))
    last_k = dn_part(x_k, 1)
    p_c = proj(OFF_CC, OFF_CH)
    last_v = dn_part(x_v, 2)
    p_h = proj(OFF_CH, OFF_QKV)
    cgh = p_c * p_h
    p_b = proj(0, OFF_CC)
    y = _shifted(cgh, tc_scr, 2, row) * cw_ref[0:1, :]
    y = y + _shifted(cgh, tc_scr, 1, row) * cw_ref[1:2, :]
    y = y + cgh * cw_ref[2:3, :]
    p_z = proj(OFF_Z, P_MAIN)
    yc_ref[...] = (p_b * y).astype(yc_ref.dtype)
    last_c = cgh[tm - (CONV_K - 1):tm, :]
    small = _mm(u_f32, ws_ref[...], _NT, mode="bf16")
    z_ref[...] = p_z.astype(z_ref.dtype)
    gb_ref[...] = _gates(small, hp_ref[...])

    last_d = jnp.concatenate([last_q, last_k, last_v], axis=-1)
    tc_scr[8 - (CONV_K - 1):8, :] = last_c
    td_scr[8 - (DN_CONV_K - 1):8, :] = last_d

    @pl.when(t == pl.num_programs(1) - 1)
    def _():
        ncb_ref[0] = last_c
        ndb_ref[0] = last_d


def _inproj_pre_prompt(x, mod3, w_main, w_small, conv_w, dn_conv_w, hp, conv_buf, dn_buf, bsz, seq, tm):
    n = x.shape[0]
    nt = seq // tm
    rows = lambda w: pl.BlockSpec((tm, w), lambda b, t: (b * nt + t, 0))
    full = lambda a: pl.BlockSpec(a.shape, lambda b, t: (0,) * a.ndim)
    per_b = lambda k, w: pl.BlockSpec((1, k, w), lambda b, t: (b, 0, 0))
    mod = lambda chunk: pl.BlockSpec((1, 1, D_MODEL), lambda b, t: (b, 0, chunk))
    return pl.pallas_call(
        _inproj_pre_prompt_kernel,
        grid=(bsz, nt),
        in_specs=[rows(D_MODEL), mod(1), mod(0), full(w_main), full(w_small),
                  full(conv_w), full(dn_conv_w), full(hp),
                  per_b(CONV_K - 1, CONV_WIDTH), per_b(DN_CONV_K - 1, 3 * DN_WIDTH)],
        out_specs=[rows(CONV_WIDTH), rows(3 * DN_WIDTH), rows(DN_WIDTH), rows(LANES),
                   per_b(CONV_K - 1, CONV_WIDTH), per_b(DN_CONV_K - 1, 3 * DN_WIDTH)],
        out_shape=[
            jax.ShapeDtypeStruct((n, CONV_WIDTH), BF16),
            jax.ShapeDtypeStruct((n, 3 * DN_WIDTH), F32),
            jax.ShapeDtypeStruct((n, DN_WIDTH), BF16),
            jax.ShapeDtypeStruct((n, LANES), F32),
            jax.ShapeDtypeStruct((bsz, CONV_K - 1, CONV_WIDTH), F32),
            jax.ShapeDtypeStruct((bsz, DN_CONV_K - 1, 3 * DN_WIDTH), F32),
        ],
        scratch_shapes=[pltpu.VMEM((8, CONV_WIDTH), F32), pltpu.VMEM((8, 3 * DN_WIDTH), F32)],
        compiler_params=_cparams("arbitrary", "arbitrary"),
        name="inproj_pre_prompt",
    )(x, mod3, mod3, w_main, w_small, conv_w, dn_conv_w, hp, conv_buf, dn_buf)


def _pre_sample_kernel(pm_ref, ps_ref, cw_ref, dw_ref, hp_ref, cb_ref, db_ref,
                       yc_ref, qkv_ref, gb_ref, ncb_ref, ndb_ref):
    steps = pm_ref.shape[0]
    cgh = [pm_ref[t, :, OFF_CC:OFF_CH] * pm_ref[t, :, OFF_CH:OFF_QKV] for t in range(steps)]
    ext = [cb_ref[i] for i in range(CONV_K - 1)] + cgh
    for t in range(steps):
        y = ext[t] * cw_ref[0:1, :]
        for i in range(1, CONV_K):
            y = y + ext[t + i] * cw_ref[i:i + 1, :]
        yc_ref[t] = pm_ref[t, :, 0:OFF_CC] * y
    for i in range(CONV_K - 1):
        ncb_ref[i] = ext[len(ext) - (CONV_K - 1) + i]

    xq = [pm_ref[t, :, OFF_QKV:OFF_Z] for t in range(steps)]
    extq = [db_ref[i] for i in range(DN_CONV_K - 1)] + xq
    for t in range(steps):
        y = extq[t] * dw_ref[0:1, :]
        for i in range(1, DN_CONV_K):
            y = y + extq[t + i] * dw_ref[i:i + 1, :]
        _qkv_finish(y, qkv_ref, (t,))
        gb_ref[t] = _gates(ps_ref[t], hp_ref[...])
    for i in range(DN_CONV_K - 1):
        ndb_ref[i] = extq[len(extq) - (DN_CONV_K - 1) + i]


def _pre_sample(pm, ps, conv_w, dn_conv_w, hp, conv_buf_tm, dn_buf_tm, bsz, steps, bt):
    slab = lambda k, w: pl.BlockSpec((k, bt, w), lambda i: (0, i, 0))
    full = lambda a: pl.BlockSpec(a.shape, lambda i: (0,) * a.ndim)
    return pl.pallas_call(
        _pre_sample_kernel,
        grid=(bsz // bt,),
        in_specs=[slab(steps, P_MAIN), slab(steps, LANES), full(conv_w), full(dn_conv_w), full(hp),
                  slab(CONV_K - 1, CONV_WIDTH), slab(DN_CONV_K - 1, 3 * DN_WIDTH)],
        out_specs=[slab(steps, CONV_WIDTH), slab(steps, 3 * DN_WIDTH), slab(steps, LANES),
                   slab(CONV_K - 1, CONV_WIDTH), slab(DN_CONV_K - 1, 3 * DN_WIDTH)],
        out_shape=[
            jax.ShapeDtypeStruct((steps, bsz, CONV_WIDTH), F32),
            jax.ShapeDtypeStruct((steps, bsz, 3 * DN_WIDTH), F32),
            jax.ShapeDtypeStruct((steps, bsz, LANES), F32),
            jax.ShapeDtypeStruct((CONV_K - 1, bsz, CONV_WIDTH), F32),
            jax.ShapeDtypeStruct((DN_CONV_K - 1, bsz, 3 * DN_WIDTH), F32),
        ],
        compiler_params=_cparams("arbitrary"),
        name="pre_sample",
    )(pm.reshape(steps, bsz, P_MAIN), ps.reshape(steps, bsz, LANES), conv_w, dn_conv_w, hp,
      conv_buf_tm, dn_buf_tm)


def _cumsum_chunks(x, c):
    row = lax.broadcasted_iota(jnp.int32, x.shape, 0) & (c - 1)
    s = 1
    while s < c:
        x = x + jnp.where(row >= s, pltpu.roll(x, s, 0), 0.0)
        s *= 2
    return x


def _gdn_problems(qkv, gb, c):
    n = qkv.shape[0] // c
    gc = _cumsum_chunks(gb, c)
    q, k, v, g_b, beta_b = [], [], [], [], []
    for ci in range(n):
        rs = slice(ci * c, (ci + 1) * c)
        for h in range(DN_HEADS):
            ls = lambda base: slice(base + h * DN_HEAD_DIM, base + (h + 1) * DN_HEAD_DIM)
            q.append(qkv[rs, ls(0)])
            k.append(qkv[rs, ls(DN_WIDTH)])
            v.append(qkv[rs, ls(2 * DN_WIDTH)])
            g_b.append(jnp.broadcast_to(gc[rs, G_LANE + h:G_LANE + h + 1], (c, DN_HEAD_DIM)))
            beta_b.append(jnp.broadcast_to(gb[rs, BETA_LANE + h:BETA_LANE + h + 1], (c, DN_HEAD_DIM)))
    return tuple(jnp.stack(a) for a in (q, k, v, g_b, beta_b))


def _unit_lower_solve(m, rhs, c, small, mode):
    if small:
        sol = rhs
        for j in range(c - 1):
            sol = sol - m[:, :, j:j + 1] * sol[:, j:j + 1, :]
        return sol
    row = lax.broadcasted_iota(jnp.int32, m.shape, 1)
    col = lax.broadcasted_iota(jnp.int32, m.shape, 2)
    p = -m
    t = jnp.where(row == col, 1.0, 0.0) + p
    p = _mm(p, p, _BNN, "bf16")
    levels = int(math.log2(c))
    for lvl in range(1, levels):
        if lvl < levels - 1:
            y = _mm(jnp.concatenate([t, p], axis=1), p, _BNN, "bf16")
            t = t + y[:, :c]
            p = y[:, c:]
        else:
            t = t + _mm(t, p, _BNN, "bf16")
    x = _mm(t, rhs, _BNN, "bf16")
    resid = rhs - x - _mm(m, x, _BNN, mode)
    return x + _mm(t, resid, _BNN, "bf16")


def _gdn_intra(q, k, v, g_b, beta_b, *, c, small, mode_gram, mode_solve, mode_apply):
    n = q.shape[0]
    row = lax.broadcasted_iota(jnp.int32, (n, c, c), 1)
    col = lax.broadcasted_iota(jnp.int32, (n, c, c), 2)
    g_cc = g_b[:, :, :c]
    g_row = jnp.sum(jnp.where(row == col, g_cc, 0.0), axis=1, keepdims=True)
    gamma = jnp.exp(jnp.where(row >= col, g_cc - g_row, -jnp.inf))
    kbeta = k * beta_b
    exp_g = jnp.exp(g_b)
    gram = _mm(jnp.concatenate([kbeta, q], axis=1), k, _BNT, mode_gram)
    m = jnp.where(row > col, gram[:, :c] * gamma, 0.0)
    attn = gram[:, c:] * gamma
    rhs = jnp.concatenate([kbeta * exp_g, v * beta_b], axis=-1)
    sol = _unit_lower_solve(m, rhs, c, small, mode_solve)
    g_last = g_b[:, c - 1:c, :]
    k_dec = k * jnp.exp(g_last - g_b)
    kd = _mm(k_dec, sol, _BTN, mode_apply)
    at = _mm(attn, sol, _BNN, mode_apply)
    q_t = q * exp_g - at[..., :DN_HEAD_DIM]
    return kd[..., :DN_HEAD_DIM], kd[..., DN_HEAD_DIM:], q_t, at[..., DN_HEAD_DIM:], jnp.exp(g_last)


def _gdn_state_step(a_mat, b_mat, q_t, o_intra, d_last, s, mode):
    r = _mm(jnp.concatenate([a_mat, q_t], axis=1), s, _BNN, mode)
    s_new = s * d_last - r[:, :DN_HEAD_DIM] + b_mat
    return r[:, DN_HEAD_DIM:] + o_intra, s_new


_PROMPT_MODES = dict(mode_gram="bf16", mode_solve="x3", mode_apply="bf16")
_PROMPT_STATE_MODE = "bf16"
_SAMPLE_MODES = dict(mode_gram="bf16", mode_solve="bf16", mode_apply="bf16")
_SAMPLE_STATE_MODE = "bf16"


def _delta_prompt_kernel(qkv_ref, gb_ref, s0_ref, o_ref, sn_ref, s_scr):
    t = pl.program_id(1)
    tc = qkv_ref.shape[0]

    @pl.when(t == 0)
    def _():
        s_scr[...] = s0_ref[0]

    probs = _gdn_problems(qkv_ref[...], gb_ref[...], DN_CHUNK)
    a_mat, b_mat, q_t, o_intra, d_last = _gdn_intra(*probs, c=DN_CHUNK, small=False, **_PROMPT_MODES)
    s = s_scr[...]
    for ci in range(tc // DN_CHUNK):
        ps = slice(ci * DN_HEADS, (ci + 1) * DN_HEADS)
        o, s = _gdn_state_step(a_mat[ps], b_mat[ps], q_t[ps], o_intra[ps], d_last[ps], s,
                               _PROMPT_STATE_MODE)
        for h in range(DN_HEADS):
            o_ref[ci * DN_CHUNK:(ci + 1) * DN_CHUNK, h * DN_HEAD_DIM:(h + 1) * DN_HEAD_DIM] = (
                o[h].astype(o_ref.dtype))
    s_scr[...] = s

    @pl.when(t == pl.num_programs(1) - 1)
    def _():
        sn_ref[0] = s


def _delta_prompt(qkv, gb, s0, bsz, seq, tc):
    n = qkv.shape[0]
    nt = seq // tc
    rows = lambda w: pl.BlockSpec((tc, w), lambda b, t: (b * nt + t, 0))
    state = pl.BlockSpec((1, DN_HEADS, DN_HEAD_DIM, DN_HEAD_DIM), lambda b, t: (b, 0, 0, 0))
    return pl.pallas_call(
        _delta_prompt_kernel,
        grid=(bsz, nt),
        in_specs=[rows(3 * DN_WIDTH), rows(LANES), state],
        out_specs=[rows(DN_WIDTH), state],
        out_shape=[
            jax.ShapeDtypeStruct((n, DN_WIDTH), BF16),
            jax.ShapeDtypeStruct((bsz, DN_HEADS, DN_HEAD_DIM, DN_HEAD_DIM), F32),
        ],
        scratch_shapes=[pltpu.VMEM((DN_HEADS, DN_HEAD_DIM, DN_HEAD_DIM), F32)],
        compiler_params=_cparams("arbitrary", "arbitrary"),
        name="delta_prompt",
    )(qkv, gb, s0)


def _delta_sample_kernel(qkv_ref, gb_ref, s0_ref, *rest):
    o_ref, sn_ref = rest[-2:]
    bt = qkv_ref.shape[0]
    c = SAMPLE_CHUNK
    probs = _gdn_problems(qkv_ref[...].reshape(bt * c, 3 * DN_WIDTH), gb_ref[...].reshape(bt * c, LANES), c)
    a_mat, b_mat, q_t, o_intra, d_last = _gdn_intra(*probs, c=c, small=True, **_SAMPLE_MODES)
    s = s0_ref[0].reshape(bt * DN_HEADS, DN_HEAD_DIM, DN_HEAD_DIM)
    o, s = _gdn_state_step(a_mat, b_mat, q_t, o_intra, d_last, s, _SAMPLE_STATE_MODE)
    for b in range(bt):
        for h in range(DN_HEADS):
            o_ref[b, :, h * DN_HEAD_DIM:(h + 1) * DN_HEAD_DIM] = o[b * DN_HEADS + h]
    sn_ref[0] = s.reshape(bt, DN_HEADS, DN_HEAD_DIM, DN_HEAD_DIM)


def _delta_sample(qkv_b, gb_b, state_all, layer, carried, bt):
    bsz = qkv_b.shape[0]
    blk = lambda w: pl.BlockSpec((bt, SAMPLE_CHUNK, w), lambda i: (i, 0, 0))
    state = pl.BlockSpec((1, bt, DN_HEADS, DN_HEAD_DIM, DN_HEAD_DIM), lambda i: (layer, i, 0, 0, 0))
    in_specs, args, aliases = [blk(3 * DN_WIDTH), blk(LANES), state], [qkv_b, gb_b, state_all], {}
    if carried is not None:
        in_specs.append(pl.BlockSpec(memory_space=pl.ANY))
        args.append(carried)
        aliases = {3: 1}
    return pl.pallas_call(
        _delta_sample_kernel,
        grid=(bsz // bt,),
        in_specs=in_specs,
        out_specs=[blk(DN_WIDTH), state],
        out_shape=[
            jax.ShapeDtypeStruct((bsz, SAMPLE_CHUNK, DN_WIDTH), F32),
            jax.ShapeDtypeStruct(state_all.shape, F32),
        ],
        input_output_aliases=aliases,
        compiler_params=_cparams("arbitrary"),
        name="delta_sample",
    )(*args)


def _outproj_kernel(yc_ref, o_ref, z_ref, x_ref, g1_ref, nw_ref, w_ref, lg_ref, lb_ref, out_ref):
    o = o_ref[...].astype(F32)
    parts = []
    for h in range(DN_HEADS):
        oh = o[:, h * DN_HEAD_DIM:(h + 1) * DN_HEAD_DIM]
        parts.append(oh * lax.rsqrt(jnp.mean(oh * oh, -1, keepdims=True) + RMS_EPS) * nw_ref[...])
    og = jnp.concatenate(parts, axis=-1) * _silu(z_ref[...].astype(F32))
    mixed = jnp.concatenate([yc_ref[...].astype(BF16), og.astype(BF16)], axis=-1)
    m = jnp.dot(mixed, w_ref[...], preferred_element_type=F32)
    r = DEEPNORM_ALPHA * x_ref[...] + (1.0 + g1_ref[0]) * m
    out_ref[...] = _layer_norm(r, lg_ref[...], lb_ref[...])


def _outproj(yc, o, z_src, z_blk, x, mod3, tpg, norm_w, w_out, ln_g, ln_b, tm):
    n = x.shape[0]
    vec = lambda w: pl.BlockSpec((1, w), lambda i: (0, 0))
    return pl.pallas_call(
        _outproj_kernel,
        grid=(n // tm,),
        in_specs=[
            pl.BlockSpec((tm, CONV_WIDTH), lambda i: (i, 0)),
            pl.BlockSpec((tm, DN_WIDTH), lambda i: (i, 0)),
            pl.BlockSpec((tm, DN_WIDTH), lambda i: (i, z_blk)),
            pl.BlockSpec((tm, D_MODEL), lambda i: (i, 0)),
            _mod_spec(mod3, 2, tpg),
            vec(DN_HEAD_DIM),
            pl.BlockSpec((D_MODEL, D_MODEL), lambda i: (0, 0)),
            vec(D_MODEL), vec(D_MODEL),
        ],
        out_specs=pl.BlockSpec((tm, D_MODEL), lambda i: (i, 0)),
        out_shape=jax.ShapeDtypeStruct((n, D_MODEL), F32),
        compiler_params=_cparams("arbitrary"),
        name="outproj",
    )(yc, o, z_src, x, mod3, norm_w, w_out, ln_g, ln_b)


R_E1, R_E2, R_G1, R_G2, R_R1, R_R2 = range(6)
N_ROW_BUFS = 2


def _joint_specs(xp, xs, mod_p, mod_s, chunks, tm, seq):
    last = xp.shape[0] // tm - 1
    tpg = seq // tm
    assert xs.shape[0] == tm
    specs = [pl.BlockSpec((tm, D_MODEL), lambda i, *_: (jnp.minimum(i, last), 0)),
             pl.BlockSpec((tm, D_MODEL), lambda i, *_: (0, 0))]
    for c in chunks:
        specs.append(pl.BlockSpec((1, 1, D_MODEL), lambda i, *_, c=c: (jnp.minimum(i, last) // tpg, 0, c)))
        specs.append(pl.BlockSpec((1, mod_s.shape[1], D_MODEL), lambda i, *_, c=c: (0, 0, c)))
    return specs


def _tile_rows(mod_ref, rows):
    m = mod_ref[0]
    return jnp.concatenate([m] * (rows // m.shape[0]), axis=0)


def _joint_modulated(i, n_prompt_tiles, xp_ref, xs_ref, scp_ref, scs_ref, shp_ref, shs_ref, u_ref):
    tm = xp_ref.shape[0]

    @pl.when(i < n_prompt_tiles)
    def _():
        u_ref[...] = (xp_ref[...] * (1.0 + scp_ref[0]) + shp_ref[0]).reshape(u_ref.shape)

    @pl.when(i >= n_prompt_tiles)
    def _():
        u_ref[...] = (xs_ref[...] * (1.0 + _tile_rows(scs_ref, tm)) + _tile_rows(shs_ref, tm)).reshape(
            u_ref.shape)


def _router_kernel(xp_ref, xs_ref, scp_ref, scs_ref, shp_ref, shs_ref, wr_ref, br_ref,
                   route_ref, route_t_ref, cnt_ref, u_scr, *, n_prompt_tiles):
    tm = xp_ref.shape[0]
    _joint_modulated(pl.program_id(0), n_prompt_tiles, xp_ref, xs_ref, scp_ref, scs_ref, shp_ref, shs_ref,
                     u_scr)
    u = u_scr[...]
    logits = _mm(u, wr_ref[...], mode="x3") + br_ref[...]
    lane = lax.broadcasted_iota(jnp.int32, logits.shape, 1)
    valid = lane < N_EXPERTS
    logits = jnp.where(valid, logits, -jnp.inf)
    ex = jnp.exp(logits - jnp.max(logits, -1, keepdims=True))
    probs = jnp.where(valid, ex / jnp.sum(ex, -1, keepdims=True), -1.0)
    p1 = jnp.max(probs, -1, keepdims=True)
    i1 = jnp.min(jnp.where(probs == p1, lane, LANES), -1, keepdims=True)
    rest = jnp.where(lane == i1, -1.0, probs)
    p2 = jnp.max(rest, -1, keepdims=True)
    i2 = jnp.min(jnp.where(rest == p2, lane, LANES), -1, keepdims=True)
    tot = p1 + p2
    sel = jnp.where(lane == i1, 1.0, 0.0) + jnp.where(lane == i2, 1.0, 0.0)
    incl = _cumsum_chunks(sel, tm)
    excl = incl - sel
    r1 = jnp.sum(jnp.where(lane == i1, excl, 0.0), -1, keepdims=True)
    r2 = jnp.sum(jnp.where(lane == i2, excl, 0.0), -1, keepdims=True)
    cnt_ref[0] = incl[tm - 1:tm, :]
    route = jnp.zeros_like(logits)
    for j, val in enumerate((i1.astype(F32), i2.astype(F32), p1 / tot, p2 / tot, r1, r2)):
        route = jnp.where(lane == j, val, route)
    route_ref[...] = route
    route_t_ref[0] = jnp.transpose(route)[:SUBLANES, :]


def _router(xp, xs, mod_p, mod_s, seq, w_router, b_router, tm):
    ntp = xp.shape[0] // tm
    nt = ntp + 1
    return pl.pallas_call(
        functools.partial(_router_kernel, n_prompt_tiles=ntp),
        grid=(nt,),
        in_specs=_joint_specs(xp, xs, mod_p, mod_s, (4, 3), tm, seq) + [
            pl.BlockSpec((D_MODEL, LANES), lambda i: (0, 0)),
            pl.BlockSpec((1, LANES), lambda i: (0, 0)),
        ],
        out_specs=[pl.BlockSpec((tm, LANES), lambda i: (i, 0)),
                   pl.BlockSpec((1, SUBLANES, tm), lambda i: (i, 0, 0)),
                   pl.BlockSpec((1, 1, LANES), lambda i: (i, 0, 0))],
        out_shape=[jax.ShapeDtypeStruct((nt * tm, LANES), F32),
                   jax.ShapeDtypeStruct((nt, SUBLANES, tm), F32),
                   jax.ShapeDtypeStruct((nt, 1, LANES), F32)],
        scratch_shapes=[pltpu.VMEM((tm, D_MODEL), F32)],
        compiler_params=_cparams("arbitrary"),
        name="router",
    )(xp, xs, mod_p, mod_s, mod_p, mod_s, w_router, b_router)


def _dispatch_plan(route_t, counts, tme):
    n_rt, _, tm = route_t.shape
    n = n_rt * tm
    cnt = counts[:, 0, :N_EXPERTS].astype(jnp.int32)
    tile_off = jnp.cumsum(cnt, axis=0) - cnt
    tot = jnp.sum(cnt, axis=0)
    gsz = (tot + tme - 1) // tme * tme
    gend = jnp.cumsum(gsz)
    goff = gend - gsz
    base = (goff[None, :] + tile_off)[:, :, None]
    ids = jnp.arange(N_EXPERTS, dtype=jnp.int32)[None, :, None]
    field = lambda j: route_t[:, j:j + 1, :].astype(jnp.int32)
    pick = lambda e: jnp.sum(jnp.where(ids == e, base, 0), axis=1, keepdims=True)
    slot1 = (pick(field(R_E1)) + field(R_R1)).reshape(n)
    slot2 = (pick(field(R_E2)) + field(R_R2)).reshape(n)
    n_tiles = 2 * n // tme + N_EXPERTS
    start = jnp.arange(n_tiles, dtype=jnp.int32) * tme
    valid = (start < gend[-1]).astype(jnp.int32)
    blk = jnp.minimum(jnp.arange(n_tiles, dtype=jnp.int32), gend[-1] // tme - 1)
    tile_e = jnp.sum((blk * tme)[:, None] >= gend[None, :], axis=1).astype(jnp.int32)
    zoff = jnp.where(gsz > 0, gend - tme, (n_tiles + jnp.arange(N_EXPERTS, dtype=jnp.int32)) * tme)
    return slot1, slot2, zoff, tile_e, valid, blk, n_tiles


def _dispatch_kernel(s1_ref, s2_ref, zoff_ref, xp_ref, xs_ref, scp_ref, scs_ref, shp_ref, shs_ref,
                     ug_hbm, ubuf, zbuf, sem, *, n_prompt_tiles):
    i = pl.program_id(0)
    tm = xp_ref.shape[0]
    tme = zbuf.shape[0]

    n_steps = n_prompt_tiles + 1
    fill_sem = 2 * N_ROW_BUFS

    @pl.when(i == 0)
    def _():
        zbuf[...] = jnp.zeros_like(zbuf)
        fill = lambda e: pltpu.make_async_copy(
            zbuf, ug_hbm.at[pl.ds(pl.multiple_of(zoff_ref[e], 8), tme)], sem.at[fill_sem])
        for e in range(N_EXPERTS):
            fill(e).start()
        for e in range(N_EXPERTS):
            fill(e).wait()

    def drain(b):
        for stream in range(2):
            pltpu.make_async_copy(ubuf.at[b], ubuf.at[b], sem.at[2 * b + stream]).wait()

    buf = i % N_ROW_BUFS

    @pl.when(i >= N_ROW_BUFS)
    def _():
        drain(buf)

    _joint_modulated(i, n_prompt_tiles, xp_ref, xs_ref, scp_ref, scs_ref, shp_ref, shs_ref, ubuf.at[buf])
    base = i * tm

    def issue(j, carry):
        for k in range(SUBLANES):
            row = ubuf.at[buf, j, pl.ds(k, 1)]
            t = base + j * SUBLANES + k
            pltpu.make_async_copy(row, ug_hbm.at[pl.ds(s1_ref[t], 1)], sem.at[2 * buf]).start()
            pltpu.make_async_copy(row, ug_hbm.at[pl.ds(s2_ref[t], 1)], sem.at[2 * buf + 1]).start(priority=1)
        return carry

    lax.fori_loop(0, tm // SUBLANES, issue, 0)

    @pl.when(i == n_steps - 1)
    def _():
        for b in range(min(N_ROW_BUFS, n_steps)):
            drain(b)


def _dispatch(xp, xs, mod_p, mod_s, seq, slot1, slot2, zoff, tm, tme, rows):
    ntp = xp.shape[0] // tm
    return pl.pallas_call(
        functools.partial(_dispatch_kernel, n_prompt_tiles=ntp),
        grid_spec=pltpu.PrefetchScalarGridSpec(
            num_scalar_prefetch=3,
            grid=(ntp + 1,),
            in_specs=_joint_specs(xp, xs, mod_p, mod_s, (4, 3), tm, seq),
            out_specs=pl.BlockSpec(memory_space=pl.ANY),
            scratch_shapes=[pltpu.VMEM((N_ROW_BUFS, tm // SUBLANES, SUBLANES, D_MODEL), F32),
                            pltpu.VMEM((tme, D_MODEL), F32),
                            pltpu.SemaphoreType.DMA((2 * N_ROW_BUFS + 1,))],
        ),
        out_shape=jax.ShapeDtypeStruct((rows, D_MODEL), F32),
        compiler_params=_cparams("arbitrary"),
        name="dispatch",
    )(slot1, slot2, zoff, xp, xs, mod_p, mod_s, mod_p, mod_s)


def _expert_kernel(te_ref, tv_ref, tb_ref, ug_ref, wg_ref, wu_ref, wd_ref, ys_ref):
    i = pl.program_id(0)

    @pl.when(tv_ref[i] == 1)
    def _():
        u = ug_ref[...].astype(BF16)
        h = _silu(jnp.dot(u, wg_ref[0].astype(BF16), preferred_element_type=F32)) * jnp.dot(
            u, wu_ref[0].astype(BF16), preferred_element_type=F32)
        ys_ref[...] = jnp.dot(h.astype(BF16), wd_ref[0].astype(BF16), preferred_element_type=F32)

    @pl.when(tv_ref[i] == 0)
    def _():
        ys_ref[...] = jnp.zeros_like(ys_ref)


def _experts(ug, tile_e, valid, blk, w_gate, w_up, w_down, tme, n_tiles):
    d_e = w_gate.shape[-1]
    return pl.pallas_call(
        _expert_kernel,
        grid_spec=pltpu.PrefetchScalarGridSpec(
            num_scalar_prefetch=3,
            grid=(n_tiles,),
            in_specs=[
                pl.BlockSpec((tme, D_MODEL), lambda i, te, tv, tb: (tb[i], 0)),
                pl.BlockSpec((1, D_MODEL, d_e), lambda i, te, tv, tb: (te[i], 0, 0)),
                pl.BlockSpec((1, D_MODEL, d_e), lambda i, te, tv, tb: (te[i], 0, 0)),
                pl.BlockSpec((1, d_e, D_MODEL), lambda i, te, tv, tb: (te[i], 0, 0)),
            ],
            out_specs=pl.BlockSpec((tme, D_MODEL), lambda i, te, tv, tb: (i, 0)),
        ),
        out_shape=jax.ShapeDtypeStruct((n_tiles * tme, D_MODEL), F32),
        compiler_params=_cparams("arbitrary"),
        name="experts",
    )(tile_e, valid, blk, ug, w_gate, w_up, w_down)


def _combine_kernel(s1_ref, s2_ref, ys_hbm, route_ref, xp_ref, xs_ref, g2p_ref, g2s_ref, lg_ref, lb_ref,
                    outp_ref, outs_ref, y1buf, y2buf, sem, *, n_prompt_tiles):
    i = pl.program_id(0)
    tm = xp_ref.shape[0]
    n_steps = n_prompt_tiles + 1

    def gather(tile, b):
        def issue(j, carry):
            for k in range(SUBLANES):
                t = tile * tm + j * SUBLANES + k
                pltpu.make_async_copy(ys_hbm.at[pl.ds(s1_ref[t], 1)], y1buf.at[b, j, pl.ds(k, 1)],
                                      sem.at[2 * b]).start()
                pltpu.make_async_copy(ys_hbm.at[pl.ds(s2_ref[t], 1)], y2buf.at[b, j, pl.ds(k, 1)],
                                      sem.at[2 * b + 1]).start(priority=1)
            return carry

        lax.fori_loop(0, tm // SUBLANES, issue, 0)

    buf = i % N_ROW_BUFS

    @pl.when(i == 0)
    def _():
        gather(0, 0)

    @pl.when(i + 1 < n_steps)
    def _():
        gather(i + 1, (i + 1) % N_ROW_BUFS)

    pltpu.make_async_copy(y1buf.at[buf], y1buf.at[buf], sem.at[2 * buf]).wait()
    pltpu.make_async_copy(y2buf.at[buf], y2buf.at[buf], sem.at[2 * buf + 1]).wait()
    route = route_ref[...]
    y1 = y1buf[buf].reshape(tm, D_MODEL)
    y2 = y2buf[buf].reshape(tm, D_MODEL)
    f = route[:, R_G1:R_G1 + 1] * y1 + route[:, R_G2:R_G2 + 1] * y2

    @pl.when(i < n_prompt_tiles)
    def _():
        r = DEEPNORM_ALPHA * xp_ref[...] + (1.0 + g2p_ref[0]) * f
        outp_ref[...] = _layer_norm(r, lg_ref[...], lb_ref[...])

    @pl.when(i >= n_prompt_tiles)
    def _():
        r = DEEPNORM_ALPHA * xs_ref[...] + (1.0 + _tile_rows(g2s_ref, tm)) * f
        outs_ref[...] = _layer_norm(r, lg_ref[...], lb_ref[...])


def _combine(ys, slot1, slot2, route, xp, xs, mod_p, mod_s, seq, ln_g, ln_b, tm):
    ntp = xp.shape[0] // tm
    vec = lambda w: pl.BlockSpec((1, w), lambda i, *_: (0, 0))
    x_specs = _joint_specs(xp, xs, mod_p, mod_s, (5,), tm, seq)
    return pl.pallas_call(
        functools.partial(_combine_kernel, n_prompt_tiles=ntp),
        grid_spec=pltpu.PrefetchScalarGridSpec(
            num_scalar_prefetch=2,
            grid=(ntp + 1,),
            in_specs=[
                pl.BlockSpec(memory_space=pl.ANY),
                pl.BlockSpec((tm, LANES), lambda i, *_: (i, 0)),
            ] + x_specs + [vec(D_MODEL), vec(D_MODEL)],
            out_specs=[x_specs[0], x_specs[1]],
            scratch_shapes=[pltpu.VMEM((N_ROW_BUFS, tm // SUBLANES, SUBLANES, D_MODEL), F32),
                            pltpu.VMEM((N_ROW_BUFS, tm // SUBLANES, SUBLANES, D_MODEL), F32),
                            pltpu.SemaphoreType.DMA((2 * N_ROW_BUFS,))],
        ),
        out_shape=[jax.ShapeDtypeStruct(xp.shape, F32), jax.ShapeDtypeStruct(xs.shape, F32)],
        compiler_params=_cparams("arbitrary"),
        name="combine",
    )(slot1, slot2, ys, route, xp, xs, mod_p, mod_s, ln_g, ln_b)


def _moe(xp, xs, mod_p, mod_s, seq, w_router, b_router, w_gate, w_up, w_down, ln_g, ln_b, tm, tme):
    route, route_t, counts = _router(xp, xs, mod_p, mod_s, seq, w_router, b_router, tm)
    slot1, slot2, zoff, tile_e, valid, blk, n_tiles = _dispatch_plan(route_t, counts, tme)
    ug = _dispatch(xp, xs, mod_p, mod_s, seq, slot1, slot2, zoff, tm, tme, (n_tiles + N_EXPERTS) * tme)
    ys = _experts(ug, tile_e, valid, blk, w_gate, w_up, w_down, tme, n_tiles)
    return _combine(ys, slot1, slot2, route, xp, xs, mod_p, mod_s, seq, ln_g, ln_b, tm)


def _ffn_kernel(x_ref, sc_ref, sh_ref, g2_ref, wg_ref, wu_ref, wd_ref, lg_ref, lb_ref, out_ref, *, tf):
    x = x_ref[...]
    u = (x * (1.0 + sc_ref[0]) + sh_ref[0]).astype(BF16)
    y = None
    for f0 in range(0, wg_ref.shape[1], tf):
        h = _silu(jnp.dot(u, wg_ref[:, f0:f0 + tf], preferred_element_type=F32)) * jnp.dot(
            u, wu_ref[:, f0:f0 + tf], preferred_element_type=F32)
        part = jnp.dot(h.astype(BF16), wd_ref[f0:f0 + tf, :], preferred_element_type=F32)
        y = part if y is None else y + part
    r = DEEPNORM_ALPHA * x + (1.0 + g2_ref[0]) * y
    out_ref[...] = _layer_norm(r, lg_ref[...], lb_ref[...])


def _ffn(x, mod3, tpg, w_gate, w_up, w_down, ln_g, ln_b, tm, tf):
    n = x.shape[0]
    vec = lambda w: pl.BlockSpec((1, w), lambda i: (0, 0))
    resident = lambda a: pl.BlockSpec(a.shape, lambda i: (0, 0), pipeline_mode=pl.Buffered(1))
    return pl.pallas_call(
        functools.partial(_ffn_kernel, tf=tf),
        grid=(n // tm,),
        in_specs=[
            pl.BlockSpec((tm, D_MODEL), lambda i: (i, 0)),
            _mod_spec(mod3, 4, tpg),
            _mod_spec(mod3, 3, tpg),
            _mod_spec(mod3, 5, tpg),
            resident(w_gate), resident(w_up), resident(w_down),
            vec(D_MODEL), vec(D_MODEL),
        ],
        out_specs=pl.BlockSpec((tm, D_MODEL), lambda i: (i, 0)),
        out_shape=jax.ShapeDtypeStruct((n, D_MODEL), F32),
        compiler_params=_cparams("arbitrary"),
        name="ffn_dense",
    )(x, mod3, mod3, mod3, w_gate, w_up, w_down, ln_g, ln_b)


def _token_mixer(l, x, mod3, rows_per_group, prm, tm, mixer_fn):
    yc, o, z_src, z_blk, st = mixer_fn(l, x, mod3)
    x = _outproj(yc, o, z_src, z_blk, x, mod3, rows_per_group // tm, prm["norm_w"][l], prm["w_out"][l],
                 prm["ln_g"][l, 0:1], prm["ln_b"][l, 0:1], tm)
    return x, st


def kernel(x_prompt, x_sample, state_conv, state_dn_conv, state_dn, c_prompt, c_sample, w_in, conv_w,
           dn_conv_w, a_log, dt_bias, dn_norm_w, w_out, w_ada, b_ada, ln_g, ln_b, w_ff_gate, w_ff_up,
           w_ff_down, w_router, b_router, w_exp_gate, w_exp_up, w_exp_down):
    bp, seq, d = x_prompt.shape
    bs, steps, _ = x_sample.shape
    pad_lanes = LANES - N_EXPERTS
    hp = jnp.zeros((DEPTH, 2, LANES), F32)
    hp = hp.at[:, 0, G_LANE:].set(a_log).at[:, 1, G_LANE:].set(dt_bias)
    w_in_t = jnp.swapaxes(w_in, 1, 2)
    prm = {
        "w_main": w_in_t[:, :P_MAIN].astype(BF16),
        "w_small": w_in_t[:, w_in_t.shape[1] - LANES:],
        "norm_w": dn_norm_w.reshape(DEPTH, 1, DN_HEAD_DIM),
        "w_out": w_out.astype(BF16),
        "ln_g": ln_g, "ln_b": ln_b,
        "w_ff_gate": w_ff_gate.astype(BF16), "w_ff_up": w_ff_up.astype(BF16),
        "w_ff_down": w_ff_down.astype(BF16),
        "w_router": jnp.pad(w_router, ((0, 0), (0, 0), (0, pad_lanes))),
        "b_router": jnp.pad(b_router, ((0, 0), (0, pad_lanes)))[:, None, :],
        "w_exp_gate": w_exp_gate, "w_exp_up": w_exp_up, "w_exp_down": w_exp_down,
    }

    mod = _ada(jnp.concatenate([c_prompt, c_sample], axis=0), w_ada, b_ada)
    mod_p = [mod[l, :bp].reshape(bp, 1, 6 * d) for l in range(DEPTH)]
    mod_s = [mod[l, bp:].reshape(1, bs, 6 * d) for l in range(DEPTH)]

    tm_p, tc_delta = 512, 512
    zc = jnp.zeros((bp, CONV_K - 1, CONV_WIDTH), F32)
    zd = jnp.zeros((bp, DN_CONV_K - 1, 3 * DN_WIDTH), F32)
    zs = jnp.zeros((bp, DN_HEADS, DN_HEAD_DIM, DN_HEAD_DIM), F32)

    def mixer_prompt(l, x, mod3):
        yc, qkv, z, gb, ncb, ndb = _inproj_pre_prompt(
            x, mod3, prm["w_main"][l], prm["w_small"][l], conv_w[l], dn_conv_w[l], hp[l], zc, zd,
            bp, seq, tm_p)
        o, s_new = _delta_prompt(qkv, gb, zs, bp, seq, tc_delta)
        return yc, o, z, 0, (ncb, ndb, s_new)

    bt_pre, bt_delta = 32, 16
    dn_states = []

    def mixer_sample(l, x, mod3):
        pm, ps = _inproj(x, mod3, steps, prm["w_main"][l], prm["w_small"][l], bs)
        cb_tm = jnp.transpose(state_conv[l], (1, 0, 2))
        db_tm = jnp.transpose(state_dn_conv[l], (1, 0, 2))
        yc, qkv, gb, ncb, ndb = _pre_sample(pm, ps, conv_w[l], dn_conv_w[l], hp[l], cb_tm, db_tm,
                                            bs, steps, bt_pre)
        to_b = lambda a: jnp.pad(jnp.transpose(a, (1, 0, 2)), ((0, 0), (0, SAMPLE_CHUNK - steps), (0, 0)))
        o_b, s_all = _delta_sample(to_b(qkv), to_b(gb), state_dn, l, dn_states[-1] if dn_states else None,
                                   bt_delta)
        dn_states.append(s_all)
        o = jnp.transpose(o_b[:, :steps], (1, 0, 2)).reshape(steps * bs, DN_WIDTH)
        return (yc.reshape(steps * bs, CONV_WIDTH), o, pm, OFF_Z // DN_WIDTH,
                (jnp.transpose(ncb, (1, 0, 2)), jnp.transpose(ndb, (1, 0, 2))))

    x_p = x_prompt.reshape(bp * seq, d)
    x_s = jnp.transpose(x_sample, (1, 0, 2)).reshape(steps * bs, d)
    n_s = steps * bs
    st_p, st_s = [], []
    for l in range(DEPTH):
        x_p, st = _token_mixer(l, x_p, mod_p[l], seq, prm, tm_p, mixer_prompt)
        st_p.append(st)
        x_s, st = _token_mixer(l, x_s, mod_s[l], n_s, prm, bs, mixer_sample)
        st_s.append(st)
        j = l // 2
        ln = (prm["ln_g"][l, 1:2], prm["ln_b"][l, 1:2])
        if l % 2 == 0:
            ffn_w = (prm["w_ff_gate"][j], prm["w_ff_up"][j], prm["w_ff_down"][j])
            x_p = _ffn(x_p, mod_p[l], seq // tm_p, *ffn_w, *ln, tm_p, 256)
            x_s = _ffn(x_s, mod_s[l], n_s // bs, *ffn_w, *ln, bs, w_ff_gate.shape[-1])
        else:
            x_p, x_s = _moe(x_p, x_s, mod_p[l], mod_s[l], seq, prm["w_router"][j], prm["b_router"][j],
                            prm["w_exp_gate"][j], prm["w_exp_up"][j], prm["w_exp_down"][j], *ln, tm_p, 512)
    y_p = x_p
    y_s = jnp.transpose(x_s.reshape(steps, bs, d), (1, 0, 2))

    stack = lambda sts, i: jnp.stack([s[i] for s in sts])
    return (y_p.reshape(bp, seq, d), y_s,
            stack(st_p, 0), stack(st_p, 1), stack(st_p, 2),
            stack(st_s, 0), stack(st_s, 1), dn_states[-1])
```

```python
import functools
import math

import jax
import jax.numpy as jnp
from jax import lax
from jax.experimental import pallas as pl
from jax.experimental.pallas import tpu as pltpu

F32 = jnp.float32
BF16 = jnp.bfloat16

D_MODEL = 1024
DEPTH = 2
CONV_WIDTH = 512
CONV_K = 3
DN_HEADS = 4
DN_HEAD_DIM = 128
DN_WIDTH = DN_HEADS * DN_HEAD_DIM
DN_CONV_K = 4
DN_CHUNK = 64
N_EXPERTS = 8
DEEPNORM_ALPHA = (2.0 * DEPTH) ** 0.25
LN_EPS = 1e-5
RMS_EPS = 1e-6

OFF_CC = CONV_WIDTH
OFF_CH = 2 * CONV_WIDTH
OFF_QKV = 3 * CONV_WIDTH
OFF_Z = OFF_QKV + 3 * DN_WIDTH
P_MAIN = OFF_Z + DN_WIDTH
LANES = 128
SUBLANES = 8
BETA_LANE = LANES - 2 * DN_HEADS
G_LANE = LANES - DN_HEADS
SAMPLE_CHUNK = 8

VMEM_LIMIT = 56 * 1024 * 1024


def _cparams(*sem):
    return pltpu.CompilerParams(dimension_semantics=sem, vmem_limit_bytes=VMEM_LIMIT)


def _silu(x):
    return x * jax.nn.sigmoid(x)


def _layer_norm(r, g, b):
    mu = jnp.mean(r, -1, keepdims=True)
    xc = r - mu
    var = jnp.mean(xc * xc, -1, keepdims=True)
    return xc * lax.rsqrt(var + LN_EPS) * g + b


def _split_bf16(a):
    hi = a.astype(BF16)
    lo = (a - hi.astype(F32)).astype(BF16)
    return hi, lo


_NN = (((1,), (0,)), ((), ()))
_NT = (((1,), (1,)), ((), ()))
_BNN = (((2,), (1,)), ((0,), (0,)))
_BNT = (((2,), (2,)), ((0,), (0,)))
_BTN = (((1,), (1,)), ((0,), (0,)))


def _mm(a, b, dims=_NN, mode="bf16"):
    if mode == "f32":
        return lax.dot_general(a, b, dims, precision=lax.Precision.HIGHEST, preferred_element_type=F32)
    if mode == "bf16":
        return lax.dot_general(a.astype(BF16), b.astype(BF16), dims, preferred_element_type=F32)
    ah, al = _split_bf16(a)
    bh, bl = _split_bf16(b)
    d = functools.partial(lax.dot_general, dimension_numbers=dims, preferred_element_type=F32)
    return d(ah, bh) + (d(ah, bl) + d(al, bh))


def _ada_kernel(c_ref, w_ref, b_ref, o_ref):
    s = _silu(c_ref[...]).astype(BF16)
    o_ref[0] = jnp.dot(s, w_ref[0].astype(BF16), preferred_element_type=F32) + b_ref[0]


def _ada(c_all, w_ada, b_ada):
    rows = c_all.shape[0]
    tn = 1536
    return pl.pallas_call(
        _ada_kernel,
        grid=(DEPTH, 6 * D_MODEL // tn),
        in_specs=[
            pl.BlockSpec((rows, D_MODEL), lambda l, j: (0, 0)),
            pl.BlockSpec((1, D_MODEL, tn), lambda l, j: (l, 0, j)),
            pl.BlockSpec((1, 1, tn), lambda l, j: (l, 0, j)),
        ],
        out_specs=pl.BlockSpec((1, rows, tn), lambda l, j: (l, 0, j)),
        out_shape=jax.ShapeDtypeStruct((DEPTH, rows, 6 * D_MODEL), F32),
        compiler_params=_cparams("arbitrary", "arbitrary"),
        name="ada",
    )(c_all, w_ada, b_ada.reshape(DEPTH, 1, 6 * D_MODEL))


def _mod_spec(mod3, chunk, tiles_per_group):
    r = mod3.shape[1]
    return pl.BlockSpec((1, r, D_MODEL), lambda i, *_: (i // tiles_per_group, 0, chunk))


def _inproj_kernel(x_ref, sc_ref, sh_ref, wm_ref, ws_ref, pm_ref, ps_ref):
    u = x_ref[...] * (1.0 + sc_ref[0]) + sh_ref[0]
    pm_ref[...] = lax.dot_general(u.astype(BF16), wm_ref[...], _NT, preferred_element_type=F32)
    ps_ref[...] = _mm(u, ws_ref[...], _NT, mode="x3")


def _inproj(x, mod3, tpg, w_main, w_small, tm):
    n = x.shape[0]
    return pl.pallas_call(
        _inproj_kernel,
        grid=(n // tm,),
        in_specs=[
            pl.BlockSpec((tm, D_MODEL), lambda i: (i, 0)),
            _mod_spec(mod3, 1, tpg),
            _mod_spec(mod3, 0, tpg),
            pl.BlockSpec((P_MAIN, D_MODEL), lambda i: (0, 0)),
            pl.BlockSpec((LANES, D_MODEL), lambda i: (0, 0)),
        ],
        out_specs=[
            pl.BlockSpec((tm, P_MAIN), lambda i: (i, 0)),
            pl.BlockSpec((tm, LANES), lambda i: (i, 0)),
        ],
        out_shape=[
            jax.ShapeDtypeStruct((n, P_MAIN), F32),
            jax.ShapeDtypeStruct((n, LANES), F32),
        ],
        compiler_params=_cparams("arbitrary"),
        name="inproj",
    )(x, mod3, mod3, w_main, w_small)


def _gates(ps, hp):
    lane = lax.broadcasted_iota(jnp.int32, ps.shape, 1)
    beta = jax.nn.sigmoid(ps)
    g = -jnp.exp(hp[0:1, :]) * jax.nn.softplus(ps + hp[1:2, :])
    return jnp.where(lane >= G_LANE, g, jnp.where(lane >= BETA_LANE, beta, 0.0))


def _qkv_finish(y, out_ref, idx, first_head=0):
    y = _silu(y)
    for j in range(y.shape[1] // DN_HEAD_DIM):
        h = first_head + j
        seg = y[:, j * DN_HEAD_DIM:(j + 1) * DN_HEAD_DIM]
        if h < 2 * DN_HEADS:
            seg = seg * lax.rsqrt(jnp.sum(seg * seg, -1, keepdims=True) + RMS_EPS)
            if h < DN_HEADS:
                seg = seg * (DN_HEAD_DIM ** -0.5)
        out_ref[idx + (slice(None), slice(h * DN_HEAD_DIM, (h + 1) * DN_HEAD_DIM))] = seg


def _shifted(x, tail_ref, i, row):
    s = pltpu.roll(x, i, 0)
    top = s[0:SUBLANES]
    for r in range(i):
        top = jnp.where(row == r, tail_ref[8 - i + r:8 - i + r + 1, :], top)
    return jnp.concatenate([top, s[SUBLANES:]], axis=0)


def _inproj_pre_prompt_kernel(x_ref, sc_ref, sh_ref, wm_ref, ws_ref, cw_ref, dw_ref, hp_ref, cb_ref, db_ref,
                              yc_ref, qkv_ref, z_ref, gb_ref, ncb_ref, ndb_ref, tc_scr, td_scr):
    t = pl.program_id(1)
    tm = x_ref.shape[0]

    @pl.when(t == 0)
    def _():
        tc_scr[8 - (CONV_K - 1):8, :] = cb_ref[0]
        td_scr[8 - (DN_CONV_K - 1):8, :] = db_ref[0]

    u_f32 = x_ref[...] * (1.0 + sc_ref[0]) + sh_ref[0]
    u = u_f32.astype(BF16)
    proj = lambda lo, hi: lax.dot_general(u, wm_ref[lo:hi, :], _NT, preferred_element_type=F32)

    row = lax.broadcasted_iota(jnp.int32, (SUBLANES, DN_WIDTH), 0)

    def dn_part(xp, part):
        cols = slice(part * DN_WIDTH, (part + 1) * DN_WIDTH)
        tail = td_scr.at[:, cols]
        yp = _shifted(xp, tail, 3, row) * dw_ref[0:1, cols]
        yp = yp + _shifted(xp, tail, 2, row) * dw_ref[1:2, cols]
        yp = yp + _shifted(xp, tail, 1, row) * dw_ref[2:3, cols]
        yp = yp + xp * dw_ref[3:4, cols]
        _qkv_finish(yp, qkv_ref, (), first_head=part * DN_HEADS)
        return xp[tm - (DN_CONV_K - 1):tm, :]

    qkv_cols = lambda part: (OFF_QKV + part * DN_WIDTH, OFF_QKV + (part + 1) * DN_WIDTH)
    x_q = proj(*qkv_cols(0))
    x_k = proj(*qkv_cols(1))
    last_q = dn_part(x_q, 0)
    x_v = proj(*qkv_cols(2))
    last_k = dn_part(x_k, 1)
    p_c = proj(OFF_CC, OFF_CH)
    last_v = dn_part(x_v, 2)
    p_h = proj(OFF_CH, OFF_QKV)
    cgh = p_c * p_h
    p_b = proj(0, OFF_CC)
    y = _shifted(cgh, tc_scr, 2, row) * cw_ref[0:1, :]
    y = y + _shifted(cgh, tc_scr, 1, row) * cw_ref[1:2, :]
    y = y + cgh * cw_ref[2:3, :]
    p_z = proj(OFF_Z, P_MAIN)
    yc_ref[...] = (p_b * y).astype(yc_ref.dtype)
    last_c = cgh[tm - (CONV_K - 1):tm, :]
    small = _mm(u_f32, ws_ref[...], _NT, mode="bf16")
    z_ref[...] = p_z.astype(z_ref.dtype)
    gb_ref[...] = _gates(small, hp_ref[...])

    last_d = jnp.concatenate([last_q, last_k, last_v], axis=-1)
    tc_scr[8 - (CONV_K - 1):8, :] = last_c
    td_scr[8 - (DN_CONV_K - 1):8, :] = last_d

    @pl.when(t == pl.num_programs(1) - 1)
    def _():
        ncb_ref[0] = last_c
        ndb_ref[0] = last_d


def _inproj_pre_prompt(x, mod3, w_main, w_small, conv_w, dn_conv_w, hp, conv_buf, dn_buf, bsz, seq, tm):
    n = x.shape[0]
    nt = seq // tm
    rows = lambda w: pl.BlockSpec((tm, w), lambda b, t: (b * nt + t, 0))
    full = lambda a: pl.BlockSpec(a.shape, lambda b, t: (0,) * a.ndim)
    per_b = lambda k, w: pl.BlockSpec((1, k, w), lambda b, t: (b, 0, 0))
    mod = lambda chunk: pl.BlockSpec((1, 1, D_MODEL), lambda b, t: (b, 0, chunk))
    return pl.pallas_call(
        _inproj_pre_prompt_kernel,
        grid=(bsz, nt),
        in_specs=[rows(D_MODEL), mod(1), mod(0), full(w_main), full(w_small),
                  full(conv_w), full(dn_conv_w), full(hp),
                  per_b(CONV_K - 1, CONV_WIDTH), per_b(DN_CONV_K - 1, 3 * DN_WIDTH)],
        out_specs=[rows(CONV_WIDTH), rows(3 * DN_WIDTH), rows(DN_WIDTH), rows(LANES),
                   per_b(CONV_K - 1, CONV_WIDTH), per_b(DN_CONV_K - 1, 3 * DN_WIDTH)],
        out_shape=[
            jax.ShapeDtypeStruct((n, CONV_WIDTH), BF16),
            jax.ShapeDtypeStruct((n, 3 * DN_WIDTH), F32),
            jax.ShapeDtypeStruct((n, DN_WIDTH), BF16),
            jax.ShapeDtypeStruct((n, LANES), F32),
            jax.ShapeDtypeStruct((bsz, CONV_K - 1, CONV_WIDTH), F32),
            jax.ShapeDtypeStruct((bsz, DN_CONV_K - 1, 3 * DN_WIDTH), F32),
        ],
        scratch_shapes=[pltpu.VMEM((8, CONV_WIDTH), F32), pltpu.VMEM((8, 3 * DN_WIDTH), F32)],
        compiler_params=_cparams("arbitrary", "arbitrary"),
        name="inproj_pre_prompt",
    )(x, mod3, mod3, w_main, w_small, conv_w, dn_conv_w, hp, conv_buf, dn_buf)


def _pre_sample_kernel(pm_ref, ps_ref, cw_ref, dw_ref, hp_ref, cb_ref, db_ref,
                       yc_ref, qkv_ref, gb_ref, ncb_ref, ndb_ref):
    steps = pm_ref.shape[0]
    cgh = [pm_ref[t, :, OFF_CC:OFF_CH] * pm_ref[t, :, OFF_CH:OFF_QKV] for t in range(steps)]
    ext = [cb_ref[i] for i in range(CONV_K - 1)] + cgh
    for t in range(steps):
        y = ext[t] * cw_ref[0:1, :]
        for i in range(1, CONV_K):
            y = y + ext[t + i] * cw_ref[i:i + 1, :]
        yc_ref[t] = pm_ref[t, :, 0:OFF_CC] * y
    for i in range(CONV_K - 1):
        ncb_ref[i] = ext[len(ext) - (CONV_K - 1) + i]

    xq = [pm_ref[t, :, OFF_QKV:OFF_Z] for t in range(steps)]
    extq = [db_ref[i] for i in range(DN_CONV_K - 1)] + xq
    for t in range(steps):
        y = extq[t] * dw_ref[0:1, :]
        for i in range(1, DN_CONV_K):
            y = y + extq[t + i] * dw_ref[i:i + 1, :]
        _qkv_finish(y, qkv_ref, (t,))
        gb_ref[t] = _gates(ps_ref[t], hp_ref[...])
    for i in range(DN_CONV_K - 1):
        ndb_ref[i] = extq[len(extq) - (DN_CONV_K - 1) + i]


def _pre_sample(pm, ps, conv_w, dn_conv_w, hp, conv_buf_tm, dn_buf_tm, bsz, steps, bt):
    slab = lambda k, w: pl.BlockSpec((k, bt, w), lambda i: (0, i, 0))
    full = lambda a: pl.BlockSpec(a.shape, lambda i: (0,) * a.ndim)
    return pl.pallas_call(
        _pre_sample_kernel,
        grid=(bsz // bt,),
        in_specs=[slab(steps, P_MAIN), slab(steps, LANES), full(conv_w), full(dn_conv_w), full(hp),
                  slab(CONV_K - 1, CONV_WIDTH), slab(DN_CONV_K - 1, 3 * DN_WIDTH)],
        out_specs=[slab(steps, CONV_WIDTH), slab(steps, 3 * DN_WIDTH), slab(steps, LANES),
                   slab(CONV_K - 1, CONV_WIDTH), slab(DN_CONV_K - 1, 3 * DN_WIDTH)],
        out_shape=[
            jax.ShapeDtypeStruct((steps, bsz, CONV_WIDTH), F32),
            jax.ShapeDtypeStruct((steps, bsz, 3 * DN_WIDTH), F32),
            jax.ShapeDtypeStruct((steps, bsz, LANES), F32),
            jax.ShapeDtypeStruct((CONV_K - 1, bsz, CONV_WIDTH), F32),
            jax.ShapeDtypeStruct((DN_CONV_K - 1, bsz, 3 * DN_WIDTH), F32),
        ],
        compiler_params=_cparams("arbitrary"),
        name="pre_sample",
    )(pm.reshape(steps, bsz, P_MAIN), ps.reshape(steps, bsz, LANES), conv_w, dn_conv_w, hp,
      conv_buf_tm, dn_buf_tm)


def _cumsum_chunks(x, c):
    row = lax.broadcasted_iota(jnp.int32, x.shape, 0) & (c - 1)
    s = 1
    while s < c:
        x = x + jnp.where(row >= s, pltpu.roll(x, s, 0), 0.0)
        s *= 2
    return x


def _gdn_problems(qkv, gb, c):
    n = qkv.shape[0] // c
    gc = _cumsum_chunks(gb, c)
    q, k, v, g_b, beta_b = [], [], [], [], []
    for ci in range(n):
        rs = slice(ci * c, (ci + 1) * c)
        for h in range(DN_HEADS):
            ls = lambda base: slice(base + h * DN_HEAD_DIM, base + (h + 1) * DN_HEAD_DIM)
            q.append(qkv[rs, ls(0)])
            k.append(qkv[rs, ls(DN_WIDTH)])
            v.append(qkv[rs, ls(2 * DN_WIDTH)])
            g_b.append(jnp.broadcast_to(gc[rs, G_LANE + h:G_LANE + h + 1], (c, DN_HEAD_DIM)))
            beta_b.append(jnp.broadcast_to(gb[rs, BETA_LANE + h:BETA_LANE + h + 1], (c, DN_HEAD_DIM)))
    return tuple(jnp.stack(a) for a in (q, k, v, g_b, beta_b))


def _unit_lower_solve(m, rhs, c, small, mode):
    if small:
        sol = rhs
        for j in range(c - 1):
            sol = sol - m[:, :, j:j + 1] * sol[:, j:j + 1, :]
        return sol
    row = lax.broadcasted_iota(jnp.int32, m.shape, 1)
    col = lax.broadcasted_iota(jnp.int32, m.shape, 2)
    p = -m
    t = jnp.where(row == col, 1.0, 0.0) + p
    p = _mm(p, p, _BNN, "bf16")
    levels = int(math.log2(c))
    for lvl in range(1, levels):
        if lvl < levels - 1:
            y = _mm(jnp.concatenate([t, p], axis=1), p, _BNN, "bf16")
            t = t + y[:, :c]
            p = y[:, c:]
        else:
            t = t + _mm(t, p, _BNN, "bf16")
    x = _mm(t, rhs, _BNN, "bf16")
    resid = rhs - x - _mm(m, x, _BNN, mode)
    return x + _mm(t, resid, _BNN, "bf16")


def _gdn_intra(q, k, v, g_b, beta_b, *, c, small, mode_gram, mode_solve, mode_apply):
    n = q.shape[0]
    row = lax.broadcasted_iota(jnp.int32, (n, c, c), 1)
    col = lax.broadcasted_iota(jnp.int32, (n, c, c), 2)
    g_cc = g_b[:, :, :c]
    g_row = jnp.sum(jnp.where(row == col, g_cc, 0.0), axis=1, keepdims=True)
    gamma = jnp.exp(jnp.where(row >= col, g_cc - g_row, -jnp.inf))
    kbeta = k * beta_b
    exp_g = jnp.exp(g_b)
    gram = _mm(jnp.concatenate([kbeta, q], axis=1), k, _BNT, mode_gram)
    m = jnp.where(row > col, gram[:, :c] * gamma, 0.0)
    attn = gram[:, c:] * gamma
    rhs = jnp.concatenate([kbeta * exp_g, v * beta_b], axis=-1)
    sol = _unit_lower_solve(m, rhs, c, small, mode_solve)
    g_last = g_b[:, c - 1:c, :]
    k_dec = k * jnp.exp(g_last - g_b)
    kd = _mm(k_dec, sol, _BTN, mode_apply)
    at = _mm(attn, sol, _BNN, mode_apply)
    q_t = q * exp_g - at[..., :DN_HEAD_DIM]
    return kd[..., :DN_HEAD_DIM], kd[..., DN_HEAD_DIM:], q_t, at[..., DN_HEAD_DIM:], jnp.exp(g_last)


def _gdn_state_step(a_mat, b_mat, q_t, o_intra, d_last, s, mode):
    r = _mm(jnp.concatenate([a_mat, q_t], axis=1), s, _BNN, mode)
    s_new = s * d_last - r[:, :DN_HEAD_DIM] + b_mat
    return r[:, DN_HEAD_DIM:] + o_intra, s_new


_PROMPT_MODES = dict(mode_gram="bf16", mode_solve="x3", mode_apply="bf16")
_PROMPT_STATE_MODE = "bf16"
_SAMPLE_MODES = dict(mode_gram="bf16", mode_solve="bf16", mode_apply="bf16")
_SAMPLE_STATE_MODE = "bf16"


def _delta_prompt_kernel(qkv_ref, gb_ref, s0_ref, o_ref, sn_ref, s_scr):
    t = pl.program_id(1)
    tc = qkv_ref.shape[0]

    @pl.when(t == 0)
    def _():
        s_scr[...] = s0_ref[0]

    probs = _gdn_problems(qkv_ref[...], gb_ref[...], DN_CHUNK)
    a_mat, b_mat, q_t, o_intra, d_last = _gdn_intra(*probs, c=DN_CHUNK, small=False, **_PROMPT_MODES)
    s = s_scr[...]
    for ci in range(tc // DN_CHUNK):
        ps = slice(ci * DN_HEADS, (ci + 1) * DN_HEADS)
        o, s = _gdn_state_step(a_mat[ps], b_mat[ps], q_t[ps], o_intra[ps], d_last[ps], s,
                               _PROMPT_STATE_MODE)
        for h in range(DN_HEADS):
            o_ref[ci * DN_CHUNK:(ci + 1) * DN_CHUNK, h * DN_HEAD_DIM:(h + 1) * DN_HEAD_DIM] = (
                o[h].astype(o_ref.dtype))
    s_scr[...] = s

    @pl.when(t == pl.num_programs(1) - 1)
    def _():
        sn_ref[0] = s


def _delta_prompt(qkv, gb, s0, bsz, seq, tc):
    n = qkv.shape[0]
    nt = seq // tc
    rows = lambda w: pl.BlockSpec((tc, w), lambda b, t: (b * nt + t, 0))
    state = pl.BlockSpec((1, DN_HEADS, DN_HEAD_DIM, DN_HEAD_DIM), lambda b, t: (b, 0, 0, 0))
    return pl.pallas_call(
        _delta_prompt_kernel,
        grid=(bsz, nt),
        in_specs=[rows(3 * DN_WIDTH), rows(LANES), state],
        out_specs=[rows(DN_WIDTH), state],
        out_shape=[
            jax.ShapeDtypeStruct((n, DN_WIDTH), BF16),
            jax.ShapeDtypeStruct((bsz, DN_HEADS, DN_HEAD_DIM, DN_HEAD_DIM), F32),
        ],
        scratch_shapes=[pltpu.VMEM((DN_HEADS, DN_HEAD_DIM, DN_HEAD_DIM), F32)],
        compiler_params=_cparams("arbitrary", "arbitrary"),
        name="delta_prompt",
    )(qkv, gb, s0)


def _delta_sample_kernel(qkv_ref, gb_ref, s0_ref, *rest):
    o_ref, sn_ref = rest[-2:]
    bt = qkv_ref.shape[0]
    c = SAMPLE_CHUNK
    probs = _gdn_problems(qkv_ref[...].reshape(bt * c, 3 * DN_WIDTH), gb_ref[...].reshape(bt * c, LANES), c)
    a_mat, b_mat, q_t, o_intra, d_last = _gdn_intra(*probs, c=c, small=True, **_SAMPLE_MODES)
    s = s0_ref[0].reshape(bt * DN_HEADS, DN_HEAD_DIM, DN_HEAD_DIM)
    o, s = _gdn_state_step(a_mat, b_mat, q_t, o_intra, d_last, s, _SAMPLE_STATE_MODE)
    for b in range(bt):
        for h in range(DN_HEADS):
            o_ref[b, :, h * DN_HEAD_DIM:(h + 1) * DN_HEAD_DIM] = o[b * DN_HEADS + h]
    sn_ref[0] = s.reshape(bt, DN_HEADS, DN_HEAD_DIM, DN_HEAD_DIM)


def _delta_sample(qkv_b, gb_b, state_all, layer, carried, bt):
    bsz = qkv_b.shape[0]
    blk = lambda w: pl.BlockSpec((bt, SAMPLE_CHUNK, w), lambda i: (i, 0, 0))
    state = pl.BlockSpec((1, bt, DN_HEADS, DN_HEAD_DIM, DN_HEAD_DIM), lambda i: (layer, i, 0, 0, 0))
    in_specs, args, aliases = [blk(3 * DN_WIDTH), blk(LANES), state], [qkv_b, gb_b, state_all], {}
    if carried is not None:
        in_specs.append(pl.BlockSpec(memory_space=pl.ANY))
        args.append(carried)
        aliases = {3: 1}
    return pl.pallas_call(
        _delta_sample_kernel,
        grid=(bsz // bt,),
        in_specs=in_specs,
        out_specs=[blk(DN_WIDTH), state],
        out_shape=[
            jax.ShapeDtypeStruct((bsz, SAMPLE_CHUNK, DN_WIDTH), F32),
            jax.ShapeDtypeStruct(state_all.shape, F32),
        ],
        input_output_aliases=aliases,
        compiler_params=_cparams("arbitrary"),
        name="delta_sample",
    )(*args)


def _outproj_kernel(yc_ref, o_ref, z_ref, x_ref, g1_ref, nw_ref, w_ref, lg_ref, lb_ref, out_ref):
    o = o_ref[...].astype(F32)
    parts = []
    for h in range(DN_HEADS):
        oh = o[:, h * DN_HEAD_DIM:(h + 1) * DN_HEAD_DIM]
        parts.append(oh * lax.rsqrt(jnp.mean(oh * oh, -1, keepdims=True) + RMS_EPS) * nw_ref[...])
    og = jnp.concatenate(parts, axis=-1) * _silu(z_ref[...].astype(F32))
    mixed = jnp.concatenate([yc_ref[...].astype(BF16), og.astype(BF16)], axis=-1)
    m = jnp.dot(mixed, w_ref[...], preferred_element_type=F32)
    r = DEEPNORM_ALPHA * x_ref[...] + (1.0 + g1_ref[0]) * m
    out_ref[...] = _layer_norm(r, lg_ref[...], lb_ref[...])


def _outproj(yc, o, z_src, z_blk, x, mod3, tpg, norm_w, w_out, ln_g, ln_b, tm):
    n = x.shape[0]
    vec = lambda w: pl.BlockSpec((1, w), lambda i: (0, 0))
    return pl.pallas_call(
        _outproj_kernel,
        grid=(n // tm,),
        in_specs=[
            pl.BlockSpec((tm, CONV_WIDTH), lambda i: (i, 0)),
            pl.BlockSpec((tm, DN_WIDTH), lambda i: (i, 0)),
            pl.BlockSpec((tm, DN_WIDTH), lambda i: (i, z_blk)),
            pl.BlockSpec((tm, D_MODEL), lambda i: (i, 0)),
            _mod_spec(mod3, 2, tpg),
            vec(DN_HEAD_DIM),
            pl.BlockSpec((D_MODEL, D_MODEL), lambda i: (0, 0)),
            vec(D_MODEL), vec(D_MODEL),
        ],
        out_specs=pl.BlockSpec((tm, D_MODEL), lambda i: (i, 0)),
        out_shape=jax.ShapeDtypeStruct((n, D_MODEL), F32),
        compiler_params=_cparams("arbitrary"),
        name="outproj",
    )(yc, o, z_src, x, mod3, norm_w, w_out, ln_g, ln_b)


R_E1, R_E2, R_G1, R_G2, R_R1, R_R2 = range(6)
N_ROW_BUFS = 2


def _joint_specs(xp, xs, mod_p, mod_s, chunks, tm, seq):
    last = xp.shape[0] // tm - 1
    tpg = seq // tm
    assert xs.shape[0] == tm
    specs = [pl.BlockSpec((tm, D_MODEL), lambda i, *_: (jnp.minimum(i, last), 0)),
             pl.BlockSpec((tm, D_MODEL), lambda i, *_: (0, 0))]
    for c in chunks:
        specs.append(pl.BlockSpec((1, 1, D_MODEL), lambda i, *_, c=c: (jnp.minimum(i, last) // tpg, 0, c)))
        specs.append(pl.BlockSpec((1, mod_s.shape[1], D_MODEL), lambda i, *_, c=c: (0, 0, c)))
    return specs


def _tile_rows(mod_ref, rows):
    m = mod_ref[0]
    return jnp.concatenate([m] * (rows // m.shape[0]), axis=0)


def _joint_modulated(i, n_prompt_tiles, xp_ref, xs_ref, scp_ref, scs_ref, shp_ref, shs_ref, u_ref):
    tm = xp_ref.shape[0]

    @pl.when(i < n_prompt_tiles)
    def _():
        u_ref[...] = (xp_ref[...] * (1.0 + scp_ref[0]) + shp_ref[0]).reshape(u_ref.shape)

    @pl.when(i >= n_prompt_tiles)
    def _():
        u_ref[...] = (xs_ref[...] * (1.0 + _tile_rows(scs_ref, tm)) + _tile_rows(shs_ref, tm)).reshape(
            u_ref.shape)


def _router_kernel(xp_ref, xs_ref, scp_ref, scs_ref, shp_ref, shs_ref, wr_ref, br_ref,
                   route_ref, route_t_ref, cnt_ref, u_scr, *, n_prompt_tiles):
    tm = xp_ref.shape[0]
    _joint_modulated(pl.program_id(0), n_prompt_tiles, xp_ref, xs_ref, scp_ref, scs_ref, shp_ref, shs_ref,
                     u_scr)
    lt = _mm(wr_ref[...], u_scr[...], _NT, mode="x3") + br_ref[:, 0:1]
    e_idx = lax.broadcasted_iota(jnp.int32, lt.shape, 0)
    ex = jnp.exp(lt - jnp.max(lt, 0, keepdims=True))
    probs = ex / jnp.sum(ex, 0, keepdims=True)
    p1 = jnp.max(probs, 0, keepdims=True)
    i1 = jnp.min(jnp.where(probs == p1, e_idx, N_EXPERTS), 0, keepdims=True)
    rest = jnp.where(e_idx == i1, -1.0, probs)
    p2 = jnp.max(rest, 0, keepdims=True)
    i2 = jnp.min(jnp.where(rest == p2, e_idx, N_EXPERTS), 0, keepdims=True)
    tot = p1 + p2
    sel = jnp.where(e_idx == i1, 1.0, 0.0) + jnp.where(e_idx == i2, 1.0, 0.0)
    r = lax.broadcasted_iota(jnp.int32, (tm, tm), 0)
    c = lax.broadcasted_iota(jnp.int32, (tm, tm), 1)
    incl = jnp.dot(sel.astype(BF16), jnp.where(r <= c, 1.0, 0.0).astype(BF16), preferred_element_type=F32)
    excl = incl - sel
    r1 = jnp.sum(jnp.where(e_idx == i1, excl, 0.0), 0, keepdims=True)
    r2 = jnp.sum(jnp.where(e_idx == i2, excl, 0.0), 0, keepdims=True)
    cnt_ref[0] = jnp.broadcast_to(incl[:, tm - 1:tm], (N_EXPERTS, LANES))
    route_t = jnp.zeros_like(lt)
    for j, val in enumerate((i1.astype(F32), i2.astype(F32), p1 / tot, p2 / tot, r1, r2)):
        route_t = jnp.where(e_idx == j, val, route_t)
    route_t_ref[0] = route_t
    route_ref[...] = jnp.transpose(
        jnp.concatenate([route_t, jnp.zeros((LANES - N_EXPERTS, tm), F32)], axis=0))


def _router(xp, xs, mod_p, mod_s, seq, w_router, b_router, tm):
    ntp = xp.shape[0] // tm
    nt = ntp + 1
    return pl.pallas_call(
        functools.partial(_router_kernel, n_prompt_tiles=ntp),
        grid=(nt,),
        in_specs=_joint_specs(xp, xs, mod_p, mod_s, (4, 3), tm, seq) + [
            pl.BlockSpec((N_EXPERTS, D_MODEL), lambda i: (0, 0)),
            pl.BlockSpec((N_EXPERTS, LANES), lambda i: (0, 0)),
        ],
        out_specs=[pl.BlockSpec((tm, LANES), lambda i: (i, 0)),
                   pl.BlockSpec((1, SUBLANES, tm), lambda i: (i, 0, 0)),
                   pl.BlockSpec((1, N_EXPERTS, LANES), lambda i: (i, 0, 0))],
        out_shape=[jax.ShapeDtypeStruct((nt * tm, LANES), F32),
                   jax.ShapeDtypeStruct((nt, SUBLANES, tm), F32),
                   jax.ShapeDtypeStruct((nt, N_EXPERTS, LANES), F32)],
        scratch_shapes=[pltpu.VMEM((tm, D_MODEL), F32)],
        compiler_params=_cparams("arbitrary"),
        name="router",
    )(xp, xs, mod_p, mod_s, mod_p, mod_s, w_router, b_router)


def _dispatch_plan(route_t, counts, tme):
    n_rt, _, tm = route_t.shape
    n = n_rt * tm
    cnt = counts[:, :, 0].astype(jnp.int32)
    tile_off = jnp.cumsum(cnt, axis=0) - cnt
    tot = jnp.sum(cnt, axis=0)
    gsz = (tot + tme - 1) // tme * tme
    gend = jnp.cumsum(gsz)
    goff = gend - gsz
    base = (goff[None, :] + tile_off)[:, :, None]
    ids = jnp.arange(N_EXPERTS, dtype=jnp.int32)[None, :, None]
    field = lambda j: route_t[:, j:j + 1, :].astype(jnp.int32)
    pick = lambda e: jnp.sum(jnp.where(ids == e, base, 0), axis=1, keepdims=True)
    slot1 = (pick(field(R_E1)) + field(R_R1)).reshape(n)
    slot2 = (pick(field(R_E2)) + field(R_R2)).reshape(n)
    n_tiles = 2 * n // tme + N_EXPERTS
    start = jnp.arange(n_tiles, dtype=jnp.int32) * tme
    valid = (start < gend[-1]).astype(jnp.int32)
    blk = jnp.minimum(jnp.arange(n_tiles, dtype=jnp.int32), gend[-1] // tme - 1)
    tile_e = jnp.sum((blk * tme)[:, None] >= gend[None, :], axis=1).astype(jnp.int32)
    zoff = jnp.where(gsz > 0, gend - tme, (n_tiles + jnp.arange(N_EXPERTS, dtype=jnp.int32)) * tme)
    return slot1, slot2, zoff, tile_e, valid, blk, n_tiles


def _dispatch_kernel(s1_ref, s2_ref, zoff_ref, xp_ref, xs_ref, scp_ref, scs_ref, shp_ref, shs_ref,
                     ug_hbm, ubuf, zbuf, sem, *, n_prompt_tiles):
    i = pl.program_id(0)
    tm = xp_ref.shape[0]
    tme = zbuf.shape[0]

    n_steps = n_prompt_tiles + 1
    fill_sem = 2 * N_ROW_BUFS

    @pl.when(i == 0)
    def _():
        zbuf[...] = jnp.zeros_like(zbuf)
        fill = lambda e: pltpu.make_async_copy(
            zbuf, ug_hbm.at[pl.ds(pl.multiple_of(zoff_ref[e], 8), tme)], sem.at[fill_sem])
        for e in range(N_EXPERTS):
            fill(e).start()
        for e in range(N_EXPERTS):
            fill(e).wait()

    def drain(b):
        for stream in range(2):
            pltpu.make_async_copy(ubuf.at[b], ubuf.at[b], sem.at[2 * b + stream]).wait()

    buf = i % N_ROW_BUFS

    @pl.when(i >= N_ROW_BUFS)
    def _():
        drain(buf)

    _joint_modulated(i, n_prompt_tiles, xp_ref, xs_ref, scp_ref, scs_ref, shp_ref, shs_ref, ubuf.at[buf])
    base = i * tm

    def issue(j, carry):
        for k in range(SUBLANES):
            row = ubuf.at[buf, j, pl.ds(k, 1)]
            t = base + j * SUBLANES + k
            pltpu.make_async_copy(row, ug_hbm.at[pl.ds(s1_ref[t], 1)], sem.at[2 * buf]).start()
            pltpu.make_async_copy(row, ug_hbm.at[pl.ds(s2_ref[t], 1)], sem.at[2 * buf + 1]).start(priority=1)
        return carry

    lax.fori_loop(0, tm // SUBLANES, issue, 0)

    @pl.when(i == n_steps - 1)
    def _():
        for b in range(min(N_ROW_BUFS, n_steps)):
            drain(b)


def _dispatch(xp, xs, mod_p, mod_s, seq, slot1, slot2, zoff, tm, tme, rows):
    ntp = xp.shape[0] // tm
    return pl.pallas_call(
        functools.partial(_dispatch_kernel, n_prompt_tiles=ntp),
        grid_spec=pltpu.PrefetchScalarGridSpec(
            num_scalar_prefetch=3,
            grid=(ntp + 1,),
            in_specs=_joint_specs(xp, xs, mod_p, mod_s, (4, 3), tm, seq),
            out_specs=pl.BlockSpec(memory_space=pl.ANY),
            scratch_shapes=[pltpu.VMEM((N_ROW_BUFS, tm // SUBLANES, SUBLANES, D_MODEL), F32),
                            pltpu.VMEM((tme, D_MODEL), F32),
                            pltpu.SemaphoreType.DMA((2 * N_ROW_BUFS + 1,))],
        ),
        out_shape=jax.ShapeDtypeStruct((rows, D_MODEL), F32),
        compiler_params=_cparams("arbitrary"),
        name="dispatch",
    )(slot1, slot2, zoff, xp, xs, mod_p, mod_s, mod_p, mod_s)


def _expert_kernel(te_ref, tv_ref, tb_ref, ug_ref, wg_ref, wu_ref, wd_ref, ys_ref):
    i = pl.program_id(0)

    @pl.when(tv_ref[i] == 1)
    def _():
        u = ug_ref[...].astype(BF16)
        h = _silu(jnp.dot(u, wg_ref[0].astype(BF16), preferred_element_type=F32)) * jnp.dot(
            u, wu_ref[0].astype(BF16), preferred_element_type=F32)
        ys_ref[...] = jnp.dot(h.astype(BF16), wd_ref[0].astype(BF16), preferred_element_type=F32)

    @pl.when(tv_ref[i] == 0)
    def _():
        ys_ref[...] = jnp.zeros_like(ys_ref)


def _experts(ug, tile_e, valid, blk, w_gate, w_up, w_down, tme, n_tiles):
    d_e = w_gate.shape[-1]
    return pl.pallas_call(
        _expert_kernel,
        grid_spec=pltpu.PrefetchScalarGridSpec(
            num_scalar_prefetch=3,
            grid=(n_tiles,),
            in_specs=[
                pl.BlockSpec((tme, D_MODEL), lambda i, te, tv, tb: (tb[i], 0)),
                pl.BlockSpec((1, D_MODEL, d_e), lambda i, te, tv, tb: (te[i], 0, 0)),
                pl.BlockSpec((1, D_MODEL, d_e), lambda i, te, tv, tb: (te[i], 0, 0)),
                pl.BlockSpec((1, d_e, D_MODEL), lambda i, te, tv, tb: (te[i], 0, 0)),
            ],
            out_specs=pl.BlockSpec((tme, D_MODEL), lambda i, te, tv, tb: (i, 0)),
        ),
        out_shape=jax.ShapeDtypeStruct((n_tiles * tme, D_MODEL), F32),
        compiler_params=_cparams("arbitrary"),
        name="experts",
    )(tile_e, valid, blk, ug, w_gate, w_up, w_down)


def _combine_kernel(s1_ref, s2_ref, ys_hbm, route_ref, xp_ref, xs_ref, g2p_ref, g2s_ref, lg_ref, lb_ref,
                    outp_ref, outs_ref, y1buf, y2buf, sem, *, n_prompt_tiles):
    i = pl.program_id(0)
    tm = xp_ref.shape[0]
    n_steps = n_prompt_tiles + 1

    def gather(tile, b):
        def issue(j, carry):
            for k in range(SUBLANES):
                t = tile * tm + j * SUBLANES + k
                pltpu.make_async_copy(ys_hbm.at[pl.ds(s1_ref[t], 1)], y1buf.at[b, j, pl.ds(k, 1)],
                                      sem.at[2 * b]).start()
                pltpu.make_async_copy(ys_hbm.at[pl.ds(s2_ref[t], 1)], y2buf.at[b, j, pl.ds(k, 1)],
                                      sem.at[2 * b + 1]).start(priority=1)
            return carry

        lax.fori_loop(0, tm // SUBLANES, issue, 0)

    buf = i % N_ROW_BUFS

    @pl.when(i == 0)
    def _():
        gather(0, 0)

    @pl.when(i + 1 < n_steps)
    def _():
        gather(i + 1, (i + 1) % N_ROW_BUFS)

    pltpu.make_async_copy(y1buf.at[buf], y1buf.at[buf], sem.at[2 * buf]).wait()
    pltpu.make_async_copy(y2buf.at[buf], y2buf.at[buf], sem.at[2 * buf + 1]).wait()
    route = route_ref[...]
    y1 = y1buf[buf].reshape(tm, D_MODEL)
    y2 = y2buf[buf].reshape(tm, D_MODEL)
    f = route[:, R_G1:R_G1 + 1] * y1 + route[:, R_G2:R_G2 + 1] * y2

    @pl.when(i < n_prompt_tiles)
    def _():
        r = DEEPNORM_ALPHA * xp_ref[...] + (1.0 + g2p_ref[0]) * f
        outp_ref[...] = _layer_norm(r, lg_ref[...], lb_ref[...])

    @pl.when(i >= n_prompt_tiles)
    def _():
        r = DEEPNORM_ALPHA * xs_ref[...] + (1.0 + _tile_rows(g2s_ref, tm)) * f
        outs_ref[...] = _layer_norm(r, lg_ref[...], lb_ref[...])


def _combine(ys, slot1, slot2, route, xp, xs, mod_p, mod_s, seq, ln_g, ln_b, tm):
    ntp = xp.shape[0] // tm
    vec = lambda w: pl.BlockSpec((1, w), lambda i, *_: (0, 0))
    x_specs = _joint_specs(xp, xs, mod_p, mod_s, (5,), tm, seq)
    return pl.pallas_call(
        functools.partial(_combine_kernel, n_prompt_tiles=ntp),
        grid_spec=pltpu.PrefetchScalarGridSpec(
            num_scalar_prefetch=2,
            grid=(ntp + 1,),
            in_specs=[
                pl.BlockSpec(memory_space=pl.ANY),
                pl.BlockSpec((tm, LANES), lambda i, *_: (i, 0)),
            ] + x_specs + [vec(D_MODEL), vec(D_MODEL)],
            out_specs=[x_specs[0], x_specs[1]],
            scratch_shapes=[pltpu.VMEM((N_ROW_BUFS, tm // SUBLANES, SUBLANES, D_MODEL), F32),
                            pltpu.VMEM((N_ROW_BUFS, tm // SUBLANES, SUBLANES, D_MODEL), F32),
                            pltpu.SemaphoreType.DMA((2 * N_ROW_BUFS,))],
        ),
        out_shape=[jax.ShapeDtypeStruct(xp.shape, F32), jax.ShapeDtypeStruct(xs.shape, F32)],
        compiler_params=_cparams("arbitrary"),
        name="combine",
    )(slot1, slot2, ys, route, xp, xs, mod_p, mod_s, ln_g, ln_b)


def _moe(xp, xs, mod_p, mod_s, seq, w_router, b_router, w_gate, w_up, w_down, ln_g, ln_b, tm, tme):
    route, route_t, counts = _router(xp, xs, mod_p, mod_s, seq, w_router, b_router, tm)
    slot1, slot2, zoff, tile_e, valid, blk, n_tiles = _dispatch_plan(route_t, counts, tme)
    ug = _dispatch(xp, xs, mod_p, mod_s, seq, slot1, slot2, zoff, tm, tme, (n_tiles + N_EXPERTS) * tme)
    ys = _experts(ug, tile_e, valid, blk, w_gate, w_up, w_down, tme, n_tiles)
    return _combine(ys, slot1, slot2, route, xp, xs, mod_p, mod_s, seq, ln_g, ln_b, tm)


def _ffn_kernel(x_ref, sc_ref, sh_ref, g2_ref, wg_ref, wu_ref, wd_ref, lg_ref, lb_ref, out_ref, *, tf):
    x = x_ref[...]
    u = (x * (1.0 + sc_ref[0]) + sh_ref[0]).astype(BF16)
    y = None
    for f0 in range(0, wg_ref.shape[1], tf):
        h = _silu(jnp.dot(u, wg_ref[:, f0:f0 + tf], preferred_element_type=F32)) * jnp.dot(
            u, wu_ref[:, f0:f0 + tf], preferred_element_type=F32)
        part = jnp.dot(h.astype(BF16), wd_ref[f0:f0 + tf, :], preferred_element_type=F32)
        y = part if y is None else y + part
    r = DEEPNORM_ALPHA * x + (1.0 + g2_ref[0]) * y
    out_ref[...] = _layer_norm(r, lg_ref[...], lb_ref[...])


def _ffn(x, mod3, tpg, w_gate, w_up, w_down, ln_g, ln_b, tm, tf):
    n = x.shape[0]
    vec = lambda w: pl.BlockSpec((1, w), lambda i: (0, 0))
    resident = lambda a: pl.BlockSpec(a.shape, lambda i: (0, 0), pipeline_mode=pl.Buffered(1))
    return pl.pallas_call(
        functools.partial(_ffn_kernel, tf=tf),
        grid=(n // tm,),
        in_specs=[
            pl.BlockSpec((tm, D_MODEL), lambda i: (i, 0)),
            _mod_spec(mod3, 4, tpg),
            _mod_spec(mod3, 3, tpg),
            _mod_spec(mod3, 5, tpg),
            resident(w_gate), resident(w_up), resident(w_down),
            vec(D_MODEL), vec(D_MODEL),
        ],
        out_specs=pl.BlockSpec((tm, D_MODEL), lambda i: (i, 0)),
        out_shape=jax.ShapeDtypeStruct((n, D_MODEL), F32),
        compiler_params=_cparams("arbitrary"),
        name="ffn_dense",
    )(x, mod3, mod3, mod3, w_gate, w_up, w_down, ln_g, ln_b)


def _token_mixer(l, x, mod3, rows_per_group, prm, tm, mixer_fn):
    yc, o, z_src, z_blk, st = mixer_fn(l, x, mod3)
    x = _outproj(yc, o, z_src, z_blk, x, mod3, rows_per_group // tm, prm["norm_w"][l], prm["w_out"][l],
                 prm["ln_g"][l, 0:1], prm["ln_b"][l, 0:1], tm)
    return x, st


def kernel(x_prompt, x_sample, state_conv, state_dn_conv, state_dn, c_prompt, c_sample, w_in, conv_w,
           dn_conv_w, a_log, dt_bias, dn_norm_w, w_out, w_ada, b_ada, ln_g, ln_b, w_ff_gate, w_ff_up,
           w_ff_down, w_router, b_router, w_exp_gate, w_exp_up, w_exp_down):
    bp, seq, d = x_prompt.shape
    bs, steps, _ = x_sample.shape
    pad_lanes = LANES - N_EXPERTS
    hp = jnp.zeros((DEPTH, 2, LANES), F32)
    hp = hp.at[:, 0, G_LANE:].set(a_log).at[:, 1, G_LANE:].set(dt_bias)
    w_in_t = jnp.swapaxes(w_in, 1, 2)
    prm = {
        "w_main": w_in_t[:, :P_MAIN].astype(BF16),
        "w_small": w_in_t[:, w_in_t.shape[1] - LANES:],
        "norm_w": dn_norm_w.reshape(DEPTH, 1, DN_HEAD_DIM),
        "w_out": w_out.astype(BF16),
        "ln_g": ln_g, "ln_b": ln_b,
        "w_ff_gate": w_ff_gate.astype(BF16), "w_ff_up": w_ff_up.astype(BF16),
        "w_ff_down": w_ff_down.astype(BF16),
        "w_router": jnp.swapaxes(w_router, 1, 2),
        "b_router": jnp.broadcast_to(b_router[:, :, None], b_router.shape + (LANES,)),
        "w_exp_gate": w_exp_gate, "w_exp_up": w_exp_up, "w_exp_down": w_exp_down,
    }

    mod = _ada(jnp.concatenate([c_prompt, c_sample], axis=0), w_ada, b_ada)
    mod_p = [mod[l, :bp].reshape(bp, 1, 6 * d) for l in range(DEPTH)]
    mod_s = [mod[l, bp:].reshape(1, bs, 6 * d) for l in range(DEPTH)]

    tm_p, tc_delta = 512, 512
    zc = jnp.zeros((bp, CONV_K - 1, CONV_WIDTH), F32)
    zd = jnp.zeros((bp, DN_CONV_K - 1, 3 * DN_WIDTH), F32)
    zs = jnp.zeros((bp, DN_HEADS, DN_HEAD_DIM, DN_HEAD_DIM), F32)

    def mixer_prompt(l, x, mod3):
        yc, qkv, z, gb, ncb, ndb = _inproj_pre_prompt(
            x, mod3, prm["w_main"][l], prm["w_small"][l], conv_w[l], dn_conv_w[l], hp[l], zc, zd,
            bp, seq, tm_p)
        o, s_new = _delta_prompt(qkv, gb, zs, bp, seq, tc_delta)
        return yc, o, z, 0, (ncb, ndb, s_new)

    bt_pre, bt_delta = 32, 16
    dn_states = []

    def mixer_sample(l, x, mod3):
        pm, ps = _inproj(x, mod3, steps, prm["w_main"][l], prm["w_small"][l], bs)
        cb_tm = jnp.transpose(state_conv[l], (1, 0, 2))
        db_tm = jnp.transpose(state_dn_conv[l], (1, 0, 2))
        yc, qkv, gb, ncb, ndb = _pre_sample(pm, ps, conv_w[l], dn_conv_w[l], hp[l], cb_tm, db_tm,
                                            bs, steps, bt_pre)
        to_b = lambda a: jnp.pad(jnp.transpose(a, (1, 0, 2)), ((0, 0), (0, SAMPLE_CHUNK - steps), (0, 0)))
        o_b, s_all = _delta_sample(to_b(qkv), to_b(gb), state_dn, l, dn_states[-1] if dn_states else None,
                                   bt_delta)
        dn_states.append(s_all)
        o = jnp.transpose(o_b[:, :steps], (1, 0, 2)).reshape(steps * bs, DN_WIDTH)
        return (yc.reshape(steps * bs, CONV_WIDTH), o, pm, OFF_Z // DN_WIDTH,
                (jnp.transpose(ncb, (1, 0, 2)), jnp.transpose(ndb, (1, 0, 2))))

    x_p = x_prompt.reshape(bp * seq, d)
    x_s = jnp.transpose(x_sample, (1, 0, 2)).reshape(steps * bs, d)
    n_s = steps * bs
    st_p, st_s = [], []
    for l in range(DEPTH):
        x_p, st = _token_mixer(l, x_p, mod_p[l], seq, prm, tm_p, mixer_prompt)
        st_p.append(st)
        x_s, st = _token_mixer(l, x_s, mod_s[l], n_s, prm, bs, mixer_sample)
        st_s.append(st)
        j = l // 2
        ln = (prm["ln_g"][l, 1:2], prm["ln_b"][l, 1:2])
        if l % 2 == 0:
            ffn_w = (prm["w_ff_gate"][j], prm["w_ff_up"][j], prm["w_ff_down"][j])
            x_p = _ffn(x_p, mod_p[l], seq // tm_p, *ffn_w, *ln, tm_p, 256)
            x_s = _ffn(x_s, mod_s[l], n_s // bs, *ffn_w, *ln, bs, w_ff_gate.shape[-1])
        else:
            x_p, x_s = _moe(x_p, x_s, mod_p[l], mod_s[l], seq, prm["w_router"][j], prm["b_router"][j],
                            prm["w_exp_gate"][j], prm["w_exp_up"][j], prm["w_exp_down"][j], *ln, tm_p, 512)
    y_p = x_p
    y_s = jnp.transpose(x_s.reshape(steps, bs, d), (1, 0, 2))

    stack = lambda sts, i: jnp.stack([s[i] for s in sts])
    return (y_p.reshape(bp, seq, d), y_s,
            stack(st_p, 0), stack(st_p, 1), stack(st_p, 2),
            stack(st_s, 0), stack(st_s, 1), dn_states[-1])
```

```python
import functools
import math

import jax
import jax.numpy as jnp
from jax import lax
from jax.experimental import pallas as pl
from jax.experimental.pallas import tpu as pltpu

F32 = jnp.float32
BF16 = jnp.bfloat16

D_MODEL = 1024
DEPTH = 2
CONV_WIDTH = 512
CONV_K = 3
DN_HEADS = 4
DN_HEAD_DIM = 128
DN_WIDTH = DN_HEADS * DN_HEAD_DIM
DN_CONV_K = 4
DN_CHUNK = 64
N_EXPERTS = 8
DEEPNORM_ALPHA = (2.0 * DEPTH) ** 0.25
LN_EPS = 1e-5
RMS_EPS = 1e-6

OFF_CC = CONV_WIDTH
OFF_CH = 2 * CONV_WIDTH
OFF_QKV = 3 * CONV_WIDTH
OFF_Z = OFF_QKV + 3 * DN_WIDTH
P_MAIN = OFF_Z + DN_WIDTH
LANES = 128
SUBLANES = 8
BETA_LANE = LANES - 2 * DN_HEADS
G_LANE = LANES - DN_HEADS
SAMPLE_CHUNK = 8

VMEM_LIMIT = 56 * 1024 * 1024
MXU_TILE = 256

PROMPT_ROW_TILE = 512
PROMPT_DELTA_TILE = 512
FFN_HIDDEN_CHUNK = MXU_TILE
EXPERT_SLOT_TILE = 512
SAMPLE_PRE_SEQS = 32
SAMPLE_DELTA_SEQS = 16


def _cparams(*sem):
    return pltpu.CompilerParams(dimension_semantics=sem, vmem_limit_bytes=VMEM_LIMIT)


def _silu(x):
    return x * jax.nn.sigmoid(x)


def _layer_norm(r, g, b):
    mu = jnp.mean(r, -1, keepdims=True)
    xc = r - mu
    var = jnp.mean(xc * xc, -1, keepdims=True)
    return xc * lax.rsqrt(var + LN_EPS) * g + b


def _split_bf16(a):
    hi = a.astype(BF16)
    lo = (a - hi.astype(F32)).astype(BF16)
    return hi, lo


_NN = (((1,), (0,)), ((), ()))
_NT = (((1,), (1,)), ((), ()))
_BNN = (((2,), (1,)), ((0,), (0,)))
_BNT = (((2,), (2,)), ((0,), (0,)))
_BTN = (((1,), (1,)), ((0,), (0,)))


def _mm(a, b, dims=_NN, mode="bf16"):
    if mode == "f32":
        return lax.dot_general(a, b, dims, precision=lax.Precision.HIGHEST, preferred_element_type=F32)
    if mode == "bf16":
        return lax.dot_general(a.astype(BF16), b.astype(BF16), dims, preferred_element_type=F32)
    ah, al = _split_bf16(a)
    bh, bl = _split_bf16(b)
    d = functools.partial(lax.dot_general, dimension_numbers=dims, preferred_element_type=F32)
    return d(ah, bh) + (d(ah, bl) + d(al, bh))


def _ada_kernel(c_ref, w_ref, b_ref, o_ref):
    s = _silu(c_ref[...]).astype(BF16)
    o_ref[0] = jnp.dot(s, w_ref[0].astype(BF16), preferred_element_type=F32) + b_ref[0]


def _ada(c_all, w_ada, b_ada):
    rows = c_all.shape[0]
    tn = 1536
    return pl.pallas_call(
        _ada_kernel,
        grid=(DEPTH, 6 * D_MODEL // tn),
        in_specs=[
            pl.BlockSpec((rows, D_MODEL), lambda l, j: (0, 0)),
            pl.BlockSpec((1, D_MODEL, tn), lambda l, j: (l, 0, j)),
            pl.BlockSpec((1, 1, tn), lambda l, j: (l, 0, j)),
        ],
        out_specs=pl.BlockSpec((1, rows, tn), lambda l, j: (l, 0, j)),
        out_shape=jax.ShapeDtypeStruct((DEPTH, rows, 6 * D_MODEL), F32),
        compiler_params=_cparams("arbitrary", "arbitrary"),
        name="ada",
    )(c_all, w_ada, b_ada.reshape(DEPTH, 1, 6 * D_MODEL))


def _mod_spec(mod3, chunk, tiles_per_group):
    r = mod3.shape[1]
    return pl.BlockSpec((1, r, D_MODEL), lambda i, *_: (i // tiles_per_group, 0, chunk))


def _inproj_kernel(x_ref, sc_ref, sh_ref, wm_ref, ws_ref, pm_ref, ps_ref):
    u = x_ref[...] * (1.0 + sc_ref[0]) + sh_ref[0]
    pm_ref[...] = lax.dot_general(u.astype(BF16), wm_ref[...], _NT, preferred_element_type=F32)
    ps_ref[...] = _mm(u, ws_ref[...], _NT, mode="x3")


def _inproj(x, mod3, tpg, w_main, w_small, tm):
    n = x.shape[0]
    return pl.pallas_call(
        _inproj_kernel,
        grid=(n // tm,),
        in_specs=[
            pl.BlockSpec((tm, D_MODEL), lambda i: (i, 0)),
            _mod_spec(mod3, 1, tpg),
            _mod_spec(mod3, 0, tpg),
            pl.BlockSpec((P_MAIN, D_MODEL), lambda i: (0, 0)),
            pl.BlockSpec((LANES, D_MODEL), lambda i: (0, 0)),
        ],
        out_specs=[
            pl.BlockSpec((tm, P_MAIN), lambda i: (i, 0)),
            pl.BlockSpec((tm, LANES), lambda i: (i, 0)),
        ],
        out_shape=[
            jax.ShapeDtypeStruct((n, P_MAIN), F32),
            jax.ShapeDtypeStruct((n, LANES), F32),
        ],
        compiler_params=_cparams("arbitrary"),
        name="inproj",
    )(x, mod3, mod3, w_main, w_small)


def _gates(ps, hp):
    lane = lax.broadcasted_iota(jnp.int32, ps.shape, 1)
    beta = jax.nn.sigmoid(ps)
    g = -jnp.exp(hp[0:1, :]) * jax.nn.softplus(ps + hp[1:2, :])
    return jnp.where(lane >= G_LANE, g, jnp.where(lane >= BETA_LANE, beta, 0.0))


def _qkv_finish(y, out_ref, idx, first_head=0):
    y = _silu(y)
    for j in range(y.shape[1] // DN_HEAD_DIM):
        h = first_head + j
        seg = y[:, j * DN_HEAD_DIM:(j + 1) * DN_HEAD_DIM]
        if h < 2 * DN_HEADS:
            seg = seg * lax.rsqrt(jnp.sum(seg * seg, -1, keepdims=True) + RMS_EPS)
            if h < DN_HEADS:
                seg = seg * (DN_HEAD_DIM ** -0.5)
        out_ref[idx + (slice(None), slice(h * DN_HEAD_DIM, (h + 1) * DN_HEAD_DIM))] = seg


def _shifted(x, tail_ref, i, row):
    s = pltpu.roll(x, i, 0)
    top = s[0:SUBLANES]
    for r in range(i):
        top = jnp.where(row == r, tail_ref[8 - i + r:8 - i + r + 1, :], top)
    return jnp.concatenate([top, s[SUBLANES:]], axis=0)


def _inproj_pre_prompt_kernel(x_ref, sc_ref, sh_ref, wm_ref, ws_ref, cw_ref, dw_ref, hp_ref, cb_ref, db_ref,
                              yc_ref, qkv_ref, z_ref, gb_ref, ncb_ref, ndb_ref, tc_scr, td_scr):
    t = pl.program_id(1)
    tm = x_ref.shape[0]

    @pl.when(t == 0)
    def _():
        tc_scr[8 - (CONV_K - 1):8, :] = cb_ref[0]
        td_scr[8 - (DN_CONV_K - 1):8, :] = db_ref[0]

    u_f32 = x_ref[...] * (1.0 + sc_ref[0]) + sh_ref[0]
    u = u_f32.astype(BF16)
    proj = lambda lo, hi: lax.dot_general(u, wm_ref[lo:hi, :], _NT, preferred_element_type=F32)

    row = lax.broadcasted_iota(jnp.int32, (SUBLANES, DN_WIDTH), 0)

    def dn_part(xp, part):
        cols = slice(part * DN_WIDTH, (part + 1) * DN_WIDTH)
        tail = td_scr.at[:, cols]
        yp = _shifted(xp, tail, 3, row) * dw_ref[0:1, cols]
        yp = yp + _shifted(xp, tail, 2, row) * dw_ref[1:2, cols]
        yp = yp + _shifted(xp, tail, 1, row) * dw_ref[2:3, cols]
        yp = yp + xp * dw_ref[3:4, cols]
        _qkv_finish(yp, qkv_ref, (), first_head=part * DN_HEADS)
        return xp[tm - (DN_CONV_K - 1):tm, :]

    qkv_cols = lambda part: (OFF_QKV + part * DN_WIDTH, OFF_QKV + (part + 1) * DN_WIDTH)
    x_q = proj(*qkv_cols(0))
    x_k = proj(*qkv_cols(1))
    last_q = dn_part(x_q, 0)
    x_v = proj(*qkv_cols(2))
    last_k = dn_part(x_k, 1)
    p_c = proj(OFF_CC, OFF_CH)
    last_v = dn_part(x_v, 2)
    p_h = proj(OFF_CH, OFF_QKV)
    cgh = p_c * p_h
    p_b = proj(0, OFF_CC)
    y = _shifted(cgh, tc_scr, 2, row) * cw_ref[0:1, :]
    y = y + _shifted(cgh, tc_scr, 1, row) * cw_ref[1:2, :]
    y = y + cgh * cw_ref[2:3, :]
    p_z = proj(OFF_Z, P_MAIN)
    yc_ref[...] = (p_b * y).astype(yc_ref.dtype)
    last_c = cgh[tm - (CONV_K - 1):tm, :]
    small = _mm(u_f32, ws_ref[...], _NT, mode="bf16")
    z_ref[...] = p_z.astype(z_ref.dtype)
    gb_ref[...] = _gates(small, hp_ref[...])

    last_d = jnp.concatenate([last_q, last_k, last_v], axis=-1)
    tc_scr[8 - (CONV_K - 1):8, :] = last_c
    td_scr[8 - (DN_CONV_K - 1):8, :] = last_d

    @pl.when(t == pl.num_programs(1) - 1)
    def _():
        ncb_ref[0] = last_c
        ndb_ref[0] = last_d


def _inproj_pre_prompt(x, mod3, w_main, w_small, conv_w, dn_conv_w, hp, conv_buf, dn_buf, bsz, seq, tm):
    n = x.shape[0]
    nt = seq // tm
    rows = lambda w: pl.BlockSpec((tm, w), lambda b, t: (b * nt + t, 0))
    full = lambda a: pl.BlockSpec(a.shape, lambda b, t: (0,) * a.ndim)
    per_b = lambda k, w: pl.BlockSpec((1, k, w), lambda b, t: (b, 0, 0))
    mod = lambda chunk: pl.BlockSpec((1, 1, D_MODEL), lambda b, t: (b, 0, chunk))
    return pl.pallas_call(
        _inproj_pre_prompt_kernel,
        grid=(bsz, nt),
        in_specs=[rows(D_MODEL), mod(1), mod(0), full(w_main), full(w_small),
                  full(conv_w), full(dn_conv_w), full(hp),
                  per_b(CONV_K - 1, CONV_WIDTH), per_b(DN_CONV_K - 1, 3 * DN_WIDTH)],
        out_specs=[rows(CONV_WIDTH), rows(3 * DN_WIDTH), rows(DN_WIDTH), rows(LANES),
                   per_b(CONV_K - 1, CONV_WIDTH), per_b(DN_CONV_K - 1, 3 * DN_WIDTH)],
        out_shape=[
            jax.ShapeDtypeStruct((n, CONV_WIDTH), BF16),
            jax.ShapeDtypeStruct((n, 3 * DN_WIDTH), F32),
            jax.ShapeDtypeStruct((n, DN_WIDTH), BF16),
            jax.ShapeDtypeStruct((n, LANES), F32),
            jax.ShapeDtypeStruct((bsz, CONV_K - 1, CONV_WIDTH), F32),
            jax.ShapeDtypeStruct((bsz, DN_CONV_K - 1, 3 * DN_WIDTH), F32),
        ],
        scratch_shapes=[pltpu.VMEM((SUBLANES, CONV_WIDTH), F32), pltpu.VMEM((SUBLANES, 3 * DN_WIDTH), F32)],
        compiler_params=_cparams("arbitrary", "arbitrary"),
        name="inproj_pre_prompt",
    )(x, mod3, mod3, w_main, w_small, conv_w, dn_conv_w, hp, conv_buf, dn_buf)


def _pre_sample_kernel(pm_ref, ps_ref, cw_ref, dw_ref, hp_ref, cb_ref, db_ref,
                       yc_ref, qkv_ref, gb_ref, ncb_ref, ndb_ref):
    steps = pm_ref.shape[0]
    cgh = [pm_ref[t, :, OFF_CC:OFF_CH] * pm_ref[t, :, OFF_CH:OFF_QKV] for t in range(steps)]
    ext = [cb_ref[i] for i in range(CONV_K - 1)] + cgh
    for t in range(steps):
        y = ext[t] * cw_ref[0:1, :]
        for i in range(1, CONV_K):
            y = y + ext[t + i] * cw_ref[i:i + 1, :]
        yc_ref[t] = pm_ref[t, :, 0:OFF_CC] * y
    for i in range(CONV_K - 1):
        ncb_ref[i] = ext[len(ext) - (CONV_K - 1) + i]

    xq = [pm_ref[t, :, OFF_QKV:OFF_Z] for t in range(steps)]
    extq = [db_ref[i] for i in range(DN_CONV_K - 1)] + xq
    for t in range(steps):
        y = extq[t] * dw_ref[0:1, :]
        for i in range(1, DN_CONV_K):
            y = y + extq[t + i] * dw_ref[i:i + 1, :]
        _qkv_finish(y, qkv_ref, (t,))
        gb_ref[t] = _gates(ps_ref[t], hp_ref[...])
    for i in range(DN_CONV_K - 1):
        ndb_ref[i] = extq[len(extq) - (DN_CONV_K - 1) + i]


def _pre_sample(pm, ps, conv_w, dn_conv_w, hp, conv_buf_tm, dn_buf_tm, bsz, steps, bt):
    slab = lambda k, w: pl.BlockSpec((k, bt, w), lambda i: (0, i, 0))
    full = lambda a: pl.BlockSpec(a.shape, lambda i: (0,) * a.ndim)
    return pl.pallas_call(
        _pre_sample_kernel,
        grid=(bsz // bt,),
        in_specs=[slab(steps, P_MAIN), slab(steps, LANES), full(conv_w), full(dn_conv_w), full(hp),
                  slab(CONV_K - 1, CONV_WIDTH), slab(DN_CONV_K - 1, 3 * DN_WIDTH)],
        out_specs=[slab(steps, CONV_WIDTH), slab(steps, 3 * DN_WIDTH), slab(steps, LANES),
                   slab(CONV_K - 1, CONV_WIDTH), slab(DN_CONV_K - 1, 3 * DN_WIDTH)],
        out_shape=[
            jax.ShapeDtypeStruct((steps, bsz, CONV_WIDTH), F32),
            jax.ShapeDtypeStruct((steps, bsz, 3 * DN_WIDTH), F32),
            jax.ShapeDtypeStruct((steps, bsz, LANES), F32),
            jax.ShapeDtypeStruct((CONV_K - 1, bsz, CONV_WIDTH), F32),
            jax.ShapeDtypeStruct((DN_CONV_K - 1, bsz, 3 * DN_WIDTH), F32),
        ],
        compiler_params=_cparams("arbitrary"),
        name="pre_sample",
    )(pm.reshape(steps, bsz, P_MAIN), ps.reshape(steps, bsz, LANES), conv_w, dn_conv_w, hp,
      conv_buf_tm, dn_buf_tm)


def _cumsum_chunks(x, c):
    row = lax.broadcasted_iota(jnp.int32, x.shape, 0) & (c - 1)
    s = 1
    while s < c:
        x = x + jnp.where(row >= s, pltpu.roll(x, s, 0), 0.0)
        s *= 2
    return x


def _gdn_problems(qkv, gb, c):
    n = qkv.shape[0] // c
    gc = _cumsum_chunks(gb, c)
    q, k, v, g_b, beta_b = [], [], [], [], []
    for ci in range(n):
        rs = slice(ci * c, (ci + 1) * c)
        for h in range(DN_HEADS):
            ls = lambda base: slice(base + h * DN_HEAD_DIM, base + (h + 1) * DN_HEAD_DIM)
            q.append(qkv[rs, ls(0)])
            k.append(qkv[rs, ls(DN_WIDTH)])
            v.append(qkv[rs, ls(2 * DN_WIDTH)])
            g_b.append(jnp.broadcast_to(gc[rs, G_LANE + h:G_LANE + h + 1], (c, DN_HEAD_DIM)))
            beta_b.append(jnp.broadcast_to(gb[rs, BETA_LANE + h:BETA_LANE + h + 1], (c, DN_HEAD_DIM)))
    return tuple(jnp.stack(a) for a in (q, k, v, g_b, beta_b))


def _unit_lower_solve(m, rhs, c, small, mode):
    if small:
        sol = rhs
        for j in range(c - 1):
            sol = sol - m[:, :, j:j + 1] * sol[:, j:j + 1, :]
        return sol
    row = lax.broadcasted_iota(jnp.int32, m.shape, 1)
    col = lax.broadcasted_iota(jnp.int32, m.shape, 2)
    p = -m
    t = jnp.where(row == col, 1.0, 0.0) + p
    p = _mm(p, p, _BNN, "bf16")
    levels = int(math.log2(c))
    for lvl in range(1, levels):
        if lvl < levels - 1:
            y = _mm(jnp.concatenate([t, p], axis=1), p, _BNN, "bf16")
            t = t + y[:, :c]
            p = y[:, c:]
        else:
            t = t + _mm(t, p, _BNN, "bf16")
    x = _mm(t, rhs, _BNN, "bf16")
    resid = rhs - x - _mm(m, x, _BNN, mode)
    return x + _mm(t, resid, _BNN, "bf16")


def _gdn_intra(q, k, v, g_b, beta_b, *, c, small, mode_gram, mode_solve, mode_apply):
    n = q.shape[0]
    row = lax.broadcasted_iota(jnp.int32, (n, c, c), 1)
    col = lax.broadcasted_iota(jnp.int32, (n, c, c), 2)
    g_cc = g_b[:, :, :c]
    g_row = jnp.sum(jnp.where(row == col, g_cc, 0.0), axis=1, keepdims=True)
    gamma = jnp.exp(jnp.where(row >= col, g_cc - g_row, -jnp.inf))
    kbeta = k * beta_b
    exp_g = jnp.exp(g_b)
    gram = _mm(jnp.concatenate([kbeta, q], axis=1), k, _BNT, mode_gram)
    m = jnp.where(row > col, gram[:, :c] * gamma, 0.0)
    attn = gram[:, c:] * gamma
    rhs = jnp.concatenate([kbeta * exp_g, v * beta_b], axis=-1)
    sol = _unit_lower_solve(m, rhs, c, small, mode_solve)
    g_last = g_b[:, c - 1:c, :]
    k_dec = k * jnp.exp(g_last - g_b)
    kd = _mm(k_dec, sol, _BTN, mode_apply)
    at = _mm(attn, sol, _BNN, mode_apply)
    q_t = q * exp_g - at[..., :DN_HEAD_DIM]
    return kd[..., :DN_HEAD_DIM], kd[..., DN_HEAD_DIM:], q_t, at[..., DN_HEAD_DIM:], jnp.exp(g_last)


def _gdn_state_step(a_mat, b_mat, q_t, o_intra, d_last, s, mode):
    r = _mm(jnp.concatenate([a_mat, q_t], axis=1), s, _BNN, mode)
    s_new = s * d_last - r[:, :DN_HEAD_DIM] + b_mat
    return r[:, DN_HEAD_DIM:] + o_intra, s_new


_PROMPT_MODES = dict(mode_gram="bf16", mode_solve="x3", mode_apply="bf16")
_PROMPT_STATE_MODE = "bf16"
_SAMPLE_MODES = dict(mode_gram="bf16", mode_solve="bf16", mode_apply="bf16")
_SAMPLE_STATE_MODE = "bf16"


def _delta_prompt_kernel(qkv_ref, gb_ref, s0_ref, o_ref, sn_ref, s_scr):
    t = pl.program_id(1)
    tc = qkv_ref.shape[0]

    @pl.when(t == 0)
    def _():
        s_scr[...] = s0_ref[0]

    probs = _gdn_problems(qkv_ref[...], gb_ref[...], DN_CHUNK)
    a_mat, b_mat, q_t, o_intra, d_last = _gdn_intra(*probs, c=DN_CHUNK, small=False, **_PROMPT_MODES)
    s = s_scr[...]
    for ci in range(tc // DN_CHUNK):
        ps = slice(ci * DN_HEADS, (ci + 1) * DN_HEADS)
        o, s = _gdn_state_step(a_mat[ps], b_mat[ps], q_t[ps], o_intra[ps], d_last[ps], s,
                               _PROMPT_STATE_MODE)
        for h in range(DN_HEADS):
            o_ref[ci * DN_CHUNK:(ci + 1) * DN_CHUNK, h * DN_HEAD_DIM:(h + 1) * DN_HEAD_DIM] = (
                o[h].astype(o_ref.dtype))
    s_scr[...] = s

    @pl.when(t == pl.num_programs(1) - 1)
    def _():
        sn_ref[0] = s


def _delta_prompt(qkv, gb, s0, bsz, seq, tc):
    n = qkv.shape[0]
    nt = seq // tc
    rows = lambda w: pl.BlockSpec((tc, w), lambda b, t: (b * nt + t, 0))
    state = pl.BlockSpec((1, DN_HEADS, DN_HEAD_DIM, DN_HEAD_DIM), lambda b, t: (b, 0, 0, 0))
    return pl.pallas_call(
        _delta_prompt_kernel,
        grid=(bsz, nt),
        in_specs=[rows(3 * DN_WIDTH), rows(LANES), state],
        out_specs=[rows(DN_WIDTH), state],
        out_shape=[
            jax.ShapeDtypeStruct((n, DN_WIDTH), BF16),
            jax.ShapeDtypeStruct((bsz, DN_HEADS, DN_HEAD_DIM, DN_HEAD_DIM), F32),
        ],
        scratch_shapes=[pltpu.VMEM((DN_HEADS, DN_HEAD_DIM, DN_HEAD_DIM), F32)],
        compiler_params=_cparams("arbitrary", "arbitrary"),
        name="delta_prompt",
    )(qkv, gb, s0)


def _delta_sample_kernel(qkv_ref, gb_ref, s0_ref, *rest):
    o_ref, sn_ref = rest[-2:]
    bt = qkv_ref.shape[0]
    c = SAMPLE_CHUNK
    probs = _gdn_problems(qkv_ref[...].reshape(bt * c, 3 * DN_WIDTH), gb_ref[...].reshape(bt * c, LANES), c)
    a_mat, b_mat, q_t, o_intra, d_last = _gdn_intra(*probs, c=c, small=True, **_SAMPLE_MODES)
    s = s0_ref[0].reshape(bt * DN_HEADS, DN_HEAD_DIM, DN_HEAD_DIM)
    o, s = _gdn_state_step(a_mat, b_mat, q_t, o_intra, d_last, s, _SAMPLE_STATE_MODE)
    for b in range(bt):
        for h in range(DN_HEADS):
            o_ref[b, :, h * DN_HEAD_DIM:(h + 1) * DN_HEAD_DIM] = o[b * DN_HEADS + h]
    sn_ref[0] = s.reshape(bt, DN_HEADS, DN_HEAD_DIM, DN_HEAD_DIM)


def _delta_sample(qkv_b, gb_b, state_all, layer, carried, bt):
    bsz = qkv_b.shape[0]
    blk = lambda w: pl.BlockSpec((bt, SAMPLE_CHUNK, w), lambda i: (i, 0, 0))
    state = pl.BlockSpec((1, bt, DN_HEADS, DN_HEAD_DIM, DN_HEAD_DIM), lambda i: (layer, i, 0, 0, 0))
    in_specs, args, aliases = [blk(3 * DN_WIDTH), blk(LANES), state], [qkv_b, gb_b, state_all], {}
    if carried is not None:
        in_specs.append(pl.BlockSpec(memory_space=pl.ANY))
        args.append(carried)
        aliases = {3: 1}
    return pl.pallas_call(
        _delta_sample_kernel,
        grid=(bsz // bt,),
        in_specs=in_specs,
        out_specs=[blk(DN_WIDTH), state],
        out_shape=[
            jax.ShapeDtypeStruct((bsz, SAMPLE_CHUNK, DN_WIDTH), F32),
            jax.ShapeDtypeStruct(state_all.shape, F32),
        ],
        input_output_aliases=aliases,
        compiler_params=_cparams("arbitrary"),
        name="delta_sample",
    )(*args)


def _outproj_kernel(yc_ref, o_ref, z_ref, x_ref, g1_ref, nw_ref, w_ref, lg_ref, lb_ref, out_ref):
    o = o_ref[...].astype(F32)
    parts = []
    for h in range(DN_HEADS):
        oh = o[:, h * DN_HEAD_DIM:(h + 1) * DN_HEAD_DIM]
        parts.append(oh * lax.rsqrt(jnp.mean(oh * oh, -1, keepdims=True) + RMS_EPS) * nw_ref[...])
    og = jnp.concatenate(parts, axis=-1) * _silu(z_ref[...].astype(F32))
    mixed = jnp.concatenate([yc_ref[...].astype(BF16), og.astype(BF16)], axis=-1)
    m = jnp.dot(mixed, w_ref[...], preferred_element_type=F32)
    r = DEEPNORM_ALPHA * x_ref[...] + (1.0 + g1_ref[0]) * m
    out_ref[...] = _layer_norm(r, lg_ref[...], lb_ref[...])


def _outproj(yc, o, z_src, z_blk, x, mod3, tpg, norm_w, w_out, ln_g, ln_b, tm):
    n = x.shape[0]
    vec = lambda w: pl.BlockSpec((1, w), lambda i: (0, 0))
    return pl.pallas_call(
        _outproj_kernel,
        grid=(n // tm,),
        in_specs=[
            pl.BlockSpec((tm, CONV_WIDTH), lambda i: (i, 0)),
            pl.BlockSpec((tm, DN_WIDTH), lambda i: (i, 0)),
            pl.BlockSpec((tm, DN_WIDTH), lambda i: (i, z_blk)),
            pl.BlockSpec((tm, D_MODEL), lambda i: (i, 0)),
            _mod_spec(mod3, 2, tpg),
            vec(DN_HEAD_DIM),
            pl.BlockSpec((D_MODEL, D_MODEL), lambda i: (0, 0)),
            vec(D_MODEL), vec(D_MODEL),
        ],
        out_specs=pl.BlockSpec((tm, D_MODEL), lambda i: (i, 0)),
        out_shape=jax.ShapeDtypeStruct((n, D_MODEL), F32),
        compiler_params=_cparams("arbitrary"),
        name="outproj",
    )(yc, o, z_src, x, mod3, norm_w, w_out, ln_g, ln_b)


R_E1, R_E2, R_G1, R_G2, R_R1, R_R2 = range(6)
N_ROW_BUFS = 2


def _joint_specs(xp, xs, mod_p, mod_s, chunks, tm, seq):
    last = xp.shape[0] // tm - 1
    tpg = seq // tm
    assert xs.shape[0] == tm
    specs = [pl.BlockSpec((tm, D_MODEL), lambda i, *_: (jnp.minimum(i, last), 0)),
             pl.BlockSpec((tm, D_MODEL), lambda i, *_: (0, 0))]
    for c in chunks:
        specs.append(pl.BlockSpec((1, 1, D_MODEL), lambda i, *_, c=c: (jnp.minimum(i, last) // tpg, 0, c)))
        specs.append(pl.BlockSpec((1, mod_s.shape[1], D_MODEL), lambda i, *_, c=c: (0, 0, c)))
    return specs


def _tile_rows(mod_ref, rows):
    m = mod_ref[0]
    return jnp.concatenate([m] * (rows // m.shape[0]), axis=0)


def _joint_modulated(i, n_prompt_tiles, xp_ref, xs_ref, scp_ref, scs_ref, shp_ref, shs_ref, u_ref):
    tm = xp_ref.shape[0]

    @pl.when(i < n_prompt_tiles)
    def _():
        u_ref[...] = (xp_ref[...] * (1.0 + scp_ref[0]) + shp_ref[0]).reshape(u_ref.shape)

    @pl.when(i >= n_prompt_tiles)
    def _():
        u_ref[...] = (xs_ref[...] * (1.0 + _tile_rows(scs_ref, tm)) + _tile_rows(shs_ref, tm)).reshape(
            u_ref.shape)


def _router_kernel(xp_ref, xs_ref, scp_ref, scs_ref, shp_ref, shs_ref, wr_ref, br_ref,
                   route_ref, route_t_ref, cnt_ref, u_scr, *, n_prompt_tiles):
    tm = xp_ref.shape[0]
    _joint_modulated(pl.program_id(0), n_prompt_tiles, xp_ref, xs_ref, scp_ref, scs_ref, shp_ref, shs_ref,
                     u_scr)
    lt = _mm(wr_ref[...], u_scr[...], _NT, mode="x3") + br_ref[:, 0:1]
    e_idx = lax.broadcasted_iota(jnp.int32, lt.shape, 0)
    ex = jnp.exp(lt - jnp.max(lt, 0, keepdims=True))
    probs = ex / jnp.sum(ex, 0, keepdims=True)
    p1 = jnp.max(probs, 0, keepdims=True)
    i1 = jnp.min(jnp.where(probs == p1, e_idx, N_EXPERTS), 0, keepdims=True)
    rest = jnp.where(e_idx == i1, -1.0, probs)
    p2 = jnp.max(rest, 0, keepdims=True)
    i2 = jnp.min(jnp.where(rest == p2, e_idx, N_EXPERTS), 0, keepdims=True)
    tot = p1 + p2
    sel = jnp.where(e_idx == i1, 1.0, 0.0) + jnp.where(e_idx == i2, 1.0, 0.0)
    r = lax.broadcasted_iota(jnp.int32, (tm, tm), 0)
    c = lax.broadcasted_iota(jnp.int32, (tm, tm), 1)
    incl = jnp.dot(sel.astype(BF16), jnp.where(r <= c, 1.0, 0.0).astype(BF16), preferred_element_type=F32)
    excl = incl - sel
    r1 = jnp.sum(jnp.where(e_idx == i1, excl, 0.0), 0, keepdims=True)
    r2 = jnp.sum(jnp.where(e_idx == i2, excl, 0.0), 0, keepdims=True)
    cnt_ref[0] = jnp.broadcast_to(incl[:, tm - 1:tm], (N_EXPERTS, LANES))
    route_t = jnp.zeros_like(lt)
    for j, val in enumerate((i1.astype(F32), i2.astype(F32), p1 / tot, p2 / tot, r1, r2)):
        route_t = jnp.where(e_idx == j, val, route_t)
    route_t_ref[0] = route_t
    route_ref[...] = jnp.transpose(
        jnp.concatenate([route_t, jnp.zeros((LANES - N_EXPERTS, tm), F32)], axis=0))


def _router(xp, xs, mod_p, mod_s, seq, w_router, b_router, tm):
    ntp = xp.shape[0] // tm
    nt = ntp + 1
    return pl.pallas_call(
        functools.partial(_router_kernel, n_prompt_tiles=ntp),
        grid=(nt,),
        in_specs=_joint_specs(xp, xs, mod_p, mod_s, (4, 3), tm, seq) + [
            pl.BlockSpec((N_EXPERTS, D_MODEL), lambda i: (0, 0)),
            pl.BlockSpec((N_EXPERTS, LANES), lambda i: (0, 0)),
        ],
        out_specs=[pl.BlockSpec((tm, LANES), lambda i: (i, 0)),
                   pl.BlockSpec((1, SUBLANES, tm), lambda i: (i, 0, 0)),
                   pl.BlockSpec((1, N_EXPERTS, LANES), lambda i: (i, 0, 0))],
        out_shape=[jax.ShapeDtypeStruct((nt * tm, LANES), F32),
                   jax.ShapeDtypeStruct((nt, SUBLANES, tm), F32),
                   jax.ShapeDtypeStruct((nt, N_EXPERTS, LANES), F32)],
        scratch_shapes=[pltpu.VMEM((tm, D_MODEL), F32)],
        compiler_params=_cparams("arbitrary"),
        name="router",
    )(xp, xs, mod_p, mod_s, mod_p, mod_s, w_router, b_router)


def _dispatch_plan(route_t, counts, tme):
    n_rt, _, tm = route_t.shape
    n = n_rt * tm
    cnt = counts[:, :, 0].astype(jnp.int32)
    tile_off = jnp.cumsum(cnt, axis=0) - cnt
    tot = jnp.sum(cnt, axis=0)
    gsz = (tot + tme - 1) // tme * tme
    gend = jnp.cumsum(gsz)
    goff = gend - gsz
    base = (goff[None, :] + tile_off)[:, :, None]
    ids = jnp.arange(N_EXPERTS, dtype=jnp.int32)[None, :, None]
    field = lambda j: route_t[:, j:j + 1, :].astype(jnp.int32)
    pick = lambda e: jnp.sum(jnp.where(ids == e, base, 0), axis=1, keepdims=True)
    slot1 = (pick(field(R_E1)) + field(R_R1)).reshape(n)
    slot2 = (pick(field(R_E2)) + field(R_R2)).reshape(n)
    n_tiles = 2 * n // tme + N_EXPERTS
    start = jnp.arange(n_tiles, dtype=jnp.int32) * tme
    valid = (start < gend[-1]).astype(jnp.int32)
    blk = jnp.minimum(jnp.arange(n_tiles, dtype=jnp.int32), gend[-1] // tme - 1)
    tile_e = jnp.sum((blk * tme)[:, None] >= gend[None, :], axis=1).astype(jnp.int32)
    zoff = jnp.where(gsz > 0, gend - tme, (n_tiles + jnp.arange(N_EXPERTS, dtype=jnp.int32)) * tme)
    return slot1, slot2, zoff, tile_e, valid, blk, n_tiles


def _dispatch_kernel(s1_ref, s2_ref, zoff_ref, xp_ref, xs_ref, scp_ref, scs_ref, shp_ref, shs_ref,
                     ug_hbm, ubuf, zbuf, sem, *, n_prompt_tiles):
    i = pl.program_id(0)
    tm = xp_ref.shape[0]
    tme = zbuf.shape[0]

    n_steps = n_prompt_tiles + 1
    fill_sem = 2 * N_ROW_BUFS

    @pl.when(i == 0)
    def _():
        zbuf[...] = jnp.zeros_like(zbuf)
        fill = lambda e: pltpu.make_async_copy(
            zbuf, ug_hbm.at[pl.ds(pl.multiple_of(zoff_ref[e], 8), tme)], sem.at[fill_sem])
        for e in range(N_EXPERTS):
            fill(e).start()
        for e in range(N_EXPERTS):
            fill(e).wait()

    def drain(b):
        for stream in range(2):
            pltpu.make_async_copy(ubuf.at[b], ubuf.at[b], sem.at[2 * b + stream]).wait()

    buf = i % N_ROW_BUFS

    @pl.when(i >= N_ROW_BUFS)
    def _():
        drain(buf)

    _joint_modulated(i, n_prompt_tiles, xp_ref, xs_ref, scp_ref, scs_ref, shp_ref, shs_ref, ubuf.at[buf])
    base = i * tm

    def issue(j, carry):
        for k in range(SUBLANES):
            row = ubuf.at[buf, j, pl.ds(k, 1)]
            t = base + j * SUBLANES + k
            pltpu.make_async_copy(row, ug_hbm.at[pl.ds(s1_ref[t], 1)], sem.at[2 * buf]).start()
            pltpu.make_async_copy(row, ug_hbm.at[pl.ds(s2_ref[t], 1)], sem.at[2 * buf + 1]).start(priority=1)
        return carry

    lax.fori_loop(0, tm // SUBLANES, issue, 0)

    @pl.when(i == n_steps - 1)
    def _():
        for b in range(min(N_ROW_BUFS, n_steps)):
            drain(b)


def _dispatch(xp, xs, mod_p, mod_s, seq, slot1, slot2, zoff, tm, tme, rows):
    ntp = xp.shape[0] // tm
    return pl.pallas_call(
        functools.partial(_dispatch_kernel, n_prompt_tiles=ntp),
        grid_spec=pltpu.PrefetchScalarGridSpec(
            num_scalar_prefetch=3,
            grid=(ntp + 1,),
            in_specs=_joint_specs(xp, xs, mod_p, mod_s, (4, 3), tm, seq),
            out_specs=pl.BlockSpec(memory_space=pl.ANY),
            scratch_shapes=[pltpu.VMEM((N_ROW_BUFS, tm // SUBLANES, SUBLANES, D_MODEL), F32),
                            pltpu.VMEM((tme, D_MODEL), F32),
                            pltpu.SemaphoreType.DMA((2 * N_ROW_BUFS + 1,))],
        ),
        out_shape=jax.ShapeDtypeStruct((rows, D_MODEL), F32),
        compiler_params=_cparams("arbitrary"),
        name="dispatch",
    )(slot1, slot2, zoff, xp, xs, mod_p, mod_s, mod_p, mod_s)


def _expert_kernel(te_ref, tv_ref, tb_ref, ug_ref, wg_ref, wu_ref, wd_ref, ys_ref):
    i = pl.program_id(0)

    @pl.when(tv_ref[i] == 1)
    def _():
        u = ug_ref[...].astype(BF16)
        h = _silu(jnp.dot(u, wg_ref[0].astype(BF16), preferred_element_type=F32)) * jnp.dot(
            u, wu_ref[0].astype(BF16), preferred_element_type=F32)
        ys_ref[...] = jnp.dot(h.astype(BF16), wd_ref[0].astype(BF16), preferred_element_type=F32)

    @pl.when(tv_ref[i] == 0)
    def _():
        ys_ref[...] = jnp.zeros_like(ys_ref)


def _experts(ug, tile_e, valid, blk, w_gate, w_up, w_down, tme, n_tiles):
    d_e = w_gate.shape[-1]
    return pl.pallas_call(
        _expert_kernel,
        grid_spec=pltpu.PrefetchScalarGridSpec(
            num_scalar_prefetch=3,
            grid=(n_tiles,),
            in_specs=[
                pl.BlockSpec((tme, D_MODEL), lambda i, te, tv, tb: (tb[i], 0)),
                pl.BlockSpec((1, D_MODEL, d_e), lambda i, te, tv, tb: (te[i], 0, 0)),
                pl.BlockSpec((1, D_MODEL, d_e), lambda i, te, tv, tb: (te[i], 0, 0)),
                pl.BlockSpec((1, d_e, D_MODEL), lambda i, te, tv, tb: (te[i], 0, 0)),
            ],
            out_specs=pl.BlockSpec((tme, D_MODEL), lambda i, te, tv, tb: (i, 0)),
        ),
        out_shape=jax.ShapeDtypeStruct((n_tiles * tme, D_MODEL), F32),
        compiler_params=_cparams("arbitrary"),
        name="experts",
    )(tile_e, valid, blk, ug, w_gate, w_up, w_down)


def _combine_kernel(s1_ref, s2_ref, ys_hbm, route_ref, xp_ref, xs_ref, g2p_ref, g2s_ref, lg_ref, lb_ref,
                    outp_ref, outs_ref, y1buf, y2buf, sem, *, n_prompt_tiles):
    i = pl.program_id(0)
    tm = xp_ref.shape[0]
    n_steps = n_prompt_tiles + 1

    def gather(tile, b):
        def issue(j, carry):
            for k in range(SUBLANES):
                t = tile * tm + j * SUBLANES + k
                pltpu.make_async_copy(ys_hbm.at[pl.ds(s1_ref[t], 1)], y1buf.at[b, j, pl.ds(k, 1)],
                                      sem.at[2 * b]).start()
                pltpu.make_async_copy(ys_hbm.at[pl.ds(s2_ref[t], 1)], y2buf.at[b, j, pl.ds(k, 1)],
                                      sem.at[2 * b + 1]).start(priority=1)
            return carry

        lax.fori_loop(0, tm // SUBLANES, issue, 0)

    buf = i % N_ROW_BUFS

    @pl.when(i == 0)
    def _():
        gather(0, 0)

    @pl.when(i + 1 < n_steps)
    def _():
        gather(i + 1, (i + 1) % N_ROW_BUFS)

    pltpu.make_async_copy(y1buf.at[buf], y1buf.at[buf], sem.at[2 * buf]).wait()
    pltpu.make_async_copy(y2buf.at[buf], y2buf.at[buf], sem.at[2 * buf + 1]).wait()
    route = route_ref[...]
    y1 = y1buf[buf].reshape(tm, D_MODEL)
    y2 = y2buf[buf].reshape(tm, D_MODEL)
    f = route[:, R_G1:R_G1 + 1] * y1 + route[:, R_G2:R_G2 + 1] * y2

    @pl.when(i < n_prompt_tiles)
    def _():
        r = DEEPNORM_ALPHA * xp_ref[...] + (1.0 + g2p_ref[0]) * f
        outp_ref[...] = _layer_norm(r, lg_ref[...], lb_ref[...])

    @pl.when(i >= n_prompt_tiles)
    def _():
        r = DEEPNORM_ALPHA * xs_ref[...] + (1.0 + _tile_rows(g2s_ref, tm)) * f
        outs_ref[...] = _layer_norm(r, lg_ref[...], lb_ref[...])


def _combine(ys, slot1, slot2, route, xp, xs, mod_p, mod_s, seq, ln_g, ln_b, tm):
    ntp = xp.shape[0] // tm
    vec = lambda w: pl.BlockSpec((1, w), lambda i, *_: (0, 0))
    x_specs = _joint_specs(xp, xs, mod_p, mod_s, (5,), tm, seq)
    return pl.pallas_call(
        functools.partial(_combine_kernel, n_prompt_tiles=ntp),
        grid_spec=pltpu.PrefetchScalarGridSpec(
            num_scalar_prefetch=2,
            grid=(ntp + 1,),
            in_specs=[
                pl.BlockSpec(memory_space=pl.ANY),
                pl.BlockSpec((tm, LANES), lambda i, *_: (i, 0)),
            ] + x_specs + [vec(D_MODEL), vec(D_MODEL)],
            out_specs=[x_specs[0], x_specs[1]],
            scratch_shapes=[pltpu.VMEM((N_ROW_BUFS, tm // SUBLANES, SUBLANES, D_MODEL), F32),
                            pltpu.VMEM((N_ROW_BUFS, tm // SUBLANES, SUBLANES, D_MODEL), F32),
                            pltpu.SemaphoreType.DMA((2 * N_ROW_BUFS,))],
        ),
        out_shape=[jax.ShapeDtypeStruct(xp.shape, F32), jax.ShapeDtypeStruct(xs.shape, F32)],
        compiler_params=_cparams("arbitrary"),
        name="combine",
    )(slot1, slot2, ys, route, xp, xs, mod_p, mod_s, ln_g, ln_b)


def _moe(xp, xs, mod_p, mod_s, seq, w_router, b_router, w_gate, w_up, w_down, ln_g, ln_b, tm, tme):
    route, route_t, counts = _router(xp, xs, mod_p, mod_s, seq, w_router, b_router, tm)
    slot1, slot2, zoff, tile_e, valid, blk, n_tiles = _dispatch_plan(route_t, counts, tme)
    ug = _dispatch(xp, xs, mod_p, mod_s, seq, slot1, slot2, zoff, tm, tme, (n_tiles + N_EXPERTS) * tme)
    ys = _experts(ug, tile_e, valid, blk, w_gate, w_up, w_down, tme, n_tiles)
    return _combine(ys, slot1, slot2, route, xp, xs, mod_p, mod_s, seq, ln_g, ln_b, tm)


def _ffn_kernel(x_ref, sc_ref, sh_ref, g2_ref, wg_ref, wu_ref, wd_ref, lg_ref, lb_ref, out_ref, *, tf):
    x = x_ref[...]
    u = (x * (1.0 + sc_ref[0]) + sh_ref[0]).astype(BF16)
    y = None
    for f0 in range(0, wg_ref.shape[1], tf):
        h = _silu(jnp.dot(u, wg_ref[:, f0:f0 + tf], preferred_element_type=F32)) * jnp.dot(
            u, wu_ref[:, f0:f0 + tf], preferred_element_type=F32)
        part = jnp.dot(h.astype(BF16), wd_ref[f0:f0 + tf, :], preferred_element_type=F32)
        y = part if y is None else y + part
    r = DEEPNORM_ALPHA * x + (1.0 + g2_ref[0]) * y
    out_ref[...] = _layer_norm(r, lg_ref[...], lb_ref[...])


def _ffn(x, mod3, tpg, w_gate, w_up, w_down, ln_g, ln_b, tm, tf):
    n = x.shape[0]
    vec = lambda w: pl.BlockSpec((1, w), lambda i: (0, 0))
    resident = lambda a: pl.BlockSpec(a.shape, lambda i: (0, 0), pipeline_mode=pl.Buffered(1))
    return pl.pallas_call(
        functools.partial(_ffn_kernel, tf=tf),
        grid=(n // tm,),
        in_specs=[
            pl.BlockSpec((tm, D_MODEL), lambda i: (i, 0)),
            _mod_spec(mod3, 4, tpg),
            _mod_spec(mod3, 3, tpg),
            _mod_spec(mod3, 5, tpg),
            resident(w_gate), resident(w_up), resident(w_down),
            vec(D_MODEL), vec(D_MODEL),
        ],
        out_specs=pl.BlockSpec((tm, D_MODEL), lambda i: (i, 0)),
        out_shape=jax.ShapeDtypeStruct((n, D_MODEL), F32),
        compiler_params=_cparams("arbitrary"),
        name="ffn_dense",
    )(x, mod3, mod3, mod3, w_gate, w_up, w_down, ln_g, ln_b)


def _token_mixer(l, x, mod3, rows_per_group, prm, tm, mixer_fn):
    yc, o, z_src, z_blk, st = mixer_fn(l, x, mod3)
    x = _outproj(yc, o, z_src, z_blk, x, mod3, rows_per_group // tm, prm["norm_w"][l], prm["w_out"][l],
                 prm["ln_g"][l, 0:1], prm["ln_b"][l, 0:1], tm)
    return x, st


def kernel(x_prompt, x_sample, state_conv, state_dn_conv, state_dn, c_prompt, c_sample, w_in, conv_w,
           dn_conv_w, a_log, dt_bias, dn_norm_w, w_out, w_ada, b_ada, ln_g, ln_b, w_ff_gate, w_ff_up,
           w_ff_down, w_router, b_router, w_exp_gate, w_exp_up, w_exp_down):
    bp, seq, d = x_prompt.shape
    bs, steps, _ = x_sample.shape
    hp = jnp.zeros((DEPTH, 2, LANES), F32)
    hp = hp.at[:, 0, G_LANE:].set(a_log).at[:, 1, G_LANE:].set(dt_bias)
    w_in_t = jnp.swapaxes(w_in, 1, 2)
    prm = {
        "w_main": w_in_t[:, :P_MAIN].astype(BF16),
        "w_small": w_in_t[:, w_in_t.shape[1] - LANES:],
        "norm_w": dn_norm_w.reshape(DEPTH, 1, DN_HEAD_DIM),
        "w_out": w_out.astype(BF16),
        "ln_g": ln_g, "ln_b": ln_b,
        "w_ff_gate": w_ff_gate.astype(BF16), "w_ff_up": w_ff_up.astype(BF16),
        "w_ff_down": w_ff_down.astype(BF16),
        "w_router": jnp.swapaxes(w_router, 1, 2),
        "b_router": jnp.broadcast_to(b_router[:, :, None], b_router.shape + (LANES,)),
        "w_exp_gate": w_exp_gate, "w_exp_up": w_exp_up, "w_exp_down": w_exp_down,
    }

    mod = _ada(jnp.concatenate([c_prompt, c_sample], axis=0), w_ada, b_ada)
    mod_p = [mod[l, :bp].reshape(bp, 1, 6 * d) for l in range(DEPTH)]
    mod_s = [mod[l, bp:].reshape(1, bs, 6 * d) for l in range(DEPTH)]

    tm_p = PROMPT_ROW_TILE
    zc = jnp.zeros((bp, CONV_K - 1, CONV_WIDTH), F32)
    zd = jnp.zeros((bp, DN_CONV_K - 1, 3 * DN_WIDTH), F32)
    zs = jnp.zeros((bp, DN_HEADS, DN_HEAD_DIM, DN_HEAD_DIM), F32)

    def mixer_prompt(l, x, mod3):
        yc, qkv, z, gb, ncb, ndb = _inproj_pre_prompt(
            x, mod3, prm["w_main"][l], prm["w_small"][l], conv_w[l], dn_conv_w[l], hp[l], zc, zd,
            bp, seq, tm_p)
        o, s_new = _delta_prompt(qkv, gb, zs, bp, seq, PROMPT_DELTA_TILE)
        return yc, o, z, 0, (ncb, ndb, s_new)

    dn_states = []

    def mixer_sample(l, x, mod3):
        pm, ps = _inproj(x, mod3, steps, prm["w_main"][l], prm["w_small"][l], bs)
        cb_tm = jnp.transpose(state_conv[l], (1, 0, 2))
        db_tm = jnp.transpose(state_dn_conv[l], (1, 0, 2))
        yc, qkv, gb, ncb, ndb = _pre_sample(pm, ps, conv_w[l], dn_conv_w[l], hp[l], cb_tm, db_tm,
                                            bs, steps, SAMPLE_PRE_SEQS)
        to_b = lambda a: jnp.pad(jnp.transpose(a, (1, 0, 2)), ((0, 0), (0, SAMPLE_CHUNK - steps), (0, 0)))
        o_b, s_all = _delta_sample(to_b(qkv), to_b(gb), state_dn, l, dn_states[-1] if dn_states else None,
                                   SAMPLE_DELTA_SEQS)
        dn_states.append(s_all)
        o = jnp.transpose(o_b[:, :steps], (1, 0, 2)).reshape(steps * bs, DN_WIDTH)
        return (yc.reshape(steps * bs, CONV_WIDTH), o, pm, OFF_Z // DN_WIDTH,
                (jnp.transpose(ncb, (1, 0, 2)), jnp.transpose(ndb, (1, 0, 2))))

    x_p = x_prompt.reshape(bp * seq, d)
    x_s = jnp.transpose(x_sample, (1, 0, 2)).reshape(steps * bs, d)
    n_s = steps * bs
    st_p, st_s = [], []
    for l in range(DEPTH):
        x_p, st = _token_mixer(l, x_p, mod_p[l], seq, prm, tm_p, mixer_prompt)
        st_p.append(st)
        x_s, st = _token_mixer(l, x_s, mod_s[l], n_s, prm, bs, mixer_sample)
        st_s.append(st)
        j = l // 2
        ln = (prm["ln_g"][l, 1:2], prm["ln_b"][l, 1:2])
        if l % 2 == 0:
            ffn_w = (prm["w_ff_gate"][j], prm["w_ff_up"][j], prm["w_ff_down"][j])
            x_p = _ffn(x_p, mod_p[l], seq // tm_p, *ffn_w, *ln, tm_p, FFN_HIDDEN_CHUNK)
            x_s = _ffn(x_s, mod_s[l], n_s // bs, *ffn_w, *ln, bs, w_ff_gate.shape[-1])
        else:
            x_p, x_s = _moe(x_p, x_s, mod_p[l], mod_s[l], seq, prm["w_router"][j], prm["b_router"][j],
                            prm["w_exp_gate"][j], prm["w_exp_up"][j], prm["w_exp_down"][j], *ln, tm_p,
                            EXPERT_SLOT_TILE)
    y_p = x_p
    y_s = jnp.transpose(x_s.reshape(steps, bs, d), (1, 0, 2))

    stack = lambda sts, i: jnp.stack([s[i] for s in sts])
    return (y_p.reshape(bp, seq, d), y_s,
            stack(st_p, 0), stack(st_p, 1), stack(st_p, 2),
            stack(st_s, 0), stack(st_s, 1), dn_states[-1])
```

```python
import functools
import math

import jax
import jax.numpy as jnp
from jax import lax
from jax.experimental import pallas as pl
from jax.experimental.pallas import tpu as pltpu

F32 = jnp.float32
BF16 = jnp.bfloat16

D_MODEL = 1024
DEPTH = 2
CONV_WIDTH = 512
CONV_K = 3
DN_HEADS = 4
DN_HEAD_DIM = 128
DN_WIDTH = DN_HEADS * DN_HEAD_DIM
DN_CONV_K = 4
DN_CHUNK = 64
N_EXPERTS = 8
DEEPNORM_ALPHA = (2.0 * DEPTH) ** 0.25
LN_EPS = 1e-5
RMS_EPS = 1e-6

OFF_CC = CONV_WIDTH
OFF_CH = 2 * CONV_WIDTH
OFF_QKV = 3 * CONV_WIDTH
OFF_Z = OFF_QKV + 3 * DN_WIDTH
P_MAIN = OFF_Z + DN_WIDTH
LANES = 128
SUBLANES = 8
BETA_LANE = LANES - 2 * DN_HEADS
G_LANE = LANES - DN_HEADS
SAMPLE_CHUNK = 8

VMEM_LIMIT = 56 * 1024 * 1024
MXU_TILE = 256

PROMPT_ROW_TILE = 512
PROMPT_DELTA_TILE = 512
FFN_HIDDEN_CHUNK = MXU_TILE
EXPERT_SLOT_TILE = 512
SAMPLE_PRE_SEQS = 32
SAMPLE_DELTA_SEQS = 16


def _cparams(*sem):
    return pltpu.CompilerParams(dimension_semantics=sem, vmem_limit_bytes=VMEM_LIMIT)


def _silu(x):
    return x * jax.nn.sigmoid(x)


def _layer_norm(r, g, b):
    mu = jnp.mean(r, -1, keepdims=True)
    xc = r - mu
    var = jnp.mean(xc * xc, -1, keepdims=True)
    return xc * lax.rsqrt(var + LN_EPS) * g + b


def _split_bf16(a):
    hi = a.astype(BF16)
    lo = (a - hi.astype(F32)).astype(BF16)
    return hi, lo


_NN = (((1,), (0,)), ((), ()))
_NT = (((1,), (1,)), ((), ()))
_BNN = (((2,), (1,)), ((0,), (0,)))
_BNT = (((2,), (2,)), ((0,), (0,)))
_BTN = (((1,), (1,)), ((0,), (0,)))


def _mm(a, b, dims=_NN, mode="bf16"):
    if mode == "f32":
        return lax.dot_general(a, b, dims, precision=lax.Precision.HIGHEST, preferred_element_type=F32)
    if mode == "bf16":
        return lax.dot_general(a.astype(BF16), b.astype(BF16), dims, preferred_element_type=F32)
    ah, al = _split_bf16(a)
    bh, bl = _split_bf16(b)
    d = functools.partial(lax.dot_general, dimension_numbers=dims, preferred_element_type=F32)
    return d(ah, bh) + (d(ah, bl) + d(al, bh))


def _ada_kernel(c_ref, w_ref, b_ref, o_ref):
    s = _silu(c_ref[...]).astype(BF16)
    o_ref[0] = jnp.dot(s, w_ref[0].astype(BF16), preferred_element_type=F32) + b_ref[0]


def _ada(c_all, w_ada, b_ada):
    rows = c_all.shape[0]
    tn = 1536
    return pl.pallas_call(
        _ada_kernel,
        grid=(DEPTH, 6 * D_MODEL // tn),
        in_specs=[
            pl.BlockSpec((rows, D_MODEL), lambda l, j: (0, 0)),
            pl.BlockSpec((1, D_MODEL, tn), lambda l, j: (l, 0, j)),
            pl.BlockSpec((1, 1, tn), lambda l, j: (l, 0, j)),
        ],
        out_specs=pl.BlockSpec((1, rows, tn), lambda l, j: (l, 0, j)),
        out_shape=jax.ShapeDtypeStruct((DEPTH, rows, 6 * D_MODEL), F32),
        compiler_params=_cparams("arbitrary", "arbitrary"),
        name="ada",
    )(c_all, w_ada, b_ada.reshape(DEPTH, 1, 6 * D_MODEL))


def _mod_spec(mod3, chunk, tiles_per_group):
    r = mod3.shape[1]
    return pl.BlockSpec((1, r, D_MODEL), lambda i, *_: (i // tiles_per_group, 0, chunk))


def _inproj_kernel(x_ref, sc_ref, sh_ref, wm_ref, ws_ref, pm_ref, ps_ref):
    u = x_ref[...] * (1.0 + sc_ref[0]) + sh_ref[0]
    pm_ref[...] = lax.dot_general(u.astype(BF16), wm_ref[...], _NT, preferred_element_type=F32)
    ps_ref[...] = _mm(u, ws_ref[...], _NT, mode="x3")


def _inproj(x, mod3, tpg, w_main, w_small, tm):
    n = x.shape[0]
    return pl.pallas_call(
        _inproj_kernel,
        grid=(n // tm,),
        in_specs=[
            pl.BlockSpec((tm, D_MODEL), lambda i: (i, 0)),
            _mod_spec(mod3, 1, tpg),
            _mod_spec(mod3, 0, tpg),
            pl.BlockSpec((P_MAIN, D_MODEL), lambda i: (0, 0)),
            pl.BlockSpec((LANES, D_MODEL), lambda i: (0, 0)),
        ],
        out_specs=[
            pl.BlockSpec((tm, P_MAIN), lambda i: (i, 0)),
            pl.BlockSpec((tm, LANES), lambda i: (i, 0)),
        ],
        out_shape=[
            jax.ShapeDtypeStruct((n, P_MAIN), F32),
            jax.ShapeDtypeStruct((n, LANES), F32),
        ],
        compiler_params=_cparams("arbitrary"),
        name="inproj",
    )(x, mod3, mod3, w_main, w_small)


def _gates(ps, hp):
    lane = lax.broadcasted_iota(jnp.int32, ps.shape, 1)
    beta = jax.nn.sigmoid(ps)
    g = -jnp.exp(hp[0:1, :]) * jax.nn.softplus(ps + hp[1:2, :])
    return jnp.where(lane >= G_LANE, g, jnp.where(lane >= BETA_LANE, beta, 0.0))


def _qkv_finish(y, out_ref, idx, first_head=0):
    y = _silu(y)
    for j in range(y.shape[1] // DN_HEAD_DIM):
        h = first_head + j
        seg = y[:, j * DN_HEAD_DIM:(j + 1) * DN_HEAD_DIM]
        if h < 2 * DN_HEADS:
            seg = seg * lax.rsqrt(jnp.sum(seg * seg, -1, keepdims=True) + RMS_EPS)
            if h < DN_HEADS:
                seg = seg * (DN_HEAD_DIM ** -0.5)
        out_ref[idx + (slice(None), slice(h * DN_HEAD_DIM, (h + 1) * DN_HEAD_DIM))] = seg


def _shifted(x, tail_ref, i, row):
    s = pltpu.roll(x, i, 0)
    top = s[0:SUBLANES]
    for r in range(i):
        top = jnp.where(row == r, tail_ref[8 - i + r:8 - i + r + 1, :], top)
    return jnp.concatenate([top, s[SUBLANES:]], axis=0)


def _inproj_pre_prompt_kernel(x_ref, sc_ref, sh_ref, wm_ref, ws_ref, cw_ref, dw_ref, hp_ref, cb_ref, db_ref,
                              yc_ref, qkv_ref, z_ref, gb_ref, ncb_ref, ndb_ref, tc_scr, td_scr):
    t = pl.program_id(1)
    tm = x_ref.shape[0]

    @pl.when(t == 0)
    def _():
        tc_scr[8 - (CONV_K - 1):8, :] = cb_ref[0]
        td_scr[8 - (DN_CONV_K - 1):8, :] = db_ref[0]

    u_f32 = x_ref[...] * (1.0 + sc_ref[0]) + sh_ref[0]
    u = u_f32.astype(BF16)
    proj = lambda lo, hi: lax.dot_general(u, wm_ref[lo:hi, :], _NT, preferred_element_type=F32)

    row = lax.broadcasted_iota(jnp.int32, (SUBLANES, DN_WIDTH), 0)

    def dn_part(xp, part):
        cols = slice(part * DN_WIDTH, (part + 1) * DN_WIDTH)
        tail = td_scr.at[:, cols]
        yp = _shifted(xp, tail, 3, row) * dw_ref[0:1, cols]
        yp = yp + _shifted(xp, tail, 2, row) * dw_ref[1:2, cols]
        yp = yp + _shifted(xp, tail, 1, row) * dw_ref[2:3, cols]
        yp = yp + xp * dw_ref[3:4, cols]
        _qkv_finish(yp, qkv_ref, (), first_head=part * DN_HEADS)
        return xp[tm - (DN_CONV_K - 1):tm, :]

    qkv_cols = lambda part: (OFF_QKV + part * DN_WIDTH, OFF_QKV + (part + 1) * DN_WIDTH)
    x_q = proj(*qkv_cols(0))
    x_k = proj(*qkv_cols(1))
    last_q = dn_part(x_q, 0)
    x_v = proj(*qkv_cols(2))
    last_k = dn_part(x_k, 1)
    p_c = proj(OFF_CC, OFF_CH)
    last_v = dn_part(x_v, 2)
    p_h = proj(OFF_CH, OFF_QKV)
    cgh = p_c * p_h
    p_b = proj(0, OFF_CC)
    y = _shifted(cgh, tc_scr, 2, row) * cw_ref[0:1, :]
    y = y + _shifted(cgh, tc_scr, 1, row) * cw_ref[1:2, :]
    y = y + cgh * cw_ref[2:3, :]
    p_z = proj(OFF_Z, P_MAIN)
    yc_ref[...] = (p_b * y).astype(yc_ref.dtype)
    last_c = cgh[tm - (CONV_K - 1):tm, :]
    small = _mm(u_f32, ws_ref[...], _NT, mode="bf16")
    z_ref[...] = p_z.astype(z_ref.dtype)
    gb_ref[...] = _gates(small, hp_ref[...])

    last_d = jnp.concatenate([last_q, last_k, last_v], axis=-1)
    tc_scr[8 - (CONV_K - 1):8, :] = last_c
    td_scr[8 - (DN_CONV_K - 1):8, :] = last_d

    @pl.when(t == pl.num_programs(1) - 1)
    def _():
        ncb_ref[0] = last_c
        ndb_ref[0] = last_d


def _inproj_pre_prompt(x, mod3, w_main, w_small, conv_w, dn_conv_w, hp, conv_buf, dn_buf, bsz, seq, tm):
    n = x.shape[0]
    nt = seq // tm
    rows = lambda w: pl.BlockSpec((tm, w), lambda b, t: (b * nt + t, 0))
    full = lambda a: pl.BlockSpec(a.shape, lambda b, t: (0,) * a.ndim)
    per_b = lambda k, w: pl.BlockSpec((1, k, w), lambda b, t: (b, 0, 0))
    mod = lambda chunk: pl.BlockSpec((1, 1, D_MODEL), lambda b, t: (b, 0, chunk))
    return pl.pallas_call(
        _inproj_pre_prompt_kernel,
        grid=(bsz, nt),
        in_specs=[rows(D_MODEL), mod(1), mod(0), full(w_main), full(w_small),
                  full(conv_w), full(dn_conv_w), full(hp),
                  per_b(CONV_K - 1, CONV_WIDTH), per_b(DN_CONV_K - 1, 3 * DN_WIDTH)],
        out_specs=[rows(CONV_WIDTH), rows(3 * DN_WIDTH), rows(DN_WIDTH), rows(LANES),
                   per_b(CONV_K - 1, CONV_WIDTH), per_b(DN_CONV_K - 1, 3 * DN_WIDTH)],
        out_shape=[
            jax.ShapeDtypeStruct((n, CONV_WIDTH), BF16),
            jax.ShapeDtypeStruct((n, 3 * DN_WIDTH), F32),
            jax.ShapeDtypeStruct((n, DN_WIDTH), BF16),
            jax.ShapeDtypeStruct((n, LANES), F32),
            jax.ShapeDtypeStruct((bsz, CONV_K - 1, CONV_WIDTH), F32),
            jax.ShapeDtypeStruct((bsz, DN_CONV_K - 1, 3 * DN_WIDTH), F32),
        ],
        scratch_shapes=[pltpu.VMEM((SUBLANES, CONV_WIDTH), F32), pltpu.VMEM((SUBLANES, 3 * DN_WIDTH), F32)],
        compiler_params=_cparams("arbitrary", "arbitrary"),
        name="inproj_pre_prompt",
    )(x, mod3, mod3, w_main, w_small, conv_w, dn_conv_w, hp, conv_buf, dn_buf)


def _pre_sample_kernel(pm_ref, ps_ref, cw_ref, dw_ref, hp_ref, cb_ref, db_ref,
                       yc_ref, qkv_ref, gb_ref, ncb_ref, ndb_ref):
    steps = pm_ref.shape[0]
    cgh = [pm_ref[t, :, OFF_CC:OFF_CH] * pm_ref[t, :, OFF_CH:OFF_QKV] for t in range(steps)]
    ext = [cb_ref[i] for i in range(CONV_K - 1)] + cgh
    for t in range(steps):
        y = ext[t] * cw_ref[0:1, :]
        for i in range(1, CONV_K):
            y = y + ext[t + i] * cw_ref[i:i + 1, :]
        yc_ref[t] = pm_ref[t, :, 0:OFF_CC] * y
    for i in range(CONV_K - 1):
        ncb_ref[i] = ext[len(ext) - (CONV_K - 1) + i]

    xq = [pm_ref[t, :, OFF_QKV:OFF_Z] for t in range(steps)]
    extq = [db_ref[i] for i in range(DN_CONV_K - 1)] + xq
    for t in range(steps):
        y = extq[t] * dw_ref[0:1, :]
        for i in range(1, DN_CONV_K):
            y = y + extq[t + i] * dw_ref[i:i + 1, :]
        _qkv_finish(y, qkv_ref, (t,))
        gb_ref[t] = _gates(ps_ref[t], hp_ref[...])
    for i in range(DN_CONV_K - 1):
        ndb_ref[i] = extq[len(extq) - (DN_CONV_K - 1) + i]


def _pre_sample(pm, ps, conv_w, dn_conv_w, hp, conv_buf_tm, dn_buf_tm, bsz, steps, bt):
    slab = lambda k, w: pl.BlockSpec((k, bt, w), lambda i: (0, i, 0))
    full = lambda a: pl.BlockSpec(a.shape, lambda i: (0,) * a.ndim)
    return pl.pallas_call(
        _pre_sample_kernel,
        grid=(bsz // bt,),
        in_specs=[slab(steps, P_MAIN), slab(steps, LANES), full(conv_w), full(dn_conv_w), full(hp),
                  slab(CONV_K - 1, CONV_WIDTH), slab(DN_CONV_K - 1, 3 * DN_WIDTH)],
        out_specs=[slab(steps, CONV_WIDTH), slab(steps, 3 * DN_WIDTH), slab(steps, LANES),
                   slab(CONV_K - 1, CONV_WIDTH), slab(DN_CONV_K - 1, 3 * DN_WIDTH)],
        out_shape=[
            jax.ShapeDtypeStruct((steps, bsz, CONV_WIDTH), F32),
            jax.ShapeDtypeStruct((steps, bsz, 3 * DN_WIDTH), F32),
            jax.ShapeDtypeStruct((steps, bsz, LANES), F32),
            jax.ShapeDtypeStruct((CONV_K - 1, bsz, CONV_WIDTH), F32),
            jax.ShapeDtypeStruct((DN_CONV_K - 1, bsz, 3 * DN_WIDTH), F32),
        ],
        compiler_params=_cparams("arbitrary"),
        name="pre_sample",
    )(pm.reshape(steps, bsz, P_MAIN), ps.reshape(steps, bsz, LANES), conv_w, dn_conv_w, hp,
      conv_buf_tm, dn_buf_tm)


def _cumsum_chunks(x, c):
    row = lax.broadcasted_iota(jnp.int32, x.shape, 0) & (c - 1)
    s = 1
    while s < c:
        x = x + jnp.where(row >= s, pltpu.roll(x, s, 0), 0.0)
        s *= 2
    return x


def _gdn_problems(qkv, gb, c):
    n = qkv.shape[0] // c
    gc = _cumsum_chunks(gb, c)
    q, k, v, g_b, beta_b = [], [], [], [], []
    for ci in range(n):
        rs = slice(ci * c, (ci + 1) * c)
        for h in range(DN_HEADS):
            ls = lambda base: slice(base + h * DN_HEAD_DIM, base + (h + 1) * DN_HEAD_DIM)
            q.append(qkv[rs, ls(0)])
            k.append(qkv[rs, ls(DN_WIDTH)])
            v.append(qkv[rs, ls(2 * DN_WIDTH)])
            g_b.append(jnp.broadcast_to(gc[rs, G_LANE + h:G_LANE + h + 1], (c, DN_HEAD_DIM)))
            beta_b.append(jnp.broadcast_to(gb[rs, BETA_LANE + h:BETA_LANE + h + 1], (c, DN_HEAD_DIM)))
    return tuple(jnp.stack(a) for a in (q, k, v, g_b, beta_b))


def _unit_lower_solve(m, rhs, c, small, mode):
    if small:
        sol = rhs
        for j in range(c - 1):
            sol = sol - m[:, :, j:j + 1] * sol[:, j:j + 1, :]
        return sol
    row = lax.broadcasted_iota(jnp.int32, m.shape, 1)
    col = lax.broadcasted_iota(jnp.int32, m.shape, 2)
    p = -m
    t = jnp.where(row == col, 1.0, 0.0) + p
    p = _mm(p, p, _BNN, "bf16")
    levels = int(math.log2(c))
    for lvl in range(1, levels):
        if lvl < levels - 1:
            y = _mm(jnp.concatenate([t, p], axis=1), p, _BNN, "bf16")
            t = t + y[:, :c]
            p = y[:, c:]
        else:
            t = t + _mm(t, p, _BNN, "bf16")
    x = _mm(t, rhs, _BNN, "bf16")
    resid = rhs - x - _mm(m, x, _BNN, mode)
    return x + _mm(t, resid, _BNN, "bf16")


def _gdn_solve(q, k, v, g_b, beta_b, *, c, small, mode_gram, mode_solve):
    n = q.shape[0]
    row = lax.broadcasted_iota(jnp.int32, (n, c, c), 1)
    col = lax.broadcasted_iota(jnp.int32, (n, c, c), 2)
    g_cc = g_b[:, :, :c]
    g_row = jnp.sum(jnp.where(row == col, g_cc, 0.0), axis=1, keepdims=True)
    gamma = jnp.exp(jnp.where(row >= col, g_cc - g_row, -jnp.inf))
    kbeta = k * beta_b
    exp_g = jnp.exp(g_b)
    gram = _mm(jnp.concatenate([kbeta, q], axis=1), k, _BNT, mode_gram)
    m = jnp.where(row > col, gram[:, :c] * gamma, 0.0)
    attn = gram[:, c:] * gamma
    rhs = jnp.concatenate([kbeta * exp_g, v * beta_b], axis=-1)
    sol = _unit_lower_solve(m, rhs, c, small, mode_solve)
    g_last = g_b[:, c - 1:c, :]
    return sol, attn, q * exp_g, k * jnp.exp(g_last - g_b), jnp.exp(g_last)


def _gdn_fold(sol, attn, q_dec, k_dec, mode):
    kd = _mm(k_dec, sol, _BTN, mode)
    at = _mm(attn, sol, _BNN, mode)
    return kd[..., :DN_HEAD_DIM], kd[..., DN_HEAD_DIM:], q_dec - at[..., :DN_HEAD_DIM], at[..., DN_HEAD_DIM:]


def _gdn_state_step(a_mat, b_mat, q_t, o_intra, d_last, s, mode):
    r = _mm(jnp.concatenate([a_mat, q_t], axis=1), s, _BNN, mode)
    s_new = s * d_last - r[:, :DN_HEAD_DIM] + b_mat
    return r[:, DN_HEAD_DIM:] + o_intra, s_new


def _gdn_direct_step(sol, attn, q_dec, k_dec, d_last, s, mode):
    c = attn.shape[1]
    r = _mm(jnp.concatenate([sol[..., :DN_HEAD_DIM], q_dec], axis=1), s, _BNN, mode)
    v_new = sol[..., DN_HEAD_DIM:] - r[:, :c]
    o = r[:, c:] + _mm(attn, v_new, _BNN, mode)
    return o, s * d_last + _mm(k_dec, v_new, _BTN, mode)


_PROMPT_MODES = dict(mode_gram="bf16", mode_solve="x3")
_SAMPLE_MODES = dict(mode_gram="bf16", mode_solve="bf16")
_STATE_MODE = "bf16"


def _delta_prompt_kernel(qkv_ref, gb_ref, s0_ref, o_ref, sn_ref, s_scr):
    t = pl.program_id(1)
    tc = qkv_ref.shape[0]

    @pl.when(t == 0)
    def _():
        s_scr[...] = s0_ref[0]

    probs = _gdn_problems(qkv_ref[...], gb_ref[...], DN_CHUNK)
    sol, attn, q_dec, k_dec, d_last = _gdn_solve(*probs, c=DN_CHUNK, small=False, **_PROMPT_MODES)
    a_mat, b_mat, q_t, o_intra = _gdn_fold(sol, attn, q_dec, k_dec, _STATE_MODE)
    s = s_scr[...]
    for ci in range(tc // DN_CHUNK):
        ps = slice(ci * DN_HEADS, (ci + 1) * DN_HEADS)
        o, s = _gdn_state_step(a_mat[ps], b_mat[ps], q_t[ps], o_intra[ps], d_last[ps], s, _STATE_MODE)
        for h in range(DN_HEADS):
            o_ref[ci * DN_CHUNK:(ci + 1) * DN_CHUNK, h * DN_HEAD_DIM:(h + 1) * DN_HEAD_DIM] = (
                o[h].astype(o_ref.dtype))
    s_scr[...] = s

    @pl.when(t == pl.num_programs(1) - 1)
    def _():
        sn_ref[0] = s


def _delta_prompt(qkv, gb, s0, bsz, seq, tc):
    n = qkv.shape[0]
    nt = seq // tc
    rows = lambda w: pl.BlockSpec((tc, w), lambda b, t: (b * nt + t, 0))
    state = pl.BlockSpec((1, DN_HEADS, DN_HEAD_DIM, DN_HEAD_DIM), lambda b, t: (b, 0, 0, 0))
    return pl.pallas_call(
        _delta_prompt_kernel,
        grid=(bsz, nt),
        in_specs=[rows(3 * DN_WIDTH), rows(LANES), state],
        out_specs=[rows(DN_WIDTH), state],
        out_shape=[
            jax.ShapeDtypeStruct((n, DN_WIDTH), BF16),
            jax.ShapeDtypeStruct((bsz, DN_HEADS, DN_HEAD_DIM, DN_HEAD_DIM), F32),
        ],
        scratch_shapes=[pltpu.VMEM((DN_HEADS, DN_HEAD_DIM, DN_HEAD_DIM), F32)],
        compiler_params=_cparams("arbitrary", "arbitrary"),
        name="delta_prompt",
    )(qkv, gb, s0)


def _delta_sample_kernel(qkv_ref, gb_ref, s0_ref, *rest):
    o_ref, sn_ref = rest[-2:]
    bt = qkv_ref.shape[0]
    c = SAMPLE_CHUNK
    probs = _gdn_problems(qkv_ref[...].reshape(bt * c, 3 * DN_WIDTH), gb_ref[...].reshape(bt * c, LANES), c)
    parts = _gdn_solve(*probs, c=c, small=True, **_SAMPLE_MODES)
    s = s0_ref[0].reshape(bt * DN_HEADS, DN_HEAD_DIM, DN_HEAD_DIM)
    o, s = _gdn_direct_step(*parts, s, _STATE_MODE)
    for b in range(bt):
        for h in range(DN_HEADS):
            o_ref[b, :, h * DN_HEAD_DIM:(h + 1) * DN_HEAD_DIM] = o[b * DN_HEADS + h]
    sn_ref[0] = s.reshape(bt, DN_HEADS, DN_HEAD_DIM, DN_HEAD_DIM)


def _delta_sample(qkv_b, gb_b, state_all, layer, carried, bt):
    bsz = qkv_b.shape[0]
    blk = lambda w: pl.BlockSpec((bt, SAMPLE_CHUNK, w), lambda i: (i, 0, 0))
    state = pl.BlockSpec((1, bt, DN_HEADS, DN_HEAD_DIM, DN_HEAD_DIM), lambda i: (layer, i, 0, 0, 0))
    in_specs, args, aliases = [blk(3 * DN_WIDTH), blk(LANES), state], [qkv_b, gb_b, state_all], {}
    if carried is not None:
        in_specs.append(pl.BlockSpec(memory_space=pl.ANY))
        args.append(carried)
        aliases = {3: 1}
    return pl.pallas_call(
        _delta_sample_kernel,
        grid=(bsz // bt,),
        in_specs=in_specs,
        out_specs=[blk(DN_WIDTH), state],
        out_shape=[
            jax.ShapeDtypeStruct((bsz, SAMPLE_CHUNK, DN_WIDTH), F32),
            jax.ShapeDtypeStruct(state_all.shape, F32),
        ],
        input_output_aliases=aliases,
        compiler_params=_cparams("arbitrary"),
        name="delta_sample",
    )(*args)


def _outproj_kernel(yc_ref, o_ref, z_ref, x_ref, g1_ref, nw_ref, w_ref, lg_ref, lb_ref, out_ref):
    o = o_ref[...].astype(F32)
    parts = []
    for h in range(DN_HEADS):
        oh = o[:, h * DN_HEAD_DIM:(h + 1) * DN_HEAD_DIM]
        parts.append(oh * lax.rsqrt(jnp.mean(oh * oh, -1, keepdims=True) + RMS_EPS) * nw_ref[...])
    og = jnp.concatenate(parts, axis=-1) * _silu(z_ref[...].astype(F32))
    mixed = jnp.concatenate([yc_ref[...].astype(BF16), og.astype(BF16)], axis=-1)
    m = jnp.dot(mixed, w_ref[...], preferred_element_type=F32)
    r = DEEPNORM_ALPHA * x_ref[...] + (1.0 + g1_ref[0]) * m
    out_ref[...] = _layer_norm(r, lg_ref[...], lb_ref[...])


def _outproj(yc, o, z_src, z_blk, x, mod3, tpg, norm_w, w_out, ln_g, ln_b, tm):
    n = x.shape[0]
    vec = lambda w: pl.BlockSpec((1, w), lambda i: (0, 0))
    return pl.pallas_call(
        _outproj_kernel,
        grid=(n // tm,),
        in_specs=[
            pl.BlockSpec((tm, CONV_WIDTH), lambda i: (i, 0)),
            pl.BlockSpec((tm, DN_WIDTH), lambda i: (i, 0)),
            pl.BlockSpec((tm, DN_WIDTH), lambda i: (i, z_blk)),
            pl.BlockSpec((tm, D_MODEL), lambda i: (i, 0)),
            _mod_spec(mod3, 2, tpg),
            vec(DN_HEAD_DIM),
            pl.BlockSpec((D_MODEL, D_MODEL), lambda i: (0, 0)),
            vec(D_MODEL), vec(D_MODEL),
        ],
        out_specs=pl.BlockSpec((tm, D_MODEL), lambda i: (i, 0)),
        out_shape=jax.ShapeDtypeStruct((n, D_MODEL), F32),
        compiler_params=_cparams("arbitrary"),
        name="outproj",
    )(yc, o, z_src, x, mod3, norm_w, w_out, ln_g, ln_b)


R_E1, R_E2, R_G1, R_G2, R_R1, R_R2 = range(6)
N_ROW_BUFS = 2


def _joint_specs(xp, xs, mod_p, mod_s, chunks, tm, seq):
    last = xp.shape[0] // tm - 1
    tpg = seq // tm
    assert xs.shape[0] == tm
    specs = [pl.BlockSpec((tm, D_MODEL), lambda i, *_: (jnp.minimum(i, last), 0)),
             pl.BlockSpec((tm, D_MODEL), lambda i, *_: (0, 0))]
    for c in chunks:
        specs.append(pl.BlockSpec((1, 1, D_MODEL), lambda i, *_, c=c: (jnp.minimum(i, last) // tpg, 0, c)))
        specs.append(pl.BlockSpec((1, mod_s.shape[1], D_MODEL), lambda i, *_, c=c: (0, 0, c)))
    return specs


def _tile_rows(mod_ref, rows):
    m = mod_ref[0]
    return jnp.concatenate([m] * (rows // m.shape[0]), axis=0)


def _joint_modulated(i, n_prompt_tiles, xp_ref, xs_ref, scp_ref, scs_ref, shp_ref, shs_ref, u_ref):
    tm = xp_ref.shape[0]

    @pl.when(i < n_prompt_tiles)
    def _():
        u_ref[...] = (xp_ref[...] * (1.0 + scp_ref[0]) + shp_ref[0]).reshape(u_ref.shape)

    @pl.when(i >= n_prompt_tiles)
    def _():
        u_ref[...] = (xs_ref[...] * (1.0 + _tile_rows(scs_ref, tm)) + _tile_rows(shs_ref, tm)).reshape(
            u_ref.shape)


def _router_kernel(xp_ref, xs_ref, scp_ref, scs_ref, shp_ref, shs_ref, wr_ref, br_ref,
                   route_ref, route_t_ref, cnt_ref, u_scr, *, n_prompt_tiles):
    tm = xp_ref.shape[0]
    _joint_modulated(pl.program_id(0), n_prompt_tiles, xp_ref, xs_ref, scp_ref, scs_ref, shp_ref, shs_ref,
                     u_scr)
    lt = _mm(wr_ref[...], u_scr[...], _NT, mode="x3") + br_ref[:, 0:1]
    e_idx = lax.broadcasted_iota(jnp.int32, lt.shape, 0)
    ex = jnp.exp(lt - jnp.max(lt, 0, keepdims=True))
    probs = ex / jnp.sum(ex, 0, keepdims=True)
    p1 = jnp.max(probs, 0, keepdims=True)
    i1 = jnp.min(jnp.where(probs == p1, e_idx, N_EXPERTS), 0, keepdims=True)
    rest = jnp.where(e_idx == i1, -1.0, probs)
    p2 = jnp.max(rest, 0, keepdims=True)
    i2 = jnp.min(jnp.where(rest == p2, e_idx, N_EXPERTS), 0, keepdims=True)
    tot = p1 + p2
    sel = jnp.where(e_idx == i1, 1.0, 0.0) + jnp.where(e_idx == i2, 1.0, 0.0)
    r = lax.broadcasted_iota(jnp.int32, (tm, tm), 0)
    c = lax.broadcasted_iota(jnp.int32, (tm, tm), 1)
    incl = jnp.dot(sel.astype(BF16), jnp.where(r <= c, 1.0, 0.0).astype(BF16), preferred_element_type=F32)
    excl = incl - sel
    r1 = jnp.sum(jnp.where(e_idx == i1, excl, 0.0), 0, keepdims=True)
    r2 = jnp.sum(jnp.where(e_idx == i2, excl, 0.0), 0, keepdims=True)
    cnt_ref[0] = jnp.broadcast_to(incl[:, tm - 1:tm], (N_EXPERTS, LANES))
    route_t = jnp.zeros_like(lt)
    for j, val in enumerate((i1.astype(F32), i2.astype(F32), p1 / tot, p2 / tot, r1, r2)):
        route_t = jnp.where(e_idx == j, val, route_t)
    route_t_ref[0] = route_t
    route_ref[...] = jnp.transpose(
        jnp.concatenate([route_t, jnp.zeros((LANES - N_EXPERTS, tm), F32)], axis=0))


def _router(xp, xs, mod_p, mod_s, seq, w_router, b_router, tm):
    ntp = xp.shape[0] // tm
    nt = ntp + 1
    return pl.pallas_call(
        functools.partial(_router_kernel, n_prompt_tiles=ntp),
        grid=(nt,),
        in_specs=_joint_specs(xp, xs, mod_p, mod_s, (4, 3), tm, seq) + [
            pl.BlockSpec((N_EXPERTS, D_MODEL), lambda i: (0, 0)),
            pl.BlockSpec((N_EXPERTS, LANES), lambda i: (0, 0)),
        ],
        out_specs=[pl.BlockSpec((tm, LANES), lambda i: (i, 0)),
                   pl.BlockSpec((1, SUBLANES, tm), lambda i: (i, 0, 0)),
                   pl.BlockSpec((1, N_EXPERTS, LANES), lambda i: (i, 0, 0))],
        out_shape=[jax.ShapeDtypeStruct((nt * tm, LANES), F32),
                   jax.ShapeDtypeStruct((nt, SUBLANES, tm), F32),
                   jax.ShapeDtypeStruct((nt, N_EXPERTS, LANES), F32)],
        scratch_shapes=[pltpu.VMEM((tm, D_MODEL), F32)],
        compiler_params=_cparams("arbitrary"),
        name="router",
    )(xp, xs, mod_p, mod_s, mod_p, mod_s, w_router, b_router)


def _dispatch_plan(route_t, counts, tme):
    n_rt, _, tm = route_t.shape
    n = n_rt * tm
    cnt = counts[:, :, 0].astype(jnp.int32)
    tile_off = jnp.cumsum(cnt, axis=0) - cnt
    tot = jnp.sum(cnt, axis=0)
    gsz = (tot + tme - 1) // tme * tme
    gend = jnp.cumsum(gsz)
    goff = gend - gsz
    base = (goff[None, :] + tile_off)[:, :, None]
    ids = jnp.arange(N_EXPERTS, dtype=jnp.int32)[None, :, None]
    field = lambda j: route_t[:, j:j + 1, :].astype(jnp.int32)
    pick = lambda e: jnp.sum(jnp.where(ids == e, base, 0), axis=1, keepdims=True)
    slot1 = (pick(field(R_E1)) + field(R_R1)).reshape(n)
    slot2 = (pick(field(R_E2)) + field(R_R2)).reshape(n)
    n_tiles = 2 * n // tme + N_EXPERTS
    start = jnp.arange(n_tiles, dtype=jnp.int32) * tme
    valid = (start < gend[-1]).astype(jnp.int32)
    blk = jnp.minimum(jnp.arange(n_tiles, dtype=jnp.int32), gend[-1] // tme - 1)
    tile_e = jnp.sum((blk * tme)[:, None] >= gend[None, :], axis=1).astype(jnp.int32)
    zoff = jnp.where(gsz > 0, gend - tme, (n_tiles + jnp.arange(N_EXPERTS, dtype=jnp.int32)) * tme)
    return slot1, slot2, zoff, tile_e, valid, blk, n_tiles


def _dispatch_kernel(s1_ref, s2_ref, zoff_ref, xp_ref, xs_ref, scp_ref, scs_ref, shp_ref, shs_ref,
                     ug_hbm, ubuf, zbuf, sem, *, n_prompt_tiles):
    i = pl.program_id(0)
    tm = xp_ref.shape[0]
    tme = zbuf.shape[0]

    n_steps = n_prompt_tiles + 1
    fill_sem = 2 * N_ROW_BUFS

    @pl.when(i == 0)
    def _():
        zbuf[...] = jnp.zeros_like(zbuf)
        fill = lambda e: pltpu.make_async_copy(
            zbuf, ug_hbm.at[pl.ds(pl.multiple_of(zoff_ref[e], 8), tme)], sem.at[fill_sem])
        for e in range(N_EXPERTS):
            fill(e).start()
        for e in range(N_EXPERTS):
            fill(e).wait()

    def drain(b):
        for stream in range(2):
            pltpu.make_async_copy(ubuf.at[b], ubuf.at[b], sem.at[2 * b + stream]).wait()

    buf = i % N_ROW_BUFS

    @pl.when(i >= N_ROW_BUFS)
    def _():
        drain(buf)

    _joint_modulated(i, n_prompt_tiles, xp_ref, xs_ref, scp_ref, scs_ref, shp_ref, shs_ref, ubuf.at[buf])
    base = i * tm

    def issue(j, carry):
        for k in range(SUBLANES):
            row = ubuf.at[buf, j, pl.ds(k, 1)]
            t = base + j * SUBLANES + k
            pltpu.make_async_copy(row, ug_hbm.at[pl.ds(s1_ref[t], 1)], sem.at[2 * buf]).start()
            pltpu.make_async_copy(row, ug_hbm.at[pl.ds(s2_ref[t], 1)], sem.at[2 * buf + 1]).start(priority=1)
        return carry

    lax.fori_loop(0, tm // SUBLANES, issue, 0)

    @pl.when(i == n_steps - 1)
    def _():
        for b in range(min(N_ROW_BUFS, n_steps)):
            drain(b)


def _dispatch(xp, xs, mod_p, mod_s, seq, slot1, slot2, zoff, tm, tme, rows):
    ntp = xp.shape[0] // tm
    return pl.pallas_call(
        functools.partial(_dispatch_kernel, n_prompt_tiles=ntp),
        grid_spec=pltpu.PrefetchScalarGridSpec(
            num_scalar_prefetch=3,
            grid=(ntp + 1,),
            in_specs=_joint_specs(xp, xs, mod_p, mod_s, (4, 3), tm, seq),
            out_specs=pl.BlockSpec(memory_space=pl.ANY),
            scratch_shapes=[pltpu.VMEM((N_ROW_BUFS, tm // SUBLANES, SUBLANES, D_MODEL), F32),
                            pltpu.VMEM((tme, D_MODEL), F32),
                            pltpu.SemaphoreType.DMA((2 * N_ROW_BUFS + 1,))],
        ),
        out_shape=jax.ShapeDtypeStruct((rows, D_MODEL), F32),
        compiler_params=_cparams("arbitrary"),
        name="dispatch",
    )(slot1, slot2, zoff, xp, xs, mod_p, mod_s, mod_p, mod_s)


def _expert_kernel(te_ref, tv_ref, tb_ref, ug_ref, wg_ref, wu_ref, wd_ref, ys_ref):
    i = pl.program_id(0)

    @pl.when(tv_ref[i] == 1)
    def _():
        u = ug_ref[...].astype(BF16)
        h = _silu(jnp.dot(u, wg_ref[0].astype(BF16), preferred_element_type=F32)) * jnp.dot(
            u, wu_ref[0].astype(BF16), preferred_element_type=F32)
        ys_ref[...] = jnp.dot(h.astype(BF16), wd_ref[0].astype(BF16), preferred_element_type=F32)

    @pl.when(tv_ref[i] == 0)
    def _():
        ys_ref[...] = jnp.zeros_like(ys_ref)


def _experts(ug, tile_e, valid, blk, w_gate, w_up, w_down, tme, n_tiles):
    d_e = w_gate.shape[-1]
    return pl.pallas_call(
        _expert_kernel,
        grid_spec=pltpu.PrefetchScalarGridSpec(
            num_scalar_prefetch=3,
            grid=(n_tiles,),
            in_specs=[
                pl.BlockSpec((tme, D_MODEL), lambda i, te, tv, tb: (tb[i], 0)),
                pl.BlockSpec((1, D_MODEL, d_e), lambda i, te, tv, tb: (te[i], 0, 0)),
                pl.BlockSpec((1, D_MODEL, d_e), lambda i, te, tv, tb: (te[i], 0, 0)),
                pl.BlockSpec((1, d_e, D_MODEL), lambda i, te, tv, tb: (te[i], 0, 0)),
            ],
            out_specs=pl.BlockSpec((tme, D_MODEL), lambda i, te, tv, tb: (i, 0)),
        ),
        out_shape=jax.ShapeDtypeStruct((n_tiles * tme, D_MODEL), F32),
        compiler_params=_cparams("arbitrary"),
        name="experts",
    )(tile_e, valid, blk, ug, w_gate, w_up, w_down)


def _combine_kernel(s1_ref, s2_ref, ys_hbm, route_ref, xp_ref, xs_ref, g2p_ref, g2s_ref, lg_ref, lb_ref,
                    outp_ref, outs_ref, y1buf, y2buf, sem, *, n_prompt_tiles):
    i = pl.program_id(0)
    tm = xp_ref.shape[0]
    n_steps = n_prompt_tiles + 1

    def gather(tile, b):
        def issue(j, carry):
            for k in range(SUBLANES):
                t = tile * tm + j * SUBLANES + k
                pltpu.make_async_copy(ys_hbm.at[pl.ds(s1_ref[t], 1)], y1buf.at[b, j, pl.ds(k, 1)],
                                      sem.at[2 * b]).start()
                pltpu.make_async_copy(ys_hbm.at[pl.ds(s2_ref[t], 1)], y2buf.at[b, j, pl.ds(k, 1)],
                                      sem.at[2 * b + 1]).start(priority=1)
            return carry

        lax.fori_loop(0, tm // SUBLANES, issue, 0)

    buf = i % N_ROW_BUFS

    @pl.when(i == 0)
    def _():
        gather(0, 0)

    @pl.when(i + 1 < n_steps)
    def _():
        gather(i + 1, (i + 1) % N_ROW_BUFS)

    pltpu.make_async_copy(y1buf.at[buf], y1buf.at[buf], sem.at[2 * buf]).wait()
    pltpu.make_async_copy(y2buf.at[buf], y2buf.at[buf], sem.at[2 * buf + 1]).wait()
    route = route_ref[...]
    y1 = y1buf[buf].reshape(tm, D_MODEL)
    y2 = y2buf[buf].reshape(tm, D_MODEL)
    f = route[:, R_G1:R_G1 + 1] * y1 + route[:, R_G2:R_G2 + 1] * y2

    @pl.when(i < n_prompt_tiles)
    def _():
        r = DEEPNORM_ALPHA * xp_ref[...] + (1.0 + g2p_ref[0]) * f
        outp_ref[...] = _layer_norm(r, lg_ref[...], lb_ref[...])

    @pl.when(i >= n_prompt_tiles)
    def _():
        r = DEEPNORM_ALPHA * xs_ref[...] + (1.0 + _tile_rows(g2s_ref, tm)) * f
        outs_ref[...] = _layer_norm(r, lg_ref[...], lb_ref[...])


def _combine(ys, slot1, slot2, route, xp, xs, mod_p, mod_s, seq, ln_g, ln_b, tm):
    ntp = xp.shape[0] // tm
    vec = lambda w: pl.BlockSpec((1, w), lambda i, *_: (0, 0))
    x_specs = _joint_specs(xp, xs, mod_p, mod_s, (5,), tm, seq)
    return pl.pallas_call(
        functools.partial(_combine_kernel, n_prompt_tiles=ntp),
        grid_spec=pltpu.PrefetchScalarGridSpec(
            num_scalar_prefetch=2,
            grid=(ntp + 1,),
            in_specs=[
                pl.BlockSpec(memory_space=pl.ANY),
                pl.BlockSpec((tm, LANES), lambda i, *_: (i, 0)),
            ] + x_specs + [vec(D_MODEL), vec(D_MODEL)],
            out_specs=[x_specs[0], x_specs[1]],
            scratch_shapes=[pltpu.VMEM((N_ROW_BUFS, tm // SUBLANES, SUBLANES, D_MODEL), F32),
                            pltpu.VMEM((N_ROW_BUFS, tm // SUBLANES, SUBLANES, D_MODEL), F32),
                            pltpu.SemaphoreType.DMA((2 * N_ROW_BUFS,))],
        ),
        out_shape=[jax.ShapeDtypeStruct(xp.shape, F32), jax.ShapeDtypeStruct(xs.shape, F32)],
        compiler_params=_cparams("arbitrary"),
        name="combine",
    )(slot1, slot2, ys, route, xp, xs, mod_p, mod_s, ln_g, ln_b)


def _moe(xp, xs, mod_p, mod_s, seq, w_router, b_router, w_gate, w_up, w_down, ln_g, ln_b, tm, tme):
    route, route_t, counts = _router(xp, xs, mod_p, mod_s, seq, w_router, b_router, tm)
    slot1, slot2, zoff, tile_e, valid, blk, n_tiles = _dispatch_plan(route_t, counts, tme)
    ug = _dispatch(xp, xs, mod_p, mod_s, seq, slot1, slot2, zoff, tm, tme, (n_tiles + N_EXPERTS) * tme)
    ys = _experts(ug, tile_e, valid, blk, w_gate, w_up, w_down, tme, n_tiles)
    return _combine(ys, slot1, slot2, route, xp, xs, mod_p, mod_s, seq, ln_g, ln_b, tm)


def _ffn_kernel(x_ref, sc_ref, sh_ref, g2_ref, wg_ref, wu_ref, wd_ref, lg_ref, lb_ref, out_ref, *, tf):
    x = x_ref[...]
    u = (x * (1.0 + sc_ref[0]) + sh_ref[0]).astype(BF16)
    y = None
    for f0 in range(0, wg_ref.shape[1], tf):
        h = _silu(jnp.dot(u, wg_ref[:, f0:f0 + tf], preferred_element_type=F32)) * jnp.dot(
            u, wu_ref[:, f0:f0 + tf], preferred_element_type=F32)
        part = jnp.dot(h.astype(BF16), wd_ref[f0:f0 + tf, :], preferred_element_type=F32)
        y = part if y is None else y + part
    r = DEEPNORM_ALPHA * x + (1.0 + g2_ref[0]) * y
    out_ref[...] = _layer_norm(r, lg_ref[...], lb_ref[...])


def _ffn(x, mod3, tpg, w_gate, w_up, w_down, ln_g, ln_b, tm, tf):
    n = x.shape[0]
    vec = lambda w: pl.BlockSpec((1, w), lambda i: (0, 0))
    resident = lambda a: pl.BlockSpec(a.shape, lambda i: (0, 0), pipeline_mode=pl.Buffered(1))
    return pl.pallas_call(
        functools.partial(_ffn_kernel, tf=tf),
        grid=(n // tm,),
        in_specs=[
            pl.BlockSpec((tm, D_MODEL), lambda i: (i, 0)),
            _mod_spec(mod3, 4, tpg),
            _mod_spec(mod3, 3, tpg),
            _mod_spec(mod3, 5, tpg),
            resident(w_gate), resident(w_up), resident(w_down),
            vec(D_MODEL), vec(D_MODEL),
        ],
        out_specs=pl.BlockSpec((tm, D_MODEL), lambda i: (i, 0)),
        out_shape=jax.ShapeDtypeStruct((n, D_MODEL), F32),
        compiler_params=_cparams("arbitrary"),
        name="ffn_dense",
    )(x, mod3, mod3, mod3, w_gate, w_up, w_down, ln_g, ln_b)


def _token_mixer(l, x, mod3, rows_per_group, prm, tm, mixer_fn):
    yc, o, z_src, z_blk, st = mixer_fn(l, x, mod3)
    x = _outproj(yc, o, z_src, z_blk, x, mod3, rows_per_group // tm, prm["norm_w"][l], prm["w_out"][l],
                 prm["ln_g"][l, 0:1], prm["ln_b"][l, 0:1], tm)
    return x, st


def kernel(x_prompt, x_sample, state_conv, state_dn_conv, state_dn, c_prompt, c_sample, w_in, conv_w,
           dn_conv_w, a_log, dt_bias, dn_norm_w, w_out, w_ada, b_ada, ln_g, ln_b, w_ff_gate, w_ff_up,
           w_ff_down, w_router, b_router, w_exp_gate, w_exp_up, w_exp_down):
    bp, seq, d = x_prompt.shape
    bs, steps, _ = x_sample.shape
    hp = jnp.zeros((DEPTH, 2, LANES), F32)
    hp = hp.at[:, 0, G_LANE:].set(a_log).at[:, 1, G_LANE:].set(dt_bias)
    w_in_t = jnp.swapaxes(w_in, 1, 2)
    prm = {
        "w_main": w_in_t[:, :P_MAIN].astype(BF16),
        "w_small": w_in_t[:, w_in_t.shape[1] - LANES:],
        "norm_w": dn_norm_w.reshape(DEPTH, 1, DN_HEAD_DIM),
        "w_out": w_out.astype(BF16),
        "ln_g": ln_g, "ln_b": ln_b,
        "w_ff_gate": w_ff_gate.astype(BF16), "w_ff_up": w_ff_up.astype(BF16),
        "w_ff_down": w_ff_down.astype(BF16),
        "w_router": jnp.swapaxes(w_router, 1, 2),
        "b_router": jnp.broadcast_to(b_router[:, :, None], b_router.shape + (LANES,)),
        "w_exp_gate": w_exp_gate, "w_exp_up": w_exp_up, "w_exp_down": w_exp_down,
    }

    mod = _ada(jnp.concatenate([c_prompt, c_sample], axis=0), w_ada, b_ada)
    mod_p = [mod[l, :bp].reshape(bp, 1, 6 * d) for l in range(DEPTH)]
    mod_s = [mod[l, bp:].reshape(1, bs, 6 * d) for l in range(DEPTH)]

    tm_p = PROMPT_ROW_TILE
    zc = jnp.zeros((bp, CONV_K - 1, CONV_WIDTH), F32)
    zd = jnp.zeros((bp, DN_CONV_K - 1, 3 * DN_WIDTH), F32)
    zs = jnp.zeros((bp, DN_HEADS, DN_HEAD_DIM, DN_HEAD_DIM), F32)

    def mixer_prompt(l, x, mod3):
        yc, qkv, z, gb, ncb, ndb = _inproj_pre_prompt(
            x, mod3, prm["w_main"][l], prm["w_small"][l], conv_w[l], dn_conv_w[l], hp[l], zc, zd,
            bp, seq, tm_p)
        o, s_new = _delta_prompt(qkv, gb, zs, bp, seq, PROMPT_DELTA_TILE)
        return yc, o, z, 0, (ncb, ndb, s_new)

    dn_states = []

    def mixer_sample(l, x, mod3):
        pm, ps = _inproj(x, mod3, steps, prm["w_main"][l], prm["w_small"][l], bs)
        cb_tm = jnp.transpose(state_conv[l], (1, 0, 2))
        db_tm = jnp.transpose(state_dn_conv[l], (1, 0, 2))
        yc, qkv, gb, ncb, ndb = _pre_sample(pm, ps, conv_w[l], dn_conv_w[l], hp[l], cb_tm, db_tm,
                                            bs, steps, SAMPLE_PRE_SEQS)
        to_b = lambda a: jnp.pad(jnp.transpose(a, (1, 0, 2)), ((0, 0), (0, SAMPLE_CHUNK - steps), (0, 0)))
        o_b, s_all = _delta_sample(to_b(qkv), to_b(gb), state_dn, l, dn_states[-1] if dn_states else None,
                                   SAMPLE_DELTA_SEQS)
        dn_states.append(s_all)
        o = jnp.transpose(o_b[:, :steps], (1, 0, 2)).reshape(steps * bs, DN_WIDTH)
        return (yc.reshape(steps * bs, CONV_WIDTH), o, pm, OFF_Z // DN_WIDTH,
                (jnp.transpose(ncb, (1, 0, 2)), jnp.transpose(ndb, (1, 0, 2))))

    x_p = x_prompt.reshape(bp * seq, d)
    x_s = jnp.transpose(x_sample, (1, 0, 2)).reshape(steps * bs, d)
    n_s = steps * bs
    st_p, st_s = [], []
    for l in range(DEPTH):
        x_p, st = _token_mixer(l, x_p, mod_p[l], seq, prm, tm_p, mixer_prompt)
        st_p.append(st)
        x_s, st = _token_mixer(l, x_s, mod_s[l], n_s, prm, bs, mixer_sample)
        st_s.append(st)
        j = l // 2
        ln = (prm["ln_g"][l, 1:2], prm["ln_b"][l, 1:2])
        if l % 2 == 0:
            ffn_w = (prm["w_ff_gate"][j], prm["w_ff_up"][j], prm["w_ff_down"][j])
            x_p = _ffn(x_p, mod_p[l], seq // tm_p, *ffn_w, *ln, tm_p, FFN_HIDDEN_CHUNK)
            x_s = _ffn(x_s, mod_s[l], n_s // bs, *ffn_w, *ln, bs, w_ff_gate.shape[-1])
        else:
            x_p, x_s = _moe(x_p, x_s, mod_p[l], mod_s[l], seq, prm["w_router"][j], prm["b_router"][j],
                            prm["w_exp_gate"][j], prm["w_exp_up"][j], prm["w_exp_down"][j], *ln, tm_p,
                            EXPERT_SLOT_TILE)
    y_p = x_p
    y_s = jnp.transpose(x_s.reshape(steps, bs, d), (1, 0, 2))

    stack = lambda sts, i: jnp.stack([s[i] for s in sts])
    return (y_p.reshape(bp, seq, d), y_s,
            stack(st_p, 0), stack(st_p, 1), stack(st_p, 2),
            stack(st_s, 0), stack(st_s, 1), dn_states[-1])
```

```python
import functools
import math

import jax
import jax.numpy as jnp
from jax import lax
from jax.experimental import pallas as pl
from jax.experimental.pallas import tpu as pltpu

F32 = jnp.float32
BF16 = jnp.bfloat16

D_MODEL = 1024
DEPTH = 2
CONV_WIDTH = 512
CONV_K = 3
DN_HEADS = 4
DN_HEAD_DIM = 128
DN_WIDTH = DN_HEADS * DN_HEAD_DIM
DN_CONV_K = 4
DN_CHUNK = 64
N_EXPERTS = 8
DEEPNORM_ALPHA = (2.0 * DEPTH) ** 0.25
LN_EPS = 1e-5
RMS_EPS = 1e-6

OFF_CC = CONV_WIDTH
OFF_CH = 2 * CONV_WIDTH
OFF_QKV = 3 * CONV_WIDTH
OFF_Z = OFF_QKV + 3 * DN_WIDTH
P_MAIN = OFF_Z + DN_WIDTH
LANES = 128
SUBLANES = 8
BETA_LANE = LANES - 2 * DN_HEADS
G_LANE = LANES - DN_HEADS
SAMPLE_CHUNK = 8

VMEM_LIMIT = 56 * 1024 * 1024
MXU_TILE = 256

PROMPT_ROW_TILE = 512
PROMPT_DELTA_TILE = 512
FFN_HIDDEN_CHUNK = MXU_TILE
EXPERT_SLOT_TILE = 512
SAMPLE_PRE_SEQS = 32
SAMPLE_DELTA_SEQS = 16


def _cparams(*sem):
    return pltpu.CompilerParams(dimension_semantics=sem, vmem_limit_bytes=VMEM_LIMIT)


def _silu(x):
    return x * jax.nn.sigmoid(x)


def _layer_norm(r, g, b):
    mu = jnp.mean(r, -1, keepdims=True)
    xc = r - mu
    var = jnp.mean(xc * xc, -1, keepdims=True)
    return xc * lax.rsqrt(var + LN_EPS) * g + b


def _split_bf16(a):
    hi = a.astype(BF16)
    lo = (a - hi.astype(F32)).astype(BF16)
    return hi, lo


_NN = (((1,), (0,)), ((), ()))
_NT = (((1,), (1,)), ((), ()))
_BNN = (((2,), (1,)), ((0,), (0,)))
_BNT = (((2,), (2,)), ((0,), (0,)))
_BTN = (((1,), (1,)), ((0,), (0,)))


def _mm(a, b, dims=_NN, mode="bf16"):
    if mode == "f32":
        return lax.dot_general(a, b, dims, precision=lax.Precision.HIGHEST, preferred_element_type=F32)
    if mode == "bf16":
        return lax.dot_general(a.astype(BF16), b.astype(BF16), dims, preferred_element_type=F32)
    ah, al = _split_bf16(a)
    bh, bl = _split_bf16(b)
    d = functools.partial(lax.dot_general, dimension_numbers=dims, preferred_element_type=F32)
    return d(ah, bh) + (d(ah, bl) + d(al, bh))


def _ada_kernel(c_ref, w_ref, b_ref, o_ref):
    s = _silu(c_ref[...]).astype(BF16)
    o_ref[0] = jnp.dot(s, w_ref[0].astype(BF16), preferred_element_type=F32) + b_ref[0]


def _ada(c_all, w_ada, b_ada):
    rows = c_all.shape[0]
    tn = 1536
    return pl.pallas_call(
        _ada_kernel,
        grid=(DEPTH, 6 * D_MODEL // tn),
        in_specs=[
            pl.BlockSpec((rows, D_MODEL), lambda l, j: (0, 0)),
            pl.BlockSpec((1, D_MODEL, tn), lambda l, j: (l, 0, j)),
            pl.BlockSpec((1, 1, tn), lambda l, j: (l, 0, j)),
        ],
        out_specs=pl.BlockSpec((1, rows, tn), lambda l, j: (l, 0, j)),
        out_shape=jax.ShapeDtypeStruct((DEPTH, rows, 6 * D_MODEL), F32),
        compiler_params=_cparams("arbitrary", "arbitrary"),
        name="ada",
    )(c_all, w_ada, b_ada.reshape(DEPTH, 1, 6 * D_MODEL))


def _mod_spec(mod3, chunk, tiles_per_group):
    r = mod3.shape[1]
    return pl.BlockSpec((1, r, D_MODEL), lambda i, *_: (i // tiles_per_group, 0, chunk))


def _tile_rows(mod_ref, rows):
    m = mod_ref[0]
    reps = rows // m.shape[0]
    return m if m.shape[0] == 1 or reps == 1 else jnp.concatenate([m] * reps, axis=0)


def _inproj_kernel(x_ref, sc_ref, sh_ref, wm_ref, ws_ref, pm_ref, ps_ref):
    tm = x_ref.shape[0]
    u = x_ref[...] * (1.0 + _tile_rows(sc_ref, tm)) + _tile_rows(sh_ref, tm)
    pm_ref[...] = lax.dot_general(u.astype(BF16), wm_ref[...], _NT, preferred_element_type=F32)
    ps_ref[...] = _mm(u, ws_ref[...], _NT, mode="x3")


def _inproj(x, mod3, tpg, w_main, w_small, tm):
    n = x.shape[0]
    return pl.pallas_call(
        _inproj_kernel,
        grid=(n // tm,),
        in_specs=[
            pl.BlockSpec((tm, D_MODEL), lambda i: (i, 0)),
            _mod_spec(mod3, 1, tpg),
            _mod_spec(mod3, 0, tpg),
            pl.BlockSpec((P_MAIN, D_MODEL), lambda i: (0, 0)),
            pl.BlockSpec((LANES, D_MODEL), lambda i: (0, 0)),
        ],
        out_specs=[
            pl.BlockSpec((tm, P_MAIN), lambda i: (i, 0)),
            pl.BlockSpec((tm, LANES), lambda i: (i, 0)),
        ],
        out_shape=[
            jax.ShapeDtypeStruct((n, P_MAIN), F32),
            jax.ShapeDtypeStruct((n, LANES), F32),
        ],
        compiler_params=_cparams("arbitrary"),
        name="inproj",
    )(x, mod3, mod3, w_main, w_small)


def _gates(ps, hp):
    lane = lax.broadcasted_iota(jnp.int32, ps.shape, 1)
    beta = jax.nn.sigmoid(ps)
    g = -jnp.exp(hp[0:1, :]) * jax.nn.softplus(ps + hp[1:2, :])
    return jnp.where(lane >= G_LANE, g, jnp.where(lane >= BETA_LANE, beta, 0.0))


def _qkv_finish(y, out_ref, idx, first_head=0):
    y = _silu(y)
    for j in range(y.shape[1] // DN_HEAD_DIM):
        h = first_head + j
        seg = y[:, j * DN_HEAD_DIM:(j + 1) * DN_HEAD_DIM]
        if h < 2 * DN_HEADS:
            seg = seg * lax.rsqrt(jnp.sum(seg * seg, -1, keepdims=True) + RMS_EPS)
            if h < DN_HEADS:
                seg = seg * (DN_HEAD_DIM ** -0.5)
        out_ref[idx + (slice(None), slice(h * DN_HEAD_DIM, (h + 1) * DN_HEAD_DIM))] = seg


def _shifted(x, tail_ref, i, row):
    s = pltpu.roll(x, i, 0)
    top = s[0:SUBLANES]
    for r in range(i):
        top = jnp.where(row == r, tail_ref[8 - i + r:8 - i + r + 1, :], top)
    return jnp.concatenate([top, s[SUBLANES:]], axis=0)


def _inproj_pre_prompt_kernel(x_ref, sc_ref, sh_ref, wm_ref, ws_ref, cw_ref, dw_ref, hp_ref, cb_ref, db_ref,
                              yc_ref, qkv_ref, z_ref, gb_ref, ncb_ref, ndb_ref, tc_scr, td_scr):
    t = pl.program_id(1)
    tm = x_ref.shape[0]

    @pl.when(t == 0)
    def _():
        tc_scr[8 - (CONV_K - 1):8, :] = cb_ref[0]
        td_scr[8 - (DN_CONV_K - 1):8, :] = db_ref[0]

    u_f32 = x_ref[...] * (1.0 + sc_ref[0]) + sh_ref[0]
    u = u_f32.astype(BF16)
    proj = lambda lo, hi: lax.dot_general(u, wm_ref[lo:hi, :], _NT, preferred_element_type=F32)

    row = lax.broadcasted_iota(jnp.int32, (SUBLANES, DN_WIDTH), 0)

    def dn_part(xp, part):
        cols = slice(part * DN_WIDTH, (part + 1) * DN_WIDTH)
        tail = td_scr.at[:, cols]
        yp = _shifted(xp, tail, 3, row) * dw_ref[0:1, cols]
        yp = yp + _shifted(xp, tail, 2, row) * dw_ref[1:2, cols]
        yp = yp + _shifted(xp, tail, 1, row) * dw_ref[2:3, cols]
        yp = yp + xp * dw_ref[3:4, cols]
        _qkv_finish(yp, qkv_ref, (), first_head=part * DN_HEADS)
        return xp[tm - (DN_CONV_K - 1):tm, :]

    qkv_cols = lambda part: (OFF_QKV + part * DN_WIDTH, OFF_QKV + (part + 1) * DN_WIDTH)
    x_q = proj(*qkv_cols(0))
    x_k = proj(*qkv_cols(1))
    last_q = dn_part(x_q, 0)
    x_v = proj(*qkv_cols(2))
    last_k = dn_part(x_k, 1)
    p_c = proj(OFF_CC, OFF_CH)
    last_v = dn_part(x_v, 2)
    p_h = proj(OFF_CH, OFF_QKV)
    cgh = p_c * p_h
    p_b = proj(0, OFF_CC)
    y = _shifted(cgh, tc_scr, 2, row) * cw_ref[0:1, :]
    y = y + _shifted(cgh, tc_scr, 1, row) * cw_ref[1:2, :]
    y = y + cgh * cw_ref[2:3, :]
    p_z = proj(OFF_Z, P_MAIN)
    yc_ref[...] = (p_b * y).astype(yc_ref.dtype)
    last_c = cgh[tm - (CONV_K - 1):tm, :]
    small = _mm(u_f32, ws_ref[...], _NT, mode="bf16")
    z_ref[...] = p_z.astype(z_ref.dtype)
    gb_ref[...] = _gates(small, hp_ref[...])

    last_d = jnp.concatenate([last_q, last_k, last_v], axis=-1)
    tc_scr[8 - (CONV_K - 1):8, :] = last_c
    td_scr[8 - (DN_CONV_K - 1):8, :] = last_d

    @pl.when(t == pl.num_programs(1) - 1)
    def _():
        ncb_ref[0] = last_c
        ndb_ref[0] = last_d


def _inproj_pre_prompt(x, mod3, w_main, w_small, conv_w, dn_conv_w, hp, conv_buf, dn_buf, bsz, seq, tm):
    n = x.shape[0]
    nt = seq // tm
    rows = lambda w: pl.BlockSpec((tm, w), lambda b, t: (b * nt + t, 0))
    full = lambda a: pl.BlockSpec(a.shape, lambda b, t: (0,) * a.ndim)
    per_b = lambda k, w: pl.BlockSpec((1, k, w), lambda b, t: (b, 0, 0))
    mod = lambda chunk: pl.BlockSpec((1, 1, D_MODEL), lambda b, t: (b, 0, chunk))
    return pl.pallas_call(
        _inproj_pre_prompt_kernel,
        grid=(bsz, nt),
        in_specs=[rows(D_MODEL), mod(1), mod(0), full(w_main), full(w_small),
                  full(conv_w), full(dn_conv_w), full(hp),
                  per_b(CONV_K - 1, CONV_WIDTH), per_b(DN_CONV_K - 1, 3 * DN_WIDTH)],
        out_specs=[rows(CONV_WIDTH), rows(3 * DN_WIDTH), rows(DN_WIDTH), rows(LANES),
                   per_b(CONV_K - 1, CONV_WIDTH), per_b(DN_CONV_K - 1, 3 * DN_WIDTH)],
        out_shape=[
            jax.ShapeDtypeStruct((n, CONV_WIDTH), BF16),
            jax.ShapeDtypeStruct((n, 3 * DN_WIDTH), F32),
            jax.ShapeDtypeStruct((n, DN_WIDTH), BF16),
            jax.ShapeDtypeStruct((n, LANES), F32),
            jax.ShapeDtypeStruct((bsz, CONV_K - 1, CONV_WIDTH), F32),
            jax.ShapeDtypeStruct((bsz, DN_CONV_K - 1, 3 * DN_WIDTH), F32),
        ],
        scratch_shapes=[pltpu.VMEM((SUBLANES, CONV_WIDTH), F32), pltpu.VMEM((SUBLANES, 3 * DN_WIDTH), F32)],
        compiler_params=_cparams("arbitrary", "arbitrary"),
        name="inproj_pre_prompt",
    )(x, mod3, mod3, w_main, w_small, conv_w, dn_conv_w, hp, conv_buf, dn_buf)


def _pre_sample_kernel(pm_ref, ps_ref, cw_ref, dw_ref, hp_ref, cb_ref, db_ref,
                       yc_ref, qkv_ref, gb_ref, ncb_ref, ndb_ref):
    steps = pm_ref.shape[0]
    cgh = [pm_ref[t, :, OFF_CC:OFF_CH] * pm_ref[t, :, OFF_CH:OFF_QKV] for t in range(steps)]
    ext = [cb_ref[i] for i in range(CONV_K - 1)] + cgh
    for t in range(steps):
        y = ext[t] * cw_ref[0:1, :]
        for i in range(1, CONV_K):
            y = y + ext[t + i] * cw_ref[i:i + 1, :]
        yc_ref[t] = pm_ref[t, :, 0:OFF_CC] * y
    for i in range(CONV_K - 1):
        ncb_ref[i] = ext[len(ext) - (CONV_K - 1) + i]

    xq = [pm_ref[t, :, OFF_QKV:OFF_Z] for t in range(steps)]
    extq = [db_ref[i] for i in range(DN_CONV_K - 1)] + xq
    for t in range(steps):
        y = extq[t] * dw_ref[0:1, :]
        for i in range(1, DN_CONV_K):
            y = y + extq[t + i] * dw_ref[i:i + 1, :]
        _qkv_finish(y, qkv_ref, (t,))
        gb_ref[t] = _gates(ps_ref[t], hp_ref[...])
    for i in range(DN_CONV_K - 1):
        ndb_ref[i] = extq[len(extq) - (DN_CONV_K - 1) + i]


def _pre_sample(pm, ps, conv_w, dn_conv_w, hp, conv_buf_tm, dn_buf_tm, bsz, steps, bt):
    slab = lambda k, w: pl.BlockSpec((k, bt, w), lambda i: (0, i, 0))
    full = lambda a: pl.BlockSpec(a.shape, lambda i: (0,) * a.ndim)
    return pl.pallas_call(
        _pre_sample_kernel,
        grid=(bsz // bt,),
        in_specs=[slab(steps, P_MAIN), slab(steps, LANES), full(conv_w), full(dn_conv_w), full(hp),
                  slab(CONV_K - 1, CONV_WIDTH), slab(DN_CONV_K - 1, 3 * DN_WIDTH)],
        out_specs=[slab(steps, CONV_WIDTH), slab(steps, 3 * DN_WIDTH), slab(steps, LANES),
                   slab(CONV_K - 1, CONV_WIDTH), slab(DN_CONV_K - 1, 3 * DN_WIDTH)],
        out_shape=[
            jax.ShapeDtypeStruct((steps, bsz, CONV_WIDTH), F32),
            jax.ShapeDtypeStruct((steps, bsz, 3 * DN_WIDTH), F32),
            jax.ShapeDtypeStruct((steps, bsz, LANES), F32),
            jax.ShapeDtypeStruct((CONV_K - 1, bsz, CONV_WIDTH), F32),
            jax.ShapeDtypeStruct((DN_CONV_K - 1, bsz, 3 * DN_WIDTH), F32),
        ],
        compiler_params=_cparams("arbitrary"),
        name="pre_sample",
    )(pm.reshape(steps, bsz, P_MAIN), ps.reshape(steps, bsz, LANES), conv_w, dn_conv_w, hp,
      conv_buf_tm, dn_buf_tm)


def _cumsum_chunks(x, c):
    row = lax.broadcasted_iota(jnp.int32, x.shape, 0) & (c - 1)
    s = 1
    while s < c:
        x = x + jnp.where(row >= s, pltpu.roll(x, s, 0), 0.0)
        s *= 2
    return x


def _gdn_problems(qkv, gb, c):
    n = qkv.shape[0] // c
    gc = _cumsum_chunks(gb, c)
    q, k, v, g_b, beta_b = [], [], [], [], []
    for ci in range(n):
        rs = slice(ci * c, (ci + 1) * c)
        for h in range(DN_HEADS):
            ls = lambda base: slice(base + h * DN_HEAD_DIM, base + (h + 1) * DN_HEAD_DIM)
            q.append(qkv[rs, ls(0)])
            k.append(qkv[rs, ls(DN_WIDTH)])
            v.append(qkv[rs, ls(2 * DN_WIDTH)])
            g_b.append(jnp.broadcast_to(gc[rs, G_LANE + h:G_LANE + h + 1], (c, DN_HEAD_DIM)))
            beta_b.append(jnp.broadcast_to(gb[rs, BETA_LANE + h:BETA_LANE + h + 1], (c, DN_HEAD_DIM)))
    return tuple(jnp.stack(a) for a in (q, k, v, g_b, beta_b))


def _unit_lower_solve(m, rhs, c, small, mode):
    if small:
        sol = rhs
        for j in range(c - 1):
            sol = sol - m[:, :, j:j + 1] * sol[:, j:j + 1, :]
        return sol
    row = lax.broadcasted_iota(jnp.int32, m.shape, 1)
    col = lax.broadcasted_iota(jnp.int32, m.shape, 2)
    p = -m
    t = jnp.where(row == col, 1.0, 0.0) + p
    p = _mm(p, p, _BNN, "bf16")
    levels = int(math.log2(c))
    for lvl in range(1, levels):
        if lvl < levels - 1:
            y = _mm(jnp.concatenate([t, p], axis=1), p, _BNN, "bf16")
            t = t + y[:, :c]
            p = y[:, c:]
        else:
            t = t + _mm(t, p, _BNN, "bf16")
    x = _mm(t, rhs, _BNN, "bf16")
    resid = rhs - x - _mm(m, x, _BNN, mode)
    return x + _mm(t, resid, _BNN, "bf16")


def _gdn_solve(q, k, v, g_b, beta_b, *, c, small, mode_gram, mode_solve):
    n = q.shape[0]
    row = lax.broadcasted_iota(jnp.int32, (n, c, c), 1)
    col = lax.broadcasted_iota(jnp.int32, (n, c, c), 2)
    g_cc = g_b[:, :, :c]
    g_row = jnp.sum(jnp.where(row == col, g_cc, 0.0), axis=1, keepdims=True)
    gamma = jnp.exp(jnp.where(row >= col, g_cc - g_row, -jnp.inf))
    kbeta = k * beta_b
    exp_g = jnp.exp(g_b)
    gram = _mm(jnp.concatenate([kbeta, q], axis=1), k, _BNT, mode_gram)
    m = jnp.where(row > col, gram[:, :c] * gamma, 0.0)
    attn = gram[:, c:] * gamma
    rhs = jnp.concatenate([kbeta * exp_g, v * beta_b], axis=-1)
    sol = _unit_lower_solve(m, rhs, c, small, mode_solve)
    g_last = g_b[:, c - 1:c, :]
    return sol, attn, q * exp_g, k * jnp.exp(g_last - g_b), jnp.exp(g_last)


def _gdn_fold(sol, attn, q_dec, k_dec, mode):
    kd = _mm(k_dec, sol, _BTN, mode)
    at = _mm(attn, sol, _BNN, mode)
    return kd[..., :DN_HEAD_DIM], kd[..., DN_HEAD_DIM:], q_dec - at[..., :DN_HEAD_DIM], at[..., DN_HEAD_DIM:]


def _gdn_state_step(a_mat, b_mat, q_t, o_intra, d_last, s, mode):
    r = _mm(jnp.concatenate([a_mat, q_t], axis=1), s, _BNN, mode)
    s_new = s * d_last - r[:, :DN_HEAD_DIM] + b_mat
    return r[:, DN_HEAD_DIM:] + o_intra, s_new


def _gdn_direct_step(sol, attn, q_dec, k_dec, d_last, s, mode):
    c = attn.shape[1]
    r = _mm(jnp.concatenate([sol[..., :DN_HEAD_DIM], q_dec], axis=1), s, _BNN, mode)
    v_new = sol[..., DN_HEAD_DIM:] - r[:, :c]
    o = r[:, c:] + _mm(attn, v_new, _BNN, mode)
    return o, s * d_last + _mm(k_dec, v_new, _BTN, mode)


_PROMPT_MODES = dict(mode_gram="bf16", mode_solve="x3")
_SAMPLE_MODES = dict(mode_gram="bf16", mode_solve="bf16")
_STATE_MODE = "bf16"


def _delta_prompt_kernel(qkv_ref, gb_ref, s0_ref, o_ref, sn_ref, s_scr):
    t = pl.program_id(1)
    tc = qkv_ref.shape[0]

    @pl.when(t == 0)
    def _():
        s_scr[...] = s0_ref[0]

    probs = _gdn_problems(qkv_ref[...], gb_ref[...], DN_CHUNK)
    sol, attn, q_dec, k_dec, d_last = _gdn_solve(*probs, c=DN_CHUNK, small=False, **_PROMPT_MODES)
    a_mat, b_mat, q_t, o_intra = _gdn_fold(sol, attn, q_dec, k_dec, _STATE_MODE)
    s = s_scr[...]
    for ci in range(tc // DN_CHUNK):
        ps = slice(ci * DN_HEADS, (ci + 1) * DN_HEADS)
        o, s = _gdn_state_step(a_mat[ps], b_mat[ps], q_t[ps], o_intra[ps], d_last[ps], s, _STATE_MODE)
        for h in range(DN_HEADS):
            o_ref[ci * DN_CHUNK:(ci + 1) * DN_CHUNK, h * DN_HEAD_DIM:(h + 1) * DN_HEAD_DIM] = (
                o[h].astype(o_ref.dtype))
    s_scr[...] = s

    @pl.when(t == pl.num_programs(1) - 1)
    def _():
        sn_ref[0] = s


def _delta_prompt(qkv, gb, s0, bsz, seq, tc):
    n = qkv.shape[0]
    nt = seq // tc
    rows = lambda w: pl.BlockSpec((tc, w), lambda b, t: (b * nt + t, 0))
    state = pl.BlockSpec((1, DN_HEADS, DN_HEAD_DIM, DN_HEAD_DIM), lambda b, t: (b, 0, 0, 0))
    return pl.pallas_call(
        _delta_prompt_kernel,
        grid=(bsz, nt),
        in_specs=[rows(3 * DN_WIDTH), rows(LANES), state],
        out_specs=[rows(DN_WIDTH), state],
        out_shape=[
            jax.ShapeDtypeStruct((n, DN_WIDTH), BF16),
            jax.ShapeDtypeStruct((bsz, DN_HEADS, DN_HEAD_DIM, DN_HEAD_DIM), F32),
        ],
        scratch_shapes=[pltpu.VMEM((DN_HEADS, DN_HEAD_DIM, DN_HEAD_DIM), F32)],
        compiler_params=_cparams("arbitrary", "arbitrary"),
        name="delta_prompt",
    )(qkv, gb, s0)


def _delta_sample_kernel(qkv_ref, gb_ref, s0_ref, *rest):
    o_ref, sn_ref = rest[-2:]
    bt = qkv_ref.shape[0]
    c = SAMPLE_CHUNK
    probs = _gdn_problems(qkv_ref[...].reshape(bt * c, 3 * DN_WIDTH), gb_ref[...].reshape(bt * c, LANES), c)
    parts = _gdn_solve(*probs, c=c, small=True, **_SAMPLE_MODES)
    s = s0_ref[0].reshape(bt * DN_HEADS, DN_HEAD_DIM, DN_HEAD_DIM)
    o, s = _gdn_direct_step(*parts, s, _STATE_MODE)
    for b in range(bt):
        for h in range(DN_HEADS):
            o_ref[b, :, h * DN_HEAD_DIM:(h + 1) * DN_HEAD_DIM] = o[b * DN_HEADS + h]
    sn_ref[0] = s.reshape(bt, DN_HEADS, DN_HEAD_DIM, DN_HEAD_DIM)


def _delta_sample(qkv_b, gb_b, state_all, layer, carried, bt):
    bsz = qkv_b.shape[0]
    blk = lambda w: pl.BlockSpec((bt, SAMPLE_CHUNK, w), lambda i: (i, 0, 0))
    state = pl.BlockSpec((1, bt, DN_HEADS, DN_HEAD_DIM, DN_HEAD_DIM), lambda i: (layer, i, 0, 0, 0))
    in_specs, args, aliases = [blk(3 * DN_WIDTH), blk(LANES), state], [qkv_b, gb_b, state_all], {}
    if carried is not None:
        in_specs.append(pl.BlockSpec(memory_space=pl.ANY))
        args.append(carried)
        aliases = {3: 1}
    return pl.pallas_call(
        _delta_sample_kernel,
        grid=(bsz // bt,),
        in_specs=in_specs,
        out_specs=[blk(DN_WIDTH), state],
        out_shape=[
            jax.ShapeDtypeStruct((bsz, SAMPLE_CHUNK, DN_WIDTH), F32),
            jax.ShapeDtypeStruct(state_all.shape, F32),
        ],
        input_output_aliases=aliases,
        compiler_params=_cparams("arbitrary"),
        name="delta_sample",
    )(*args)


def _outproj_kernel(yc_ref, o_ref, z_ref, x_ref, g1_ref, nw_ref, w_ref, lg_ref, lb_ref, out_ref):
    o = o_ref[...].astype(F32)
    parts = []
    for h in range(DN_HEADS):
        oh = o[:, h * DN_HEAD_DIM:(h + 1) * DN_HEAD_DIM]
        parts.append(oh * lax.rsqrt(jnp.mean(oh * oh, -1, keepdims=True) + RMS_EPS) * nw_ref[...])
    og = jnp.concatenate(parts, axis=-1) * _silu(z_ref[...].astype(F32))
    mixed = jnp.concatenate([yc_ref[...].astype(BF16), og.astype(BF16)], axis=-1)
    m = jnp.dot(mixed, w_ref[...], preferred_element_type=F32)
    r = DEEPNORM_ALPHA * x_ref[...] + (1.0 + _tile_rows(g1_ref, x_ref.shape[0])) * m
    out_ref[...] = _layer_norm(r, lg_ref[...], lb_ref[...])


def _outproj(yc, o, z_src, z_blk, x, mod3, tpg, norm_w, w_out, ln_g, ln_b, tm):
    n = x.shape[0]
    vec = lambda w: pl.BlockSpec((1, w), lambda i: (0, 0))
    return pl.pallas_call(
        _outproj_kernel,
        grid=(n // tm,),
        in_specs=[
            pl.BlockSpec((tm, CONV_WIDTH), lambda i: (i, 0)),
            pl.BlockSpec((tm, DN_WIDTH), lambda i: (i, 0)),
            pl.BlockSpec((tm, DN_WIDTH), lambda i: (i, z_blk)),
            pl.BlockSpec((tm, D_MODEL), lambda i: (i, 0)),
            _mod_spec(mod3, 2, tpg),
            vec(DN_HEAD_DIM),
            pl.BlockSpec((D_MODEL, D_MODEL), lambda i: (0, 0)),
            vec(D_MODEL), vec(D_MODEL),
        ],
        out_specs=pl.BlockSpec((tm, D_MODEL), lambda i: (i, 0)),
        out_shape=jax.ShapeDtypeStruct((n, D_MODEL), F32),
        compiler_params=_cparams("arbitrary"),
        name="outproj",
    )(yc, o, z_src, x, mod3, norm_w, w_out, ln_g, ln_b)


R_E1, R_E2, R_G1, R_G2, R_R1, R_R2 = range(6)
N_ROW_BUFS = 2


def _joint_specs(xp, xs, mod_p, mod_s, chunks, tm, seq):
    last = xp.shape[0] // tm - 1
    tpg = seq // tm
    assert xs.shape[0] == tm
    specs = [pl.BlockSpec((tm, D_MODEL), lambda i, *_: (jnp.minimum(i, last), 0)),
             pl.BlockSpec((tm, D_MODEL), lambda i, *_: (0, 0))]
    for c in chunks:
        specs.append(pl.BlockSpec((1, 1, D_MODEL), lambda i, *_, c=c: (jnp.minimum(i, last) // tpg, 0, c)))
        specs.append(pl.BlockSpec((1, mod_s.shape[1], D_MODEL), lambda i, *_, c=c: (0, 0, c)))
    return specs


def _joint_modulated(i, n_prompt_tiles, xp_ref, xs_ref, scp_ref, scs_ref, shp_ref, shs_ref, u_ref):
    tm = xp_ref.shape[0]

    @pl.when(i < n_prompt_tiles)
    def _():
        u_ref[...] = (xp_ref[...] * (1.0 + scp_ref[0]) + shp_ref[0]).reshape(u_ref.shape)

    @pl.when(i >= n_prompt_tiles)
    def _():
        u_ref[...] = (xs_ref[...] * (1.0 + _tile_rows(scs_ref, tm)) + _tile_rows(shs_ref, tm)).reshape(
            u_ref.shape)


def _router_kernel(xp_ref, xs_ref, scp_ref, scs_ref, shp_ref, shs_ref, wr_ref, br_ref,
                   route_ref, route_t_ref, cnt_ref, u_scr, *, n_prompt_tiles):
    tm = xp_ref.shape[0]
    _joint_modulated(pl.program_id(0), n_prompt_tiles, xp_ref, xs_ref, scp_ref, scs_ref, shp_ref, shs_ref,
                     u_scr)
    lt = _mm(wr_ref[...], u_scr[...], _NT, mode="x3") + br_ref[:, 0:1]
    e_idx = lax.broadcasted_iota(jnp.int32, lt.shape, 0)
    ex = jnp.exp(lt - jnp.max(lt, 0, keepdims=True))
    probs = ex / jnp.sum(ex, 0, keepdims=True)
    p1 = jnp.max(probs, 0, keepdims=True)
    i1 = jnp.min(jnp.where(probs == p1, e_idx, N_EXPERTS), 0, keepdims=True)
    rest = jnp.where(e_idx == i1, -1.0, probs)
    p2 = jnp.max(rest, 0, keepdims=True)
    i2 = jnp.min(jnp.where(rest == p2, e_idx, N_EXPERTS), 0, keepdims=True)
    tot = p1 + p2
    sel = jnp.where(e_idx == i1, 1.0, 0.0) + jnp.where(e_idx == i2, 1.0, 0.0)
    r = lax.broadcasted_iota(jnp.int32, (tm, tm), 0)
    c = lax.broadcasted_iota(jnp.int32, (tm, tm), 1)
    incl = jnp.dot(sel.astype(BF16), jnp.where(r <= c, 1.0, 0.0).astype(BF16), preferred_element_type=F32)
    excl = incl - sel
    r1 = jnp.sum(jnp.where(e_idx == i1, excl, 0.0), 0, keepdims=True)
    r2 = jnp.sum(jnp.where(e_idx == i2, excl, 0.0), 0, keepdims=True)
    cnt_ref[0] = jnp.broadcast_to(incl[:, tm - 1:tm], (N_EXPERTS, LANES))
    route_t = jnp.zeros_like(lt)
    for j, val in enumerate((i1.astype(F32), i2.astype(F32), p1 / tot, p2 / tot, r1, r2)):
        route_t = jnp.where(e_idx == j, val, route_t)
    route_t_ref[0] = route_t
    route_ref[...] = jnp.transpose(
        jnp.concatenate([route_t, jnp.zeros((LANES - N_EXPERTS, tm), F32)], axis=0))


def _router(xp, xs, mod_p, mod_s, seq, w_router, b_router, tm):
    ntp = xp.shape[0] // tm
    nt = ntp + 1
    return pl.pallas_call(
        functools.partial(_router_kernel, n_prompt_tiles=ntp),
        grid=(nt,),
        in_specs=_joint_specs(xp, xs, mod_p, mod_s, (4, 3), tm, seq) + [
            pl.BlockSpec((N_EXPERTS, D_MODEL), lambda i: (0, 0)),
            pl.BlockSpec((N_EXPERTS, LANES), lambda i: (0, 0)),
        ],
        out_specs=[pl.BlockSpec((tm, LANES), lambda i: (i, 0)),
                   pl.BlockSpec((1, SUBLANES, tm), lambda i: (i, 0, 0)),
                   pl.BlockSpec((1, N_EXPERTS, LANES), lambda i: (i, 0, 0))],
        out_shape=[jax.ShapeDtypeStruct((nt * tm, LANES), F32),
                   jax.ShapeDtypeStruct((nt, SUBLANES, tm), F32),
                   jax.ShapeDtypeStruct((nt, N_EXPERTS, LANES), F32)],
        scratch_shapes=[pltpu.VMEM((tm, D_MODEL), F32)],
        compiler_params=_cparams("arbitrary"),
        name="router",
    )(xp, xs, mod_p, mod_s, mod_p, mod_s, w_router, b_router)


def _dispatch_plan(route_t, counts, tme):
    n_rt, _, tm = route_t.shape
    n = n_rt * tm
    cnt = counts[:, :, 0].astype(jnp.int32)
    tile_off = jnp.cumsum(cnt, axis=0) - cnt
    tot = jnp.sum(cnt, axis=0)
    gsz = (tot + tme - 1) // tme * tme
    gend = jnp.cumsum(gsz)
    goff = gend - gsz
    base = (goff[None, :] + tile_off)[:, :, None]
    ids = jnp.arange(N_EXPERTS, dtype=jnp.int32)[None, :, None]
    field = lambda j: route_t[:, j:j + 1, :].astype(jnp.int32)
    pick = lambda e: jnp.sum(jnp.where(ids == e, base, 0), axis=1, keepdims=True)
    slot1 = (pick(field(R_E1)) + field(R_R1)).reshape(n)
    slot2 = (pick(field(R_E2)) + field(R_R2)).reshape(n)
    n_tiles = 2 * n // tme + N_EXPERTS
    start = jnp.arange(n_tiles, dtype=jnp.int32) * tme
    valid = (start < gend[-1]).astype(jnp.int32)
    blk = jnp.minimum(jnp.arange(n_tiles, dtype=jnp.int32), gend[-1] // tme - 1)
    tile_e = jnp.sum((blk * tme)[:, None] >= gend[None, :], axis=1).astype(jnp.int32)
    zoff = jnp.where(gsz > 0, gend - tme, (n_tiles + jnp.arange(N_EXPERTS, dtype=jnp.int32)) * tme)
    return slot1, slot2, zoff, tile_e, valid, blk, n_tiles


def _dispatch_kernel(s1_ref, s2_ref, zoff_ref, xp_ref, xs_ref, scp_ref, scs_ref, shp_ref, shs_ref,
                     ug_hbm, ubuf, zbuf, sem, *, n_prompt_tiles):
    i = pl.program_id(0)
    tm = xp_ref.shape[0]
    tme = zbuf.shape[0]

    n_steps = n_prompt_tiles + 1
    fill_sem = 2 * N_ROW_BUFS

    @pl.when(i == 0)
    def _():
        zbuf[...] = jnp.zeros_like(zbuf)
        fill = lambda e: pltpu.make_async_copy(
            zbuf, ug_hbm.at[pl.ds(pl.multiple_of(zoff_ref[e], 8), tme)], sem.at[fill_sem])
        for e in range(N_EXPERTS):
            fill(e).start()
        for e in range(N_EXPERTS):
            fill(e).wait()

    def drain(b):
        for stream in range(2):
            pltpu.make_async_copy(ubuf.at[b], ubuf.at[b], sem.at[2 * b + stream]).wait()

    buf = i % N_ROW_BUFS

    @pl.when(i >= N_ROW_BUFS)
    def _():
        drain(buf)

    _joint_modulated(i, n_prompt_tiles, xp_ref, xs_ref, scp_ref, scs_ref, shp_ref, shs_ref, ubuf.at[buf])
    base = i * tm

    def issue(j, carry):
        for k in range(SUBLANES):
            row = ubuf.at[buf, j, pl.ds(k, 1)]
            t = base + j * SUBLANES + k
            pltpu.make_async_copy(row, ug_hbm.at[pl.ds(s1_ref[t], 1)], sem.at[2 * buf]).start()
            pltpu.make_async_copy(row, ug_hbm.at[pl.ds(s2_ref[t], 1)], sem.at[2 * buf + 1]).start(priority=1)
        return carry

    lax.fori_loop(0, tm // SUBLANES, issue, 0)

    @pl.when(i == n_steps - 1)
    def _():
        for b in range(min(N_ROW_BUFS, n_steps)):
            drain(b)


def _dispatch(xp, xs, mod_p, mod_s, seq, slot1, slot2, zoff, tm, tme, rows):
    ntp = xp.shape[0] // tm
    return pl.pallas_call(
        functools.partial(_dispatch_kernel, n_prompt_tiles=ntp),
        grid_spec=pltpu.PrefetchScalarGridSpec(
            num_scalar_prefetch=3,
            grid=(ntp + 1,),
            in_specs=_joint_specs(xp, xs, mod_p, mod_s, (4, 3), tm, seq),
            out_specs=pl.BlockSpec(memory_space=pl.ANY),
            scratch_shapes=[pltpu.VMEM((N_ROW_BUFS, tm // SUBLANES, SUBLANES, D_MODEL), F32),
                            pltpu.VMEM((tme, D_MODEL), F32),
                            pltpu.SemaphoreType.DMA((2 * N_ROW_BUFS + 1,))],
        ),
        out_shape=jax.ShapeDtypeStruct((rows, D_MODEL), F32),
        compiler_params=_cparams("arbitrary"),
        name="dispatch",
    )(slot1, slot2, zoff, xp, xs, mod_p, mod_s, mod_p, mod_s)


def _expert_kernel(te_ref, tv_ref, tb_ref, ug_ref, wg_ref, wu_ref, wd_ref, ys_ref):
    i = pl.program_id(0)

    @pl.when(tv_ref[i] == 1)
    def _():
        u = ug_ref[...].astype(BF16)
        h = _silu(jnp.dot(u, wg_ref[0].astype(BF16), preferred_element_type=F32)) * jnp.dot(
            u, wu_ref[0].astype(BF16), preferred_element_type=F32)
        ys_ref[...] = jnp.dot(h.astype(BF16), wd_ref[0].astype(BF16), preferred_element_type=F32)

    @pl.when(tv_ref[i] == 0)
    def _():
        ys_ref[...] = jnp.zeros_like(ys_ref)


def _experts(ug, tile_e, valid, blk, w_gate, w_up, w_down, tme, n_tiles):
    d_e = w_gate.shape[-1]
    return pl.pallas_call(
        _expert_kernel,
        grid_spec=pltpu.PrefetchScalarGridSpec(
            num_scalar_prefetch=3,
            grid=(n_tiles,),
            in_specs=[
                pl.BlockSpec((tme, D_MODEL), lambda i, te, tv, tb: (tb[i], 0)),
                pl.BlockSpec((1, D_MODEL, d_e), lambda i, te, tv, tb: (te[i], 0, 0)),
                pl.BlockSpec((1, D_MODEL, d_e), lambda i, te, tv, tb: (te[i], 0, 0)),
                pl.BlockSpec((1, d_e, D_MODEL), lambda i, te, tv, tb: (te[i], 0, 0)),
            ],
            out_specs=pl.BlockSpec((tme, D_MODEL), lambda i, te, tv, tb: (i, 0)),
        ),
        out_shape=jax.ShapeDtypeStruct((n_tiles * tme, D_MODEL), F32),
        compiler_params=_cparams("arbitrary"),
        name="experts",
    )(tile_e, valid, blk, ug, w_gate, w_up, w_down)


def _combine_kernel(s1_ref, s2_ref, ys_hbm, route_ref, xp_ref, xs_ref, g2p_ref, g2s_ref, lg_ref, lb_ref,
                    outp_ref, outs_ref, y1buf, y2buf, sem, *, n_prompt_tiles):
    i = pl.program_id(0)
    tm = xp_ref.shape[0]
    n_steps = n_prompt_tiles + 1

    def gather(tile, b):
        def issue(j, carry):
            for k in range(SUBLANES):
                t = tile * tm + j * SUBLANES + k
                pltpu.make_async_copy(ys_hbm.at[pl.ds(s1_ref[t], 1)], y1buf.at[b, j, pl.ds(k, 1)],
                                      sem.at[2 * b]).start()
                pltpu.make_async_copy(ys_hbm.at[pl.ds(s2_ref[t], 1)], y2buf.at[b, j, pl.ds(k, 1)],
                                      sem.at[2 * b + 1]).start(priority=1)
            return carry

        lax.fori_loop(0, tm // SUBLANES, issue, 0)

    buf = i % N_ROW_BUFS

    @pl.when(i == 0)
    def _():
        gather(0, 0)

    @pl.when(i + 1 < n_steps)
    def _():
        gather(i + 1, (i + 1) % N_ROW_BUFS)

    pltpu.make_async_copy(y1buf.at[buf], y1buf.at[buf], sem.at[2 * buf]).wait()
    pltpu.make_async_copy(y2buf.at[buf], y2buf.at[buf], sem.at[2 * buf + 1]).wait()
    route = route_ref[...]
    y1 = y1buf[buf].reshape(tm, D_MODEL)
    y2 = y2buf[buf].reshape(tm, D_MODEL)
    f = route[:, R_G1:R_G1 + 1] * y1 + route[:, R_G2:R_G2 + 1] * y2

    @pl.when(i < n_prompt_tiles)
    def _():
        r = DEEPNORM_ALPHA * xp_ref[...] + (1.0 + g2p_ref[0]) * f
        outp_ref[...] = _layer_norm(r, lg_ref[...], lb_ref[...])

    @pl.when(i >= n_prompt_tiles)
    def _():
        r = DEEPNORM_ALPHA * xs_ref[...] + (1.0 + _tile_rows(g2s_ref, tm)) * f
        outs_ref[...] = _layer_norm(r, lg_ref[...], lb_ref[...])


def _combine(ys, slot1, slot2, route, xp, xs, mod_p, mod_s, seq, ln_g, ln_b, tm):
    ntp = xp.shape[0] // tm
    vec = lambda w: pl.BlockSpec((1, w), lambda i, *_: (0, 0))
    x_specs = _joint_specs(xp, xs, mod_p, mod_s, (5,), tm, seq)
    return pl.pallas_call(
        functools.partial(_combine_kernel, n_prompt_tiles=ntp),
        grid_spec=pltpu.PrefetchScalarGridSpec(
            num_scalar_prefetch=2,
            grid=(ntp + 1,),
            in_specs=[
                pl.BlockSpec(memory_space=pl.ANY),
                pl.BlockSpec((tm, LANES), lambda i, *_: (i, 0)),
            ] + x_specs + [vec(D_MODEL), vec(D_MODEL)],
            out_specs=[x_specs[0], x_specs[1]],
            scratch_shapes=[pltpu.VMEM((N_ROW_BUFS, tm // SUBLANES, SUBLANES, D_MODEL), F32),
                            pltpu.VMEM((N_ROW_BUFS, tm // SUBLANES, SUBLANES, D_MODEL), F32),
                            pltpu.SemaphoreType.DMA((2 * N_ROW_BUFS,))],
        ),
        out_shape=[jax.ShapeDtypeStruct(xp.shape, F32), jax.ShapeDtypeStruct(xs.shape, F32)],
        compiler_params=_cparams("arbitrary"),
        name="combine",
    )(slot1, slot2, ys, route, xp, xs, mod_p, mod_s, ln_g, ln_b)


def _moe(xp, xs, mod_p, mod_s, seq, w_router, b_router, w_gate, w_up, w_down, ln_g, ln_b, tm, tme):
    route, route_t, counts = _router(xp, xs, mod_p, mod_s, seq, w_router, b_router, tm)
    slot1, slot2, zoff, tile_e, valid, blk, n_tiles = _dispatch_plan(route_t, counts, tme)
    ug = _dispatch(xp, xs, mod_p, mod_s, seq, slot1, slot2, zoff, tm, tme, (n_tiles + N_EXPERTS) * tme)
    ys = _experts(ug, tile_e, valid, blk, w_gate, w_up, w_down, tme, n_tiles)
    return _combine(ys, slot1, slot2, route, xp, xs, mod_p, mod_s, seq, ln_g, ln_b, tm)


def _ffn_kernel(x_ref, sc_ref, sh_ref, g2_ref, wg_ref, wu_ref, wd_ref, lg_ref, lb_ref, out_ref, *, tf):
    x = x_ref[...]
    tm = x.shape[0]
    u = (x * (1.0 + _tile_rows(sc_ref, tm)) + _tile_rows(sh_ref, tm)).astype(BF16)
    y = None
    for f0 in range(0, wg_ref.shape[1], tf):
        h = _silu(jnp.dot(u, wg_ref[:, f0:f0 + tf], preferred_element_type=F32)) * jnp.dot(
            u, wu_ref[:, f0:f0 + tf], preferred_element_type=F32)
        part = jnp.dot(h.astype(BF16), wd_ref[f0:f0 + tf, :], preferred_element_type=F32)
        y = part if y is None else y + part
    r = DEEPNORM_ALPHA * x + (1.0 + _tile_rows(g2_ref, tm)) * y
    out_ref[...] = _layer_norm(r, lg_ref[...], lb_ref[...])


def _ffn(x, mod3, tpg, w_gate, w_up, w_down, ln_g, ln_b, tm, tf):
    n = x.shape[0]
    vec = lambda w: pl.BlockSpec((1, w), lambda i: (0, 0))
    resident = lambda a: pl.BlockSpec(a.shape, lambda i: (0, 0), pipeline_mode=pl.Buffered(1))
    return pl.pallas_call(
        functools.partial(_ffn_kernel, tf=tf),
        grid=(n // tm,),
        in_specs=[
            pl.BlockSpec((tm, D_MODEL), lambda i: (i, 0)),
            _mod_spec(mod3, 4, tpg),
            _mod_spec(mod3, 3, tpg),
            _mod_spec(mod3, 5, tpg),
            resident(w_gate), resident(w_up), resident(w_down),
            vec(D_MODEL), vec(D_MODEL),
        ],
        out_specs=pl.BlockSpec((tm, D_MODEL), lambda i: (i, 0)),
        out_shape=jax.ShapeDtypeStruct((n, D_MODEL), F32),
        compiler_params=_cparams("arbitrary"),
        name="ffn_dense",
    )(x, mod3, mod3, mod3, w_gate, w_up, w_down, ln_g, ln_b)


def _token_mixer(l, x, mod3, rows_per_group, prm, tm, mixer_fn):
    yc, o, z_src, z_blk, st = mixer_fn(l, x, mod3)
    x = _outproj(yc, o, z_src, z_blk, x, mod3, rows_per_group // tm, prm["norm_w"][l], prm["w_out"][l],
                 prm["ln_g"][l, 0:1], prm["ln_b"][l, 0:1], tm)
    return x, st


def kernel(x_prompt, x_sample, state_conv, state_dn_conv, state_dn, c_prompt, c_sample, w_in, conv_w,
           dn_conv_w, a_log, dt_bias, dn_norm_w, w_out, w_ada, b_ada, ln_g, ln_b, w_ff_gate, w_ff_up,
           w_ff_down, w_router, b_router, w_exp_gate, w_exp_up, w_exp_down):
    bp, seq, d = x_prompt.shape
    bs, steps, _ = x_sample.shape
    hp = jnp.zeros((DEPTH, 2, LANES), F32)
    hp = hp.at[:, 0, G_LANE:].set(a_log).at[:, 1, G_LANE:].set(dt_bias)
    w_in_t = jnp.swapaxes(w_in, 1, 2)
    prm = {
        "w_main": w_in_t[:, :P_MAIN].astype(BF16),
        "w_small": w_in_t[:, w_in_t.shape[1] - LANES:],
        "norm_w": dn_norm_w.reshape(DEPTH, 1, DN_HEAD_DIM),
        "w_out": w_out.astype(BF16),
        "ln_g": ln_g, "ln_b": ln_b,
        "w_ff_gate": w_ff_gate.astype(BF16), "w_ff_up": w_ff_up.astype(BF16),
        "w_ff_down": w_ff_down.astype(BF16),
        "w_router": jnp.swapaxes(w_router, 1, 2),
        "b_router": jnp.broadcast_to(b_router[:, :, None], b_router.shape + (LANES,)),
        "w_exp_gate": w_exp_gate, "w_exp_up": w_exp_up, "w_exp_down": w_exp_down,
    }

    mod = _ada(jnp.concatenate([c_prompt, c_sample], axis=0), w_ada, b_ada)
    mod_p = [mod[l, :bp].reshape(bp, 1, 6 * d) for l in range(DEPTH)]
    mod_s = [mod[l, bp:].reshape(1, bs, 6 * d) for l in range(DEPTH)]

    tm_p = PROMPT_ROW_TILE
    zc = jnp.zeros((bp, CONV_K - 1, CONV_WIDTH), F32)
    zd = jnp.zeros((bp, DN_CONV_K - 1, 3 * DN_WIDTH), F32)
    zs = jnp.zeros((bp, DN_HEADS, DN_HEAD_DIM, DN_HEAD_DIM), F32)

    def mixer_prompt(l, x, mod3):
        yc, qkv, z, gb, ncb, ndb = _inproj_pre_prompt(
            x, mod3, prm["w_main"][l], prm["w_small"][l], conv_w[l], dn_conv_w[l], hp[l], zc, zd,
            bp, seq, tm_p)
        o, s_new = _delta_prompt(qkv, gb, zs, bp, seq, PROMPT_DELTA_TILE)
        return yc, o, z, 0, (ncb, ndb, s_new)

    dn_states = []

    def mixer_sample(l, x, mod3):
        pm, ps = _inproj(x, mod3, 1, prm["w_main"][l], prm["w_small"][l], steps * bs)
        cb_tm = jnp.transpose(state_conv[l], (1, 0, 2))
        db_tm = jnp.transpose(state_dn_conv[l], (1, 0, 2))
        yc, qkv, gb, ncb, ndb = _pre_sample(pm, ps, conv_w[l], dn_conv_w[l], hp[l], cb_tm, db_tm,
                                            bs, steps, SAMPLE_PRE_SEQS)
        to_b = lambda a: jnp.pad(jnp.transpose(a, (1, 0, 2)), ((0, 0), (0, SAMPLE_CHUNK - steps), (0, 0)))
        o_b, s_all = _delta_sample(to_b(qkv), to_b(gb), state_dn, l, dn_states[-1] if dn_states else None,
                                   SAMPLE_DELTA_SEQS)
        dn_states.append(s_all)
        o = jnp.transpose(o_b[:, :steps], (1, 0, 2)).reshape(steps * bs, DN_WIDTH)
        return (yc.reshape(steps * bs, CONV_WIDTH), o, pm, OFF_Z // DN_WIDTH,
                (jnp.transpose(ncb, (1, 0, 2)), jnp.transpose(ndb, (1, 0, 2))))

    x_p = x_prompt.reshape(bp * seq, d)
    x_s = jnp.transpose(x_sample, (1, 0, 2)).reshape(steps * bs, d)
    n_s = steps * bs
    st_p, st_s = [], []
    for l in range(DEPTH):
        x_p, st = _token_mixer(l, x_p, mod_p[l], seq, prm, tm_p, mixer_prompt)
        st_p.append(st)
        x_s, st = _token_mixer(l, x_s, mod_s[l], n_s, prm, n_s, mixer_sample)
        st_s.append(st)
        j = l // 2
        ln = (prm["ln_g"][l, 1:2], prm["ln_b"][l, 1:2])
        if l % 2 == 0:
            ffn_w = (prm["w_ff_gate"][j], prm["w_ff_up"][j], prm["w_ff_down"][j])
            x_p = _ffn(x_p, mod_p[l], seq // tm_p, *ffn_w, *ln, tm_p, FFN_HIDDEN_CHUNK)
            x_s = _ffn(x_s, mod_s[l], 1, *ffn_w, *ln, n_s, FFN_HIDDEN_CHUNK)
        else:
            x_p, x_s = _moe(x_p, x_s, mod_p[l], mod_s[l], seq, prm["w_router"][j], prm["b_router"][j],
                            prm["w_exp_gate"][j], prm["w_exp_up"][j], prm["w_exp_down"][j], *ln, tm_p,
                            EXPERT_SLOT_TILE)
    y_p = x_p
    y_s = jnp.transpose(x_s.reshape(steps, bs, d), (1, 0, 2))

    stack = lambda sts, i: jnp.stack([s[i] for s in sts])
    return (y_p.reshape(bp, seq, d), y_s,
            stack(st_p, 0), stack(st_p, 1), stack(st_p, 2),
            stack(st_s, 0), stack(st_s, 1), dn_states[-1])
```

```python
import functools
import math

import jax
import jax.numpy as jnp
from jax import lax
from jax.experimental import pallas as pl
from jax.experimental.pallas import tpu as pltpu

F32 = jnp.float32
BF16 = jnp.bfloat16

D_MODEL = 1024
DEPTH = 2
CONV_WIDTH = 512
CONV_K = 3
DN_HEADS = 4
DN_HEAD_DIM = 128
DN_WIDTH = DN_HEADS * DN_HEAD_DIM
DN_CONV_K = 4
DN_CHUNK = 64
N_EXPERTS = 8
DEEPNORM_ALPHA = (2.0 * DEPTH) ** 0.25
LN_EPS = 1e-5
RMS_EPS = 1e-6

OFF_CC = CONV_WIDTH
OFF_CH = 2 * CONV_WIDTH
OFF_QKV = 3 * CONV_WIDTH
OFF_Z = OFF_QKV + 3 * DN_WIDTH
P_MAIN = OFF_Z + DN_WIDTH
LANES = 128
SUBLANES = 8
BETA_LANE = LANES - 2 * DN_HEADS
G_LANE = LANES - DN_HEADS
SAMPLE_CHUNK = 8

VMEM_LIMIT = 56 * 1024 * 1024
MXU_TILE = 256

PROMPT_ROW_TILE = 512
PROMPT_DELTA_TILE = 512
FFN_HIDDEN_CHUNK = MXU_TILE
EXPERT_SLOT_TILE = 512
SAMPLE_PRE_SEQS = 32
SAMPLE_DELTA_SEQS = 16


def _cparams(*sem):
    return pltpu.CompilerParams(dimension_semantics=sem, vmem_limit_bytes=VMEM_LIMIT)


def _silu(x):
    return x * jax.nn.sigmoid(x)


def _layer_norm(r, g, b):
    mu = jnp.mean(r, -1, keepdims=True)
    xc = r - mu
    var = jnp.mean(xc * xc, -1, keepdims=True)
    return xc * lax.rsqrt(var + LN_EPS) * g + b


def _split_bf16(a):
    hi = a.astype(BF16)
    lo = (a - hi.astype(F32)).astype(BF16)
    return hi, lo


_NN = (((1,), (0,)), ((), ()))
_NT = (((1,), (1,)), ((), ()))
_BNN = (((2,), (1,)), ((0,), (0,)))
_BNT = (((2,), (2,)), ((0,), (0,)))
_BTN = (((1,), (1,)), ((0,), (0,)))


def _mm(a, b, dims=_NN, mode="bf16"):
    if mode == "f32":
        return lax.dot_general(a, b, dims, precision=lax.Precision.HIGHEST, preferred_element_type=F32)
    if mode == "bf16":
        return lax.dot_general(a.astype(BF16), b.astype(BF16), dims, preferred_element_type=F32)
    ah, al = _split_bf16(a)
    bh, bl = _split_bf16(b)
    d = functools.partial(lax.dot_general, dimension_numbers=dims, preferred_element_type=F32)
    return d(ah, bh) + (d(ah, bl) + d(al, bh))


def _ada_kernel(c_ref, w_ref, b_ref, o_ref):
    s = _silu(c_ref[...]).astype(BF16)
    o_ref[0] = jnp.dot(s, w_ref[0].astype(BF16), preferred_element_type=F32) + b_ref[0]


def _ada(c_all, w_ada, b_ada):
    rows = c_all.shape[0]
    tn = 1536
    return pl.pallas_call(
        _ada_kernel,
        grid=(DEPTH, 6 * D_MODEL // tn),
        in_specs=[
            pl.BlockSpec((rows, D_MODEL), lambda l, j: (0, 0)),
            pl.BlockSpec((1, D_MODEL, tn), lambda l, j: (l, 0, j)),
            pl.BlockSpec((1, 1, tn), lambda l, j: (l, 0, j)),
        ],
        out_specs=pl.BlockSpec((1, rows, tn), lambda l, j: (l, 0, j)),
        out_shape=jax.ShapeDtypeStruct((DEPTH, rows, 6 * D_MODEL), F32),
        compiler_params=_cparams("arbitrary", "arbitrary"),
        name="ada",
    )(c_all, w_ada, b_ada.reshape(DEPTH, 1, 6 * D_MODEL))


def _mod_spec(mod3, chunk, tiles_per_group):
    r = mod3.shape[1]
    return pl.BlockSpec((1, r, D_MODEL), lambda i, *_: (i // tiles_per_group, 0, chunk))


def _tile_rows(mod_ref, rows):
    m = mod_ref[0]
    reps = rows // m.shape[0]
    return m if m.shape[0] == 1 or reps == 1 else jnp.concatenate([m] * reps, axis=0)


def _inproj_kernel(x_ref, sc_ref, sh_ref, wm_ref, ws_ref, pm_ref, ps_ref):
    tm = x_ref.shape[0]
    u = x_ref[...] * (1.0 + _tile_rows(sc_ref, tm)) + _tile_rows(sh_ref, tm)
    pm_ref[...] = lax.dot_general(u.astype(BF16), wm_ref[...], _NT, preferred_element_type=F32)
    ps_ref[...] = _mm(u, ws_ref[...], _NT, mode="x3")


def _inproj(x, mod3, tpg, w_main, w_small, tm):
    n = x.shape[0]
    return pl.pallas_call(
        _inproj_kernel,
        grid=(n // tm,),
        in_specs=[
            pl.BlockSpec((tm, D_MODEL), lambda i: (i, 0)),
            _mod_spec(mod3, 1, tpg),
            _mod_spec(mod3, 0, tpg),
            pl.BlockSpec((P_MAIN, D_MODEL), lambda i: (0, 0)),
            pl.BlockSpec((LANES, D_MODEL), lambda i: (0, 0)),
        ],
        out_specs=[
            pl.BlockSpec((tm, P_MAIN), lambda i: (i, 0)),
            pl.BlockSpec((tm, LANES), lambda i: (i, 0)),
        ],
        out_shape=[
            jax.ShapeDtypeStruct((n, P_MAIN), F32),
            jax.ShapeDtypeStruct((n, LANES), F32),
        ],
        compiler_params=_cparams("arbitrary"),
        name="inproj",
    )(x, mod3, mod3, w_main, w_small)


def _gates(ps, hp):
    lane = lax.broadcasted_iota(jnp.int32, ps.shape, 1)
    beta = jax.nn.sigmoid(ps)
    g = -jnp.exp(hp[0:1, :]) * jax.nn.softplus(ps + hp[1:2, :])
    return jnp.where(lane >= G_LANE, g, jnp.where(lane >= BETA_LANE, beta, 0.0))


def _qkv_finish(y, out_ref, idx, first_head=0):
    y = _silu(y)
    for j in range(y.shape[1] // DN_HEAD_DIM):
        h = first_head + j
        seg = y[:, j * DN_HEAD_DIM:(j + 1) * DN_HEAD_DIM]
        if h < 2 * DN_HEADS:
            seg = seg * lax.rsqrt(jnp.sum(seg * seg, -1, keepdims=True) + RMS_EPS)
            if h < DN_HEADS:
                seg = seg * (DN_HEAD_DIM ** -0.5)
        out_ref[idx + (slice(None), slice(h * DN_HEAD_DIM, (h + 1) * DN_HEAD_DIM))] = seg


def _shifted(x, tail_ref, i, row):
    s = pltpu.roll(x, i, 0)
    top = s[0:SUBLANES]
    for r in range(i):
        top = jnp.where(row == r, tail_ref[8 - i + r:8 - i + r + 1, :], top)
    return jnp.concatenate([top, s[SUBLANES:]], axis=0)


def _inproj_pre_prompt_kernel(x_ref, sc_ref, sh_ref, wm_ref, ws_ref, cw_ref, dw_ref, hp_ref, cb_ref, db_ref,
                              yc_ref, qkv_ref, z_ref, gb_ref, ncb_ref, ndb_ref, tc_scr, td_scr):
    t = pl.program_id(1)
    tm = x_ref.shape[0]

    @pl.when(t == 0)
    def _():
        tc_scr[8 - (CONV_K - 1):8, :] = cb_ref[0]
        td_scr[8 - (DN_CONV_K - 1):8, :] = db_ref[0]

    u_f32 = x_ref[...] * (1.0 + sc_ref[0]) + sh_ref[0]
    u = u_f32.astype(BF16)
    proj = lambda lo, hi: lax.dot_general(u, wm_ref[lo:hi, :], _NT, preferred_element_type=F32)

    row = lax.broadcasted_iota(jnp.int32, (SUBLANES, DN_WIDTH), 0)

    def dn_part(xp, part):
        cols = slice(part * DN_WIDTH, (part + 1) * DN_WIDTH)
        tail = td_scr.at[:, cols]
        yp = _shifted(xp, tail, 3, row) * dw_ref[0:1, cols]
        yp = yp + _shifted(xp, tail, 2, row) * dw_ref[1:2, cols]
        yp = yp + _shifted(xp, tail, 1, row) * dw_ref[2:3, cols]
        yp = yp + xp * dw_ref[3:4, cols]
        _qkv_finish(yp, qkv_ref, (), first_head=part * DN_HEADS)
        return xp[tm - (DN_CONV_K - 1):tm, :]

    qkv_cols = lambda part: (OFF_QKV + part * DN_WIDTH, OFF_QKV + (part + 1) * DN_WIDTH)
    x_q = proj(*qkv_cols(0))
    x_k = proj(*qkv_cols(1))
    last_q = dn_part(x_q, 0)
    x_v = proj(*qkv_cols(2))
    last_k = dn_part(x_k, 1)
    p_c = proj(OFF_CC, OFF_CH)
    last_v = dn_part(x_v, 2)
    p_h = proj(OFF_CH, OFF_QKV)
    cgh = p_c * p_h
    p_b = proj(0, OFF_CC)
    y = _shifted(cgh, tc_scr, 2, row) * cw_ref[0:1, :]
    y = y + _shifted(cgh, tc_scr, 1, row) * cw_ref[1:2, :]
    y = y + cgh * cw_ref[2:3, :]
    p_z = proj(OFF_Z, P_MAIN)
    yc_ref[...] = (p_b * y).astype(yc_ref.dtype)
    last_c = cgh[tm - (CONV_K - 1):tm, :]
    small = _mm(u_f32, ws_ref[...], _NT, mode="bf16")
    z_ref[...] = p_z.astype(z_ref.dtype)
    gb_ref[...] = _gates(small, hp_ref[...])

    last_d = jnp.concatenate([last_q, last_k, last_v], axis=-1)
    tc_scr[8 - (CONV_K - 1):8, :] = last_c
    td_scr[8 - (DN_CONV_K - 1):8, :] = last_d

    @pl.when(t == pl.num_programs(1) - 1)
    def _():
        ncb_ref[0] = last_c
        ndb_ref[0] = last_d


def _inproj_pre_prompt(x, mod3, w_main, w_small, conv_w, dn_conv_w, hp, conv_buf, dn_buf, bsz, seq, tm):
    n = x.shape[0]
    nt = seq // tm
    rows = lambda w: pl.BlockSpec((tm, w), lambda b, t: (b * nt + t, 0))
    full = lambda a: pl.BlockSpec(a.shape, lambda b, t: (0,) * a.ndim)
    per_b = lambda k, w: pl.BlockSpec((1, k, w), lambda b, t: (b, 0, 0))
    mod = lambda chunk: pl.BlockSpec((1, 1, D_MODEL), lambda b, t: (b, 0, chunk))
    return pl.pallas_call(
        _inproj_pre_prompt_kernel,
        grid=(bsz, nt),
        in_specs=[rows(D_MODEL), mod(1), mod(0), full(w_main), full(w_small),
                  full(conv_w), full(dn_conv_w), full(hp),
                  per_b(CONV_K - 1, CONV_WIDTH), per_b(DN_CONV_K - 1, 3 * DN_WIDTH)],
        out_specs=[rows(CONV_WIDTH), rows(3 * DN_WIDTH), rows(DN_WIDTH), rows(LANES),
                   per_b(CONV_K - 1, CONV_WIDTH), per_b(DN_CONV_K - 1, 3 * DN_WIDTH)],
        out_shape=[
            jax.ShapeDtypeStruct((n, CONV_WIDTH), BF16),
            jax.ShapeDtypeStruct((n, 3 * DN_WIDTH), F32),
            jax.ShapeDtypeStruct((n, DN_WIDTH), BF16),
            jax.ShapeDtypeStruct((n, LANES), F32),
            jax.ShapeDtypeStruct((bsz, CONV_K - 1, CONV_WIDTH), F32),
            jax.ShapeDtypeStruct((bsz, DN_CONV_K - 1, 3 * DN_WIDTH), F32),
        ],
        scratch_shapes=[pltpu.VMEM((SUBLANES, CONV_WIDTH), F32), pltpu.VMEM((SUBLANES, 3 * DN_WIDTH), F32)],
        compiler_params=_cparams("arbitrary", "arbitrary"),
        name="inproj_pre_prompt",
    )(x, mod3, mod3, w_main, w_small, conv_w, dn_conv_w, hp, conv_buf, dn_buf)


def _pre_sample_kernel(pm_ref, ps_ref, cw_ref, dw_ref, hp_ref, cb_ref, db_ref,
                       yc_ref, qkv_ref, gb_ref, ncb_ref, ndb_ref):
    steps = pm_ref.shape[0]
    cgh = [pm_ref[t, :, OFF_CC:OFF_CH] * pm_ref[t, :, OFF_CH:OFF_QKV] for t in range(steps)]
    ext = [cb_ref[i] for i in range(CONV_K - 1)] + cgh
    for t in range(steps):
        y = ext[t] * cw_ref[0:1, :]
        for i in range(1, CONV_K):
            y = y + ext[t + i] * cw_ref[i:i + 1, :]
        yc_ref[t] = pm_ref[t, :, 0:OFF_CC] * y
    for i in range(CONV_K - 1):
        ncb_ref[i] = ext[len(ext) - (CONV_K - 1) + i]

    xq = [pm_ref[t, :, OFF_QKV:OFF_Z] for t in range(steps)]
    extq = [db_ref[i] for i in range(DN_CONV_K - 1)] + xq
    for t in range(steps):
        y = extq[t] * dw_ref[0:1, :]
        for i in range(1, DN_CONV_K):
            y = y + extq[t + i] * dw_ref[i:i + 1, :]
        _qkv_finish(y, qkv_ref, (t,))
        gb_ref[t] = _gates(ps_ref[t], hp_ref[...])
    for i in range(DN_CONV_K - 1):
        ndb_ref[i] = extq[len(extq) - (DN_CONV_K - 1) + i]


def _pre_sample(pm, ps, conv_w, dn_conv_w, hp, conv_buf_tm, dn_buf_tm, bsz, steps, bt):
    slab = lambda k, w: pl.BlockSpec((k, bt, w), lambda i: (0, i, 0))
    full = lambda a: pl.BlockSpec(a.shape, lambda i: (0,) * a.ndim)
    return pl.pallas_call(
        _pre_sample_kernel,
        grid=(bsz // bt,),
        in_specs=[slab(steps, P_MAIN), slab(steps, LANES), full(conv_w), full(dn_conv_w), full(hp),
                  slab(CONV_K - 1, CONV_WIDTH), slab(DN_CONV_K - 1, 3 * DN_WIDTH)],
        out_specs=[slab(steps, CONV_WIDTH), slab(steps, 3 * DN_WIDTH), slab(steps, LANES),
                   slab(CONV_K - 1, CONV_WIDTH), slab(DN_CONV_K - 1, 3 * DN_WIDTH)],
        out_shape=[
            jax.ShapeDtypeStruct((steps, bsz, CONV_WIDTH), F32),
            jax.ShapeDtypeStruct((steps, bsz, 3 * DN_WIDTH), F32),
            jax.ShapeDtypeStruct((steps, bsz, LANES), F32),
            jax.ShapeDtypeStruct((CONV_K - 1, bsz, CONV_WIDTH), F32),
            jax.ShapeDtypeStruct((DN_CONV_K - 1, bsz, 3 * DN_WIDTH), F32),
        ],
        compiler_params=_cparams("arbitrary"),
        name="pre_sample",
    )(pm.reshape(steps, bsz, P_MAIN), ps.reshape(steps, bsz, LANES), conv_w, dn_conv_w, hp,
      conv_buf_tm, dn_buf_tm)


def _cumsum_chunks(x, c):
    row = lax.broadcasted_iota(jnp.int32, x.shape, 0) & (c - 1)
    s = 1
    while s < c:
        x = x + jnp.where(row >= s, pltpu.roll(x, s, 0), 0.0)
        s *= 2
    return x


def _gdn_problems(qkv, gb, c):
    n = qkv.shape[0] // c
    gc = _cumsum_chunks(gb, c)
    q, k, v, g_b, beta_b = [], [], [], [], []
    for ci in range(n):
        rs = slice(ci * c, (ci + 1) * c)
        for h in range(DN_HEADS):
            ls = lambda base: slice(base + h * DN_HEAD_DIM, base + (h + 1) * DN_HEAD_DIM)
            q.append(qkv[rs, ls(0)])
            k.append(qkv[rs, ls(DN_WIDTH)])
            v.append(qkv[rs, ls(2 * DN_WIDTH)])
            g_b.append(jnp.broadcast_to(gc[rs, G_LANE + h:G_LANE + h + 1], (c, DN_HEAD_DIM)))
            beta_b.append(jnp.broadcast_to(gb[rs, BETA_LANE + h:BETA_LANE + h + 1], (c, DN_HEAD_DIM)))
    return tuple(jnp.stack(a) for a in (q, k, v, g_b, beta_b))


def _unit_lower_solve(m, rhs, c, small, mode):
    if small:
        sol = rhs
        for j in range(c - 1):
            sol = sol - m[:, :, j:j + 1] * sol[:, j:j + 1, :]
        return sol
    row = lax.broadcasted_iota(jnp.int32, m.shape, 1)
    col = lax.broadcasted_iota(jnp.int32, m.shape, 2)
    p = -m
    t = jnp.where(row == col, 1.0, 0.0) + p
    p = _mm(p, p, _BNN, "bf16")
    levels = int(math.log2(c))
    for lvl in range(1, levels):
        if lvl < levels - 1:
            y = _mm(jnp.concatenate([t, p], axis=1), p, _BNN, "bf16")
            t = t + y[:, :c]
            p = y[:, c:]
        else:
            t = t + _mm(t, p, _BNN, "bf16")
    x = _mm(t, rhs, _BNN, "bf16")
    resid = rhs - x - _mm(m, x, _BNN, mode)
    return x + _mm(t, resid, _BNN, "bf16")


def _gdn_solve(q, k, v, g_b, beta_b, *, c, small, mode_gram, mode_solve):
    n = q.shape[0]
    row = lax.broadcasted_iota(jnp.int32, (n, c, c), 1)
    col = lax.broadcasted_iota(jnp.int32, (n, c, c), 2)
    g_cc = g_b[:, :, :c]
    g_row = jnp.sum(jnp.where(row == col, g_cc, 0.0), axis=1, keepdims=True)
    gamma = jnp.exp(jnp.where(row >= col, g_cc - g_row, -jnp.inf))
    kbeta = k * beta_b
    exp_g = jnp.exp(g_b)
    gram = _mm(jnp.concatenate([kbeta, q], axis=1), k, _BNT, mode_gram)
    m = jnp.where(row > col, gram[:, :c] * gamma, 0.0)
    attn = gram[:, c:] * gamma
    rhs = jnp.concatenate([kbeta * exp_g, v * beta_b], axis=-1)
    sol = _unit_lower_solve(m, rhs, c, small, mode_solve)
    g_last = g_b[:, c - 1:c, :]
    return sol, attn, q * exp_g, k * jnp.exp(g_last - g_b), jnp.exp(g_last)


def _gdn_fold(sol, attn, q_dec, k_dec, mode):
    kd = _mm(k_dec, sol, _BTN, mode)
    at = _mm(attn, sol, _BNN, mode)
    return kd[..., :DN_HEAD_DIM], kd[..., DN_HEAD_DIM:], q_dec - at[..., :DN_HEAD_DIM], at[..., DN_HEAD_DIM:]


def _gdn_state_step(a_mat, b_mat, q_t, o_intra, d_last, s, mode):
    r = _mm(jnp.concatenate([a_mat, q_t], axis=1), s, _BNN, mode)
    s_new = s * d_last - r[:, :DN_HEAD_DIM] + b_mat
    return r[:, DN_HEAD_DIM:] + o_intra, s_new


def _gdn_direct_step(sol, attn, q_dec, k_dec, d_last, s, mode):
    c = attn.shape[1]
    r = _mm(jnp.concatenate([sol[..., :DN_HEAD_DIM], q_dec], axis=1), s, _BNN, mode)
    v_new = sol[..., DN_HEAD_DIM:] - r[:, :c]
    o = r[:, c:] + _mm(attn, v_new, _BNN, mode)
    return o, s * d_last + _mm(k_dec, v_new, _BTN, mode)


_PROMPT_MODES = dict(mode_gram="bf16", mode_solve="x3")
_SAMPLE_MODES = dict(mode_gram="bf16", mode_solve="bf16")
_STATE_MODE = "bf16"


def _delta_prompt_kernel(qkv_ref, gb_ref, s0_ref, o_ref, sn_ref, s_scr):
    t = pl.program_id(1)
    tc = qkv_ref.shape[0]

    @pl.when(t == 0)
    def _():
        s_scr[...] = s0_ref[0]

    probs = _gdn_problems(qkv_ref[...], gb_ref[...], DN_CHUNK)
    sol, attn, q_dec, k_dec, d_last = _gdn_solve(*probs, c=DN_CHUNK, small=False, **_PROMPT_MODES)
    a_mat, b_mat, q_t, o_intra = _gdn_fold(sol, attn, q_dec, k_dec, _STATE_MODE)
    s = s_scr[...]
    for ci in range(tc // DN_CHUNK):
        ps = slice(ci * DN_HEADS, (ci + 1) * DN_HEADS)
        o, s = _gdn_state_step(a_mat[ps], b_mat[ps], q_t[ps], o_intra[ps], d_last[ps], s, _STATE_MODE)
        for h in range(DN_HEADS):
            o_ref[ci * DN_CHUNK:(ci + 1) * DN_CHUNK, h * DN_HEAD_DIM:(h + 1) * DN_HEAD_DIM] = (
                o[h].astype(o_ref.dtype))
    s_scr[...] = s

    @pl.when(t == pl.num_programs(1) - 1)
    def _():
        sn_ref[0] = s


def _delta_prompt(qkv, gb, s0, bsz, seq, tc):
    n = qkv.shape[0]
    nt = seq // tc
    rows = lambda w: pl.BlockSpec((tc, w), lambda b, t: (b * nt + t, 0))
    state = pl.BlockSpec((1, DN_HEADS, DN_HEAD_DIM, DN_HEAD_DIM), lambda b, t: (b, 0, 0, 0))
    return pl.pallas_call(
        _delta_prompt_kernel,
        grid=(bsz, nt),
        in_specs=[rows(3 * DN_WIDTH), rows(LANES), state],
        out_specs=[rows(DN_WIDTH), state],
        out_shape=[
            jax.ShapeDtypeStruct((n, DN_WIDTH), BF16),
            jax.ShapeDtypeStruct((bsz, DN_HEADS, DN_HEAD_DIM, DN_HEAD_DIM), F32),
        ],
        scratch_shapes=[pltpu.VMEM((DN_HEADS, DN_HEAD_DIM, DN_HEAD_DIM), F32)],
        compiler_params=_cparams("arbitrary", "arbitrary"),
        name="delta_prompt",
    )(qkv, gb, s0)


def _delta_sample_kernel(qkv_ref, gb_ref, s0_ref, *rest):
    o_ref, sn_ref = rest[-2:]
    bt = qkv_ref.shape[0]
    c = SAMPLE_CHUNK
    probs = _gdn_problems(qkv_ref[...].reshape(bt * c, 3 * DN_WIDTH), gb_ref[...].reshape(bt * c, LANES), c)
    parts = _gdn_solve(*probs, c=c, small=True, **_SAMPLE_MODES)
    s = s0_ref[0].reshape(bt * DN_HEADS, DN_HEAD_DIM, DN_HEAD_DIM)
    o, s = _gdn_direct_step(*parts, s, _STATE_MODE)
    for b in range(bt):
        for h in range(DN_HEADS):
            o_ref[b, :, h * DN_HEAD_DIM:(h + 1) * DN_HEAD_DIM] = o[b * DN_HEADS + h]
    sn_ref[0] = s.reshape(bt, DN_HEADS, DN_HEAD_DIM, DN_HEAD_DIM)


def _delta_sample(qkv_b, gb_b, state_all, layer, carried, bt):
    bsz = qkv_b.shape[0]
    blk = lambda w: pl.BlockSpec((bt, SAMPLE_CHUNK, w), lambda i: (i, 0, 0))
    state = pl.BlockSpec((1, bt, DN_HEADS, DN_HEAD_DIM, DN_HEAD_DIM), lambda i: (layer, i, 0, 0, 0))
    in_specs, args, aliases = [blk(3 * DN_WIDTH), blk(LANES), state], [qkv_b, gb_b, state_all], {}
    if carried is not None:
        in_specs.append(pl.BlockSpec(memory_space=pl.ANY))
        args.append(carried)
        aliases = {3: 1}
    return pl.pallas_call(
        _delta_sample_kernel,
        grid=(bsz // bt,),
        in_specs=in_specs,
        out_specs=[blk(DN_WIDTH), state],
        out_shape=[
            jax.ShapeDtypeStruct((bsz, SAMPLE_CHUNK, DN_WIDTH), F32),
            jax.ShapeDtypeStruct(state_all.shape, F32),
        ],
        input_output_aliases=aliases,
        compiler_params=_cparams("arbitrary"),
        name="delta_sample",
    )(*args)


def _outproj_kernel(yc_ref, o_ref, z_ref, x_ref, g1_ref, nw_ref, w_ref, lg_ref, lb_ref, out_ref):
    o = o_ref[...].astype(F32)
    parts = []
    for h in range(DN_HEADS):
        oh = o[:, h * DN_HEAD_DIM:(h + 1) * DN_HEAD_DIM]
        parts.append(oh * lax.rsqrt(jnp.mean(oh * oh, -1, keepdims=True) + RMS_EPS) * nw_ref[...])
    og = jnp.concatenate(parts, axis=-1) * _silu(z_ref[...].astype(F32))
    mixed = jnp.concatenate([yc_ref[...].astype(BF16), og.astype(BF16)], axis=-1)
    m = jnp.dot(mixed, w_ref[...], preferred_element_type=F32)
    r = DEEPNORM_ALPHA * x_ref[...] + (1.0 + _tile_rows(g1_ref, x_ref.shape[0])) * m
    out_ref[...] = _layer_norm(r, lg_ref[...], lb_ref[...])


def _outproj(yc, o, z_src, z_blk, x, mod3, tpg, norm_w, w_out, ln_g, ln_b, tm):
    n = x.shape[0]
    vec = lambda w: pl.BlockSpec((1, w), lambda i: (0, 0))
    return pl.pallas_call(
        _outproj_kernel,
        grid=(n // tm,),
        in_specs=[
            pl.BlockSpec((tm, CONV_WIDTH), lambda i: (i, 0)),
            pl.BlockSpec((tm, DN_WIDTH), lambda i: (i, 0)),
            pl.BlockSpec((tm, DN_WIDTH), lambda i: (i, z_blk)),
            pl.BlockSpec((tm, D_MODEL), lambda i: (i, 0)),
            _mod_spec(mod3, 2, tpg),
            vec(DN_HEAD_DIM),
            pl.BlockSpec((D_MODEL, D_MODEL), lambda i: (0, 0)),
            vec(D_MODEL), vec(D_MODEL),
        ],
        out_specs=pl.BlockSpec((tm, D_MODEL), lambda i: (i, 0)),
        out_shape=jax.ShapeDtypeStruct((n, D_MODEL), F32),
        compiler_params=_cparams("arbitrary"),
        name="outproj",
    )(yc, o, z_src, x, mod3, norm_w, w_out, ln_g, ln_b)


R_E1, R_E2, R_G1, R_G2, R_R1, R_R2 = range(6)
N_ROW_BUFS = 2


def _joint_specs(xp, xs, mod_p, mod_s, chunks, tm, seq):
    last = xp.shape[0] // tm - 1
    tpg = seq // tm
    assert xs.shape[0] == tm
    specs = [pl.BlockSpec((tm, D_MODEL), lambda i, *_: (jnp.minimum(i, last), 0)),
             pl.BlockSpec((tm, D_MODEL), lambda i, *_: (0, 0))]
    for c in chunks:
        specs.append(pl.BlockSpec((1, 1, D_MODEL), lambda i, *_, c=c: (jnp.minimum(i, last) // tpg, 0, c)))
        specs.append(pl.BlockSpec((1, mod_s.shape[1], D_MODEL), lambda i, *_, c=c: (0, 0, c)))
    return specs


def _joint_modulated(i, n_prompt_tiles, xp_ref, xs_ref, scp_ref, scs_ref, shp_ref, shs_ref, u_ref):
    tm = xp_ref.shape[0]

    @pl.when(i < n_prompt_tiles)
    def _():
        u_ref[...] = (xp_ref[...] * (1.0 + scp_ref[0]) + shp_ref[0]).reshape(u_ref.shape)

    @pl.when(i >= n_prompt_tiles)
    def _():
        u_ref[...] = (xs_ref[...] * (1.0 + _tile_rows(scs_ref, tm)) + _tile_rows(shs_ref, tm)).reshape(
            u_ref.shape)


def _router_kernel(xp_ref, xs_ref, scp_ref, scs_ref, shp_ref, shs_ref, wr_ref, br_ref,
                   route_ref, route_t_ref, cnt_ref, u_scr, *, n_prompt_tiles):
    tm = xp_ref.shape[0]
    _joint_modulated(pl.program_id(0), n_prompt_tiles, xp_ref, xs_ref, scp_ref, scs_ref, shp_ref, shs_ref,
                     u_scr)
    lt = _mm(wr_ref[...], u_scr[...], _NT, mode="x3") + br_ref[:, 0:1]
    e_idx = lax.broadcasted_iota(jnp.int32, lt.shape, 0)
    ex = jnp.exp(lt - jnp.max(lt, 0, keepdims=True))
    probs = ex / jnp.sum(ex, 0, keepdims=True)
    p1 = jnp.max(probs, 0, keepdims=True)
    i1 = jnp.min(jnp.where(probs == p1, e_idx, N_EXPERTS), 0, keepdims=True)
    rest = jnp.where(e_idx == i1, -1.0, probs)
    p2 = jnp.max(rest, 0, keepdims=True)
    i2 = jnp.min(jnp.where(rest == p2, e_idx, N_EXPERTS), 0, keepdims=True)
    tot = p1 + p2
    sel = jnp.where(e_idx == i1, 1.0, 0.0) + jnp.where(e_idx == i2, 1.0, 0.0)
    r = lax.broadcasted_iota(jnp.int32, (tm, tm), 0)
    c = lax.broadcasted_iota(jnp.int32, (tm, tm), 1)
    incl = jnp.dot(sel.astype(BF16), jnp.where(r <= c, 1.0, 0.0).astype(BF16), preferred_element_type=F32)
    excl = incl - sel
    r1 = jnp.sum(jnp.where(e_idx == i1, excl, 0.0), 0, keepdims=True)
    r2 = jnp.sum(jnp.where(e_idx == i2, excl, 0.0), 0, keepdims=True)
    cnt_ref[0] = jnp.broadcast_to(incl[:, tm - 1:tm], (N_EXPERTS, LANES))
    route_t = jnp.zeros_like(lt)
    for j, val in enumerate((i1.astype(F32), i2.astype(F32), p1 / tot, p2 / tot, r1, r2)):
        route_t = jnp.where(e_idx == j, val, route_t)
    route_t_ref[0] = route_t
    route_ref[...] = jnp.transpose(
        jnp.concatenate([route_t, jnp.zeros((LANES - N_EXPERTS, tm), F32)], axis=0))


def _router(xp, xs, mod_p, mod_s, seq, w_router, b_router, tm):
    ntp = xp.shape[0] // tm
    nt = ntp + 1
    return pl.pallas_call(
        functools.partial(_router_kernel, n_prompt_tiles=ntp),
        grid=(nt,),
        in_specs=_joint_specs(xp, xs, mod_p, mod_s, (4, 3), tm, seq) + [
            pl.BlockSpec((N_EXPERTS, D_MODEL), lambda i: (0, 0)),
            pl.BlockSpec((N_EXPERTS, LANES), lambda i: (0, 0)),
        ],
        out_specs=[pl.BlockSpec((tm, LANES), lambda i: (i, 0)),
                   pl.BlockSpec((1, SUBLANES, tm), lambda i: (i, 0, 0)),
                   pl.BlockSpec((1, N_EXPERTS, LANES), lambda i: (i, 0, 0))],
        out_shape=[jax.ShapeDtypeStruct((nt * tm, LANES), F32),
                   jax.ShapeDtypeStruct((nt, SUBLANES, tm), F32),
                   jax.ShapeDtypeStruct((nt, N_EXPERTS, LANES), F32)],
        scratch_shapes=[pltpu.VMEM((tm, D_MODEL), F32)],
        compiler_params=_cparams("arbitrary"),
        name="router",
    )(xp, xs, mod_p, mod_s, mod_p, mod_s, w_router, b_router)


def _dispatch_plan(route_t, counts, tme):
    n_rt, _, tm = route_t.shape
    n = n_rt * tm
    cnt = counts[:, :, 0].astype(jnp.int32)
    tile_off = jnp.cumsum(cnt, axis=0) - cnt
    tot = jnp.sum(cnt, axis=0)
    gsz = (tot + tme - 1) // tme * tme
    gend = jnp.cumsum(gsz)
    goff = gend - gsz
    base = (goff[None, :] + tile_off)[:, :, None]
    ids = jnp.arange(N_EXPERTS, dtype=jnp.int32)[None, :, None]
    field = lambda j: route_t[:, j:j + 1, :].astype(jnp.int32)
    pick = lambda e: jnp.sum(jnp.where(ids == e, base, 0), axis=1, keepdims=True)
    slot1 = (pick(field(R_E1)) + field(R_R1)).reshape(n)
    slot2 = (pick(field(R_E2)) + field(R_R2)).reshape(n)
    n_tiles = 2 * n // tme + N_EXPERTS
    start = jnp.arange(n_tiles, dtype=jnp.int32) * tme
    valid = (start < gend[-1]).astype(jnp.int32)
    blk = jnp.minimum(jnp.arange(n_tiles, dtype=jnp.int32), gend[-1] // tme - 1)
    tile_e = jnp.sum((blk * tme)[:, None] >= gend[None, :], axis=1).astype(jnp.int32)
    zoff = jnp.where(gsz > 0, gend - tme, (n_tiles + jnp.arange(N_EXPERTS, dtype=jnp.int32)) * tme)
    return slot1, slot2, zoff, tile_e, valid, blk, n_tiles


def _dispatch_kernel(s1_ref, s2_ref, zoff_ref, xp_ref, xs_ref, scp_ref, scs_ref, shp_ref, shs_ref,
                     wg_ref, wu_ref, wd_ref, ug_hbm, wgb_ref, wub_ref, wdb_ref, ubuf, zbuf, sem,
                     *, n_prompt_tiles):
    i = pl.program_id(0)

    @pl.when(i < 2 * N_EXPERTS)
    def _():
        wgb_ref[...] = wg_ref[...].astype(BF16)
        wub_ref[...] = wu_ref[...].astype(BF16)
        wdb_ref[...] = wd_ref[...].astype(BF16)

    tm = xp_ref.shape[0]
    tme = zbuf.shape[0]

    n_steps = n_prompt_tiles + 1
    fill_sem = 2 * N_ROW_BUFS

    @pl.when(i == 0)
    def _():
        zbuf[...] = jnp.zeros_like(zbuf)
        fill = lambda e: pltpu.make_async_copy(
            zbuf, ug_hbm.at[pl.ds(pl.multiple_of(zoff_ref[e], 8), tme)], sem.at[fill_sem])
        for e in range(N_EXPERTS):
            fill(e).start()
        for e in range(N_EXPERTS):
            fill(e).wait()

    def drain(b):
        for stream in range(2):
            pltpu.make_async_copy(ubuf.at[b], ubuf.at[b], sem.at[2 * b + stream]).wait()

    buf = i % N_ROW_BUFS

    @pl.when(i >= N_ROW_BUFS)
    def _():
        drain(buf)

    _joint_modulated(i, n_prompt_tiles, xp_ref, xs_ref, scp_ref, scs_ref, shp_ref, shs_ref, ubuf.at[buf])
    base = i * tm

    def issue(j, carry):
        for k in range(SUBLANES):
            row = ubuf.at[buf, j, pl.ds(k, 1)]
            t = base + j * SUBLANES + k
            pltpu.make_async_copy(row, ug_hbm.at[pl.ds(s1_ref[t], 1)], sem.at[2 * buf]).start()
            pltpu.make_async_copy(row, ug_hbm.at[pl.ds(s2_ref[t], 1)], sem.at[2 * buf + 1]).start(priority=1)
        return carry

    lax.fori_loop(0, tm // SUBLANES, issue, 0)

    @pl.when(i == n_steps - 1)
    def _():
        for b in range(min(N_ROW_BUFS, n_steps)):
            drain(b)


def _dispatch(xp, xs, mod_p, mod_s, seq, slot1, slot2, zoff, w_gate, w_up, w_down, tm, tme, rows):
    ntp = xp.shape[0] // tm
    assert ntp + 1 >= 2 * N_EXPERTS
    d_e = w_gate.shape[-1]
    half = lambda i: jnp.minimum(i, 2 * N_EXPERTS - 1)
    w_in_spec = pl.BlockSpec((1, D_MODEL // 2, d_e), lambda i, *_: (half(i) // 2, half(i) % 2, 0))
    w_dn_spec = pl.BlockSpec((1, d_e // 2, D_MODEL), lambda i, *_: (half(i) // 2, half(i) % 2, 0))
    return pl.pallas_call(
        functools.partial(_dispatch_kernel, n_prompt_tiles=ntp),
        grid_spec=pltpu.PrefetchScalarGridSpec(
            num_scalar_prefetch=3,
            grid=(ntp + 1,),
            in_specs=_joint_specs(xp, xs, mod_p, mod_s, (4, 3), tm, seq) + [w_in_spec, w_in_spec, w_dn_spec],
            out_specs=[pl.BlockSpec(memory_space=pl.ANY), w_in_spec, w_in_spec, w_dn_spec],
            scratch_shapes=[pltpu.VMEM((N_ROW_BUFS, tm // SUBLANES, SUBLANES, D_MODEL), F32),
                            pltpu.VMEM((tme, D_MODEL), F32),
                            pltpu.SemaphoreType.DMA((2 * N_ROW_BUFS + 1,))],
        ),
        out_shape=[jax.ShapeDtypeStruct((rows, D_MODEL), F32),
                   jax.ShapeDtypeStruct(w_gate.shape, BF16), jax.ShapeDtypeStruct(w_up.shape, BF16),
                   jax.ShapeDtypeStruct(w_down.shape, BF16)],
        compiler_params=_cparams("arbitrary"),
        name="dispatch",
    )(slot1, slot2, zoff, xp, xs, mod_p, mod_s, mod_p, mod_s, w_gate, w_up, w_down)


def _expert_kernel(te_ref, tv_ref, tb_ref, ug_ref, wg_ref, wu_ref, wd_ref, ys_ref):
    i = pl.program_id(0)

    @pl.when(tv_ref[i] == 1)
    def _():
        u = ug_ref[...].astype(BF16)
        h = _silu(jnp.dot(u, wg_ref[0], preferred_element_type=F32)) * jnp.dot(
            u, wu_ref[0], preferred_element_type=F32)
        ys_ref[...] = jnp.dot(h.astype(BF16), wd_ref[0], preferred_element_type=F32)

    @pl.when(tv_ref[i] == 0)
    def _():
        ys_ref[...] = jnp.zeros_like(ys_ref)


def _experts(ug, tile_e, valid, blk, w_gate, w_up, w_down, tme, n_tiles):
    d_e = w_gate.shape[-1]
    return pl.pallas_call(
        _expert_kernel,
        grid_spec=pltpu.PrefetchScalarGridSpec(
            num_scalar_prefetch=3,
            grid=(n_tiles,),
            in_specs=[
                pl.BlockSpec((tme, D_MODEL), lambda i, te, tv, tb: (tb[i], 0)),
                pl.BlockSpec((1, D_MODEL, d_e), lambda i, te, tv, tb: (te[i], 0, 0)),
                pl.BlockSpec((1, D_MODEL, d_e), lambda i, te, tv, tb: (te[i], 0, 0)),
                pl.BlockSpec((1, d_e, D_MODEL), lambda i, te, tv, tb: (te[i], 0, 0)),
            ],
            out_specs=pl.BlockSpec((tme, D_MODEL), lambda i, te, tv, tb: (i, 0)),
        ),
        out_shape=jax.ShapeDtypeStruct((n_tiles * tme, D_MODEL), F32),
        compiler_params=_cparams("arbitrary"),
        name="experts",
    )(tile_e, valid, blk, ug, w_gate, w_up, w_down)


def _combine_kernel(s1_ref, s2_ref, ys_hbm, route_ref, xp_ref, xs_ref, g2p_ref, g2s_ref, lg_ref, lb_ref,
                    outp_ref, outs_ref, y1buf, y2buf, sem, *, n_prompt_tiles):
    i = pl.program_id(0)
    tm = xp_ref.shape[0]
    n_steps = n_prompt_tiles + 1

    def gather(tile, b):
        def issue(j, carry):
            for k in range(SUBLANES):
                t = tile * tm + j * SUBLANES + k
                pltpu.make_async_copy(ys_hbm.at[pl.ds(s1_ref[t], 1)], y1buf.at[b, j, pl.ds(k, 1)],
                                      sem.at[2 * b]).start()
                pltpu.make_async_copy(ys_hbm.at[pl.ds(s2_ref[t], 1)], y2buf.at[b, j, pl.ds(k, 1)],
                                      sem.at[2 * b + 1]).start(priority=1)
            return carry

        lax.fori_loop(0, tm // SUBLANES, issue, 0)

    buf = i % N_ROW_BUFS

    @pl.when(i == 0)
    def _():
        gather(0, 0)

    @pl.when(i + 1 < n_steps)
    def _():
        gather(i + 1, (i + 1) % N_ROW_BUFS)

    pltpu.make_async_copy(y1buf.at[buf], y1buf.at[buf], sem.at[2 * buf]).wait()
    pltpu.make_async_copy(y2buf.at[buf], y2buf.at[buf], sem.at[2 * buf + 1]).wait()
    route = route_ref[...]
    y1 = y1buf[buf].reshape(tm, D_MODEL)
    y2 = y2buf[buf].reshape(tm, D_MODEL)
    f = route[:, R_G1:R_G1 + 1] * y1 + route[:, R_G2:R_G2 + 1] * y2

    @pl.when(i < n_prompt_tiles)
    def _():
        r = DEEPNORM_ALPHA * xp_ref[...] + (1.0 + g2p_ref[0]) * f
        outp_ref[...] = _layer_norm(r, lg_ref[...], lb_ref[...])

    @pl.when(i >= n_prompt_tiles)
    def _():
        r = DEEPNORM_ALPHA * xs_ref[...] + (1.0 + _tile_rows(g2s_ref, tm)) * f
        outs_ref[...] = _layer_norm(r, lg_ref[...], lb_ref[...])


def _combine(ys, slot1, slot2, route, xp, xs, mod_p, mod_s, seq, ln_g, ln_b, tm):
    ntp = xp.shape[0] // tm
    vec = lambda w: pl.BlockSpec((1, w), lambda i, *_: (0, 0))
    x_specs = _joint_specs(xp, xs, mod_p, mod_s, (5,), tm, seq)
    return pl.pallas_call(
        functools.partial(_combine_kernel, n_prompt_tiles=ntp),
        grid_spec=pltpu.PrefetchScalarGridSpec(
            num_scalar_prefetch=2,
            grid=(ntp + 1,),
            in_specs=[
                pl.BlockSpec(memory_space=pl.ANY),
                pl.BlockSpec((tm, LANES), lambda i, *_: (i, 0)),
            ] + x_specs + [vec(D_MODEL), vec(D_MODEL)],
            out_specs=[x_specs[0], x_specs[1]],
            scratch_shapes=[pltpu.VMEM((N_ROW_BUFS, tm // SUBLANES, SUBLANES, D_MODEL), F32),
                            pltpu.VMEM((N_ROW_BUFS, tm // SUBLANES, SUBLANES, D_MODEL), F32),
                            pltpu.SemaphoreType.DMA((2 * N_ROW_BUFS,))],
        ),
        out_shape=[jax.ShapeDtypeStruct(xp.shape, F32), jax.ShapeDtypeStruct(xs.shape, F32)],
        compiler_params=_cparams("arbitrary"),
        name="combine",
    )(slot1, slot2, ys, route, xp, xs, mod_p, mod_s, ln_g, ln_b)


def _moe(xp, xs, mod_p, mod_s, seq, w_router, b_router, w_gate, w_up, w_down, ln_g, ln_b, tm, tme):
    route, route_t, counts = _router(xp, xs, mod_p, mod_s, seq, w_router, b_router, tm)
    slot1, slot2, zoff, tile_e, valid, blk, n_tiles = _dispatch_plan(route_t, counts, tme)
    ug, wg_b, wu_b, wd_b = _dispatch(xp, xs, mod_p, mod_s, seq, slot1, slot2, zoff, w_gate, w_up, w_down,
                                     tm, tme, (n_tiles + N_EXPERTS) * tme)
    ys = _experts(ug, tile_e, valid, blk, wg_b, wu_b, wd_b, tme, n_tiles)
    return _combine(ys, slot1, slot2, route, xp, xs, mod_p, mod_s, seq, ln_g, ln_b, tm)


def _ffn_kernel(x_ref, sc_ref, sh_ref, g2_ref, wg_ref, wu_ref, wd_ref, lg_ref, lb_ref, out_ref, *, tf):
    x = x_ref[...]
    tm = x.shape[0]
    u = (x * (1.0 + _tile_rows(sc_ref, tm)) + _tile_rows(sh_ref, tm)).astype(BF16)
    y = None
    for f0 in range(0, wg_ref.shape[1], tf):
        h = _silu(jnp.dot(u, wg_ref[:, f0:f0 + tf], preferred_element_type=F32)) * jnp.dot(
            u, wu_ref[:, f0:f0 + tf], preferred_element_type=F32)
        part = jnp.dot(h.astype(BF16), wd_ref[f0:f0 + tf, :], preferred_element_type=F32)
        y = part if y is None else y + part
    r = DEEPNORM_ALPHA * x + (1.0 + _tile_rows(g2_ref, tm)) * y
    out_ref[...] = _layer_norm(r, lg_ref[...], lb_ref[...])


def _ffn(x, mod3, tpg, w_gate, w_up, w_down, ln_g, ln_b, tm, tf):
    n = x.shape[0]
    vec = lambda w: pl.BlockSpec((1, w), lambda i: (0, 0))
    resident = lambda a: pl.BlockSpec(a.shape, lambda i: (0, 0), pipeline_mode=pl.Buffered(1))
    return pl.pallas_call(
        functools.partial(_ffn_kernel, tf=tf),
        grid=(n // tm,),
        in_specs=[
            pl.BlockSpec((tm, D_MODEL), lambda i: (i, 0)),
            _mod_spec(mod3, 4, tpg),
            _mod_spec(mod3, 3, tpg),
            _mod_spec(mod3, 5, tpg),
            resident(w_gate), resident(w_up), resident(w_down),
            vec(D_MODEL), vec(D_MODEL),
        ],
        out_specs=pl.BlockSpec((tm, D_MODEL), lambda i: (i, 0)),
        out_shape=jax.ShapeDtypeStruct((n, D_MODEL), F32),
        compiler_params=_cparams("arbitrary"),
        name="ffn_dense",
    )(x, mod3, mod3, mod3, w_gate, w_up, w_down, ln_g, ln_b)


def _token_mixer(l, x, mod3, rows_per_group, prm, tm, mixer_fn):
    yc, o, z_src, z_blk, st = mixer_fn(l, x, mod3)
    x = _outproj(yc, o, z_src, z_blk, x, mod3, rows_per_group // tm, prm["norm_w"][l], prm["w_out"][l],
                 prm["ln_g"][l, 0:1], prm["ln_b"][l, 0:1], tm)
    return x, st


def kernel(x_prompt, x_sample, state_conv, state_dn_conv, state_dn, c_prompt, c_sample, w_in, conv_w,
           dn_conv_w, a_log, dt_bias, dn_norm_w, w_out, w_ada, b_ada, ln_g, ln_b, w_ff_gate, w_ff_up,
           w_ff_down, w_router, b_router, w_exp_gate, w_exp_up, w_exp_down):
    bp, seq, d = x_prompt.shape
    bs, steps, _ = x_sample.shape
    hp = jnp.zeros((DEPTH, 2, LANES), F32)
    hp = hp.at[:, 0, G_LANE:].set(a_log).at[:, 1, G_LANE:].set(dt_bias)
    w_in_t = jnp.swapaxes(w_in, 1, 2)
    prm = {
        "w_main": w_in_t[:, :P_MAIN].astype(BF16),
        "w_small": w_in_t[:, w_in_t.shape[1] - LANES:],
        "norm_w": dn_norm_w.reshape(DEPTH, 1, DN_HEAD_DIM),
        "w_out": w_out.astype(BF16),
        "ln_g": ln_g, "ln_b": ln_b,
        "w_ff_gate": w_ff_gate.astype(BF16), "w_ff_up": w_ff_up.astype(BF16),
        "w_ff_down": w_ff_down.astype(BF16),
        "w_router": jnp.swapaxes(w_router, 1, 2),
        "b_router": jnp.broadcast_to(b_router[:, :, None], b_router.shape + (LANES,)),
        "w_exp_gate": w_exp_gate, "w_exp_up": w_exp_up, "w_exp_down": w_exp_down,
    }

    mod = _ada(jnp.concatenate([c_prompt, c_sample], axis=0), w_ada, b_ada)
    mod_p = [mod[l, :bp].reshape(bp, 1, 6 * d) for l in range(DEPTH)]
    mod_s = [mod[l, bp:].reshape(1, bs, 6 * d) for l in range(DEPTH)]

    tm_p = PROMPT_ROW_TILE
    zc = jnp.zeros((bp, CONV_K - 1, CONV_WIDTH), F32)
    zd = jnp.zeros((bp, DN_CONV_K - 1, 3 * DN_WIDTH), F32)
    zs = jnp.zeros((bp, DN_HEADS, DN_HEAD_DIM, DN_HEAD_DIM), F32)

    def mixer_prompt(l, x, mod3):
        yc, qkv, z, gb, ncb, ndb = _inproj_pre_prompt(
            x, mod3, prm["w_main"][l], prm["w_small"][l], conv_w[l], dn_conv_w[l], hp[l], zc, zd,
            bp, seq, tm_p)
        o, s_new = _delta_prompt(qkv, gb, zs, bp, seq, PROMPT_DELTA_TILE)
        return yc, o, z, 0, (ncb, ndb, s_new)

    dn_states = []

    def mixer_sample(l, x, mod3):
        pm, ps = _inproj(x, mod3, 1, prm["w_main"][l], prm["w_small"][l], steps * bs)
        cb_tm = jnp.transpose(state_conv[l], (1, 0, 2))
        db_tm = jnp.transpose(state_dn_conv[l], (1, 0, 2))
        yc, qkv, gb, ncb, ndb = _pre_sample(pm, ps, conv_w[l], dn_conv_w[l], hp[l], cb_tm, db_tm,
                                            bs, steps, SAMPLE_PRE_SEQS)
        to_b = lambda a: jnp.pad(jnp.transpose(a, (1, 0, 2)), ((0, 0), (0, SAMPLE_CHUNK - steps), (0, 0)))
        o_b, s_all = _delta_sample(to_b(qkv), to_b(gb), state_dn, l, dn_states[-1] if dn_states else None,
                                   SAMPLE_DELTA_SEQS)
        dn_states.append(s_all)
        o = jnp.transpose(o_b[:, :steps], (1, 0, 2)).reshape(steps * bs, DN_WIDTH)
        return (yc.reshape(steps * bs, CONV_WIDTH), o, pm, OFF_Z // DN_WIDTH,
                (jnp.transpose(ncb, (1, 0, 2)), jnp.transpose(ndb, (1, 0, 2))))

    x_p = x_prompt.reshape(bp * seq, d)
    x_s = jnp.transpose(x_sample, (1, 0, 2)).reshape(steps * bs, d)
    n_s = steps * bs
    st_p, st_s = [], []
    for l in range(DEPTH):
        x_p, st = _token_mixer(l, x_p, mod_p[l], seq, prm, tm_p, mixer_prompt)
        st_p.append(st)
        x_s, st = _token_mixer(l, x_s, mod_s[l], n_s, prm, n_s, mixer_sample)
        st_s.append(st)
        j = l // 2
        ln = (prm["ln_g"][l, 1:2], prm["ln_b"][l, 1:2])
        if l % 2 == 0:
            ffn_w = (prm["w_ff_gate"][j], prm["w_ff_up"][j], prm["w_ff_down"][j])
            x_p = _ffn(x_p, mod_p[l], seq // tm_p, *ffn_w, *ln, tm_p, FFN_HIDDEN_CHUNK)
            x_s = _ffn(x_s, mod_s[l], 1, *ffn_w, *ln, n_s, FFN_HIDDEN_CHUNK)
        else:
            x_p, x_s = _moe(x_p, x_s, mod_p[l], mod_s[l], seq, prm["w_router"][j], prm["b_router"][j],
                            prm["w_exp_gate"][j], prm["w_exp_up"][j], prm["w_exp_down"][j], *ln, tm_p,
                            EXPERT_SLOT_TILE)
    y_p = x_p
    y_s = jnp.transpose(x_s.reshape(steps, bs, d), (1, 0, 2))

    stack = lambda sts, i: jnp.stack([s[i] for s in sts])
    return (y_p.reshape(bp, seq, d), y_s,
            stack(st_p, 0), stack(st_p, 1), stack(st_p, 2),
            stack(st_s, 0), stack(st_s, 1), dn_states[-1])
```

```python
import functools
import math

import jax
import jax.numpy as jnp
from jax import lax
from jax.experimental import pallas as pl
from jax.experimental.pallas import tpu as pltpu

F32 = jnp.float32
BF16 = jnp.bfloat16

D_MODEL = 1024
DEPTH = 2
CONV_WIDTH = 512
CONV_K = 3
DN_HEADS = 4
DN_HEAD_DIM = 128
DN_WIDTH = DN_HEADS * DN_HEAD_DIM
DN_CONV_K = 4
DN_CHUNK = 64
N_EXPERTS = 8
DEEPNORM_ALPHA = (2.0 * DEPTH) ** 0.25
LN_EPS = 1e-5
RMS_EPS = 1e-6

OFF_CC = CONV_WIDTH
OFF_CH = 2 * CONV_WIDTH
OFF_QKV = 3 * CONV_WIDTH
OFF_Z = OFF_QKV + 3 * DN_WIDTH
P_MAIN = OFF_Z + DN_WIDTH
LANES = 128
SUBLANES = 8
BETA_LANE = LANES - 2 * DN_HEADS
G_LANE = LANES - DN_HEADS
SAMPLE_CHUNK = 8

VMEM_LIMIT = 56 * 1024 * 1024
MXU_TILE = 256

PROMPT_ROW_TILE = 512
PROMPT_OUTPROJ_TILE = 1024
PROMPT_DELTA_TILE = 512
FFN_HIDDEN_CHUNK = MXU_TILE
EXPERT_SLOT_TILE = 512
SAMPLE_PRE_SEQS = 32
SAMPLE_DELTA_SEQS = 16


def _cparams(*sem):
    return pltpu.CompilerParams(dimension_semantics=sem, vmem_limit_bytes=VMEM_LIMIT)


def _silu(x):
    return x * jax.nn.sigmoid(x)


def _layer_norm(r, g, b):
    mu = jnp.mean(r, -1, keepdims=True)
    xc = r - mu
    var = jnp.mean(xc * xc, -1, keepdims=True)
    return xc * lax.rsqrt(var + LN_EPS) * g + b


def _split_bf16(a):
    hi = a.astype(BF16)
    lo = (a - hi.astype(F32)).astype(BF16)
    return hi, lo


_NN = (((1,), (0,)), ((), ()))
_NT = (((1,), (1,)), ((), ()))
_BNN = (((2,), (1,)), ((0,), (0,)))
_BNT = (((2,), (2,)), ((0,), (0,)))
_BTN = (((1,), (1,)), ((0,), (0,)))


def _mm(a, b, dims=_NN, mode="bf16"):
    if mode == "f32":
        return lax.dot_general(a, b, dims, precision=lax.Precision.HIGHEST, preferred_element_type=F32)
    if mode == "bf16":
        return lax.dot_general(a.astype(BF16), b.astype(BF16), dims, preferred_element_type=F32)
    ah, al = _split_bf16(a)
    bh, bl = _split_bf16(b)
    d = functools.partial(lax.dot_general, dimension_numbers=dims, preferred_element_type=F32)
    return d(ah, bh) + (d(ah, bl) + d(al, bh))


def _ada_kernel(c_ref, w_ref, b_ref, o_ref):
    s = _silu(c_ref[...]).astype(BF16)
    o_ref[0] = jnp.dot(s, w_ref[0].astype(BF16), preferred_element_type=F32) + b_ref[0]


def _ada(c_all, w_ada, b_ada):
    rows = c_all.shape[0]
    tn = 3072
    return pl.pallas_call(
        _ada_kernel,
        grid=(DEPTH, 6 * D_MODEL // tn),
        in_specs=[
            pl.BlockSpec((rows, D_MODEL), lambda l, j: (0, 0)),
            pl.BlockSpec((1, D_MODEL, tn), lambda l, j: (l, 0, j)),
            pl.BlockSpec((1, 1, tn), lambda l, j: (l, 0, j)),
        ],
        out_specs=pl.BlockSpec((1, rows, tn), lambda l, j: (l, 0, j)),
        out_shape=jax.ShapeDtypeStruct((DEPTH, rows, 6 * D_MODEL), F32),
        compiler_params=_cparams("arbitrary", "arbitrary"),
        name="ada",
    )(c_all, w_ada, b_ada.reshape(DEPTH, 1, 6 * D_MODEL))


def _mod_spec(mod3, chunk, tiles_per_group):
    r = mod3.shape[1]
    return pl.BlockSpec((1, r, D_MODEL), lambda i, *_: (i // tiles_per_group, 0, chunk))


def _tile_rows(mod_ref, rows):
    m = mod_ref[0]
    reps = rows // m.shape[0]
    return m if m.shape[0] == 1 or reps == 1 else jnp.concatenate([m] * reps, axis=0)


def _inproj_kernel(x_ref, sc_ref, sh_ref, wm_ref, ws_ref, pm_ref, ps_ref):
    tm = x_ref.shape[0]
    u = x_ref[...] * (1.0 + _tile_rows(sc_ref, tm)) + _tile_rows(sh_ref, tm)
    pm_ref[...] = lax.dot_general(u.astype(BF16), wm_ref[...], _NT, preferred_element_type=F32)
    ps_ref[...] = _mm(u, ws_ref[...], _NT, mode="x3")


def _inproj(x, mod3, tpg, w_main, w_small, tm):
    n = x.shape[0]
    return pl.pallas_call(
        _inproj_kernel,
        grid=(n // tm,),
        in_specs=[
            pl.BlockSpec((tm, D_MODEL), lambda i: (i, 0)),
            _mod_spec(mod3, 1, tpg),
            _mod_spec(mod3, 0, tpg),
            pl.BlockSpec((P_MAIN, D_MODEL), lambda i: (0, 0)),
            pl.BlockSpec((LANES, D_MODEL), lambda i: (0, 0)),
        ],
        out_specs=[
            pl.BlockSpec((tm, P_MAIN), lambda i: (i, 0)),
            pl.BlockSpec((tm, LANES), lambda i: (i, 0)),
        ],
        out_shape=[
            jax.ShapeDtypeStruct((n, P_MAIN), F32),
            jax.ShapeDtypeStruct((n, LANES), F32),
        ],
        compiler_params=_cparams("arbitrary"),
        name="inproj",
    )(x, mod3, mod3, w_main, w_small)


def _gates(ps, hp):
    lane = lax.broadcasted_iota(jnp.int32, ps.shape, 1)
    beta = jax.nn.sigmoid(ps)
    g = -jnp.exp(hp[0:1, :]) * jax.nn.softplus(ps + hp[1:2, :])
    return jnp.where(lane >= G_LANE, g, jnp.where(lane >= BETA_LANE, beta, 0.0))


def _qkv_finish(y, out_ref, idx, first_head=0):
    y = _silu(y)
    for j in range(y.shape[1] // DN_HEAD_DIM):
        h = first_head + j
        seg = y[:, j * DN_HEAD_DIM:(j + 1) * DN_HEAD_DIM]
        if h < 2 * DN_HEADS:
            seg = seg * lax.rsqrt(jnp.sum(seg * seg, -1, keepdims=True) + RMS_EPS)
            if h < DN_HEADS:
                seg = seg * (DN_HEAD_DIM ** -0.5)
        out_ref[idx + (slice(None), slice(h * DN_HEAD_DIM, (h + 1) * DN_HEAD_DIM))] = seg


def _shifted(x, tail_ref, i, row):
    s = pltpu.roll(x, i, 0)
    top = s[0:SUBLANES]
    for r in range(i):
        top = jnp.where(row == r, tail_ref[8 - i + r:8 - i + r + 1, :], top)
    return jnp.concatenate([top, s[SUBLANES:]], axis=0)


def _inproj_pre_prompt_kernel(x_ref, sc_ref, sh_ref, wm_ref, ws_ref, cw_ref, dw_ref, hp_ref, cb_ref, db_ref,
                              yc_ref, qkv_ref, z_ref, gb_ref, ncb_ref, ndb_ref, tc_scr, td_scr):
    t = pl.program_id(1)
    tm = x_ref.shape[0]

    @pl.when(t == 0)
    def _():
        tc_scr[8 - (CONV_K - 1):8, :] = cb_ref[0]
        td_scr[8 - (DN_CONV_K - 1):8, :] = db_ref[0]

    u_f32 = x_ref[...] * (1.0 + sc_ref[0]) + sh_ref[0]
    u = u_f32.astype(BF16)
    proj = lambda lo, hi: lax.dot_general(u, wm_ref[lo:hi, :], _NT, preferred_element_type=F32)

    row = lax.broadcasted_iota(jnp.int32, (SUBLANES, DN_WIDTH), 0)

    def dn_part(xp, part):
        cols = slice(part * DN_WIDTH, (part + 1) * DN_WIDTH)
        tail = td_scr.at[:, cols]
        yp = _shifted(xp, tail, 3, row) * dw_ref[0:1, cols]
        yp = yp + _shifted(xp, tail, 2, row) * dw_ref[1:2, cols]
        yp = yp + _shifted(xp, tail, 1, row) * dw_ref[2:3, cols]
        yp = yp + xp * dw_ref[3:4, cols]
        _qkv_finish(yp, qkv_ref, (), first_head=part * DN_HEADS)
        return xp[tm - (DN_CONV_K - 1):tm, :]

    qkv_cols = lambda part: (OFF_QKV + part * DN_WIDTH, OFF_QKV + (part + 1) * DN_WIDTH)
    x_q = proj(*qkv_cols(0))
    x_k = proj(*qkv_cols(1))
    last_q = dn_part(x_q, 0)
    x_v = proj(*qkv_cols(2))
    last_k = dn_part(x_k, 1)
    p_c = proj(OFF_CC, OFF_CH)
    last_v = dn_part(x_v, 2)
    p_h = proj(OFF_CH, OFF_QKV)
    cgh = p_c * p_h
    p_b = proj(0, OFF_CC)
    y = _shifted(cgh, tc_scr, 2, row) * cw_ref[0:1, :]
    y = y + _shifted(cgh, tc_scr, 1, row) * cw_ref[1:2, :]
    y = y + cgh * cw_ref[2:3, :]
    p_z = proj(OFF_Z, P_MAIN)
    yc_ref[...] = (p_b * y).astype(yc_ref.dtype)
    last_c = cgh[tm - (CONV_K - 1):tm, :]
    small = _mm(u_f32, ws_ref[...], _NT, mode="bf16")
    z_ref[...] = p_z.astype(z_ref.dtype)
    gb_ref[...] = _gates(small, hp_ref[...])

    last_d = jnp.concatenate([last_q, last_k, last_v], axis=-1)
    tc_scr[8 - (CONV_K - 1):8, :] = last_c
    td_scr[8 - (DN_CONV_K - 1):8, :] = last_d

    @pl.when(t == pl.num_programs(1) - 1)
    def _():
        ncb_ref[0] = last_c
        ndb_ref[0] = last_d


def _inproj_pre_prompt(x, mod3, w_main, w_small, conv_w, dn_conv_w, hp, conv_buf, dn_buf, bsz, seq, tm):
    n = x.shape[0]
    nt = seq // tm
    rows = lambda w: pl.BlockSpec((tm, w), lambda b, t: (b * nt + t, 0))
    full = lambda a: pl.BlockSpec(a.shape, lambda b, t: (0,) * a.ndim)
    per_b = lambda k, w: pl.BlockSpec((1, k, w), lambda b, t: (b, 0, 0))
    mod = lambda chunk: pl.BlockSpec((1, 1, D_MODEL), lambda b, t: (b, 0, chunk))
    return pl.pallas_call(
        _inproj_pre_prompt_kernel,
        grid=(bsz, nt),
        in_specs=[rows(D_MODEL), mod(1), mod(0), full(w_main), full(w_small),
                  full(conv_w), full(dn_conv_w), full(hp),
                  per_b(CONV_K - 1, CONV_WIDTH), per_b(DN_CONV_K - 1, 3 * DN_WIDTH)],
        out_specs=[rows(CONV_WIDTH), rows(3 * DN_WIDTH), rows(DN_WIDTH), rows(LANES),
                   per_b(CONV_K - 1, CONV_WIDTH), per_b(DN_CONV_K - 1, 3 * DN_WIDTH)],
        out_shape=[
            jax.ShapeDtypeStruct((n, CONV_WIDTH), BF16),
            jax.ShapeDtypeStruct((n, 3 * DN_WIDTH), F32),
            jax.ShapeDtypeStruct((n, DN_WIDTH), BF16),
            jax.ShapeDtypeStruct((n, LANES), F32),
            jax.ShapeDtypeStruct((bsz, CONV_K - 1, CONV_WIDTH), F32),
            jax.ShapeDtypeStruct((bsz, DN_CONV_K - 1, 3 * DN_WIDTH), F32),
        ],
        scratch_shapes=[pltpu.VMEM((SUBLANES, CONV_WIDTH), F32), pltpu.VMEM((SUBLANES, 3 * DN_WIDTH), F32)],
        compiler_params=_cparams("arbitrary", "arbitrary"),
        name="inproj_pre_prompt",
    )(x, mod3, mod3, w_main, w_small, conv_w, dn_conv_w, hp, conv_buf, dn_buf)


def _pre_sample_kernel(pm_ref, ps_ref, cw_ref, dw_ref, hp_ref, cb_ref, db_ref,
                       yc_ref, qkv_ref, gb_ref, ncb_ref, ndb_ref):
    steps = pm_ref.shape[0]
    cgh = [pm_ref[t, :, OFF_CC:OFF_CH] * pm_ref[t, :, OFF_CH:OFF_QKV] for t in range(steps)]
    ext = [cb_ref[i] for i in range(CONV_K - 1)] + cgh
    for t in range(steps):
        y = ext[t] * cw_ref[0:1, :]
        for i in range(1, CONV_K):
            y = y + ext[t + i] * cw_ref[i:i + 1, :]
        yc_ref[t] = pm_ref[t, :, 0:OFF_CC] * y
    for i in range(CONV_K - 1):
        ncb_ref[i] = ext[len(ext) - (CONV_K - 1) + i]

    xq = [pm_ref[t, :, OFF_QKV:OFF_Z] for t in range(steps)]
    extq = [db_ref[i] for i in range(DN_CONV_K - 1)] + xq
    for t in range(steps):
        y = extq[t] * dw_ref[0:1, :]
        for i in range(1, DN_CONV_K):
            y = y + extq[t + i] * dw_ref[i:i + 1, :]
        _qkv_finish(y, qkv_ref, (t,))
        gb_ref[t] = _gates(ps_ref[t], hp_ref[...])
    for i in range(DN_CONV_K - 1):
        ndb_ref[i] = extq[len(extq) - (DN_CONV_K - 1) + i]


def _pre_sample(pm, ps, conv_w, dn_conv_w, hp, conv_buf_tm, dn_buf_tm, bsz, steps, bt):
    slab = lambda k, w: pl.BlockSpec((k, bt, w), lambda i: (0, i, 0))
    full = lambda a: pl.BlockSpec(a.shape, lambda i: (0,) * a.ndim)
    return pl.pallas_call(
        _pre_sample_kernel,
        grid=(bsz // bt,),
        in_specs=[slab(steps, P_MAIN), slab(steps, LANES), full(conv_w), full(dn_conv_w), full(hp),
                  slab(CONV_K - 1, CONV_WIDTH), slab(DN_CONV_K - 1, 3 * DN_WIDTH)],
        out_specs=[slab(steps, CONV_WIDTH), slab(steps, 3 * DN_WIDTH), slab(steps, LANES),
                   slab(CONV_K - 1, CONV_WIDTH), slab(DN_CONV_K - 1, 3 * DN_WIDTH)],
        out_shape=[
            jax.ShapeDtypeStruct((steps, bsz, CONV_WIDTH), F32),
            jax.ShapeDtypeStruct((steps, bsz, 3 * DN_WIDTH), F32),
            jax.ShapeDtypeStruct((steps, bsz, LANES), F32),
            jax.ShapeDtypeStruct((CONV_K - 1, bsz, CONV_WIDTH), F32),
            jax.ShapeDtypeStruct((DN_CONV_K - 1, bsz, 3 * DN_WIDTH), F32),
        ],
        compiler_params=_cparams("arbitrary"),
        name="pre_sample",
    )(pm.reshape(steps, bsz, P_MAIN), ps.reshape(steps, bsz, LANES), conv_w, dn_conv_w, hp,
      conv_buf_tm, dn_buf_tm)


def _cumsum_chunks(x, c):
    row = lax.broadcasted_iota(jnp.int32, x.shape, 0) & (c - 1)
    s = 1
    while s < c:
        x = x + jnp.where(row >= s, pltpu.roll(x, s, 0), 0.0)
        s *= 2
    return x


def _gdn_problems(qkv, gb, c):
    n = qkv.shape[0] // c
    gc = _cumsum_chunks(gb, c)
    q, k, v, g_b, beta_b = [], [], [], [], []
    for ci in range(n):
        rs = slice(ci * c, (ci + 1) * c)
        for h in range(DN_HEADS):
            ls = lambda base: slice(base + h * DN_HEAD_DIM, base + (h + 1) * DN_HEAD_DIM)
            q.append(qkv[rs, ls(0)])
            k.append(qkv[rs, ls(DN_WIDTH)])
            v.append(qkv[rs, ls(2 * DN_WIDTH)])
            g_b.append(jnp.broadcast_to(gc[rs, G_LANE + h:G_LANE + h + 1], (c, DN_HEAD_DIM)))
            beta_b.append(jnp.broadcast_to(gb[rs, BETA_LANE + h:BETA_LANE + h + 1], (c, DN_HEAD_DIM)))
    return tuple(jnp.stack(a) for a in (q, k, v, g_b, beta_b))


def _unit_lower_solve(m, rhs, c, small, mode):
    if small:
        sol = rhs
        for j in range(c - 1):
            sol = sol - m[:, :, j:j + 1] * sol[:, j:j + 1, :]
        return sol
    row = lax.broadcasted_iota(jnp.int32, m.shape, 1)
    col = lax.broadcasted_iota(jnp.int32, m.shape, 2)
    p = -m
    t = jnp.where(row == col, 1.0, 0.0) + p
    p = _mm(p, p, _BNN, "bf16")
    levels = int(math.log2(c))
    for lvl in range(1, levels):
        if lvl < levels - 1:
            y = _mm(jnp.concatenate([t, p], axis=1), p, _BNN, "bf16")
            t = t + y[:, :c]
            p = y[:, c:]
        else:
            t = t + _mm(t, p, _BNN, "bf16")
    x = _mm(t, rhs, _BNN, "bf16")
    resid = rhs - x - _mm(m, x, _BNN, mode)
    return x + _mm(t, resid, _BNN, "bf16")


def _gdn_solve(q, k, v, g_b, beta_b, *, c, small, mode_gram, mode_solve):
    n = q.shape[0]
    row = lax.broadcasted_iota(jnp.int32, (n, c, c), 1)
    col = lax.broadcasted_iota(jnp.int32, (n, c, c), 2)
    g_cc = g_b[:, :, :c]
    g_row = jnp.sum(jnp.where(row == col, g_cc, 0.0), axis=1, keepdims=True)
    gamma = jnp.exp(jnp.where(row >= col, g_cc - g_row, -jnp.inf))
    kbeta = k * beta_b
    exp_g = jnp.exp(g_b)
    gram = _mm(jnp.concatenate([kbeta, q], axis=1), k, _BNT, mode_gram)
    m = jnp.where(row > col, gram[:, :c] * gamma, 0.0)
    attn = gram[:, c:] * gamma
    rhs = jnp.concatenate([kbeta * exp_g, v * beta_b], axis=-1)
    sol = _unit_lower_solve(m, rhs, c, small, mode_solve)
    g_last = g_b[:, c - 1:c, :]
    return sol, attn, q * exp_g, k * jnp.exp(g_last - g_b), jnp.exp(g_last)


def _gdn_fold(sol, attn, q_dec, k_dec, mode):
    kd = _mm(k_dec, sol, _BTN, mode)
    at = _mm(attn, sol, _BNN, mode)
    return kd[..., :DN_HEAD_DIM], kd[..., DN_HEAD_DIM:], q_dec - at[..., :DN_HEAD_DIM], at[..., DN_HEAD_DIM:]


def _gdn_state_step(a_mat, b_mat, q_t, o_intra, d_last, s, mode):
    r = _mm(jnp.concatenate([a_mat, q_t], axis=1), s, _BNN, mode)
    s_new = s * d_last - r[:, :DN_HEAD_DIM] + b_mat
    return r[:, DN_HEAD_DIM:] + o_intra, s_new


def _gdn_direct_step(sol, attn, q_dec, k_dec, d_last, s, mode):
    c = attn.shape[1]
    r = _mm(jnp.concatenate([sol[..., :DN_HEAD_DIM], q_dec], axis=1), s, _BNN, mode)
    v_new = sol[..., DN_HEAD_DIM:] - r[:, :c]
    o = r[:, c:] + _mm(attn, v_new, _BNN, mode)
    return o, s * d_last + _mm(k_dec, v_new, _BTN, mode)


_PROMPT_MODES = dict(mode_gram="bf16", mode_solve="x3")
_SAMPLE_MODES = dict(mode_gram="bf16", mode_solve="bf16")
_STATE_MODE = "bf16"


def _delta_prompt_kernel(qkv_ref, gb_ref, s0_ref, o_ref, sn_ref, s_scr):
    t = pl.program_id(1)
    tc = qkv_ref.shape[0]

    @pl.when(t == 0)
    def _():
        s_scr[...] = s0_ref[0]

    probs = _gdn_problems(qkv_ref[...], gb_ref[...], DN_CHUNK)
    sol, attn, q_dec, k_dec, d_last = _gdn_solve(*probs, c=DN_CHUNK, small=False, **_PROMPT_MODES)
    a_mat, b_mat, q_t, o_intra = _gdn_fold(sol, attn, q_dec, k_dec, _STATE_MODE)
    s = s_scr[...]
    for ci in range(tc // DN_CHUNK):
        ps = slice(ci * DN_HEADS, (ci + 1) * DN_HEADS)
        o, s = _gdn_state_step(a_mat[ps], b_mat[ps], q_t[ps], o_intra[ps], d_last[ps], s, _STATE_MODE)
        for h in range(DN_HEADS):
            o_ref[ci * DN_CHUNK:(ci + 1) * DN_CHUNK, h * DN_HEAD_DIM:(h + 1) * DN_HEAD_DIM] = (
                o[h].astype(o_ref.dtype))
    s_scr[...] = s

    @pl.when(t == pl.num_programs(1) - 1)
    def _():
        sn_ref[0] = s


def _delta_prompt(qkv, gb, s0, bsz, seq, tc):
    n = qkv.shape[0]
    nt = seq // tc
    rows = lambda w: pl.BlockSpec((tc, w), lambda b, t: (b * nt + t, 0))
    state = pl.BlockSpec((1, DN_HEADS, DN_HEAD_DIM, DN_HEAD_DIM), lambda b, t: (b, 0, 0, 0))
    return pl.pallas_call(
        _delta_prompt_kernel,
        grid=(bsz, nt),
        in_specs=[rows(3 * DN_WIDTH), rows(LANES), state],
        out_specs=[rows(DN_WIDTH), state],
        out_shape=[
            jax.ShapeDtypeStruct((n, DN_WIDTH), BF16),
            jax.ShapeDtypeStruct((bsz, DN_HEADS, DN_HEAD_DIM, DN_HEAD_DIM), F32),
        ],
        scratch_shapes=[pltpu.VMEM((DN_HEADS, DN_HEAD_DIM, DN_HEAD_DIM), F32)],
        compiler_params=_cparams("arbitrary", "arbitrary"),
        name="delta_prompt",
    )(qkv, gb, s0)


def _delta_sample_kernel(qkv_ref, gb_ref, s0_ref, *rest):
    o_ref, sn_ref = rest[-2:]
    bt = qkv_ref.shape[0]
    c = SAMPLE_CHUNK
    probs = _gdn_problems(qkv_ref[...].reshape(bt * c, 3 * DN_WIDTH), gb_ref[...].reshape(bt * c, LANES), c)
    parts = _gdn_solve(*probs, c=c, small=True, **_SAMPLE_MODES)
    s = s0_ref[0].reshape(bt * DN_HEADS, DN_HEAD_DIM, DN_HEAD_DIM)
    o, s = _gdn_direct_step(*parts, s, _STATE_MODE)
    for b in range(bt):
        for h in range(DN_HEADS):
            o_ref[b, :, h * DN_HEAD_DIM:(h + 1) * DN_HEAD_DIM] = o[b * DN_HEADS + h]
    sn_ref[0] = s.reshape(bt, DN_HEADS, DN_HEAD_DIM, DN_HEAD_DIM)


def _delta_sample(qkv_b, gb_b, state_all, layer, carried, bt):
    bsz = qkv_b.shape[0]
    blk = lambda w: pl.BlockSpec((bt, SAMPLE_CHUNK, w), lambda i: (i, 0, 0))
    state = pl.BlockSpec((1, bt, DN_HEADS, DN_HEAD_DIM, DN_HEAD_DIM), lambda i: (layer, i, 0, 0, 0))
    in_specs, args, aliases = [blk(3 * DN_WIDTH), blk(LANES), state], [qkv_b, gb_b, state_all], {}
    if carried is not None:
        in_specs.append(pl.BlockSpec(memory_space=pl.ANY))
        args.append(carried)
        aliases = {3: 1}
    return pl.pallas_call(
        _delta_sample_kernel,
        grid=(bsz // bt,),
        in_specs=in_specs,
        out_specs=[blk(DN_WIDTH), state],
        out_shape=[
            jax.ShapeDtypeStruct((bsz, SAMPLE_CHUNK, DN_WIDTH), F32),
            jax.ShapeDtypeStruct(state_all.shape, F32),
        ],
        input_output_aliases=aliases,
        compiler_params=_cparams("arbitrary"),
        name="delta_sample",
    )(*args)


def _outproj_kernel(yc_ref, o_ref, z_ref, x_ref, g1_ref, nw_ref, w_ref, lg_ref, lb_ref, out_ref):
    o = o_ref[...].astype(F32)
    parts = []
    for h in range(DN_HEADS):
        oh = o[:, h * DN_HEAD_DIM:(h + 1) * DN_HEAD_DIM]
        parts.append(oh * lax.rsqrt(jnp.mean(oh * oh, -1, keepdims=True) + RMS_EPS) * nw_ref[...])
    og = jnp.concatenate(parts, axis=-1) * _silu(z_ref[...].astype(F32))
    mixed = jnp.concatenate([yc_ref[...].astype(BF16), og.astype(BF16)], axis=-1)
    m = jnp.dot(mixed, w_ref[...], preferred_element_type=F32)
    r = DEEPNORM_ALPHA * x_ref[...] + (1.0 + _tile_rows(g1_ref, x_ref.shape[0])) * m
    out_ref[...] = _layer_norm(r, lg_ref[...], lb_ref[...])


def _outproj(yc, o, z_src, z_blk, x, mod3, tpg, norm_w, w_out, ln_g, ln_b, tm):
    n = x.shape[0]
    vec = lambda w: pl.BlockSpec((1, w), lambda i: (0, 0))
    return pl.pallas_call(
        _outproj_kernel,
        grid=(n // tm,),
        in_specs=[
            pl.BlockSpec((tm, CONV_WIDTH), lambda i: (i, 0)),
            pl.BlockSpec((tm, DN_WIDTH), lambda i: (i, 0)),
            pl.BlockSpec((tm, DN_WIDTH), lambda i: (i, z_blk)),
            pl.BlockSpec((tm, D_MODEL), lambda i: (i, 0)),
            _mod_spec(mod3, 2, tpg),
            vec(DN_HEAD_DIM),
            pl.BlockSpec((D_MODEL, D_MODEL), lambda i: (0, 0)),
            vec(D_MODEL), vec(D_MODEL),
        ],
        out_specs=pl.BlockSpec((tm, D_MODEL), lambda i: (i, 0)),
        out_shape=jax.ShapeDtypeStruct((n, D_MODEL), F32),
        compiler_params=_cparams("arbitrary"),
        name="outproj",
    )(yc, o, z_src, x, mod3, norm_w, w_out, ln_g, ln_b)


R_E1, R_E2, R_G1, R_G2, R_R1, R_R2 = range(6)
N_ROW_BUFS = 2


def _joint_specs(xp, xs, mod_p, mod_s, chunks, tm, seq):
    last = xp.shape[0] // tm - 1
    tpg = seq // tm
    assert xs.shape[0] == tm
    specs = [pl.BlockSpec((tm, D_MODEL), lambda i, *_: (jnp.minimum(i, last), 0)),
             pl.BlockSpec((tm, D_MODEL), lambda i, *_: (0, 0))]
    for c in chunks:
        specs.append(pl.BlockSpec((1, 1, D_MODEL), lambda i, *_, c=c: (jnp.minimum(i, last) // tpg, 0, c)))
        specs.append(pl.BlockSpec((1, mod_s.shape[1], D_MODEL), lambda i, *_, c=c: (0, 0, c)))
    return specs


def _joint_modulated(i, n_prompt_tiles, xp_ref, xs_ref, scp_ref, scs_ref, shp_ref, shs_ref, u_ref):
    tm = xp_ref.shape[0]

    @pl.when(i < n_prompt_tiles)
    def _():
        u_ref[...] = (xp_ref[...] * (1.0 + scp_ref[0]) + shp_ref[0]).reshape(u_ref.shape)

    @pl.when(i >= n_prompt_tiles)
    def _():
        u_ref[...] = (xs_ref[...] * (1.0 + _tile_rows(scs_ref, tm)) + _tile_rows(shs_ref, tm)).reshape(
            u_ref.shape)


def _router_kernel(xp_ref, xs_ref, scp_ref, scs_ref, shp_ref, shs_ref, wr_ref, br_ref,
                   route_ref, route_t_ref, cnt_ref, u_scr, *, n_prompt_tiles):
    tm = xp_ref.shape[0]
    _joint_modulated(pl.program_id(0), n_prompt_tiles, xp_ref, xs_ref, scp_ref, scs_ref, shp_ref, shs_ref,
                     u_scr)
    lt = _mm(wr_ref[...], u_scr[...], _NT, mode="x3") + br_ref[:, 0:1]
    e_idx = lax.broadcasted_iota(jnp.int32, lt.shape, 0)
    ex = jnp.exp(lt - jnp.max(lt, 0, keepdims=True))
    probs = ex / jnp.sum(ex, 0, keepdims=True)
    p1 = jnp.max(probs, 0, keepdims=True)
    i1 = jnp.min(jnp.where(probs == p1, e_idx, N_EXPERTS), 0, keepdims=True)
    rest = jnp.where(e_idx == i1, -1.0, probs)
    p2 = jnp.max(rest, 0, keepdims=True)
    i2 = jnp.min(jnp.where(rest == p2, e_idx, N_EXPERTS), 0, keepdims=True)
    tot = p1 + p2
    sel = jnp.where(e_idx == i1, 1.0, 0.0) + jnp.where(e_idx == i2, 1.0, 0.0)
    r = lax.broadcasted_iota(jnp.int32, (tm, tm), 0)
    c = lax.broadcasted_iota(jnp.int32, (tm, tm), 1)
    incl = jnp.dot(sel.astype(BF16), jnp.where(r <= c, 1.0, 0.0).astype(BF16), preferred_element_type=F32)
    excl = incl - sel
    r1 = jnp.sum(jnp.where(e_idx == i1, excl, 0.0), 0, keepdims=True)
    r2 = jnp.sum(jnp.where(e_idx == i2, excl, 0.0), 0, keepdims=True)
    cnt_ref[0] = jnp.broadcast_to(incl[:, tm - 1:tm], (N_EXPERTS, LANES))
    route_t = jnp.zeros_like(lt)
    for j, val in enumerate((i1.astype(F32), i2.astype(F32), p1 / tot, p2 / tot, r1, r2)):
        route_t = jnp.where(e_idx == j, val, route_t)
    route_t_ref[0] = route_t
    route_ref[...] = jnp.transpose(
        jnp.concatenate([route_t, jnp.zeros((LANES - N_EXPERTS, tm), F32)], axis=0))


def _router(xp, xs, mod_p, mod_s, seq, w_router, b_router, tm):
    ntp = xp.shape[0] // tm
    nt = ntp + 1
    return pl.pallas_call(
        functools.partial(_router_kernel, n_prompt_tiles=ntp),
        grid=(nt,),
        in_specs=_joint_specs(xp, xs, mod_p, mod_s, (4, 3), tm, seq) + [
            pl.BlockSpec((N_EXPERTS, D_MODEL), lambda i: (0, 0)),
            pl.BlockSpec((N_EXPERTS, LANES), lambda i: (0, 0)),
        ],
        out_specs=[pl.BlockSpec((tm, LANES), lambda i: (i, 0)),
                   pl.BlockSpec((1, SUBLANES, tm), lambda i: (i, 0, 0)),
                   pl.BlockSpec((1, N_EXPERTS, LANES), lambda i: (i, 0, 0))],
        out_shape=[jax.ShapeDtypeStruct((nt * tm, LANES), F32),
                   jax.ShapeDtypeStruct((nt, SUBLANES, tm), F32),
                   jax.ShapeDtypeStruct((nt, N_EXPERTS, LANES), F32)],
        scratch_shapes=[pltpu.VMEM((tm, D_MODEL), F32)],
        compiler_params=_cparams("arbitrary"),
        name="router",
    )(xp, xs, mod_p, mod_s, mod_p, mod_s, w_router, b_router)


def _dispatch_plan(route_t, counts, tme):
    n_rt, _, tm = route_t.shape
    n = n_rt * tm
    cnt = counts[:, :, 0].astype(jnp.int32)
    tile_off = jnp.cumsum(cnt, axis=0) - cnt
    tot = jnp.sum(cnt, axis=0)
    gsz = (tot + tme - 1) // tme * tme
    gend = jnp.cumsum(gsz)
    goff = gend - gsz
    base = (goff[None, :] + tile_off)[:, :, None]
    ids = jnp.arange(N_EXPERTS, dtype=jnp.int32)[None, :, None]
    field = lambda j: route_t[:, j:j + 1, :].astype(jnp.int32)
    pick = lambda e: jnp.sum(jnp.where(ids == e, base, 0), axis=1, keepdims=True)
    slot1 = (pick(field(R_E1)) + field(R_R1)).reshape(n)
    slot2 = (pick(field(R_E2)) + field(R_R2)).reshape(n)
    n_tiles = 2 * n // tme + N_EXPERTS
    start = jnp.arange(n_tiles, dtype=jnp.int32) * tme
    valid = (start < gend[-1]).astype(jnp.int32)
    blk = jnp.minimum(jnp.arange(n_tiles, dtype=jnp.int32), gend[-1] // tme - 1)
    tile_e = jnp.sum((blk * tme)[:, None] >= gend[None, :], axis=1).astype(jnp.int32)
    zoff = jnp.where(gsz > 0, gend - tme, (n_tiles + jnp.arange(N_EXPERTS, dtype=jnp.int32)) * tme)
    return slot1, slot2, zoff, tile_e, valid, blk, n_tiles


def _dispatch_kernel(s1_ref, s2_ref, zoff_ref, xp_ref, xs_ref, scp_ref, scs_ref, shp_ref, shs_ref,
                     ug_hbm, ubuf, zbuf, sem, *, n_prompt_tiles):
    i = pl.program_id(0)
    tm = xp_ref.shape[0]
    tme = zbuf.shape[0]

    n_steps = n_prompt_tiles + 1
    fill_sem = 2 * N_ROW_BUFS

    @pl.when(i == 0)
    def _():
        zbuf[...] = jnp.zeros_like(zbuf)
        fill = lambda e: pltpu.make_async_copy(
            zbuf, ug_hbm.at[pl.ds(pl.multiple_of(zoff_ref[e], 8), tme)], sem.at[fill_sem])
        for e in range(N_EXPERTS):
            fill(e).start()
        for e in range(N_EXPERTS):
            fill(e).wait()

    def drain(b):
        for stream in range(2):
            pltpu.make_async_copy(ubuf.at[b], ubuf.at[b], sem.at[2 * b + stream]).wait()

    buf = i % N_ROW_BUFS

    @pl.when(i >= N_ROW_BUFS)
    def _():
        drain(buf)

    _joint_modulated(i, n_prompt_tiles, xp_ref, xs_ref, scp_ref, scs_ref, shp_ref, shs_ref, ubuf.at[buf])
    base = i * tm

    def issue(j, carry):
        for k in range(SUBLANES):
            row = ubuf.at[buf, j, pl.ds(k, 1)]
            t = base + j * SUBLANES + k
            pltpu.make_async_copy(row, ug_hbm.at[pl.ds(s1_ref[t], 1)], sem.at[2 * buf]).start()
            pltpu.make_async_copy(row, ug_hbm.at[pl.ds(s2_ref[t], 1)], sem.at[2 * buf + 1]).start(priority=1)
        return carry

    lax.fori_loop(0, tm // SUBLANES, issue, 0)

    @pl.when(i == n_steps - 1)
    def _():
        for b in range(min(N_ROW_BUFS, n_steps)):
            drain(b)


def _dispatch(xp, xs, mod_p, mod_s, seq, slot1, slot2, zoff, tm, tme, rows):
    ntp = xp.shape[0] // tm
    return pl.pallas_call(
        functools.partial(_dispatch_kernel, n_prompt_tiles=ntp),
        grid_spec=pltpu.PrefetchScalarGridSpec(
            num_scalar_prefetch=3,
            grid=(ntp + 1,),
            in_specs=_joint_specs(xp, xs, mod_p, mod_s, (4, 3), tm, seq),
            out_specs=pl.BlockSpec(memory_space=pl.ANY),
            scratch_shapes=[pltpu.VMEM((N_ROW_BUFS, tm // SUBLANES, SUBLANES, D_MODEL), F32),
                            pltpu.VMEM((tme, D_MODEL), F32),
                            pltpu.SemaphoreType.DMA((2 * N_ROW_BUFS + 1,))],
        ),
        out_shape=jax.ShapeDtypeStruct((rows, D_MODEL), F32),
        compiler_params=_cparams("arbitrary"),
        name="dispatch",
    )(slot1, slot2, zoff, xp, xs, mod_p, mod_s, mod_p, mod_s)


def _expert_kernel(te_ref, tv_ref, tb_ref, ug_ref, wg_ref, wu_ref, wd_ref, ys_ref):
    i = pl.program_id(0)

    @pl.when(tv_ref[i] == 1)
    def _():
        u = ug_ref[...].astype(BF16)
        h = _silu(jnp.dot(u, wg_ref[0].astype(BF16), preferred_element_type=F32)) * jnp.dot(
            u, wu_ref[0].astype(BF16), preferred_element_type=F32)
        ys_ref[...] = jnp.dot(h.astype(BF16), wd_ref[0].astype(BF16), preferred_element_type=F32)

    @pl.when(tv_ref[i] == 0)
    def _():
        ys_ref[...] = jnp.zeros_like(ys_ref)


def _experts(ug, tile_e, valid, blk, w_gate, w_up, w_down, tme, n_tiles):
    d_e = w_gate.shape[-1]
    return pl.pallas_call(
        _expert_kernel,
        grid_spec=pltpu.PrefetchScalarGridSpec(
            num_scalar_prefetch=3,
            grid=(n_tiles,),
            in_specs=[
                pl.BlockSpec((tme, D_MODEL), lambda i, te, tv, tb: (tb[i], 0)),
                pl.BlockSpec((1, D_MODEL, d_e), lambda i, te, tv, tb: (te[i], 0, 0)),
                pl.BlockSpec((1, D_MODEL, d_e), lambda i, te, tv, tb: (te[i], 0, 0)),
                pl.BlockSpec((1, d_e, D_MODEL), lambda i, te, tv, tb: (te[i], 0, 0)),
            ],
            out_specs=pl.BlockSpec((tme, D_MODEL), lambda i, te, tv, tb: (i, 0)),
        ),
        out_shape=jax.ShapeDtypeStruct((n_tiles * tme, D_MODEL), F32),
        compiler_params=_cparams("arbitrary"),
        name="experts",
    )(tile_e, valid, blk, ug, w_gate, w_up, w_down)


def _combine_kernel(s1_ref, s2_ref, ys_hbm, route_ref, xp_ref, xs_ref, g2p_ref, g2s_ref, lg_ref, lb_ref,
                    outp_ref, outs_ref, y1buf, y2buf, sem, *, n_prompt_tiles):
    i = pl.program_id(0)
    tm = xp_ref.shape[0]
    n_steps = n_prompt_tiles + 1

    def gather(tile, b):
        def issue(j, carry):
            for k in range(SUBLANES):
                t = tile * tm + j * SUBLANES + k
                pltpu.make_async_copy(ys_hbm.at[pl.ds(s1_ref[t], 1)], y1buf.at[b, j, pl.ds(k, 1)],
                                      sem.at[2 * b]).start()
                pltpu.make_async_copy(ys_hbm.at[pl.ds(s2_ref[t], 1)], y2buf.at[b, j, pl.ds(k, 1)],
                                      sem.at[2 * b + 1]).start(priority=1)
            return carry

        lax.fori_loop(0, tm // SUBLANES, issue, 0)

    buf = i % N_ROW_BUFS

    @pl.when(i == 0)
    def _():
        gather(0, 0)

    @pl.when(i + 1 < n_steps)
    def _():
        gather(i + 1, (i + 1) % N_ROW_BUFS)

    pltpu.make_async_copy(y1buf.at[buf], y1buf.at[buf], sem.at[2 * buf]).wait()
    pltpu.make_async_copy(y2buf.at[buf], y2buf.at[buf], sem.at[2 * buf + 1]).wait()
    route = route_ref[...]
    y1 = y1buf[buf].reshape(tm, D_MODEL)
    y2 = y2buf[buf].reshape(tm, D_MODEL)
    f = route[:, R_G1:R_G1 + 1] * y1 + route[:, R_G2:R_G2 + 1] * y2

    @pl.when(i < n_prompt_tiles)
    def _():
        r = DEEPNORM_ALPHA * xp_ref[...] + (1.0 + g2p_ref[0]) * f
        outp_ref[...] = _layer_norm(r, lg_ref[...], lb_ref[...])

    @pl.when(i >= n_prompt_tiles)
    def _():
        r = DEEPNORM_ALPHA * xs_ref[...] + (1.0 + _tile_rows(g2s_ref, tm)) * f
        outs_ref[...] = _layer_norm(r, lg_ref[...], lb_ref[...])


def _combine(ys, slot1, slot2, route, xp, xs, mod_p, mod_s, seq, ln_g, ln_b, tm):
    ntp = xp.shape[0] // tm
    vec = lambda w: pl.BlockSpec((1, w), lambda i, *_: (0, 0))
    x_specs = _joint_specs(xp, xs, mod_p, mod_s, (5,), tm, seq)
    return pl.pallas_call(
        functools.partial(_combine_kernel, n_prompt_tiles=ntp),
        grid_spec=pltpu.PrefetchScalarGridSpec(
            num_scalar_prefetch=2,
            grid=(ntp + 1,),
            in_specs=[
                pl.BlockSpec(memory_space=pl.ANY),
                pl.BlockSpec((tm, LANES), lambda i, *_: (i, 0)),
            ] + x_specs + [vec(D_MODEL), vec(D_MODEL)],
            out_specs=[x_specs[0], x_specs[1]],
            scratch_shapes=[pltpu.VMEM((N_ROW_BUFS, tm // SUBLANES, SUBLANES, D_MODEL), F32),
                            pltpu.VMEM((N_ROW_BUFS, tm // SUBLANES, SUBLANES, D_MODEL), F32),
                            pltpu.SemaphoreType.DMA((2 * N_ROW_BUFS,))],
        ),
        out_shape=[jax.ShapeDtypeStruct(xp.shape, F32), jax.ShapeDtypeStruct(xs.shape, F32)],
        compiler_params=_cparams("arbitrary"),
        name="combine",
    )(slot1, slot2, ys, route, xp, xs, mod_p, mod_s, ln_g, ln_b)


def _moe(xp, xs, mod_p, mod_s, seq, w_router, b_router, w_gate, w_up, w_down, ln_g, ln_b, tm, tme):
    route, route_t, counts = _router(xp, xs, mod_p, mod_s, seq, w_router, b_router, tm)
    slot1, slot2, zoff, tile_e, valid, blk, n_tiles = _dispatch_plan(route_t, counts, tme)
    ug = _dispatch(xp, xs, mod_p, mod_s, seq, slot1, slot2, zoff, tm, tme, (n_tiles + N_EXPERTS) * tme)
    ys = _experts(ug, tile_e, valid, blk, w_gate, w_up, w_down, tme, n_tiles)
    return _combine(ys, slot1, slot2, route, xp, xs, mod_p, mod_s, seq, ln_g, ln_b, tm)


def _ffn_kernel(x_ref, sc_ref, sh_ref, g2_ref, wg_ref, wu_ref, wd_ref, lg_ref, lb_ref, out_ref, *, tf):
    x = x_ref[...]
    tm = x.shape[0]
    u = (x * (1.0 + _tile_rows(sc_ref, tm)) + _tile_rows(sh_ref, tm)).astype(BF16)
    y = None
    for f0 in range(0, wg_ref.shape[1], tf):
        h = _silu(jnp.dot(u, wg_ref[:, f0:f0 + tf], preferred_element_type=F32)) * jnp.dot(
            u, wu_ref[:, f0:f0 + tf], preferred_element_type=F32)
        part = jnp.dot(h.astype(BF16), wd_ref[f0:f0 + tf, :], preferred_element_type=F32)
        y = part if y is None else y + part
    r = DEEPNORM_ALPHA * x + (1.0 + _tile_rows(g2_ref, tm)) * y
    out_ref[...] = _layer_norm(r, lg_ref[...], lb_ref[...])


def _ffn(x, mod3, tpg, w_gate, w_up, w_down, ln_g, ln_b, tm, tf):
    n = x.shape[0]
    vec = lambda w: pl.BlockSpec((1, w), lambda i: (0, 0))
    resident = lambda a: pl.BlockSpec(a.shape, lambda i: (0, 0), pipeline_mode=pl.Buffered(1))
    return pl.pallas_call(
        functools.partial(_ffn_kernel, tf=tf),
        grid=(n // tm,),
        in_specs=[
            pl.BlockSpec((tm, D_MODEL), lambda i: (i, 0)),
            _mod_spec(mod3, 4, tpg),
            _mod_spec(mod3, 3, tpg),
            _mod_spec(mod3, 5, tpg),
            resident(w_gate), resident(w_up), resident(w_down),
            vec(D_MODEL), vec(D_MODEL),
        ],
        out_specs=pl.BlockSpec((tm, D_MODEL), lambda i: (i, 0)),
        out_shape=jax.ShapeDtypeStruct((n, D_MODEL), F32),
        compiler_params=_cparams("arbitrary"),
        name="ffn_dense",
    )(x, mod3, mod3, mod3, w_gate, w_up, w_down, ln_g, ln_b)


def _token_mixer(l, x, mod3, rows_per_group, prm, tm, mixer_fn):
    yc, o, z_src, z_blk, st = mixer_fn(l, x, mod3)
    x = _outproj(yc, o, z_src, z_blk, x, mod3, rows_per_group // tm, prm["norm_w"][l], prm["w_out"][l],
                 prm["ln_g"][l, 0:1], prm["ln_b"][l, 0:1], tm)
    return x, st


def kernel(x_prompt, x_sample, state_conv, state_dn_conv, state_dn, c_prompt, c_sample, w_in, conv_w,
           dn_conv_w, a_log, dt_bias, dn_norm_w, w_out, w_ada, b_ada, ln_g, ln_b, w_ff_gate, w_ff_up,
           w_ff_down, w_router, b_router, w_exp_gate, w_exp_up, w_exp_down):
    bp, seq, d = x_prompt.shape
    bs, steps, _ = x_sample.shape
    hp = jnp.zeros((DEPTH, 2, LANES), F32)
    hp = hp.at[:, 0, G_LANE:].set(a_log).at[:, 1, G_LANE:].set(dt_bias)
    w_in_t = jnp.swapaxes(w_in, 1, 2)
    prm = {
        "w_main": w_in_t[:, :P_MAIN].astype(BF16),
        "w_small": w_in_t[:, w_in_t.shape[1] - LANES:],
        "norm_w": dn_norm_w.reshape(DEPTH, 1, DN_HEAD_DIM),
        "w_out": w_out.astype(BF16),
        "ln_g": ln_g, "ln_b": ln_b,
        "w_ff_gate": w_ff_gate.astype(BF16), "w_ff_up": w_ff_up.astype(BF16),
        "w_ff_down": w_ff_down.astype(BF16),
        "w_router": jnp.swapaxes(w_router, 1, 2),
        "b_router": jnp.broadcast_to(b_router[:, :, None], b_router.shape + (LANES,)),
        "w_exp_gate": w_exp_gate, "w_exp_up": w_exp_up, "w_exp_down": w_exp_down,
    }

    mod = _ada(jnp.concatenate([c_prompt, c_sample], axis=0), w_ada, b_ada)
    mod_p = [mod[l, :bp].reshape(bp, 1, 6 * d) for l in range(DEPTH)]
    mod_s = [mod[l, bp:].reshape(1, bs, 6 * d) for l in range(DEPTH)]

    tm_p = PROMPT_ROW_TILE
    zc = jnp.zeros((bp, CONV_K - 1, CONV_WIDTH), F32)
    zd = jnp.zeros((bp, DN_CONV_K - 1, 3 * DN_WIDTH), F32)
    zs = jnp.zeros((bp, DN_HEADS, DN_HEAD_DIM, DN_HEAD_DIM), F32)

    def mixer_prompt(l, x, mod3):
        yc, qkv, z, gb, ncb, ndb = _inproj_pre_prompt(
            x, mod3, prm["w_main"][l], prm["w_small"][l], conv_w[l], dn_conv_w[l], hp[l], zc, zd,
            bp, seq, tm_p)
        o, s_new = _delta_prompt(qkv, gb, zs, bp, seq, PROMPT_DELTA_TILE)
        return yc, o, z, 0, (ncb, ndb, s_new)

    dn_states = []

    def mixer_sample(l, x, mod3):
        pm, ps = _inproj(x, mod3, 1, prm["w_main"][l], prm["w_small"][l], steps * bs)
        cb_tm = jnp.transpose(state_conv[l], (1, 0, 2))
        db_tm = jnp.transpose(state_dn_conv[l], (1, 0, 2))
        yc, qkv, gb, ncb, ndb = _pre_sample(pm, ps, conv_w[l], dn_conv_w[l], hp[l], cb_tm, db_tm,
                                            bs, steps, SAMPLE_PRE_SEQS)
        to_b = lambda a: jnp.pad(jnp.transpose(a, (1, 0, 2)), ((0, 0), (0, SAMPLE_CHUNK - steps), (0, 0)))
        o_b, s_all = _delta_sample(to_b(qkv), to_b(gb), state_dn, l, dn_states[-1] if dn_states else None,
                                   SAMPLE_DELTA_SEQS)
        dn_states.append(s_all)
        o = jnp.transpose(o_b[:, :steps], (1, 0, 2)).reshape(steps * bs, DN_WIDTH)
        return (yc.reshape(steps * bs, CONV_WIDTH), o, pm, OFF_Z // DN_WIDTH,
                (jnp.transpose(ncb, (1, 0, 2)), jnp.transpose(ndb, (1, 0, 2))))

    x_p = x_prompt.reshape(bp * seq, d)
    x_s = jnp.transpose(x_sample, (1, 0, 2)).reshape(steps * bs, d)
    n_s = steps * bs
    st_p, st_s = [], []
    for l in range(DEPTH):
        x_p, st = _token_mixer(l, x_p, mod_p[l], seq, prm, PROMPT_OUTPROJ_TILE, mixer_prompt)
        st_p.append(st)
        x_s, st = _token_mixer(l, x_s, mod_s[l], n_s, prm, n_s, mixer_sample)
        st_s.append(st)
        j = l // 2
        ln = (prm["ln_g"][l, 1:2], prm["ln_b"][l, 1:2])
        if l % 2 == 0:
            ffn_w = (prm["w_ff_gate"][j], prm["w_ff_up"][j], prm["w_ff_down"][j])
            x_p = _ffn(x_p, mod_p[l], seq // tm_p, *ffn_w, *ln, tm_p, FFN_HIDDEN_CHUNK)
            x_s = _ffn(x_s, mod_s[l], 1, *ffn_w, *ln, n_s, FFN_HIDDEN_CHUNK)
        else:
            x_p, x_s = _moe(x_p, x_s, mod_p[l], mod_s[l], seq, prm["w_router"][j], prm["b_router"][j],
                            prm["w_exp_gate"][j], prm["w_exp_up"][j], prm["w_exp_down"][j], *ln, tm_p,
                            EXPERT_SLOT_TILE)
    y_p = x_p
    y_s = jnp.transpose(x_s.reshape(steps, bs, d), (1, 0, 2))

    stack = lambda sts, i: jnp.stack([s[i] for s in sts])
    return (y_p.reshape(bp, seq, d), y_s,
            stack(st_p, 0), stack(st_p, 1), stack(st_p, 2),
            stack(st_s, 0), stack(st_s, 1), dn_states[-1])
```
